```python
import math
import jax, jax.numpy as jnp
from jax import lax
import numpy as np

D_MODEL = 1024
BATCH = 16
SEQ = 256
DEPTH = 2
DEC_BATCH = 4
DEC_SEQ = 4096
PAST_LEN = 256

GRID_W = 64
POOL_WINDOWS = (2, 4, 8, 16)
POOL_WIDTH = D_MODEL // 4
POOL_GROUP = POOL_WIDTH // len(POOL_WINDOWS)
HEAD_DIM = 64
N_HEADS = (D_MODEL // 2) // HEAD_DIM
N_KV_HEADS = 2
GQA_GROUP = N_HEADS // N_KV_HEADS
ATTN_WIDTH = N_HEADS * HEAD_DIM
KV_WIDTH = N_KV_HEADS * HEAD_DIM
WINDOW = 128
BLOCK = 128
ROPE_THETA = 10000.0
HY_WIDTH = D_MODEL - POOL_WIDTH - ATTN_WIDTH
HY_ORDER = 2
HY_EMB_DIM = 33
HY_FILTER_HIDDEN = 64
HY_MOD_SHIFT = 0.05
D_FF = 2816
IN_WIDTH = POOL_WIDTH + ATTN_WIDTH + 2 * KV_WIDTH + (HY_ORDER + 1) * HY_WIDTH
N_MOD = 9
NORM_EPS = 1e-6
NEG_INF = -1e30

kernel_name = "hybrid_pool_swa_hyena_diffusion_step"

F32 = jnp.float32


def rmsnorm(x, g):
    xf = x.astype(F32)
    y = xf * lax.rsqrt(jnp.mean(xf * xf, axis=-1, keepdims=True) + NORM_EPS)
    return (y * g.astype(F32)).astype(x.dtype)


def swiglu(h, wg, wu, wd):
    return (jax.nn.silu(h @ wg) * (h @ wu)) @ wd


def pool_mixer(u, w, scale):
    B, L, _ = u.shape
    uf = u.astype(F32)
    cs = jnp.concatenate([jnp.zeros((B, 1, POOL_WIDTH), F32), jnp.cumsum(uf, axis=1)], axis=1)
    t = jnp.arange(L)
    outs = []
    for gi, win in enumerate(POOL_WINDOWS):
        lo = jnp.clip(t - win // 2, 0, L)
        hi = jnp.clip(t + win // 2, 0, L)
        csg = cs[..., gi * POOL_GROUP:(gi + 1) * POOL_GROUP]
        mean = (csg[:, hi] - csg[:, lo]) / (hi - lo).astype(F32)[None, :, None]
        outs.append(mean - uf[..., gi * POOL_GROUP:(gi + 1) * POOL_GROUP])
    d = jnp.stack(outs, axis=2)
    y = jnp.einsum('blgc,gcd->blgd', d, w.astype(F32)).reshape(B, L, POOL_WIDTH)
    return (y * scale.astype(F32)).astype(u.dtype)


def axial_rope(x, pos_row, pos_col):
    half = HEAD_DIM // 2
    quarter = half // 2
    inv = ROPE_THETA ** (-jnp.arange(quarter, dtype=F32) / quarter)

    def rot(xa, pos):
        ang = pos.astype(F32)[:, None] * inv[None, :]
        cos = jnp.cos(ang)[None, :, None, :]
        sin = jnp.sin(ang)[None, :, None, :]
        x1, x2 = xa[..., :quarter], xa[..., quarter:]
        return jnp.concatenate([x1 * cos - x2 * sin, x2 * cos + x1 * sin], axis=-1)

    xf = x.astype(F32)
    return jnp.concatenate([rot(xf[..., :half], pos_row), rot(xf[..., half:], pos_col)], axis=-1).astype(x.dtype)


def context_attention(q, k, v, sink):
    B, Lc = q.shape[:2]
    nb = Lc // BLOCK
    qb = q.reshape(B, nb, BLOCK, N_KV_HEADS, GQA_GROUP, HEAD_DIM).transpose(1, 0, 2, 3, 4, 5)
    sink_b = jnp.broadcast_to(sink.astype(F32).reshape(1, N_KV_HEADS, GQA_GROUP, 1, 1),
                              (B, N_KV_HEADS, GQA_GROUP, BLOCK, 1))
    scale = HEAD_DIM ** -0.5

    def one_block(qi):
        s = jnp.einsum('bqkgd,bckd->bkgqc', qi, k, preferred_element_type=F32) * scale
        p = jax.nn.softmax(jnp.concatenate([s, sink_b], axis=-1), axis=-1)[..., :Lc]
        return jnp.einsum('bkgqc,bckd->bqkgd', p.astype(v.dtype), v, preferred_element_type=F32)

    out = lax.map(one_block, qb)
    return out.transpose(1, 0, 2, 3, 4, 5).reshape(B, Lc, ATTN_WIDTH).astype(q.dtype)


def latent_attention(q, k, v, k_ctx, v_ctx, sink):
    B, L = q.shape[:2]
    Lc = k_ctx.shape[1]
    nb = L // BLOCK
    qb = q.reshape(B, nb, BLOCK, N_KV_HEADS, GQA_GROUP, HEAD_DIM).transpose(1, 0, 2, 3, 4, 5)

    def windows(a):
        ap = jnp.pad(a, ((0, 0), (BLOCK, BLOCK), (0, 0), (0, 0))).reshape(B, nb + 2, BLOCK, N_KV_HEADS, HEAD_DIM)
        w = jnp.concatenate([ap[:, :-2], ap[:, 1:-1], ap[:, 2:]], axis=2)
        return w.transpose(1, 0, 2, 3, 4)

    kw, vw = windows(k), windows(v)
    nidx = jnp.arange(nb)[:, None, None]
    qpos = nidx * BLOCK + jnp.arange(BLOCK)[None, :, None]
    kpos = (nidx - 1) * BLOCK + jnp.arange(3 * BLOCK)[None, None, :]
    valid = (jnp.abs(kpos - qpos) <= WINDOW) & (kpos >= 0) & (kpos < L)
    sink_b = jnp.broadcast_to(sink.astype(F32).reshape(1, N_KV_HEADS, GQA_GROUP, 1, 1),
                              (B, N_KV_HEADS, GQA_GROUP, BLOCK, 1))
    scale = HEAD_DIM ** -0.5

    def one_block(args):
        qi, ki, vi, mi = args
        s_loc = jnp.einsum('bqkgd,bjkd->bkgqj', qi, ki, preferred_element_type=F32) * scale
        s_loc = jnp.where(mi, s_loc, NEG_INF)
        s_ctx = jnp.einsum('bqkgd,bckd->bkgqc', qi, k_ctx, preferred_element_type=F32) * scale
        p = jax.nn.softmax(jnp.concatenate([s_loc, s_ctx, sink_b], axis=-1), axis=-1).astype(v.dtype)
        o_loc = jnp.einsum('bkgqj,bjkd->bqkgd', p[..., :3 * BLOCK], vi, preferred_element_type=F32)
        o_ctx = jnp.einsum('bkgqc,bckd->bqkgd', p[..., 3 * BLOCK:3 * BLOCK + Lc], v_ctx, preferred_element_type=F32)
        return o_loc + o_ctx

    out = lax.map(one_block, (qb, kw, vw, valid))
    return out.transpose(1, 0, 2, 3, 4, 5).reshape(B, L, ATTN_WIDTH).astype(q.dtype)


def hyena_filters(L, w1, b1, w2, b2, w3, b3, freq, deltas):
    t = jnp.linspace(0.0, 1.0, L, dtype=F32)[:, None]
    bands = (HY_EMB_DIM - 1) // 2
    f = jnp.linspace(1e-4, bands - 1, bands, dtype=F32)[None, :]
    w = 2.0 * math.pi * jnp.arange(L, dtype=F32)[:, None] / L
    z = jnp.concatenate([t, jnp.cos(f * w), -jnp.sin(f * w)], axis=-1)
    fr = freq.astype(F32)
    h = jnp.sin(fr[0] * (z @ w1.astype(F32) + b1.astype(F32)))
    h = jnp.sin(fr[1] * (h @ w2.astype(F32) + b2.astype(F32)))
    h = (h @ w3.astype(F32) + b3.astype(F32)).reshape(L, 2, HY_ORDER, HY_WIDTH)
    decay = jnp.exp(-t[:, :, None, None] * jnp.abs(deltas.astype(F32))[None, None])
    h = h * (decay + HY_MOD_SHIFT)
    fwd, bwd = h[:, 0], h[:, 1]
    k = jnp.concatenate([fwd, jnp.zeros((1, HY_ORDER, HY_WIDTH), F32), jnp.flip(bwd[1:], axis=0)], axis=0)
    return k / (jnp.sum(jnp.abs(k), axis=0, keepdims=True) + 1e-6)


def short_conv(u, w, b):
    up = jnp.pad(u, ((0, 0), (1, 1), (0, 0)))
    return up[:, :-2] * w[0] + up[:, 1:-1] * w[1] + up[:, 2:] * w[2] + b


def fft_conv(z, k):
    L = z.shape[1]
    zf = jnp.fft.rfft(z, n=2 * L, axis=1)
    kf = jnp.fft.rfft(k, n=2 * L, axis=0)
    return jnp.fft.irfft(zf * kf[None], n=2 * L, axis=1)[:, :L]


def hyena_mixer(u, sw, sb, filt, bias_d):
    uc = short_conv(u, sw, sb).astype(F32)
    v, x1, x2 = jnp.split(uc, 3, axis=-1)
    bd = bias_d.astype(F32)
    z = v
    for o, gate in enumerate((x1, x2)):
        z = gate * (fft_conv(z, filt[:, o]) + bd[o] * z)
    return z.astype(u.dtype)


def trunk_layer(x, cond, p, pos, ctx_kv):
    B, L, _ = x.shape
    mod = (jax.nn.silu(cond.astype(F32)) @ p['ada_w'].astype(F32) + p['ada_b'].astype(F32)).astype(x.dtype)
    sh1, sc1, g1, sh2, sc2, g2, sh3, sc3, g3 = jnp.split(mod[:, None, :], N_MOD, axis=-1)

    h = rmsnorm(x, p['norm'][0]) * (1 + sc1) + sh1
    x = x + 0.5 * g1 * swiglu(h, p['wg'][0], p['wu'][0], p['wd'][0])

    h = rmsnorm(x, p['norm'][1]) * (1 + sc2) + sh2
    u = h @ p['w_in']
    s1 = POOL_WIDTH
    s2 = s1 + ATTN_WIDTH
    s3 = s2 + KV_WIDTH
    s4 = s3 + KV_WIDTH
    u_pool, q, k, v, u_hy = jnp.split(u, [s1, s2, s3, s4], axis=-1)
    q = rmsnorm(q.reshape(B, L, N_HEADS, HEAD_DIM), p['q_norm'])
    k = rmsnorm(k.reshape(B, L, N_KV_HEADS, HEAD_DIM), p['k_norm'])
    v = v.reshape(B, L, N_KV_HEADS, HEAD_DIM)
    if ctx_kv is None:
        a = context_attention(q, k, v, p['sink'])
        new_kv = (k, v)
    else:
        q = axial_rope(q, pos[0], pos[1])
        k = axial_rope(k, pos[0], pos[1])
        a = latent_attention(q, k, v, ctx_kv[0], ctx_kv[1], p['sink'])
        new_kv = None
    y_pool = pool_mixer(u_pool, p['pool_w'], p['pool_scale'])
    filt = hyena_filters(L, p['f_w1'], p['f_b1'], p['f_w2'], p['f_b2'], p['f_w3'], p['f_b3'], p['f_freq'], p['decay'])
    y_hy = hyena_mixer(u_hy, p['short_w'], p['short_b'], filt, p['hy_bias'])
    mix = jnp.concatenate([y_pool, a, y_hy], axis=-1) @ p['w_out']
    x = x + g2 * mix

    h = rmsnorm(x, p['norm'][2]) * (1 + sc3) + sh3
    x = x + 0.5 * g3 * swiglu(h, p['wg'][1], p['wu'][1], p['wd'][1])
    return x, new_kv


def setup_inputs(seed: int = 0) -> dict:
    key = jax.random.key(seed)
    ks = jax.random.split(key, 40)

    def nrm(k, shape, s):
        return jax.random.normal(k, shape, F32) * s

    base_decay = jnp.linspace(abs(math.log(1e-2) / 1.5), abs(math.log(1e-2) / 0.3), HY_WIDTH, dtype=F32)
    return {
        "x_prompt": nrm(ks[0], (BATCH, SEQ, D_MODEL), 1.0),
        "x_sample": nrm(ks[1], (DEC_BATCH, DEC_SEQ, D_MODEL), 1.0),
        "cache_k": nrm(ks[2], (DEC_BATCH, DEPTH, PAST_LEN, N_KV_HEADS, HEAD_DIM), 1.0),
        "cache_v": nrm(ks[3], (DEC_BATCH, DEPTH, PAST_LEN, N_KV_HEADS, HEAD_DIM), 1.0),
        "c": nrm(ks[4], (DEC_BATCH, D_MODEL), 1.0),
        "c_ctx": nrm(ks[5], (D_MODEL,), 1.0),
        "ada_w": nrm(ks[6], (DEPTH, D_MODEL, N_MOD * D_MODEL), 0.5 * D_MODEL ** -0.5),
        "ada_b": nrm(ks[7], (DEPTH, N_MOD * D_MODEL), 0.02),
        "norm_w": 1.0 + nrm(ks[8], (DEPTH, 3, D_MODEL), 0.05),
        "ffn_wg": nrm(ks[9], (DEPTH, 2, D_MODEL, D_FF), D_MODEL ** -0.5),
        "ffn_wu": nrm(ks[10], (DEPTH, 2, D_MODEL, D_FF), D_MODEL ** -0.5),
        "ffn_wd": nrm(ks[11], (DEPTH, 2, D_FF, D_MODEL), D_FF ** -0.5),
        "w_in": nrm(ks[12], (DEPTH, D_MODEL, IN_WIDTH), D_MODEL ** -0.5),
        "w_out": nrm(ks[13], (DEPTH, D_MODEL, D_MODEL), D_MODEL ** -0.5),
        "pool_w": nrm(ks[14], (DEPTH, len(POOL_WINDOWS), POOL_GROUP, POOL_GROUP), POOL_GROUP ** -0.5),
        "pool_scale": 1.0 + nrm(ks[15], (DEPTH, POOL_WIDTH), 0.1),
        "q_norm": 1.0 + nrm(ks[16], (DEPTH, HEAD_DIM), 0.05),
        "k_norm": 1.0 + nrm(ks[17], (DEPTH, HEAD_DIM), 0.05),
        "attn_sink": nrm(ks[18], (DEPTH, N_HEADS), 0.5),
        "hy_short_w": nrm(ks[19], (DEPTH, 3, (HY_ORDER + 1) * HY_WIDTH), 3 ** -0.5),
        "hy_short_b": nrm(ks[20], (DEPTH, (HY_ORDER + 1) * HY_WIDTH), 0.02),
        "hy_f_w1": nrm(ks[21], (DEPTH, HY_EMB_DIM, HY_FILTER_HIDDEN), HY_EMB_DIM ** -0.5),
        "hy_f_b1": nrm(ks[22], (DEPTH, HY_FILTER_HIDDEN), 0.1),
        "hy_f_w2": nrm(ks[23], (DEPTH, HY_FILTER_HIDDEN, HY_FILTER_HIDDEN), HY_FILTER_HIDDEN ** -0.5),
        "hy_f_b2": nrm(ks[24], (DEPTH, HY_FILTER_HIDDEN), 0.1),
        "hy_f_w3": nrm(ks[25], (DEPTH, HY_FILTER_HIDDEN, 2 * HY_ORDER * HY_WIDTH), HY_FILTER_HIDDEN ** -0.5),
        "hy_f_b3": nrm(ks[26], (DEPTH, 2 * HY_ORDER * HY_WIDTH), 0.02),
        "hy_sin_freq": 1.0 + nrm(ks[27], (DEPTH, 2, HY_FILTER_HIDDEN), 0.1),
        "hy_decay": base_decay[None, None, :] + nrm(ks[28], (DEPTH, HY_ORDER, HY_WIDTH), 0.1),
        "hy_bias": nrm(ks[29], (DEPTH, HY_ORDER, HY_WIDTH), 1.0),
    }


def reference(x_prompt, x_sample, cache_k, cache_v, c, c_ctx, ada_w, ada_b, norm_w,
              ffn_wg, ffn_wu, ffn_wd, w_in, w_out, pool_w, pool_scale, q_norm, k_norm,
              attn_sink, hy_short_w, hy_short_b, hy_f_w1, hy_f_b1, hy_f_w2, hy_f_b2,
              hy_f_w3, hy_f_b3, hy_sin_freq, hy_decay, hy_bias):
    L_lat = x_sample.shape[1]
    n_rows = L_lat // GRID_W
    t = jnp.arange(n_rows * GRID_W)
    pos = (t // GRID_W, t % GRID_W)
    cond_ctx = c_ctx[None, :]
    yp, ys = x_prompt, x_sample
    ks, vs = [], []
    for l in range(DEPTH):
        p = {
            'ada_w': ada_w[l], 'ada_b': ada_b[l], 'norm': norm_w[l],
            'wg': ffn_wg[l], 'wu': ffn_wu[l], 'wd': ffn_wd[l],
            'w_in': w_in[l], 'w_out': w_out[l],
            'pool_w': pool_w[l], 'pool_scale': pool_scale[l],
            'q_norm': q_norm[l], 'k_norm': k_norm[l], 'sink': attn_sink[l],
            'short_w': hy_short_w[l], 'short_b': hy_short_b[l],
            'f_w1': hy_f_w1[l], 'f_b1': hy_f_b1[l], 'f_w2': hy_f_w2[l], 'f_b2': hy_f_b2[l],
            'f_w3': hy_f_w3[l], 'f_b3': hy_f_b3[l], 'f_freq': hy_sin_freq[l],
            'decay': hy_decay[l], 'hy_bias': hy_bias[l],
        }
        yp, kv = trunk_layer(yp, cond_ctx, p, None, None)
        ks.append(kv[0])
        vs.append(kv[1])
        ys, _ = trunk_layer(ys, c, p, pos, (cache_k[:, l], cache_v[:, l]))
    new_cache_k = jnp.stack(ks, axis=1)
    new_cache_v = jnp.stack(vs, axis=1)
    return (yp, ys, new_cache_k, new_cache_v)
```

```python
import functools
import math

import jax
import jax.numpy as jnp
from jax import lax
from jax.experimental import pallas as pl
from jax.experimental.pallas import tpu as pltpu

F32 = jnp.float32
BF16 = jnp.bfloat16

D_MODEL = 1024
DEPTH = 2
GRID_W = 64
POOL_WINDOWS = (2, 4, 8, 16)
POOL_WIDTH = 256
POOL_GROUP = 64
HEAD_DIM = 64
N_HEADS = 8
N_KV_HEADS = 2
GQA_GROUP = 4
ATTN_WIDTH = 512
KV_WIDTH = 128
WINDOW = 128
BLOCK = 128
ROPE_THETA = 10000.0
HY_WIDTH = 256
HY_ORDER = 2
HY_EMB_DIM = 33
HY_FILTER_HIDDEN = 64
HY_MOD_SHIFT = 0.05
D_FF = 2816
IN_WIDTH = 1792
N_MOD = 9
NORM_EPS = 1e-6
NEG_INF = -1e30

LANES = 128
SUBLANES = 8
VMEM_LIMIT = 56 * 1024 * 1024

TOKEN_TILE = 512
FF_CHUNK = 1408
SEQ_CHUNK = 512
FREQ_TILE = 128
MAC_ROWS = 32
COND_ROWS = 8


def _cparams(sem):
    return pltpu.CompilerParams(dimension_semantics=sem, vmem_limit_bytes=VMEM_LIMIT)


def _dot(a, b):
    return jnp.dot(a, b, preferred_element_type=F32)


def _ada_kernel(c_ref, w_ref, b_ref, o_ref):
    c = c_ref[...]
    s = (c * jax.nn.sigmoid(c)).astype(BF16)
    o_ref[...] = _dot(s, w_ref[...].astype(BF16)) + b_ref[...]


def _ada_mod(cond, ada_w, ada_b):
    tn = 3072
    nw = N_MOD * D_MODEL
    return pl.pallas_call(
        _ada_kernel,
        grid=(DEPTH, nw // tn),
        in_specs=[
            pl.BlockSpec((COND_ROWS, D_MODEL), lambda l, j: (0, 0)),
            pl.BlockSpec((None, D_MODEL, tn), lambda l, j: (l, 0, j)),
            pl.BlockSpec((None, 1, tn), lambda l, j: (l, 0, j)),
        ],
        out_specs=pl.BlockSpec((None, COND_ROWS, tn), lambda l, j: (l, 0, j)),
        out_shape=jax.ShapeDtypeStruct((DEPTH, COND_ROWS, nw), F32),
        compiler_params=_cparams(("arbitrary", "arbitrary")),
        name="ada_mod",
    )(cond, ada_w, ada_b.reshape(DEPTH, 1, nw))


def _mod_slice(mod_ref, k):
    return mod_ref[:, k * D_MODEL:(k + 1) * D_MODEL]


def _rms_mod(x, gain, scale, shift):
    y = x * lax.rsqrt(jnp.mean(x * x, axis=-1, keepdims=True) + NORM_EPS)
    return (y * gain) * (1.0 + scale) + shift


def _swiglu(hb, wg_ref, wu_ref, wd_ref):
    acc = None
    for c in range(D_FF // FF_CHUNK):
        sl = slice(c * FF_CHUNK, (c + 1) * FF_CHUNK)
        g = _dot(hb, wg_ref[:, sl])
        u = _dot(hb, wu_ref[:, sl])
        a = ((g * jax.nn.sigmoid(g)) * u).astype(BF16)
        y = _dot(a, wd_ref[sl, :])
        acc = y if acc is None else acc + y
    return acc


def _head_norm(x, gmat, gain):
    sq = x * x
    hi = sq.astype(BF16)
    lo = (sq - hi.astype(F32)).astype(BF16)
    ss = _dot(hi, gmat) + _dot(lo, gmat)
    return (x * lax.rsqrt(ss * (1.0 / HEAD_DIM) + NORM_EPS)) * gain


def _rope(x, cos, sa, sb):
    w = x.shape[1]
    xn = pltpu.roll(x, w - 16, axis=1)
    xp = pltpu.roll(x, 16, axis=1)
    return x * cos + xn * sa + xp * sb


def _token_a_kernel(*refs, rope):
    if rope:
        (x_ref, mod_ref, nrm_ref, wg_ref, wu_ref, wd_ref, win_ref, gq_ref, gk_ref, qg_ref, kg_ref,
         cos_ref, sa_ref, sb_ref, x1_ref, up_ref, q_ref, k_ref, v_ref, uh_ref) = refs
    else:
        (x_ref, mod_ref, nrm_ref, wg_ref, wu_ref, wd_ref, win_ref, gq_ref, gk_ref, qg_ref, kg_ref,
         x1_ref, up_ref, q_ref, k_ref, v_ref, uh_ref) = refs
    x = x_ref[...]
    sh1, sc1, g1 = _mod_slice(mod_ref, 0), _mod_slice(mod_ref, 1), _mod_slice(mod_ref, 2)
    sh2, sc2 = _mod_slice(mod_ref, 3), _mod_slice(mod_ref, 4)
    h = _rms_mod(x, nrm_ref[0:1, :], sc1, sh1).astype(BF16)
    x1 = x + (0.5 * g1) * _swiglu(h, wg_ref, wu_ref, wd_ref)
    x1_ref[...] = x1
    h2 = _rms_mod(x1, nrm_ref[1:2, :], sc2, sh2).astype(BF16)
    u = _dot(h2, win_ref[...])
    s1 = POOL_WIDTH
    s2 = s1 + ATTN_WIDTH
    s3 = s2 + KV_WIDTH
    s4 = s3 + KV_WIDTH
    up_ref[...] = u[:, :s1]
    q = _head_norm(u[:, s1:s2], gq_ref[...], qg_ref[...])
    k = _head_norm(u[:, s2:s3], gk_ref[...], kg_ref[...])
    if rope:
        cos, sa, sb = cos_ref[...], sa_ref[...], sb_ref[...]
        reps = ATTN_WIDTH // LANES
        q = _rope(q, jnp.concatenate([cos] * reps, axis=1), jnp.concatenate([sa] * reps, axis=1),
                  jnp.concatenate([sb] * reps, axis=1))
        k = _rope(k, cos, sa, sb)
    q_ref[...] = (q * (HEAD_DIM ** -0.5)).astype(BF16)
    k_ref[...] = k
    v_ref[...] = u[:, s3:s4]
    uh_ref[...] = u[:, s4:]


def _token_a(x, mod_l, nrm, wg, wu, wd, w_in, gq, gk, qg, kg, rope_tabs, seq_len):
    n = x.shape[0]
    tm = TOKEN_TILE
    tiles_per_seq = max(seq_len // tm, 1)
    rope = rope_tabs is not None
    if rope:
        mod_row = lambda i: (1 + i // tiles_per_seq, 0, 0)
    else:
        mod_row = lambda i: (0, 0, 0)
    const = lambda i: (0, 0)
    row = lambda i: (i, 0)
    in_specs = [
        pl.BlockSpec((tm, D_MODEL), row),
        pl.BlockSpec((None, 1, N_MOD * D_MODEL), mod_row),
        pl.BlockSpec((3, D_MODEL), const),
        pl.BlockSpec((D_MODEL, D_FF), const),
        pl.BlockSpec((D_MODEL, D_FF), const),
        pl.BlockSpec((D_FF, D_MODEL), const),
        pl.BlockSpec((D_MODEL, IN_WIDTH), const),
        pl.BlockSpec((ATTN_WIDTH, ATTN_WIDTH), const),
        pl.BlockSpec((KV_WIDTH, KV_WIDTH), const),
        pl.BlockSpec((1, ATTN_WIDTH), const),
        pl.BlockSpec((1, KV_WIDTH), const),
    ]
    args = [x, mod_l, nrm, wg, wu, wd, w_in, gq, gk, qg, kg]
    if rope:
        tab = pl.BlockSpec((tm, LANES), lambda i: (i % tiles_per_seq, 0))
        in_specs += [tab, tab, tab]
        args += list(rope_tabs)
    widths = (D_MODEL, POOL_WIDTH, ATTN_WIDTH, KV_WIDTH, KV_WIDTH, 3 * HY_WIDTH)
    dtypes = (F32, F32, BF16, F32, F32, F32)
    return pl.pallas_call(
        functools.partial(_token_a_kernel, rope=rope),
        grid=(n // tm,),
        in_specs=in_specs,
        out_specs=[pl.BlockSpec((tm, w), row) for w in widths],
        out_shape=[jax.ShapeDtypeStruct((n, w), d) for w, d in zip(widths, dtypes)],
        compiler_params=_cparams(("arbitrary",)),
        name="token_a_rope" if rope else "token_a",
    )(*args)


def _token_b_kernel(x_ref, yp_ref, a_ref, yh_ref, mod_ref, nrm_ref, wo_ref, wg_ref, wu_ref, wd_ref, o_ref):
    x1 = x_ref[...]
    g2 = _mod_slice(mod_ref, 5)
    sh3, sc3, g3 = _mod_slice(mod_ref, 6), _mod_slice(mod_ref, 7), _mod_slice(mod_ref, 8)
    cat = jnp.concatenate([yp_ref[...], a_ref[...], yh_ref[...]], axis=1)
    x2 = x1 + g2 * _dot(cat, wo_ref[...])
    h3 = _rms_mod(x2, nrm_ref[2:3, :], sc3, sh3).astype(BF16)
    o_ref[...] = x2 + (0.5 * g3) * _swiglu(h3, wg_ref, wu_ref, wd_ref)


def _token_b(x1, y_pool, a, y_hy, mod_l, nrm, w_out, wg, wu, wd, seq_len, per_seq_cond):
    n = x1.shape[0]
    tm = TOKEN_TILE
    tiles_per_seq = max(seq_len // tm, 1)
    if per_seq_cond:
        mod_row = lambda i: (1 + i // tiles_per_seq, 0, 0)
    else:
        mod_row = lambda i: (0, 0, 0)
    const = lambda i: (0, 0)
    row = lambda i: (i, 0)
    return pl.pallas_call(
        _token_b_kernel,
        grid=(n // tm,),
        in_specs=[
            pl.BlockSpec((tm, D_MODEL), row),
            pl.BlockSpec((tm, POOL_WIDTH), row),
            pl.BlockSpec((tm, ATTN_WIDTH), row),
            pl.BlockSpec((tm, HY_WIDTH), row),
            pl.BlockSpec((None, 1, N_MOD * D_MODEL), mod_row),
            pl.BlockSpec((3, D_MODEL), const),
            pl.BlockSpec((D_MODEL, D_MODEL), const),
            pl.BlockSpec((D_MODEL, D_FF), const),
            pl.BlockSpec((D_MODEL, D_FF), const),
            pl.BlockSpec((D_FF, D_MODEL), const),
        ],
        out_specs=pl.BlockSpec((tm, D_MODEL), row),
        out_shape=jax.ShapeDtypeStruct((n, D_MODEL), F32),
        compiler_params=_cparams(("arbitrary",)),
        name="token_b",
    )(x1, y_pool, a, y_hy, mod_l, nrm, w_out, wg, wu, wd)


def _halo_rows(src_ref, r0, rows, seq_len):
    c = src_ref.shape[1]
    zero = jnp.zeros((SUBLANES, c), F32)
    prev = src_ref[r0 - SUBLANES:r0, :] if r0 > 0 else zero
    nxt = src_ref[r0 + rows:r0 + rows + SUBLANES, :] if r0 + rows < seq_len else zero
    return prev, nxt


def _short_conv_chunk(src_ref, r0, rows, seq_len, w, b):
    x = src_ref[r0:r0 + rows, :]
    prev, nxt = _halo_rows(src_ref, r0, rows, seq_len)
    ridx = lax.broadcasted_iota(jnp.int32, x.shape, 0)
    xp = jnp.where(ridx == 0, prev[SUBLANES - 1:SUBLANES, :], pltpu.roll(x, 1, axis=0))
    xn = jnp.where(ridx == rows - 1, nxt[0:1, :], pltpu.roll(x, rows - 1, axis=0))
    return xp * w[0:1, :] + x * w[1:2, :] + xn * w[2:3, :] + b


def _pool_kernel(u_ref, w_ref, scale_ref, o_ref, *, seq_len, rows):
    lane = lax.broadcasted_iota(jnp.int32, (rows, POOL_WIDTH), 1)
    grp = lane // POOL_GROUP
    half = jnp.where(grp == 0, 1, jnp.where(grp == 1, 2, jnp.where(grp == 2, 4, 8)))
    ext = rows + 2 * SUBLANES
    for r0 in range(0, seq_len, rows):
        x = u_ref[r0:r0 + rows, :]
        prev, nxt = _halo_rows(u_ref, r0, rows, seq_len)
        a = jnp.concatenate([prev, x, nxt], axis=0)
        back = lambda v, s: pltpu.roll(v, s, axis=0)
        fwd = lambda v, s: pltpu.roll(v, ext - s, axis=0)
        b1 = back(a, 1)
        b2 = b1 + back(b1, 1)
        b4 = b2 + back(b2, 2)
        b8 = b4 + back(b4, 4)
        f2 = a + fwd(a, 1)
        f4 = f2 + fwd(f2, 2)
        f8 = f4 + fwd(f4, 4)
        core = lambda v: v[SUBLANES:SUBLANES + rows, :]
        wsum = jnp.where(grp == 0, core(b1) + x,
                         jnp.where(grp == 1, core(b2) + core(f2),
                                   jnp.where(grp == 2, core(b4) + core(f4), core(b8) + core(f8))))
        t = r0 + lax.broadcasted_iota(jnp.int32, (rows, POOL_WIDTH), 0)
        cnt = jnp.minimum(t + half, seq_len) - jnp.maximum(t - half, 0)
        d = wsum / cnt.astype(F32) - x
        y = _dot(d.astype(BF16), w_ref[...]) * scale_ref[...]
        o_ref[r0:r0 + rows, :] = y.astype(o_ref.dtype)


def _pool_mix(u_pool, w_bd, scale, batch, seq_len):
    rows = min(SEQ_CHUNK // 2, seq_len)
    return pl.pallas_call(
        functools.partial(_pool_kernel, seq_len=seq_len, rows=rows),
        grid=(batch,),
        in_specs=[
            pl.BlockSpec((seq_len, POOL_WIDTH), lambda b: (b, 0)),
            pl.BlockSpec((POOL_WIDTH, POOL_WIDTH), lambda b: (0, 0)),
            pl.BlockSpec((1, POOL_WIDTH), lambda b: (0, 0)),
        ],
        out_specs=pl.BlockSpec((seq_len, POOL_WIDTH), lambda b: (b, 0)),
        out_shape=jax.ShapeDtypeStruct((batch * seq_len, POOL_WIDTH), BF16),
        compiler_params=_cparams(("arbitrary",)),
        name="pool_mix",
    )(u_pool, w_bd, scale)


def _attn_kernel(*refs, has_local, seq_len):
    if has_local:
        sink_ref, q_ref, km_ref, k0_ref, kp_ref, vm_ref, v0_ref, vp_ref, kc_ref, vc_ref, o_ref = refs
    else:
        sink_ref, q_ref, kc_ref, vc_ref, o_ref = refs
    i = pl.program_id(1)
    q = q_ref[...]
    lane_q = lax.broadcasted_iota(jnp.int32, (BLOCK, LANES), 1)
    if has_local:
        k_all = jnp.concatenate([km_ref[...], k0_ref[...], kp_ref[...], kc_ref[...]], axis=0)
        v_all = jnp.concatenate([vm_ref[...], v0_ref[...], vp_ref[...], vc_ref[...]], axis=0)
        n_loc = 3 * BLOCK
        r = lax.broadcasted_iota(jnp.int32, (BLOCK, n_loc), 0)
        j = lax.broadcasted_iota(jnp.int32, (BLOCK, n_loc), 1)
        kpos = (i - 1) * BLOCK + j
        valid = (jnp.abs(j - BLOCK - r) <= WINDOW) & (kpos >= 0) & (kpos < seq_len)
    else:
        k_all = kc_ref[...]
        v_all = vc_ref[...]
    lane_k = lax.broadcasted_iota(jnp.int32, k_all.shape, 1)
    k_sw = pltpu.roll(k_all, HEAD_DIM, axis=1)
    v_sw = pltpu.roll(v_all, HEAD_DIM, axis=1)
    outs = []
    for kvh in range(N_KV_HEADS):
        first = (lane_k < HEAD_DIM) == (kvh == 0)
        kd = jnp.where(first, k_all, k_sw).astype(BF16)
        vd = jnp.where(first, v_all, v_sw).astype(BF16)
        for g in range(GQA_GROUP):
            hd = kvh * GQA_GROUP + g
            qp = q[:, (hd // 2) * LANES:(hd // 2 + 1) * LANES]
            keep = (lane_q < HEAD_DIM) == (hd % 2 == 0)
            qm = jnp.where(keep, qp, jnp.zeros_like(qp))
            s = lax.dot_general(qm, kd, (((1,), (1,)), ((), ())), preferred_element_type=F32)
            if has_local:
                s = jnp.concatenate([jnp.where(valid, s[:, :n_loc], NEG_INF), s[:, n_loc:]], axis=1)
            sk = sink_ref[0, hd]
            m = jnp.maximum(jnp.max(s, axis=1, keepdims=True), sk)
            e = jnp.exp(s - m)
            den = jnp.sum(e, axis=1, keepdims=True) + jnp.exp(sk - m)
            o = _dot(e.astype(BF16), vd)
            outs.append(o / den)
    blks = [jnp.where(lane_q < HEAD_DIM, outs[2 * p], outs[2 * p + 1]) for p in range(N_HEADS // 2)]
    o_ref[...] = jnp.concatenate(blks, axis=1).astype(o_ref.dtype)


def _attention(q, k, v, sink, batch, seq_len, ctx_k=None, ctx_v=None, layer=0):
    nb = seq_len // BLOCK
    has_local = ctx_k is not None
    qspec = pl.BlockSpec((BLOCK, ATTN_WIDTH), lambda b, i: (b * nb + i, 0))
    sspec = pl.BlockSpec(memory_space=pltpu.SMEM)
    if has_local:
        lc = ctx_k.shape[2]
        blk = lambda f: pl.BlockSpec((BLOCK, KV_WIDTH), f)
        prev = lambda b, i: (b * nb + jnp.maximum(i - 1, 0), 0)
        cur = lambda b, i: (b * nb + i, 0)
        nxt = lambda b, i: (b * nb + jnp.minimum(i + 1, nb - 1), 0)
        cspec = pl.BlockSpec((None, None, lc, KV_WIDTH), lambda b, i: (b, layer, 0, 0))
        in_specs = [sspec, qspec, blk(prev), blk(cur), blk(nxt), blk(prev), blk(cur), blk(nxt), cspec, cspec]
        args = [sink, q, k, k, k, v, v, v, ctx_k, ctx_v]
    else:
        kv = pl.BlockSpec((seq_len, KV_WIDTH), lambda b, i: (b, 0))
        in_specs = [sspec, qspec, kv, kv]
        args = [sink, q, k, v]
    return pl.pallas_call(
        functools.partial(_attn_kernel, has_local=has_local, seq_len=seq_len),
        grid=(batch, nb),
        in_specs=in_specs,
        out_specs=pl.BlockSpec((BLOCK, ATTN_WIDTH), lambda b, i: (b * nb + i, 0)),
        out_shape=jax.ShapeDtypeStruct((batch * seq_len, ATTN_WIDTH), BF16),
        compiler_params=_cparams(("arbitrary", "arbitrary")),
        name="attn_latent" if has_local else "attn_context",
    )(*args)


def _filter_kernel(z_ref, w1_ref, b1_ref, w2_ref, b2_ref, w3_ref, b3_ref, fr_ref, dl_ref, k_ref, sum_ref,
                   *, seq_len, rows):
    i = pl.program_id(0)
    z = z_ref[...]
    h = jnp.sin(fr_ref[0:1, :] * (_dot(z.astype(BF16), w1_ref[...].astype(BF16)) + b1_ref[...]))
    h = jnp.sin(fr_ref[1:2, :] * (_dot(h.astype(BF16), w2_ref[...].astype(BF16)) + b2_ref[...]))
    h3 = _dot(h.astype(BF16), w3_ref[...].astype(BF16)) + b3_ref[...]
    oc = HY_ORDER * HY_WIDTH
    row = i * rows + lax.broadcasted_iota(jnp.int32, (rows, oc), 0)
    t = z[:, 0:1]
    decay = jnp.exp(-t * jnp.abs(dl_ref[...]))
    sel = jnp.where(row < seq_len, h3[:, :oc], jnp.where(row > seq_len, h3[:, oc:], 0.0))
    k = sel * (decay + HY_MOD_SHIFT)
    k_ref[...] = k

    @pl.when(i == 0)
    def _():
        sum_ref[...] = jnp.zeros_like(sum_ref)

    sum_ref[...] += jnp.sum(jnp.abs(k), axis=0, keepdims=True)


def _hyena_filter(z_ext, w1p, b1, w2, b2, w3, b3, freq, deltas, seq_len):
    n = 2 * seq_len
    rows = min(1024, n)
    oc = HY_ORDER * HY_WIDTH
    const = lambda i: (0, 0)
    return pl.pallas_call(
        functools.partial(_filter_kernel, seq_len=seq_len, rows=rows),
        grid=(n // rows,),
        in_specs=[
            pl.BlockSpec((rows, LANES), lambda i: (i, 0)),
            pl.BlockSpec((LANES, HY_FILTER_HIDDEN), const),
            pl.BlockSpec((1, HY_FILTER_HIDDEN), const),
            pl.BlockSpec((HY_FILTER_HIDDEN, HY_FILTER_HIDDEN), const),
            pl.BlockSpec((1, HY_FILTER_HIDDEN), const),
            pl.BlockSpec((HY_FILTER_HIDDEN, 2 * oc), const),
            pl.BlockSpec((1, 2 * oc), const),
            pl.BlockSpec((2, HY_FILTER_HIDDEN), const),
            pl.BlockSpec((1, oc), const),
        ],
        out_specs=[pl.BlockSpec((rows, oc), lambda i: (i, 0)), pl.BlockSpec((1, oc), const)],
        out_shape=[jax.ShapeDtypeStruct((n, oc), F32), jax.ShapeDtypeStruct((1, oc), F32)],
        compiler_params=_cparams(("arbitrary",)),
        name="hyena_filter",
    )(z_ext, w1p, b1, w2, b2, w3, b3, freq, deltas)


def _spectrum_kernel(ka_ref, kb_ref, sum_ref, fwd_ref, kf_ref, *, blk):
    den = sum_ref[...] + 1e-6
    a = ka_ref[...] / den
    b = kb_ref[...] / den
    g0 = b[0:1, :]
    ha = _dot(fwd_ref[...], a.astype(BF16))
    hb = _dot(fwd_ref[...], b.astype(BF16))
    tf = FREQ_TILE
    par = lax.broadcasted_iota(jnp.int32, (tf, 1), 0) % 2
    sgn = (1 - 2 * par).astype(F32)
    for c in range(blk // tf):
        re = slice(2 * c * tf, (2 * c + 1) * tf)
        im = slice((2 * c + 1) * tf, (2 * c + 2) * tf)
        kf_ref[re, :] = ha[re, :] - sgn * hb[im, :]
        kf_ref[im, :] = ha[im, :] + sgn * (hb[re, :] - g0)


def _filter_spectrum(kraw, ksum, fwd, seq_len, blk):
    nb = seq_len // blk
    nlags = 2 * nb - 1
    oc = HY_ORDER * HY_WIDTH
    m = 2 * blk
    return pl.pallas_call(
        functools.partial(_spectrum_kernel, blk=blk),
        grid=(nlags,),
        in_specs=[
            pl.BlockSpec((blk, oc), lambda d: ((d + nb + 1) % (2 * nb), 0)),
            pl.BlockSpec((blk, oc), lambda d: ((d + nb) % (2 * nb), 0)),
            pl.BlockSpec((1, oc), lambda d: (0, 0)),
            pl.BlockSpec((m, blk), lambda d: (0, 0)),
        ],
        out_specs=pl.BlockSpec((None, m, oc), lambda d: (d, 0, 0)),
        out_shape=jax.ShapeDtypeStruct((nlags, m, oc), F32),
        compiler_params=_cparams(("arbitrary",)),
        name="filter_spectrum",
    )(kraw, kraw, ksum, fwd)


def _conv_kernel(z_ref, g_ref, swz_ref, sbz_ref, swg_ref, sbg_ref, bd_ref, kf_ref, fwd_ref, inv_ref, o_ref,
                 zb_ref, acc_ref, zf_ref, yf_ref, *, seq_len, blk, conv_z):
    fc = pl.program_id(1)
    nfc = pl.num_programs(1)
    nb = seq_len // blk
    tf = FREQ_TILE
    rows = min(SEQ_CHUNK, seq_len)

    def z_chunk(r0):
        if conv_z:
            return _short_conv_chunk(z_ref, r0, rows, seq_len, swz_ref[...], sbz_ref[...])
        return z_ref[r0:r0 + rows, :]

    @pl.when(fc == 0)
    def _():
        for r0 in range(0, seq_len, rows):
            zb_ref[r0:r0 + rows, :] = z_chunk(r0).astype(BF16)
        acc_ref[...] = jnp.zeros_like(acc_ref)

    for bj in range(nb):
        zf_ref[bj] = _dot(fwd_ref[...], zb_ref[bj * blk:(bj + 1) * blk, :])

    for bi in range(nb):
        for r in range(0, tf, MAC_ROWS):
            re = slice(r, r + MAC_ROWS)
            im = slice(tf + r, tf + r + MAC_ROWS)
            yr = jnp.zeros((MAC_ROWS, HY_WIDTH), F32)
            yi = jnp.zeros((MAC_ROWS, HY_WIDTH), F32)
            for bj in range(nb):
                lag = bi - bj + nb - 1
                kr, ki = kf_ref[lag, re, :], kf_ref[lag, im, :]
                zr, zi = zf_ref[bj, re, :], zf_ref[bj, im, :]
                yr = yr + (kr * zr - ki * zi)
                yi = yi + (kr * zi + ki * zr)
            yf_ref[re, :] = yr.astype(BF16)
            yf_ref[im, :] = yi.astype(BF16)
        acc_ref[bi * blk:(bi + 1) * blk, :] += _dot(inv_ref[...], yf_ref[...])

    @pl.when(fc == nfc - 1)
    def _():
        for r0 in range(0, seq_len, rows):
            z = z_chunk(r0)
            gate = _short_conv_chunk(g_ref, r0, rows, seq_len, swg_ref[...], sbg_ref[...])
            y = gate * (acc_ref[r0:r0 + rows, :] + bd_ref[...] * z)
            o_ref[r0:r0 + rows, :] = y.astype(o_ref.dtype)


def _hyena_conv(z_src, z_col, conv_z, u_hy, g_col, short_w, short_b, bd, kf, order, fwd, inv,
                batch, seq_len, blk, out_dtype):
    nb = seq_len // blk
    nlags = 2 * nb - 1
    tf = FREQ_TILE
    nfc = blk // tf
    w = HY_WIDTH
    return pl.pallas_call(
        functools.partial(_conv_kernel, seq_len=seq_len, blk=blk, conv_z=conv_z),
        grid=(batch, nfc),
        in_specs=[
            pl.BlockSpec((seq_len, w), lambda b, f: (b, z_col)),
            pl.BlockSpec((seq_len, w), lambda b, f: (b, g_col)),
            pl.BlockSpec((3, w), lambda b, f: (0, z_col if conv_z else 0)),
            pl.BlockSpec((1, w), lambda b, f: (0, z_col if conv_z else 0)),
            pl.BlockSpec((3, w), lambda b, f: (0, g_col)),
            pl.BlockSpec((1, w), lambda b, f: (0, g_col)),
            pl.BlockSpec((1, w), lambda b, f: (0, 0)),
            pl.BlockSpec((nlags, 2 * tf, w), lambda b, f: (0, f, order)),
            pl.BlockSpec((2 * tf, blk), lambda b, f: (f, 0)),
            pl.BlockSpec((blk, 2 * tf), lambda b, f: (0, f)),
        ],
        out_specs=pl.BlockSpec((seq_len, w), lambda b, f: (b, 0)),
        out_shape=jax.ShapeDtypeStruct((batch * seq_len, w), out_dtype),
        scratch_shapes=[
            pltpu.VMEM((seq_len, w), BF16),
            pltpu.VMEM((seq_len, w), F32),
            pltpu.VMEM((nb, 2 * tf, w), F32),
            pltpu.VMEM((2 * tf, w), BF16),
        ],
        compiler_params=_cparams(("arbitrary", "arbitrary")),
        name="hyena_conv",
    )(z_src, u_hy, short_w, short_b, short_w, short_b, bd, kf, fwd, inv)


def _dft_tables(blk):
    m = 2 * blk
    tf = FREQ_TILE
    f = jnp.arange(blk, dtype=jnp.int32)
    n = jnp.arange(blk, dtype=jnp.int32)
    r = ((2 * f[:, None] + 1) * n[None, :]) % (2 * m)
    ang = r.astype(F32) * (math.pi / m)
    c, s = jnp.cos(ang), jnp.sin(ang)
    fwd = jnp.stack([c.reshape(blk // tf, tf, blk), (-s).reshape(blk // tf, tf, blk)], axis=1).reshape(m, blk)
    inv = fwd.T * (2.0 / m)
    return fwd.astype(BF16), inv.astype(BF16)


def _filter_embedding(seq_len):
    t = jnp.linspace(0.0, 1.0, seq_len, dtype=F32)[:, None]
    bands = (HY_EMB_DIM - 1) // 2
    f = jnp.linspace(1e-4, bands - 1, bands, dtype=F32)[None, :]
    w = 2.0 * math.pi * jnp.arange(seq_len, dtype=F32)[:, None] / seq_len
    z = jnp.concatenate([t, jnp.cos(f * w), -jnp.sin(f * w)], axis=-1)
    z_ext = jnp.concatenate([z, jnp.zeros((1, HY_EMB_DIM), F32), jnp.flip(z[1:], axis=0)], axis=0)
    return jnp.pad(z_ext, ((0, 0), (0, LANES - HY_EMB_DIM)))


def _rope_tables(seq_len):
    t = jnp.arange(seq_len)
    quarter = HEAD_DIM // 4
    inv = ROPE_THETA ** (-jnp.arange(quarter, dtype=F32) / quarter)
    ang_r = (t // GRID_W).astype(F32)[:, None] * inv[None, :]
    ang_c = (t % GRID_W).astype(F32)[:, None] * inv[None, :]
    zero = jnp.zeros_like(ang_r)
    cos = jnp.concatenate([jnp.cos(ang_r)] * 2 + [jnp.cos(ang_c)] * 2, axis=1)
    sa = jnp.concatenate([-jnp.sin(ang_r), zero, -jnp.sin(ang_c), zero], axis=1)
    sb = jnp.concatenate([zero, jnp.sin(ang_r), zero, jnp.sin(ang_c)], axis=1)
    two = lambda a: jnp.concatenate([a, a], axis=1)
    return two(cos), two(sa), two(sb)


def _block_ones(width):
    h = jnp.arange(width) // HEAD_DIM
    return (h[:, None] == h[None, :]).astype(BF16)


def _stream_layer(x, p, mod_l, batch, seq_len, rope_tabs, ctx, layer, hy_blk, consts):
    gq, gk, fwd, inv, z_ext = consts
    latent = ctx is not None
    x1, u_pool, q, k, v, u_hy = _token_a(x, mod_l, p["norm"], p["wg"][0], p["wu"][0], p["wd"][0], p["w_in"],
                                         gq, gk, p["qg"], p["kg"], rope_tabs, seq_len)
    y_pool = _pool_mix(u_pool, p["pool_w"], p["pool_scale"], batch, seq_len)
    if latent:
        a = _attention(q, k, v, p["sink"], batch, seq_len, ctx[0], ctx[1], layer)
    else:
        a = _attention(q, k, v, p["sink"], batch, seq_len)
    kraw, ksum = _hyena_filter(z_ext, p["f_w1"], p["f_b1"], p["f_w2"], p["f_b2"], p["f_w3"], p["f_b3"],
                               p["f_freq"], p["decay"], seq_len)
    kf = _filter_spectrum(kraw, ksum, fwd, seq_len, hy_blk)
    z1 = _hyena_conv(u_hy, 0, True, u_hy, 1, p["short_w"], p["short_b"], p["hy_bias"][0:1], kf, 0, fwd, inv,
                     batch, seq_len, hy_blk, F32)
    y_hy = _hyena_conv(z1, 0, False, u_hy, 2, p["short_w"], p["short_b"], p["hy_bias"][1:2], kf, 1, fwd, inv,
                       batch, seq_len, hy_blk, BF16)
    x3 = _token_b(x1, y_pool, a, y_hy, mod_l, p["norm"], p["w_out"], p["wg"][1], p["wu"][1], p["wd"][1],
                  seq_len, latent)
    return x3, k, v


def kernel(x_prompt, x_sample, cache_k, cache_v, c, c_ctx, ada_w, ada_b, norm_w, ffn_wg, ffn_wu, ffn_wd, w_in, w_out, pool_w, pool_scale, q_norm, k_norm, attn_sink, hy_short_w, hy_short_b, hy_f_w1, hy_f_b1, hy_f_w2, hy_f_b2, hy_f_w3, hy_f_b3, hy_sin_freq, hy_decay, hy_bias):
    bp, lp, _ = x_prompt.shape
    bs, ls, _ = x_sample.shape
    lc = cache_k.shape[2]

    cond = jnp.concatenate([c_ctx[None, :], c, jnp.zeros((COND_ROWS - 1 - bs, D_MODEL), F32)], axis=0)
    mod = _ada_mod(cond, ada_w, ada_b).reshape(DEPTH, COND_ROWS, 1, N_MOD * D_MODEL)

    gq, gk = _block_ones(ATTN_WIDTH), _block_ones(KV_WIDTH)
    rope_tabs = _rope_tables(ls)
    blk_p, blk_s = min(lp, 512), min(ls, 512)
    consts_p = (gq, gk) + _dft_tables(blk_p) + (_filter_embedding(lp),)
    consts_s = (gq, gk) + _dft_tables(blk_s) + (_filter_embedding(ls),)
    ctx_k = cache_k.reshape(bs, DEPTH, lc, KV_WIDTH)
    ctx_v = cache_v.reshape(bs, DEPTH, lc, KV_WIDTH)

    yp = x_prompt.reshape(bp * lp, D_MODEL)
    ys = x_sample.reshape(bs * ls, D_MODEL)
    ks, vs = [], []
    eye = jnp.eye(len(POOL_WINDOWS), dtype=F32)
    for l in range(DEPTH):
        pool_bd = (eye[:, None, :, None] * pool_w[l][:, :, None, :]).reshape(POOL_WIDTH, POOL_WIDTH)
        p = {
            "norm": norm_w[l],
            "wg": ffn_wg[l].astype(BF16), "wu": ffn_wu[l].astype(BF16), "wd": ffn_wd[l].astype(BF16),
            "w_in": w_in[l].astype(BF16), "w_out": w_out[l].astype(BF16),
            "pool_w": pool_bd.astype(BF16), "pool_scale": pool_scale[l][None, :],
            "qg": jnp.tile(q_norm[l], N_HEADS)[None, :], "kg": jnp.tile(k_norm[l], N_KV_HEADS)[None, :],
            "sink": attn_sink[l][None, :],
            "short_w": hy_short_w[l], "short_b": hy_short_b[l][None, :],
            "f_w1": jnp.pad(hy_f_w1[l], ((0, LANES - HY_EMB_DIM), (0, 0))), "f_b1": hy_f_b1[l][None, :],
            "f_w2": hy_f_w2[l], "f_b2": hy_f_b2[l][None, :],
            "f_w3": hy_f_w3[l], "f_b3": hy_f_b3[l][None, :],
            "f_freq": hy_sin_freq[l], "decay": hy_decay[l].reshape(1, HY_ORDER * HY_WIDTH),
            "hy_bias": hy_bias[l],
        }
        yp, k_l, v_l = _stream_layer(yp, p, mod[l], bp, lp, None, None, l, blk_p, consts_p)
        ks.append(k_l.reshape(bp, lp, N_KV_HEADS, HEAD_DIM))
        vs.append(v_l.reshape(bp, lp, N_KV_HEADS, HEAD_DIM))
        ys, _, _ = _stream_layer(ys, p, mod[l], bs, ls, rope_tabs, (ctx_k, ctx_v), l, blk_s, consts_s)
    return (yp.reshape(bp, lp, D_MODEL), ys.reshape(bs, ls, D_MODEL),
            jnp.stack(ks, axis=1), jnp.stack(vs, axis=1))
```

```python
import functools
import math

import jax
import jax.numpy as jnp
from jax import lax
from jax.experimental import pallas as pl
from jax.experimental.pallas import tpu as pltpu

F32 = jnp.float32
BF16 = jnp.bfloat16

D_MODEL = 1024
DEPTH = 2
GRID_W = 64
POOL_WINDOWS = (2, 4, 8, 16)
POOL_WIDTH = 256
POOL_GROUP = 64
HEAD_DIM = 64
N_HEADS = 8
N_KV_HEADS = 2
GQA_GROUP = 4
ATTN_WIDTH = 512
KV_WIDTH = 128
WINDOW = 128
BLOCK = 128
ROPE_THETA = 10000.0
HY_WIDTH = 256
HY_ORDER = 2
HY_EMB_DIM = 33
HY_FILTER_HIDDEN = 64
HY_MOD_SHIFT = 0.05
D_FF = 2816
IN_WIDTH = 1792
N_MOD = 9
NORM_EPS = 1e-6
NEG_INF = -1e30

LANES = 128
SUBLANES = 8
VMEM_LIMIT = 56 * 1024 * 1024

TOKEN_TILE = 512
FF_CHUNK = 1408
SEQ_CHUNK = 512
FREQ_TILE = 128
MAC_ROWS = 32
ATTN_SUB = 2
COND_ROWS = 8


def _cparams(sem):
    return pltpu.CompilerParams(dimension_semantics=sem, vmem_limit_bytes=VMEM_LIMIT)


def _dot(a, b):
    return jnp.dot(a, b, preferred_element_type=F32)


def _ada_kernel(c_ref, w_ref, b_ref, o_ref):
    c = c_ref[...]
    s = (c * jax.nn.sigmoid(c)).astype(BF16)
    o_ref[...] = _dot(s, w_ref[...].astype(BF16)) + b_ref[...]


def _ada_mod(cond, ada_w, ada_b):
    tn = 3072
    nw = N_MOD * D_MODEL
    return pl.pallas_call(
        _ada_kernel,
        grid=(DEPTH, nw // tn),
        in_specs=[
            pl.BlockSpec((COND_ROWS, D_MODEL), lambda l, j: (0, 0)),
            pl.BlockSpec((None, D_MODEL, tn), lambda l, j: (l, 0, j)),
            pl.BlockSpec((None, 1, tn), lambda l, j: (l, 0, j)),
        ],
        out_specs=pl.BlockSpec((None, COND_ROWS, tn), lambda l, j: (l, 0, j)),
        out_shape=jax.ShapeDtypeStruct((DEPTH, COND_ROWS, nw), F32),
        compiler_params=_cparams(("arbitrary", "arbitrary")),
        name="ada_mod",
    )(cond, ada_w, ada_b.reshape(DEPTH, 1, nw))


def _mod_slice(mod_ref, k):
    return mod_ref[:, k * D_MODEL:(k + 1) * D_MODEL]


def _rms_mod(x, gain, scale, shift):
    y = x * lax.rsqrt(jnp.mean(x * x, axis=-1, keepdims=True) + NORM_EPS)
    return (y * gain) * (1.0 + scale) + shift


def _swiglu(hb, wg_ref, wu_ref, wd_ref):
    acc = None
    for c in range(D_FF // FF_CHUNK):
        sl = slice(c * FF_CHUNK, (c + 1) * FF_CHUNK)
        g = _dot(hb, wg_ref[:, sl])
        u = _dot(hb, wu_ref[:, sl])
        a = ((g * jax.nn.sigmoid(g)) * u).astype(BF16)
        y = _dot(a, wd_ref[sl, :])
        acc = y if acc is None else acc + y
    return acc


def _head_norm(x, gmat, gain):
    sq = x * x
    hi = sq.astype(BF16)
    lo = (sq - hi.astype(F32)).astype(BF16)
    ss = _dot(hi, gmat) + _dot(lo, gmat)
    return (x * lax.rsqrt(ss * (1.0 / HEAD_DIM) + NORM_EPS)) * gain


def _rope(x, cos, sa, sb):
    w = x.shape[1]
    xn = pltpu.roll(x, w - 16, axis=1)
    xp = pltpu.roll(x, 16, axis=1)
    return x * cos + xn * sa + xp * sb


def _token_a_kernel(*refs, rope):
    if rope:
        (x_ref, mod_ref, nrm_ref, wg_ref, wu_ref, wd_ref, win_ref, gq_ref, gk_ref, qg_ref, kg_ref,
         cos_ref, sa_ref, sb_ref, x1_ref, up_ref, q_ref, kd_ref, vd_ref, uh_ref) = refs
    else:
        (x_ref, mod_ref, nrm_ref, wg_ref, wu_ref, wd_ref, win_ref, gq_ref, gk_ref, qg_ref, kg_ref,
         x1_ref, up_ref, q_ref, kd_ref, vd_ref, uh_ref, k_ref, v_ref) = refs
    x = x_ref[...]
    sh1, sc1, g1 = _mod_slice(mod_ref, 0), _mod_slice(mod_ref, 1), _mod_slice(mod_ref, 2)
    sh2, sc2 = _mod_slice(mod_ref, 3), _mod_slice(mod_ref, 4)
    h = _rms_mod(x, nrm_ref[0:1, :], sc1, sh1).astype(BF16)
    x1 = x + (0.5 * g1) * _swiglu(h, wg_ref, wu_ref, wd_ref)
    x1_ref[...] = x1
    h2 = _rms_mod(x1, nrm_ref[1:2, :], sc2, sh2).astype(BF16)
    u = _dot(h2, win_ref[...])
    s1 = POOL_WIDTH
    s2 = s1 + ATTN_WIDTH
    s3 = s2 + KV_WIDTH
    s4 = s3 + KV_WIDTH
    up_ref[...] = u[:, :s1]
    q = _head_norm(u[:, s1:s2], gq_ref[...], qg_ref[...])
    k = _head_norm(u[:, s2:s3], gk_ref[...], kg_ref[...])
    if rope:
        cos, sa, sb = cos_ref[...], sa_ref[...], sb_ref[...]
        reps = ATTN_WIDTH // LANES
        q = _rope(q, jnp.concatenate([cos] * reps, axis=1), jnp.concatenate([sa] * reps, axis=1),
                  jnp.concatenate([sb] * reps, axis=1))
        k = _rope(k, cos, sa, sb)
    v = u[:, s3:s4]
    q_ref[...] = (q * (HEAD_DIM ** -0.5 * math.log2(math.e))).astype(BF16)
    kd_ref[...] = _dup_heads(k)
    vd_ref[...] = _dup_heads(v)
    uh_ref[...] = u[:, s4:]
    if not rope:
        k_ref[...] = k
        v_ref[...] = v


def _token_a(x, mod_l, nrm, wg, wu, wd, w_in, gq, gk, qg, kg, rope_tabs, seq_len):
    n = x.shape[0]
    tm = TOKEN_TILE
    tiles_per_seq = max(seq_len // tm, 1)
    rope = rope_tabs is not None
    if rope:
        mod_row = lambda i: (1 + i // tiles_per_seq, 0, 0)
    else:
        mod_row = lambda i: (0, 0, 0)
    const = lambda i: (0, 0)
    row = lambda i: (i, 0)
    in_specs = [
        pl.BlockSpec((tm, D_MODEL), row),
        pl.BlockSpec((None, 1, N_MOD * D_MODEL), mod_row),
        pl.BlockSpec((3, D_MODEL), const),
        pl.BlockSpec((D_MODEL, D_FF), const),
        pl.BlockSpec((D_MODEL, D_FF), const),
        pl.BlockSpec((D_FF, D_MODEL), const),
        pl.BlockSpec((D_MODEL, IN_WIDTH), const),
        pl.BlockSpec((ATTN_WIDTH, ATTN_WIDTH), const),
        pl.BlockSpec((KV_WIDTH, KV_WIDTH), const),
        pl.BlockSpec((1, ATTN_WIDTH), const),
        pl.BlockSpec((1, KV_WIDTH), const),
    ]
    args = [x, mod_l, nrm, wg, wu, wd, w_in, gq, gk, qg, kg]
    if rope:
        tab = pl.BlockSpec((tm, LANES), lambda i: (i % tiles_per_seq, 0))
        in_specs += [tab, tab, tab]
        args += list(rope_tabs)
    widths = (D_MODEL, POOL_WIDTH, ATTN_WIDTH, 2 * KV_WIDTH, 2 * KV_WIDTH, 3 * HY_WIDTH)
    dtypes = (F32, F32, BF16, BF16, BF16, F32)
    if not rope:
        widths += (KV_WIDTH, KV_WIDTH)
        dtypes += (F32, F32)
    return pl.pallas_call(
        functools.partial(_token_a_kernel, rope=rope),
        grid=(n // tm,),
        in_specs=in_specs,
        out_specs=[pl.BlockSpec((tm, w), row) for w in widths],
        out_shape=[jax.ShapeDtypeStruct((n, w), d) for w, d in zip(widths, dtypes)],
        compiler_params=_cparams(("arbitrary",)),
        name="token_a_rope" if rope else "token_a",
    )(*args)


def _token_b_kernel(x_ref, yp_ref, a_ref, yh_ref, mod_ref, nrm_ref, wo_ref, wg_ref, wu_ref, wd_ref, o_ref):
    x1 = x_ref[...]
    g2 = _mod_slice(mod_ref, 5)
    sh3, sc3, g3 = _mod_slice(mod_ref, 6), _mod_slice(mod_ref, 7), _mod_slice(mod_ref, 8)
    cat = jnp.concatenate([yp_ref[...], a_ref[...], yh_ref[...]], axis=1)
    x2 = x1 + g2 * _dot(cat, wo_ref[...])
    h3 = _rms_mod(x2, nrm_ref[2:3, :], sc3, sh3).astype(BF16)
    o_ref[...] = x2 + (0.5 * g3) * _swiglu(h3, wg_ref, wu_ref, wd_ref)


def _token_b(x1, y_pool, a, y_hy, mod_l, nrm, w_out, wg, wu, wd, seq_len, per_seq_cond):
    n = x1.shape[0]
    tm = TOKEN_TILE
    tiles_per_seq = max(seq_len // tm, 1)
    if per_seq_cond:
        mod_row = lambda i: (1 + i // tiles_per_seq, 0, 0)
    else:
        mod_row = lambda i: (0, 0, 0)
    const = lambda i: (0, 0)
    row = lambda i: (i, 0)
    return pl.pallas_call(
        _token_b_kernel,
        grid=(n // tm,),
        in_specs=[
            pl.BlockSpec((tm, D_MODEL), row),
            pl.BlockSpec((tm, POOL_WIDTH), row),
            pl.BlockSpec((tm, ATTN_WIDTH), row),
            pl.BlockSpec((tm, HY_WIDTH), row),
            pl.BlockSpec((None, 1, N_MOD * D_MODEL), mod_row),
            pl.BlockSpec((3, D_MODEL), const),
            pl.BlockSpec((D_MODEL, D_MODEL), const),
            pl.BlockSpec((D_MODEL, D_FF), const),
            pl.BlockSpec((D_MODEL, D_FF), const),
            pl.BlockSpec((D_FF, D_MODEL), const),
        ],
        out_specs=pl.BlockSpec((tm, D_MODEL), row),
        out_shape=jax.ShapeDtypeStruct((n, D_MODEL), F32),
        compiler_params=_cparams(("arbitrary",)),
        name="token_b",
    )(x1, y_pool, a, y_hy, mod_l, nrm, w_out, wg, wu, wd)


def _halo_rows(src_ref, r0, rows, seq_len):
    c = src_ref.shape[1]
    zero = jnp.zeros((SUBLANES, c), F32)
    prev = src_ref[r0 - SUBLANES:r0, :] if r0 > 0 else zero
    nxt = src_ref[r0 + rows:r0 + rows + SUBLANES, :] if r0 + rows < seq_len else zero
    return prev, nxt


def _short_conv_chunk(src_ref, r0, rows, seq_len, w, b):
    x = src_ref[r0:r0 + rows, :]
    prev, nxt = _halo_rows(src_ref, r0, rows, seq_len)
    ridx = lax.broadcasted_iota(jnp.int32, x.shape, 0)
    xp = jnp.where(ridx == 0, prev[SUBLANES - 1:SUBLANES, :], pltpu.roll(x, 1, axis=0))
    xn = jnp.where(ridx == rows - 1, nxt[0:1, :], pltpu.roll(x, rows - 1, axis=0))
    return xp * w[0:1, :] + x * w[1:2, :] + xn * w[2:3, :] + b


def _pool_kernel(u_ref, w_ref, scale_ref, o_ref, *, seq_len, rows):
    lane = lax.broadcasted_iota(jnp.int32, (rows, POOL_WIDTH), 1)
    grp = lane // POOL_GROUP
    half = jnp.where(grp == 0, 1, jnp.where(grp == 1, 2, jnp.where(grp == 2, 4, 8)))
    ext = rows + 2 * SUBLANES
    for r0 in range(0, seq_len, rows):
        x = u_ref[r0:r0 + rows, :]
        prev, nxt = _halo_rows(u_ref, r0, rows, seq_len)
        a = jnp.concatenate([prev, x, nxt], axis=0)
        back = lambda v, s: pltpu.roll(v, s, axis=0)
        fwd = lambda v, s: pltpu.roll(v, ext - s, axis=0)
        b1 = back(a, 1)
        b2 = b1 + back(b1, 1)
        b4 = b2 + back(b2, 2)
        b8 = b4 + back(b4, 4)
        f2 = a + fwd(a, 1)
        f4 = f2 + fwd(f2, 2)
        f8 = f4 + fwd(f4, 4)
        core = lambda v: v[SUBLANES:SUBLANES + rows, :]
        wsum = jnp.where(grp == 0, core(b1) + x,
                         jnp.where(grp == 1, core(b2) + core(f2),
                                   jnp.where(grp == 2, core(b4) + core(f4), core(b8) + core(f8))))
        t = r0 + lax.broadcasted_iota(jnp.int32, (rows, POOL_WIDTH), 0)
        cnt = jnp.minimum(t + half, seq_len) - jnp.maximum(t - half, 0)
        d = wsum / cnt.astype(F32) - x
        y = _dot(d.astype(BF16), w_ref[...]) * scale_ref[...]
        o_ref[r0:r0 + rows, :] = y.astype(o_ref.dtype)


def _pool_mix(u_pool, w_bd, scale, batch, seq_len):
    rows = min(SEQ_CHUNK // 2, seq_len)
    return pl.pallas_call(
        functools.partial(_pool_kernel, seq_len=seq_len, rows=rows),
        grid=(batch,),
        in_specs=[
            pl.BlockSpec((seq_len, POOL_WIDTH), lambda b: (b, 0)),
            pl.BlockSpec((POOL_WIDTH, POOL_WIDTH), lambda b: (0, 0)),
            pl.BlockSpec((1, POOL_WIDTH), lambda b: (0, 0)),
        ],
        out_specs=pl.BlockSpec((seq_len, POOL_WIDTH), lambda b: (b, 0)),
        out_shape=jax.ShapeDtypeStruct((batch * seq_len, POOL_WIDTH), BF16),
        compiler_params=_cparams(("arbitrary",)),
        name="pool_mix",
    )(u_pool, w_bd, scale)


def _dup_heads(x):
    lane = lax.broadcasted_iota(jnp.int32, x.shape, 1)
    sw = pltpu.roll(x, HEAD_DIM, axis=1)
    lo = lane < HEAD_DIM
    return jnp.concatenate([jnp.where(lo, x, sw), jnp.where(lo, sw, x)], axis=1).astype(BF16)


def _attn_kernel(*refs, has_local, nblocks, sub):
    if has_local:
        sink_ref, q_ref, kp_ref, kc_ref, kn_ref, vp_ref, vc_ref, vn_ref, ck_ref, cv_ref, o_ref = refs
        kwin = jnp.concatenate([kp_ref[...], kc_ref[...], kn_ref[...]], axis=0)
        vwin = jnp.concatenate([vp_ref[...], vc_ref[...], vn_ref[...]], axis=0)
        kctx, vctx = _dup_heads(ck_ref[...]), _dup_heads(cv_ref[...])
        r = lax.broadcasted_iota(jnp.int32, (GQA_GROUP * BLOCK, BLOCK), 0) % BLOCK
        j = lax.broadcasted_iota(jnp.int32, (GQA_GROUP * BLOCK, BLOCK), 1)
    else:
        sink_ref, q_ref, kd_ref, vd_ref, o_ref = refs
    i = pl.program_id(1)
    lane_q = lax.broadcasted_iota(jnp.int32, (BLOCK, LANES), 1)
    log2e = math.log2(math.e)
    for sb in range(sub):
        q = q_ref[sb * BLOCK:(sb + 1) * BLOCK, :]
        gb = i * sub + sb
        if has_local:
            below = j >= r + jnp.where(gb >= 1, 0, BLOCK)
            above = j <= r - jnp.where(gb <= nblocks - 2, 0, BLOCK)
        outs = []
        for kvh in range(N_KV_HEADS):
            heads = range(kvh * GQA_GROUP, (kvh + 1) * GQA_GROUP)
            qparts = []
            for hd in heads:
                qp = q[:, (hd // 2) * LANES:(hd // 2 + 1) * LANES]
                keep = (lane_q < HEAD_DIM) == (hd % 2 == 0)
                qparts.append(jnp.where(keep, qp, jnp.zeros_like(qp)))
            qs = jnp.concatenate(qparts, axis=0)
            cols = slice(kvh * LANES, (kvh + 1) * LANES)
            if has_local:
                loc = slice(sb * BLOCK, (sb + 3) * BLOCK)
                kk = jnp.concatenate([kwin[loc, cols], kctx[:, cols]], axis=0)
                vv = jnp.concatenate([vwin[loc, cols], vctx[:, cols]], axis=0)
            else:
                kk, vv = kd_ref[:, cols], vd_ref[:, cols]
            s = lax.dot_general(qs, kk, (((1,), (1,)), ((), ())), preferred_element_type=F32)
            if has_local:
                s = jnp.concatenate([jnp.where(below, s[:, :BLOCK], NEG_INF), s[:, BLOCK:2 * BLOCK],
                                     jnp.where(above, s[:, 2 * BLOCK:3 * BLOCK], NEG_INF), s[:, 3 * BLOCK:]], axis=1)
            sk = jnp.concatenate([jnp.full((BLOCK, 1), sink_ref[0, hd] * log2e, F32) for hd in heads], axis=0)
            m = jnp.maximum(jnp.max(s, axis=1, keepdims=True), sk)
            e = jnp.exp2(s - m).astype(BF16)
            ow = _dot(e, jnp.concatenate([vv, jnp.ones_like(vv)], axis=1))
            o = ow[:, :LANES] / (ow[:, LANES:] + jnp.exp2(sk - m))
            outs += [o[g * BLOCK:(g + 1) * BLOCK, :] for g in range(GQA_GROUP)]
        blks = [jnp.where(lane_q < HEAD_DIM, outs[2 * p], outs[2 * p + 1]) for p in range(N_HEADS // 2)]
        o_ref[sb * BLOCK:(sb + 1) * BLOCK, :] = jnp.concatenate(blks, axis=1).astype(o_ref.dtype)


def _attention(q, kd, vd, sink, batch, seq_len, ctx_k=None, ctx_v=None, layer=0):
    nb = seq_len // BLOCK
    has_local = ctx_k is not None
    sub = ATTN_SUB
    steps = nb // sub
    qb = sub * BLOCK
    qspec = pl.BlockSpec((qb, ATTN_WIDTH), lambda b, i: (b * steps + i, 0))
    sspec = pl.BlockSpec(memory_space=pltpu.SMEM)
    if has_local:
        lc = ctx_k.shape[2]
        edge = lambda f: pl.BlockSpec((BLOCK, 2 * KV_WIDTH), f)
        prev = lambda b, i: (b * nb + jnp.maximum(i * sub - 1, 0), 0)
        nxt = lambda b, i: (b * nb + jnp.minimum((i + 1) * sub, nb - 1), 0)
        cur = pl.BlockSpec((qb, 2 * KV_WIDTH), lambda b, i: (b * steps + i, 0))
        cspec = pl.BlockSpec((None, None, lc, KV_WIDTH), lambda b, i: (b, layer, 0, 0))
        in_specs = [sspec, qspec, edge(prev), cur, edge(nxt), edge(prev), cur, edge(nxt), cspec, cspec]
        args = [sink, q, kd, kd, kd, vd, vd, vd, ctx_k, ctx_v]
    else:
        kv = pl.BlockSpec((seq_len, 2 * KV_WIDTH), lambda b, i: (b, 0))
        in_specs = [sspec, qspec, kv, kv]
        args = [sink, q, kd, vd]
    return pl.pallas_call(
        functools.partial(_attn_kernel, has_local=has_local, nblocks=nb, sub=sub),
        grid=(batch, steps),
        in_specs=in_specs,
        out_specs=pl.BlockSpec((qb, ATTN_WIDTH), lambda b, i: (b * steps + i, 0)),
        out_shape=jax.ShapeDtypeStruct((batch * seq_len, ATTN_WIDTH), BF16),
        compiler_params=_cparams(("arbitrary", "arbitrary")),
        name="attn_latent" if has_local else "attn_context",
    )(*args)


def _filter_kernel(z_ref, w1_ref, b1_ref, w2_ref, b2_ref, w3_ref, b3_ref, fr_ref, dl_ref, k_ref, sum_ref,
                   *, seq_len, rows):
    i = pl.program_id(0)
    z = z_ref[...]
    h = jnp.sin(fr_ref[0:1, :] * (_dot(z.astype(BF16), w1_ref[...].astype(BF16)) + b1_ref[...]))
    h = jnp.sin(fr_ref[1:2, :] * (_dot(h.astype(BF16), w2_ref[...].astype(BF16)) + b2_ref[...]))
    h3 = _dot(h.astype(BF16), w3_ref[...].astype(BF16)) + b3_ref[...]
    oc = HY_ORDER * HY_WIDTH
    row = i * rows + lax.broadcasted_iota(jnp.int32, (rows, oc), 0)
    t = z[:, 0:1]
    decay = jnp.exp(-t * jnp.abs(dl_ref[...]))
    sel = jnp.where(row < seq_len, h3[:, :oc], jnp.where(row > seq_len, h3[:, oc:], 0.0))
    k = sel * (decay + HY_MOD_SHIFT)
    k_ref[...] = k

    @pl.when(i == 0)
    def _():
        sum_ref[...] = jnp.zeros_like(sum_ref)

    sum_ref[...] += jnp.sum(jnp.abs(k), axis=0, keepdims=True)


def _hyena_filter(z_ext, w1p, b1, w2, b2, w3, b3, freq, deltas, seq_len):
    n = 2 * seq_len
    rows = min(1024, n)
    oc = HY_ORDER * HY_WIDTH
    const = lambda i: (0, 0)
    return pl.pallas_call(
        functools.partial(_filter_kernel, seq_len=seq_len, rows=rows),
        grid=(n // rows,),
        in_specs=[
            pl.BlockSpec((rows, LANES), lambda i: (i, 0)),
            pl.BlockSpec((LANES, HY_FILTER_HIDDEN), const),
            pl.BlockSpec((1, HY_FILTER_HIDDEN), const),
            pl.BlockSpec((HY_FILTER_HIDDEN, HY_FILTER_HIDDEN), const),
            pl.BlockSpec((1, HY_FILTER_HIDDEN), const),
            pl.BlockSpec((HY_FILTER_HIDDEN, 2 * oc), const),
            pl.BlockSpec((1, 2 * oc), const),
            pl.BlockSpec((2, HY_FILTER_HIDDEN), const),
            pl.BlockSpec((1, oc), const),
        ],
        out_specs=[pl.BlockSpec((rows, oc), lambda i: (i, 0)), pl.BlockSpec((1, oc), const)],
        out_shape=[jax.ShapeDtypeStruct((n, oc), F32), jax.ShapeDtypeStruct((1, oc), F32)],
        compiler_params=_cparams(("arbitrary",)),
        name="hyena_filter",
    )(z_ext, w1p, b1, w2, b2, w3, b3, freq, deltas)


def _spectrum_kernel(ka_ref, kb_ref, sum_ref, fwd_ref, kf_ref, *, blk):
    den = sum_ref[...] + 1e-6
    a = ka_ref[...] / den
    b = kb_ref[...] / den
    g0 = b[0:1, :]
    ha = _dot(fwd_ref[...], a.astype(BF16))
    hb = _dot(fwd_ref[...], b.astype(BF16))
    tf = FREQ_TILE
    par = lax.broadcasted_iota(jnp.int32, (tf, 1), 0) % 2
    sgn = (1 - 2 * par).astype(F32)
    for c in range(blk // tf):
        re = slice(2 * c * tf, (2 * c + 1) * tf)
        im = slice((2 * c + 1) * tf, (2 * c + 2) * tf)
        kf_ref[re, :] = ha[re, :] - sgn * hb[im, :]
        kf_ref[im, :] = ha[im, :] + sgn * (hb[re, :] - g0)


def _filter_spectrum(kraw, ksum, fwd, seq_len, blk):
    nb = seq_len // blk
    nlags = 2 * nb - 1
    oc = HY_ORDER * HY_WIDTH
    m = 2 * blk
    return pl.pallas_call(
        functools.partial(_spectrum_kernel, blk=blk),
        grid=(nlags,),
        in_specs=[
            pl.BlockSpec((blk, oc), lambda d: ((d + nb + 1) % (2 * nb), 0)),
            pl.BlockSpec((blk, oc), lambda d: ((d + nb) % (2 * nb), 0)),
            pl.BlockSpec((1, oc), lambda d: (0, 0)),
            pl.BlockSpec((m, blk), lambda d: (0, 0)),
        ],
        out_specs=pl.BlockSpec((None, m, oc), lambda d: (d, 0, 0)),
        out_shape=jax.ShapeDtypeStruct((nlags, m, oc), F32),
        compiler_params=_cparams(("arbitrary",)),
        name="filter_spectrum",
    )(kraw, kraw, ksum, fwd)


def _conv_kernel(z_ref, g_ref, swz_ref, sbz_ref, swg_ref, sbg_ref, bd_ref, kf_ref, fwd_ref, inv_ref, o_ref,
                 zb_ref, acc_ref, zf_ref, yf_ref, *, seq_len, blk, conv_z):
    fc = pl.program_id(1)
    nfc = pl.num_programs(1)
    nb = seq_len // blk
    tf = FREQ_TILE
    rows = min(SEQ_CHUNK, seq_len)

    def z_chunk(r0):
        if conv_z:
            return _short_conv_chunk(z_ref, r0, rows, seq_len, swz_ref[...], sbz_ref[...])
        return z_ref[r0:r0 + rows, :]

    @pl.when(fc == 0)
    def _():
        for r0 in range(0, seq_len, rows):
            zb_ref[r0:r0 + rows, :] = z_chunk(r0).astype(BF16)
        acc_ref[...] = jnp.zeros_like(acc_ref)

    for bj in range(nb):
        zf_ref[bj] = _dot(fwd_ref[...], zb_ref[bj * blk:(bj + 1) * blk, :])

    for bi in range(nb):
        for r in range(0, tf, MAC_ROWS):
            re = slice(r, r + MAC_ROWS)
            im = slice(tf + r, tf + r + MAC_ROWS)
            yr = jnp.zeros((MAC_ROWS, HY_WIDTH), F32)
            yi = jnp.zeros((MAC_ROWS, HY_WIDTH), F32)
            for bj in range(nb):
                lag = bi - bj + nb - 1
                kr, ki = kf_ref[lag, re, :], kf_ref[lag, im, :]
                zr, zi = zf_ref[bj, re, :], zf_ref[bj, im, :]
                yr = yr + (kr * zr - ki * zi)
                yi = yi + (kr * zi + ki * zr)
            yf_ref[re, :] = yr.astype(BF16)
            yf_ref[im, :] = yi.astype(BF16)
        acc_ref[bi * blk:(bi + 1) * blk, :] += _dot(inv_ref[...], yf_ref[...])

    @pl.when(fc == nfc - 1)
    def _():
        for r0 in range(0, seq_len, rows):
            z = z_chunk(r0)
            gate = _short_conv_chunk(g_ref, r0, rows, seq_len, swg_ref[...], sbg_ref[...])
            y = gate * (acc_ref[r0:r0 + rows, :] + bd_ref[...] * z)
            o_ref[r0:r0 + rows, :] = y.astype(o_ref.dtype)


def _hyena_conv(z_src, z_col, conv_z, u_hy, g_col, short_w, short_b, bd, kf, order, fwd, inv,
                batch, seq_len, blk, out_dtype):
    nb = seq_len // blk
    nlags = 2 * nb - 1
    tf = FREQ_TILE
    nfc = blk // tf
    w = HY_WIDTH
    return pl.pallas_call(
        functools.partial(_conv_kernel, seq_len=seq_len, blk=blk, conv_z=conv_z),
        grid=(batch, nfc),
        in_specs=[
            pl.BlockSpec((seq_len, w), lambda b, f: (b, z_col)),
            pl.BlockSpec((seq_len, w), lambda b, f: (b, g_col)),
            pl.BlockSpec((3, w), lambda b, f: (0, z_col if conv_z else 0)),
            pl.BlockSpec((1, w), lambda b, f: (0, z_col if conv_z else 0)),
            pl.BlockSpec((3, w), lambda b, f: (0, g_col)),
            pl.BlockSpec((1, w), lambda b, f: (0, g_col)),
            pl.BlockSpec((1, w), lambda b, f: (0, 0)),
            pl.BlockSpec((nlags, 2 * tf, w), lambda b, f: (0, f, order)),
            pl.BlockSpec((2 * tf, blk), lambda b, f: (f, 0)),
            pl.BlockSpec((blk, 2 * tf), lambda b, f: (0, f)),
        ],
        out_specs=pl.BlockSpec((seq_len, w), lambda b, f: (b, 0)),
        out_shape=jax.ShapeDtypeStruct((batch * seq_len, w), out_dtype),
        scratch_shapes=[
            pltpu.VMEM((seq_len, w), BF16),
            pltpu.VMEM((seq_len, w), F32),
            pltpu.VMEM((nb, 2 * tf, w), F32),
            pltpu.VMEM((2 * tf, w), BF16),
        ],
        compiler_params=_cparams(("arbitrary", "arbitrary")),
        name="hyena_conv",
    )(z_src, u_hy, short_w, short_b, short_w, short_b, bd, kf, fwd, inv)


def _dft_tables(blk):
    m = 2 * blk
    tf = FREQ_TILE
    f = jnp.arange(blk, dtype=jnp.int32)
    n = jnp.arange(blk, dtype=jnp.int32)
    r = ((2 * f[:, None] + 1) * n[None, :]) % (2 * m)
    ang = r.astype(F32) * (math.pi / m)
    c, s = jnp.cos(ang), jnp.sin(ang)
    fwd = jnp.stack([c.reshape(blk // tf, tf, blk), (-s).reshape(blk // tf, tf, blk)], axis=1).reshape(m, blk)
    inv = fwd.T * (2.0 / m)
    return fwd.astype(BF16), inv.astype(BF16)


def _filter_embedding(seq_len):
    t = jnp.linspace(0.0, 1.0, seq_len, dtype=F32)[:, None]
    bands = (HY_EMB_DIM - 1) // 2
    f = jnp.linspace(1e-4, bands - 1, bands, dtype=F32)[None, :]
    w = 2.0 * math.pi * jnp.arange(seq_len, dtype=F32)[:, None] / seq_len
    z = jnp.concatenate([t, jnp.cos(f * w), -jnp.sin(f * w)], axis=-1)
    z_ext = jnp.concatenate([z, jnp.zeros((1, HY_EMB_DIM), F32), jnp.flip(z[1:], axis=0)], axis=0)
    return jnp.pad(z_ext, ((0, 0), (0, LANES - HY_EMB_DIM)))


def _rope_tables(seq_len):
    t = jnp.arange(seq_len)
    quarter = HEAD_DIM // 4
    inv = ROPE_THETA ** (-jnp.arange(quarter, dtype=F32) / quarter)
    ang_r = (t // GRID_W).astype(F32)[:, None] * inv[None, :]
    ang_c = (t % GRID_W).astype(F32)[:, None] * inv[None, :]
    zero = jnp.zeros_like(ang_r)
    cos = jnp.concatenate([jnp.cos(ang_r)] * 2 + [jnp.cos(ang_c)] * 2, axis=1)
    sa = jnp.concatenate([-jnp.sin(ang_r), zero, -jnp.sin(ang_c), zero], axis=1)
    sb = jnp.concatenate([zero, jnp.sin(ang_r), zero, jnp.sin(ang_c)], axis=1)
    two = lambda a: jnp.concatenate([a, a], axis=1)
    return two(cos), two(sa), two(sb)


def _block_ones(width):
    h = jnp.arange(width) // HEAD_DIM
    return (h[:, None] == h[None, :]).astype(BF16)


def _stream_layer(x, p, mod_l, batch, seq_len, rope_tabs, ctx, layer, hy_blk, consts):
    gq, gk, fwd, inv, z_ext = consts
    latent = ctx is not None
    outs = _token_a(x, mod_l, p["norm"], p["wg"][0], p["wu"][0], p["wd"][0], p["w_in"],
                    gq, gk, p["qg"], p["kg"], rope_tabs, seq_len)
    x1, u_pool, q, kd, vd, u_hy = outs[:6]
    k, v = (None, None) if latent else outs[6:]
    y_pool = _pool_mix(u_pool, p["pool_w"], p["pool_scale"], batch, seq_len)
    if latent:
        a = _attention(q, kd, vd, p["sink"], batch, seq_len, ctx[0], ctx[1], layer)
    else:
        a = _attention(q, kd, vd, p["sink"], batch, seq_len)
    kraw, ksum = _hyena_filter(z_ext, p["f_w1"], p["f_b1"], p["f_w2"], p["f_b2"], p["f_w3"], p["f_b3"],
                               p["f_freq"], p["decay"], seq_len)
    kf = _filter_spectrum(kraw, ksum, fwd, seq_len, hy_blk)
    z1 = _hyena_conv(u_hy, 0, True, u_hy, 1, p["short_w"], p["short_b"], p["hy_bias"][0:1], kf, 0, fwd, inv,
                     batch, seq_len, hy_blk, F32)
    y_hy = _hyena_conv(z1, 0, False, u_hy, 2, p["short_w"], p["short_b"], p["hy_bias"][1:2], kf, 1, fwd, inv,
                       batch, seq_len, hy_blk, BF16)
    x3 = _token_b(x1, y_pool, a, y_hy, mod_l, p["norm"], p["w_out"], p["wg"][1], p["wu"][1], p["wd"][1],
                  seq_len, latent)
    return x3, k, v


def kernel(x_prompt, x_sample, cache_k, cache_v, c, c_ctx, ada_w, ada_b, norm_w, ffn_wg, ffn_wu, ffn_wd, w_in, w_out, pool_w, pool_scale, q_norm, k_norm, attn_sink, hy_short_w, hy_short_b, hy_f_w1, hy_f_b1, hy_f_w2, hy_f_b2, hy_f_w3, hy_f_b3, hy_sin_freq, hy_decay, hy_bias):
    bp, lp, _ = x_prompt.shape
    bs, ls, _ = x_sample.shape
    lc = cache_k.shape[2]

    cond = jnp.concatenate([c_ctx[None, :], c, jnp.zeros((COND_ROWS - 1 - bs, D_MODEL), F32)], axis=0)
    mod = _ada_mod(cond, ada_w, ada_b).reshape(DEPTH, COND_ROWS, 1, N_MOD * D_MODEL)

    gq, gk = _block_ones(ATTN_WIDTH), _block_ones(KV_WIDTH)
    rope_tabs = _rope_tables(ls)
    blk_p, blk_s = min(lp, 512), min(ls, 512)
    consts_p = (gq, gk) + _dft_tables(blk_p) + (_filter_embedding(lp),)
    consts_s = (gq, gk) + _dft_tables(blk_s) + (_filter_embedding(ls),)
    ctx_k = cache_k.reshape(bs, DEPTH, lc, KV_WIDTH)
    ctx_v = cache_v.reshape(bs, DEPTH, lc, KV_WIDTH)

    yp = x_prompt.reshape(bp * lp, D_MODEL)
    ys = x_sample.reshape(bs * ls, D_MODEL)
    ks, vs = [], []
    eye = jnp.eye(len(POOL_WINDOWS), dtype=F32)
    for l in range(DEPTH):
        pool_bd = (eye[:, None, :, None] * pool_w[l][:, :, None, :]).reshape(POOL_WIDTH, POOL_WIDTH)
        p = {
            "norm": norm_w[l],
            "wg": ffn_wg[l].astype(BF16), "wu": ffn_wu[l].astype(BF16), "wd": ffn_wd[l].astype(BF16),
            "w_in": w_in[l].astype(BF16), "w_out": w_out[l].astype(BF16),
            "pool_w": pool_bd.astype(BF16), "pool_scale": pool_scale[l][None, :],
            "qg": jnp.tile(q_norm[l], N_HEADS)[None, :], "kg": jnp.tile(k_norm[l], N_KV_HEADS)[None, :],
            "sink": attn_sink[l][None, :],
            "short_w": hy_short_w[l], "short_b": hy_short_b[l][None, :],
            "f_w1": jnp.pad(hy_f_w1[l], ((0, LANES - HY_EMB_DIM), (0, 0))), "f_b1": hy_f_b1[l][None, :],
            "f_w2": hy_f_w2[l], "f_b2": hy_f_b2[l][None, :],
            "f_w3": hy_f_w3[l], "f_b3": hy_f_b3[l][None, :],
            "f_freq": hy_sin_freq[l], "decay": hy_decay[l].reshape(1, HY_ORDER * HY_WIDTH),
            "hy_bias": hy_bias[l],
        }
        yp, k_l, v_l = _stream_layer(yp, p, mod[l], bp, lp, None, None, l, blk_p, consts_p)
        ks.append(k_l.reshape(bp, lp, N_KV_HEADS, HEAD_DIM))
        vs.append(v_l.reshape(bp, lp, N_KV_HEADS, HEAD_DIM))
        ys, _, _ = _stream_layer(ys, p, mod[l], bs, ls, rope_tabs, (ctx_k, ctx_v), l, blk_s, consts_s)
    return (yp.reshape(bp, lp, D_MODEL), ys.reshape(bs, ls, D_MODEL),
            jnp.stack(ks, axis=1), jnp.stack(vs, axis=1))
```

```python
import functools
import math

import jax
import jax.numpy as jnp
from jax import lax
from jax.experimental import pallas as pl
from jax.experimental.pallas import tpu as pltpu

F32 = jnp.float32
BF16 = jnp.bfloat16

D_MODEL = 1024
DEPTH = 2
GRID_W = 64
POOL_WINDOWS = (2, 4, 8, 16)
POOL_WIDTH = 256
POOL_GROUP = 64
HEAD_DIM = 64
N_HEADS = 8
N_KV_HEADS = 2
GQA_GROUP = 4
ATTN_WIDTH = 512
KV_WIDTH = 128
WINDOW = 128
BLOCK = 128
ROPE_THETA = 10000.0
HY_WIDTH = 256
HY_ORDER = 2
HY_EMB_DIM = 33
HY_FILTER_HIDDEN = 64
HY_MOD_SHIFT = 0.05
D_FF = 2816
IN_WIDTH = 1792
N_MOD = 9
NORM_EPS = 1e-6
NEG_INF = -1e30

LANES = 128
SUBLANES = 8
VMEM_LIMIT = 56 * 1024 * 1024

TOKEN_TILE = 512
MXU_DIM = 256
FF_CHUNKS = (1280, 1536)
SEQ_CHUNK = 512
FREQ_TILE = 128
MAC_ROWS = 32
ATTN_SUB = 2
COND_ROWS = 8


def _cparams(sem):
    return pltpu.CompilerParams(dimension_semantics=sem, vmem_limit_bytes=VMEM_LIMIT)


def _dot(a, b):
    return jnp.dot(a, b, preferred_element_type=F32)


def _cast_kernel(*refs):
    n = len(refs) // 2
    for i_ref, o_ref in zip(refs[:n], refs[n:]):
        o_ref[...] = i_ref[...].astype(o_ref.dtype)


def _cast_weights(ffn_wg, ffn_wu, ffn_wd, w_in, w_out):
    steps = 16
    arrs = {"wg": ffn_wg, "wu": ffn_wu, "wd": ffn_wd, "w_in": w_in, "w_out": w_out}
    flat = [a.reshape(-1, a.shape[-1]) for a in arrs.values()]
    specs = [pl.BlockSpec((a.shape[0] // steps, a.shape[1]), lambda i: (i, 0)) for a in flat]
    outs = pl.pallas_call(
        _cast_kernel,
        grid=(steps,),
        in_specs=specs,
        out_specs=specs,
        out_shape=[jax.ShapeDtypeStruct(a.shape, BF16) for a in flat],
        compiler_params=_cparams(("arbitrary",)),
        name="cast_weights",
    )(*flat)
    return {k: o.reshape(a.shape) for (k, a), o in zip(arrs.items(), outs)}


def _ada_kernel(c_ref, w_ref, b_ref, o_ref):
    c = c_ref[...]
    s = (c * jax.nn.sigmoid(c)).astype(BF16)
    o_ref[...] = _dot(s, w_ref[...].astype(BF16)) + b_ref[...]


def _ada_mod(cond, ada_w, ada_b):
    tn = 3072
    nw = N_MOD * D_MODEL
    return pl.pallas_call(
        _ada_kernel,
        grid=(DEPTH, nw // tn),
        in_specs=[
            pl.BlockSpec((COND_ROWS, D_MODEL), lambda l, j: (0, 0)),
            pl.BlockSpec((None, D_MODEL, tn), lambda l, j: (l, 0, j)),
            pl.BlockSpec((None, 1, tn), lambda l, j: (l, 0, j)),
        ],
        out_specs=pl.BlockSpec((None, COND_ROWS, tn), lambda l, j: (l, 0, j)),
        out_shape=jax.ShapeDtypeStruct((DEPTH, COND_ROWS, nw), F32),
        compiler_params=_cparams(("arbitrary", "arbitrary")),
        name="ada_mod",
    )(cond, ada_w, ada_b.reshape(DEPTH, 1, nw))


def _mod_slice(mod_ref, k):
    return mod_ref[:, k * D_MODEL:(k + 1) * D_MODEL]


def _rms_mod(x, gain, scale, shift):
    y = x * lax.rsqrt(jnp.mean(x * x, axis=-1, keepdims=True) + NORM_EPS)
    return (y * gain) * (1.0 + scale) + shift


def _swiglu(hb, wg_ref, wu_ref, wd_ref):
    acc = None
    lo = 0
    for width in FF_CHUNKS:
        sl = slice(lo, lo + width)
        lo += width
        g = _dot(hb, wg_ref[:, sl])
        u = _dot(hb, wu_ref[:, sl])
        a = ((g * jax.nn.sigmoid(g)) * u).astype(BF16)
        y = _dot(a, wd_ref[sl, :])
        acc = y if acc is None else acc + y
    return acc


def _head_norm(x, gmat, gain):
    ss = _dot((x * x).astype(BF16), gmat)
    return (x * lax.rsqrt(ss * (1.0 / HEAD_DIM) + NORM_EPS)) * gain


def _rope(x, cos, sa, sb):
    w = x.shape[1]
    xn = pltpu.roll(x, w - 16, axis=1)
    xp = pltpu.roll(x, 16, axis=1)
    return x * cos + xn * sa + xp * sb


def _token_a_kernel(*refs, rope):
    if rope:
        (x_ref, mod_ref, nrm_ref, wg_ref, wu_ref, wd_ref, win_ref, gq_ref, gk_ref, qg_ref, kg_ref,
         cos_ref, sa_ref, sb_ref, x1_ref, up_ref, q_ref, kd_ref, vd_ref, uh_ref) = refs
    else:
        (x_ref, mod_ref, nrm_ref, wg_ref, wu_ref, wd_ref, win_ref, gq_ref, gk_ref, qg_ref, kg_ref,
         x1_ref, up_ref, q_ref, kd_ref, vd_ref, uh_ref, k_ref, v_ref) = refs
    x = x_ref[...]
    sh1, sc1, g1 = _mod_slice(mod_ref, 0), _mod_slice(mod_ref, 1), _mod_slice(mod_ref, 2)
    sh2, sc2 = _mod_slice(mod_ref, 3), _mod_slice(mod_ref, 4)
    h = _rms_mod(x, nrm_ref[0:1, :], sc1, sh1).astype(BF16)
    x1 = x + (0.5 * g1) * _swiglu(h, wg_ref, wu_ref, wd_ref)
    x1_ref[...] = x1
    h2 = _rms_mod(x1, nrm_ref[1:2, :], sc2, sh2).astype(BF16)
    u = _dot(h2, win_ref[...])
    s1 = POOL_WIDTH
    s2 = s1 + ATTN_WIDTH
    s3 = s2 + KV_WIDTH
    s4 = s3 + KV_WIDTH
    up_ref[...] = u[:, :s1]
    q = _head_norm(u[:, s1:s2], gq_ref[...], qg_ref[...])
    k = _head_norm(u[:, s2:s3], gk_ref[...], kg_ref[...])
    if rope:
        cos, sa, sb = cos_ref[...], sa_ref[...], sb_ref[...]
        reps = ATTN_WIDTH // LANES
        q = _rope(q, jnp.concatenate([cos] * reps, axis=1), jnp.concatenate([sa] * reps, axis=1),
                  jnp.concatenate([sb] * reps, axis=1))
        k = _rope(k, cos, sa, sb)
    v = u[:, s3:s4]
    q_ref[...] = (q * (HEAD_DIM ** -0.5 * math.log2(math.e))).astype(BF16)
    kd_ref[...] = _dup_heads(k)
    vd_ref[...] = _dup_heads(v)
    uh_ref[...] = u[:, s4:]
    if not rope:
        k_ref[...] = k
        v_ref[...] = v


def _token_a(x, mod_l, nrm, wts, layer, gq, gk, qg, kg, rope_tabs, seq_len):
    n = x.shape[0]
    tm = TOKEN_TILE
    tiles_per_seq = max(seq_len // tm, 1)
    rope = rope_tabs is not None
    if rope:
        mod_row = lambda i: (1 + i // tiles_per_seq, 0, 0)
    else:
        mod_row = lambda i: (0, 0, 0)
    const = lambda i: (0, 0)
    row = lambda i: (i, 0)
    in_specs = [
        pl.BlockSpec((tm, D_MODEL), row),
        pl.BlockSpec((None, 1, N_MOD * D_MODEL), mod_row),
        pl.BlockSpec((3, D_MODEL), const),
        pl.BlockSpec((None, None, D_MODEL, D_FF), lambda i: (layer, 0, 0, 0)),
        pl.BlockSpec((None, None, D_MODEL, D_FF), lambda i: (layer, 0, 0, 0)),
        pl.BlockSpec((None, None, D_FF, D_MODEL), lambda i: (layer, 0, 0, 0)),
        pl.BlockSpec((None, D_MODEL, IN_WIDTH), lambda i: (layer, 0, 0)),
        pl.BlockSpec((ATTN_WIDTH, ATTN_WIDTH), const),
        pl.BlockSpec((KV_WIDTH, KV_WIDTH), const),
        pl.BlockSpec((1, ATTN_WIDTH), const),
        pl.BlockSpec((1, KV_WIDTH), const),
    ]
    args = [x, mod_l, nrm, wts["wg"], wts["wu"], wts["wd"], wts["w_in"], gq, gk, qg, kg]
    if rope:
        tab = pl.BlockSpec((tm, LANES), lambda i: (i % tiles_per_seq, 0))
        in_specs += [tab, tab, tab]
        args += list(rope_tabs)
    widths = (D_MODEL, POOL_WIDTH, ATTN_WIDTH, 2 * KV_WIDTH, 2 * KV_WIDTH, 3 * HY_WIDTH)
    dtypes = (F32, F32, BF16, BF16, BF16, F32)
    if not rope:
        widths += (KV_WIDTH, KV_WIDTH)
        dtypes += (F32, F32)
    return pl.pallas_call(
        functools.partial(_token_a_kernel, rope=rope),
        grid=(n // tm,),
        in_specs=in_specs,
        out_specs=[pl.BlockSpec((tm, w), row) for w in widths],
        out_shape=[jax.ShapeDtypeStruct((n, w), d) for w, d in zip(widths, dtypes)],
        compiler_params=_cparams(("arbitrary",)),
        name="token_a_rope" if rope else "token_a",
    )(*args)


def _token_b_kernel(x_ref, yp_ref, a_ref, yh_ref, mod_ref, nrm_ref, wo_ref, wg_ref, wu_ref, wd_ref, o_ref):
    x1 = x_ref[...]
    g2 = _mod_slice(mod_ref, 5)
    sh3, sc3, g3 = _mod_slice(mod_ref, 6), _mod_slice(mod_ref, 7), _mod_slice(mod_ref, 8)
    cat = jnp.concatenate([yp_ref[...], a_ref[...], yh_ref[...]], axis=1)
    x2 = x1 + g2 * _dot(cat, wo_ref[...])
    h3 = _rms_mod(x2, nrm_ref[2:3, :], sc3, sh3).astype(BF16)
    o_ref[...] = x2 + (0.5 * g3) * _swiglu(h3, wg_ref, wu_ref, wd_ref)


def _token_b(x1, y_pool, a, y_hy, mod_l, nrm, wts, layer, seq_len, per_seq_cond):
    n = x1.shape[0]
    tm = TOKEN_TILE
    tiles_per_seq = max(seq_len // tm, 1)
    if per_seq_cond:
        mod_row = lambda i: (1 + i // tiles_per_seq, 0, 0)
    else:
        mod_row = lambda i: (0, 0, 0)
    const = lambda i: (0, 0)
    row = lambda i: (i, 0)
    return pl.pallas_call(
        _token_b_kernel,
        grid=(n // tm,),
        in_specs=[
            pl.BlockSpec((tm, D_MODEL), row),
            pl.BlockSpec((tm, POOL_WIDTH), row),
            pl.BlockSpec((tm, ATTN_WIDTH), row),
            pl.BlockSpec((tm, HY_WIDTH), row),
            pl.BlockSpec((None, 1, N_MOD * D_MODEL), mod_row),
            pl.BlockSpec((3, D_MODEL), const),
            pl.BlockSpec((None, D_MODEL, D_MODEL), lambda i: (layer, 0, 0)),
            pl.BlockSpec((None, None, D_MODEL, D_FF), lambda i: (layer, 1, 0, 0)),
            pl.BlockSpec((None, None, D_MODEL, D_FF), lambda i: (layer, 1, 0, 0)),
            pl.BlockSpec((None, None, D_FF, D_MODEL), lambda i: (layer, 1, 0, 0)),
        ],
        out_specs=pl.BlockSpec((tm, D_MODEL), row),
        out_shape=jax.ShapeDtypeStruct((n, D_MODEL), F32),
        compiler_params=_cparams(("arbitrary",)),
        name="token_b",
    )(x1, y_pool, a, y_hy, mod_l, nrm, wts["w_out"], wts["wg"], wts["wu"], wts["wd"])


def _halo_rows(src_ref, r0, rows, seq_len):
    c = src_ref.shape[1]
    zero = jnp.zeros((SUBLANES, c), F32)
    prev = src_ref[r0 - SUBLANES:r0, :] if r0 > 0 else zero
    nxt = src_ref[r0 + rows:r0 + rows + SUBLANES, :] if r0 + rows < seq_len else zero
    return prev, nxt


def _short_conv_chunk(src_ref, r0, rows, seq_len, w, b):
    x = src_ref[r0:r0 + rows, :]
    prev, nxt = _halo_rows(src_ref, r0, rows, seq_len)
    ridx = lax.broadcasted_iota(jnp.int32, x.shape, 0)
    xp = jnp.where(ridx == 0, prev[SUBLANES - 1:SUBLANES, :], pltpu.roll(x, 1, axis=0))
    xn = jnp.where(ridx == rows - 1, nxt[0:1, :], pltpu.roll(x, rows - 1, axis=0))
    return xp * w[0:1, :] + x * w[1:2, :] + xn * w[2:3, :] + b


def _pool_kernel(u_ref, w_ref, scale_ref, o_ref, *, seq_len, rows):
    lane = lax.broadcasted_iota(jnp.int32, (rows, POOL_WIDTH), 1)
    grp = lane // POOL_GROUP
    half = jnp.where(grp == 0, 1, jnp.where(grp == 1, 2, jnp.where(grp == 2, 4, 8)))
    ext = rows + 2 * SUBLANES
    for r0 in range(0, seq_len, rows):
        x = u_ref[r0:r0 + rows, :]
        prev, nxt = _halo_rows(u_ref, r0, rows, seq_len)
        a = jnp.concatenate([prev, x, nxt], axis=0)
        back = lambda v, s: pltpu.roll(v, s, axis=0)
        fwd = lambda v, s: pltpu.roll(v, ext - s, axis=0)
        b1 = back(a, 1)
        b2 = b1 + back(b1, 1)
        b4 = b2 + back(b2, 2)
        b8 = b4 + back(b4, 4)
        f2 = a + fwd(a, 1)
        f4 = f2 + fwd(f2, 2)
        f8 = f4 + fwd(f4, 4)
        core = lambda v: v[SUBLANES:SUBLANES + rows, :]
        wsum = jnp.where(grp == 0, core(b1) + x,
                         jnp.where(grp == 1, core(b2) + core(f2),
                                   jnp.where(grp == 2, core(b4) + core(f4), core(b8) + core(f8))))
        t = r0 + lax.broadcasted_iota(jnp.int32, (rows, POOL_WIDTH), 0)
        cnt = jnp.minimum(t + half, seq_len) - jnp.maximum(t - half, 0)
        d = wsum / cnt.astype(F32) - x
        y = _dot(d.astype(BF16), w_ref[...]) * scale_ref[...]
        o_ref[r0:r0 + rows, :] = y.astype(o_ref.dtype)


def _pool_mix(u_pool, w_bd, scale, batch, seq_len):
    rows = min(SEQ_CHUNK // 2, seq_len)
    return pl.pallas_call(
        functools.partial(_pool_kernel, seq_len=seq_len, rows=rows),
        grid=(batch,),
        in_specs=[
            pl.BlockSpec((seq_len, POOL_WIDTH), lambda b: (b, 0)),
            pl.BlockSpec((POOL_WIDTH, POOL_WIDTH), lambda b: (0, 0)),
            pl.BlockSpec((1, POOL_WIDTH), lambda b: (0, 0)),
        ],
        out_specs=pl.BlockSpec((seq_len, POOL_WIDTH), lambda b: (b, 0)),
        out_shape=jax.ShapeDtypeStruct((batch * seq_len, POOL_WIDTH), BF16),
        compiler_params=_cparams(("arbitrary",)),
        name="pool_mix",
    )(u_pool, w_bd, scale)


def _dup_heads(x):
    lane = lax.broadcasted_iota(jnp.int32, x.shape, 1)
    sw = pltpu.roll(x, HEAD_DIM, axis=1)
    lo = lane < HEAD_DIM
    return jnp.concatenate([jnp.where(lo, x, sw), jnp.where(lo, sw, x)], axis=1).astype(BF16)


def _attn_kernel(*refs, has_local, nblocks, sub):
    if has_local:
        sink_ref, q_ref, kp_ref, kc_ref, kn_ref, vp_ref, vc_ref, vn_ref, ck_ref, cv_ref, o_ref = refs
        kwin = jnp.concatenate([kp_ref[...], kc_ref[...], kn_ref[...]], axis=0)
        vwin = jnp.concatenate([vp_ref[...], vc_ref[...], vn_ref[...]], axis=0)
        kctx, vctx = _dup_heads(ck_ref[...]), _dup_heads(cv_ref[...])
        r = lax.broadcasted_iota(jnp.int32, (GQA_GROUP * BLOCK, BLOCK), 0) % BLOCK
        j = lax.broadcasted_iota(jnp.int32, (GQA_GROUP * BLOCK, BLOCK), 1)
    else:
        sink_ref, q_ref, kd_ref, vd_ref, o_ref = refs
    i = pl.program_id(1)
    lane_q = lax.broadcasted_iota(jnp.int32, (BLOCK, LANES), 1)
    log2e = math.log2(math.e)
    for sb in range(sub):
        q = q_ref[sb * BLOCK:(sb + 1) * BLOCK, :]
        gb = i * sub + sb
        if has_local:
            below = j >= r + jnp.where(gb >= 1, 0, BLOCK)
            above = j <= r - jnp.where(gb <= nblocks - 2, 0, BLOCK)
        outs = []
        for kvh in range(N_KV_HEADS):
            heads = range(kvh * GQA_GROUP, (kvh + 1) * GQA_GROUP)
            qparts = []
            for hd in heads:
                qp = q[:, (hd // 2) * LANES:(hd // 2 + 1) * LANES]
                keep = (lane_q < HEAD_DIM) == (hd % 2 == 0)
                qparts.append(jnp.where(keep, qp, jnp.zeros_like(qp)))
            qs = jnp.concatenate(qparts, axis=0)
            cols = slice(kvh * LANES, (kvh + 1) * LANES)
            if has_local:
                loc = slice(sb * BLOCK, (sb + 3) * BLOCK)
                kk = jnp.concatenate([kwin[loc, cols], kctx[:, cols]], axis=0)
                vv = jnp.concatenate([vwin[loc, cols], vctx[:, cols]], axis=0)
            else:
                kk, vv = kd_ref[:, cols], vd_ref[:, cols]
            s = lax.dot_general(qs, kk, (((1,), (1,)), ((), ())), preferred_element_type=F32)
            if has_local:
                s = jnp.concatenate([jnp.where(below, s[:, :BLOCK], NEG_INF), s[:, BLOCK:2 * BLOCK],
                                     jnp.where(above, s[:, 2 * BLOCK:3 * BLOCK], NEG_INF), s[:, 3 * BLOCK:]], axis=1)
            sk = jnp.concatenate([jnp.full((BLOCK, 1), sink_ref[0, hd] * log2e, F32) for hd in heads], axis=0)
            m = jnp.maximum(jnp.max(s, axis=1, keepdims=True), sk)
            e = jnp.exp2(s - m).astype(BF16)
            ow = _dot(e, jnp.concatenate([vv, jnp.ones_like(vv)], axis=1))
            o = ow[:, :LANES] / (ow[:, LANES:] + jnp.exp2(sk - m))
            outs += [o[g * BLOCK:(g + 1) * BLOCK, :] for g in range(GQA_GROUP)]
        blks = [jnp.where(lane_q < HEAD_DIM, outs[2 * p], outs[2 * p + 1]) for p in range(N_HEADS // 2)]
        o_ref[sb * BLOCK:(sb + 1) * BLOCK, :] = jnp.concatenate(blks, axis=1).astype(o_ref.dtype)


def _attention(q, kd, vd, sink, batch, seq_len, ctx_k=None, ctx_v=None, layer=0):
    nb = seq_len // BLOCK
    has_local = ctx_k is not None
    sub = ATTN_SUB
    steps = nb // sub
    qb = sub * BLOCK
    qspec = pl.BlockSpec((qb, ATTN_WIDTH), lambda b, i: (b * steps + i, 0))
    sspec = pl.BlockSpec(memory_space=pltpu.SMEM)
    if has_local:
        lc = ctx_k.shape[2]
        edge = lambda f: pl.BlockSpec((BLOCK, 2 * KV_WIDTH), f)
        prev = lambda b, i: (b * nb + jnp.maximum(i * sub - 1, 0), 0)
        nxt = lambda b, i: (b * nb + jnp.minimum((i + 1) * sub, nb - 1), 0)
        cur = pl.BlockSpec((qb, 2 * KV_WIDTH), lambda b, i: (b * steps + i, 0))
        cspec = pl.BlockSpec((None, None, lc, KV_WIDTH), lambda b, i: (b, layer, 0, 0))
        in_specs = [sspec, qspec, edge(prev), cur, edge(nxt), edge(prev), cur, edge(nxt), cspec, cspec]
        args = [sink, q, kd, kd, kd, vd, vd, vd, ctx_k, ctx_v]
    else:
        kv = pl.BlockSpec((seq_len, 2 * KV_WIDTH), lambda b, i: (b, 0))
        in_specs = [sspec, qspec, kv, kv]
        args = [sink, q, kd, vd]
    return pl.pallas_call(
        functools.partial(_attn_kernel, has_local=has_local, nblocks=nb, sub=sub),
        grid=(batch, steps),
        in_specs=in_specs,
        out_specs=pl.BlockSpec((qb, ATTN_WIDTH), lambda b, i: (b * steps + i, 0)),
        out_shape=jax.ShapeDtypeStruct((batch * seq_len, ATTN_WIDTH), BF16),
        compiler_params=_cparams(("arbitrary", "arbitrary")),
        name="attn_latent" if has_local else "attn_context",
    )(*args)


def _filter_kernel(z_ref, w1_ref, b1_ref, w2_ref, b2_ref, w3_ref, b3_ref, fr_ref, dl_ref, k_ref, sum_ref,
                   *, seq_len, rows):
    i = pl.program_id(0)
    z = z_ref[...]
    h = jnp.sin(fr_ref[0:1, :] * (_dot(z.astype(BF16), w1_ref[...].astype(BF16)) + b1_ref[...]))
    h = jnp.sin(fr_ref[1:2, :] * (_dot(h.astype(BF16), w2_ref[...].astype(BF16)) + b2_ref[...]))
    h3 = _dot(h.astype(BF16), w3_ref[...].astype(BF16)) + b3_ref[...]
    oc = HY_ORDER * HY_WIDTH
    row = i * rows + lax.broadcasted_iota(jnp.int32, (rows, oc), 0)
    t = z[:, 0:1]
    decay = jnp.exp(-t * jnp.abs(dl_ref[...]))
    sel = jnp.where(row < seq_len, h3[:, :oc], jnp.where(row > seq_len, h3[:, oc:], 0.0))
    k = sel * (decay + HY_MOD_SHIFT)
    k_ref[...] = k

    @pl.when(i == 0)
    def _():
        sum_ref[...] = jnp.zeros_like(sum_ref)

    sum_ref[...] += jnp.sum(jnp.abs(k), axis=0, keepdims=True)


def _hyena_filter(z_ext, w1p, b1, w2, b2, w3, b3, freq, deltas, seq_len):
    n = 2 * seq_len
    rows = min(1024, n)
    oc = HY_ORDER * HY_WIDTH
    const = lambda i: (0, 0)
    return pl.pallas_call(
        functools.partial(_filter_kernel, seq_len=seq_len, rows=rows),
        grid=(n // rows,),
        in_specs=[
            pl.BlockSpec((rows, LANES), lambda i: (i, 0)),
            pl.BlockSpec((LANES, HY_FILTER_HIDDEN), const),
            pl.BlockSpec((1, HY_FILTER_HIDDEN), const),
            pl.BlockSpec((HY_FILTER_HIDDEN, HY_FILTER_HIDDEN), const),
            pl.BlockSpec((1, HY_FILTER_HIDDEN), const),
            pl.BlockSpec((HY_FILTER_HIDDEN, 2 * oc), const),
            pl.BlockSpec((1, 2 * oc), const),
            pl.BlockSpec((2, HY_FILTER_HIDDEN), const),
            pl.BlockSpec((1, oc), const),
        ],
        out_specs=[pl.BlockSpec((rows, oc), lambda i: (i, 0)), pl.BlockSpec((1, oc), const)],
        out_shape=[jax.ShapeDtypeStruct((n, oc), F32), jax.ShapeDtypeStruct((1, oc), F32)],
        compiler_params=_cparams(("arbitrary",)),
        name="hyena_filter",
    )(z_ext, w1p, b1, w2, b2, w3, b3, freq, deltas)


def _spectrum_kernel(ka_ref, kb_ref, sum_ref, fwd_ref, kf_ref, *, blk):
    den = sum_ref[...] + 1e-6
    a = ka_ref[...] / den
    b = kb_ref[...] / den
    g0 = b[0:1, :]
    ha = _dot(fwd_ref[...], a.astype(BF16))
    hb = _dot(fwd_ref[...], b.astype(BF16))
    tf = FREQ_TILE
    par = lax.broadcasted_iota(jnp.int32, (tf, 1), 0) % 2
    sgn = (1 - 2 * par).astype(F32)
    for c in range(blk // tf):
        re = slice(2 * c * tf, (2 * c + 1) * tf)
        im = slice((2 * c + 1) * tf, (2 * c + 2) * tf)
        kf_ref[re, :] = ha[re, :] - sgn * hb[im, :]
        kf_ref[im, :] = ha[im, :] + sgn * (hb[re, :] - g0)


def _filter_spectrum(kraw, ksum, fwd, seq_len, blk):
    nb = seq_len // blk
    nlags = 2 * nb - 1
    oc = HY_ORDER * HY_WIDTH
    m = 2 * blk
    return pl.pallas_call(
        functools.partial(_spectrum_kernel, blk=blk),
        grid=(nlags,),
        in_specs=[
            pl.BlockSpec((blk, oc), lambda d: ((d + nb + 1) % (2 * nb), 0)),
            pl.BlockSpec((blk, oc), lambda d: ((d + nb) % (2 * nb), 0)),
            pl.BlockSpec((1, oc), lambda d: (0, 0)),
            pl.BlockSpec((m, blk), lambda d: (0, 0)),
        ],
        out_specs=pl.BlockSpec((None, m, oc), lambda d: (d, 0, 0)),
        out_shape=jax.ShapeDtypeStruct((nlags, m, oc), F32),
        compiler_params=_cparams(("arbitrary",)),
        name="filter_spectrum",
    )(kraw, kraw, ksum, fwd)


def _conv_kernel(z_ref, g_ref, swz_ref, sbz_ref, swg_ref, sbg_ref, bd_ref, kf_ref, fwd_ref, inv_ref, o_ref,
                 zb_ref, acc_ref, zf_ref, yf_ref, *, seq_len, blk, conv_z):
    fc = pl.program_id(1)
    nfc = pl.num_programs(1)
    nb = seq_len // blk
    tf = FREQ_TILE
    rows = min(SEQ_CHUNK, seq_len)

    def z_chunk(r0):
        if conv_z:
            return _short_conv_chunk(z_ref, r0, rows, seq_len, swz_ref[...], sbz_ref[...])
        return z_ref[r0:r0 + rows, :]

    @pl.when(fc == 0)
    def _():
        for r0 in range(0, seq_len, rows):
            zb_ref[r0:r0 + rows, :] = z_chunk(r0).astype(BF16)
        acc_ref[...] = jnp.zeros_like(acc_ref)

    for bj in range(nb):
        zf_ref[bj] = _dot(fwd_ref[...], zb_ref[bj * blk:(bj + 1) * blk, :])

    for bi in range(nb):
        for r in range(0, tf, MAC_ROWS):
            re = slice(r, r + MAC_ROWS)
            im = slice(tf + r, tf + r + MAC_ROWS)
            yr = jnp.zeros((MAC_ROWS, HY_WIDTH), F32)
            yi = jnp.zeros((MAC_ROWS, HY_WIDTH), F32)
            for bj in range(nb):
                lag = bi - bj + nb - 1
                kr, ki = kf_ref[lag, re, :], kf_ref[lag, im, :]
                zr, zi = zf_ref[bj, re, :], zf_ref[bj, im, :]
                yr = yr + (kr * zr - ki * zi)
                yi = yi + (kr * zi + ki * zr)
            yf_ref[re, :] = yr.astype(BF16)
            yf_ref[im, :] = yi.astype(BF16)
        acc_ref[bi * blk:(bi + 1) * blk, :] += _dot(inv_ref[...], yf_ref[...])

    @pl.when(fc == nfc - 1)
    def _():
        for r0 in range(0, seq_len, rows):
            z = z_chunk(r0)
            gate = _short_conv_chunk(g_ref, r0, rows, seq_len, swg_ref[...], sbg_ref[...])
            y = gate * (acc_ref[r0:r0 + rows, :] + bd_ref[...] * z)
            o_ref[r0:r0 + rows, :] = y.astype(o_ref.dtype)


def _hyena_conv(z_src, z_col, conv_z, u_hy, g_col, short_w, short_b, bd, kf, order, fwd, inv,
                batch, seq_len, blk, out_dtype):
    nb = seq_len // blk
    nlags = 2 * nb - 1
    tf = FREQ_TILE
    nfc = blk // tf
    w = HY_WIDTH
    return pl.pallas_call(
        functools.partial(_conv_kernel, seq_len=seq_len, blk=blk, conv_z=conv_z),
        grid=(batch, nfc),
        in_specs=[
            pl.BlockSpec((seq_len, w), lambda b, f: (b, z_col)),
            pl.BlockSpec((seq_len, w), lambda b, f: (b, g_col)),
            pl.BlockSpec((3, w), lambda b, f: (0, z_col if conv_z else 0)),
            pl.BlockSpec((1, w), lambda b, f: (0, z_col if conv_z else 0)),
            pl.BlockSpec((3, w), lambda b, f: (0, g_col)),
            pl.BlockSpec((1, w), lambda b, f: (0, g_col)),
            pl.BlockSpec((1, w), lambda b, f: (0, 0)),
            pl.BlockSpec((nlags, 2 * tf, w), lambda b, f: (0, f, order)),
            pl.BlockSpec((2 * tf, blk), lambda b, f: (f, 0)),
            pl.BlockSpec((blk, 2 * tf), lambda b, f: (0, f)),
        ],
        out_specs=pl.BlockSpec((seq_len, w), lambda b, f: (b, 0)),
        out_shape=jax.ShapeDtypeStruct((batch * seq_len, w), out_dtype),
        scratch_shapes=[
            pltpu.VMEM((seq_len, w), BF16),
            pltpu.VMEM((seq_len, w), F32),
            pltpu.VMEM((nb, 2 * tf, w), F32),
            pltpu.VMEM((2 * tf, w), BF16),
        ],
        compiler_params=_cparams(("arbitrary", "arbitrary")),
        name="hyena_conv",
    )(z_src, u_hy, short_w, short_b, short_w, short_b, bd, kf, fwd, inv)


def _dft_tables(blk):
    m = 2 * blk
    tf = FREQ_TILE
    f = jnp.arange(blk, dtype=jnp.int32)
    n = jnp.arange(blk, dtype=jnp.int32)
    r = ((2 * f[:, None] + 1) * n[None, :]) % (2 * m)
    ang = r.astype(F32) * (math.pi / m)
    c, s = jnp.cos(ang), jnp.sin(ang)
    fwd = jnp.stack([c.reshape(blk // tf, tf, blk), (-s).reshape(blk // tf, tf, blk)], axis=1).reshape(m, blk)
    inv = fwd.T * (2.0 / m)
    return fwd.astype(BF16), inv.astype(BF16)


def _filter_embedding(seq_len):
    t = jnp.linspace(0.0, 1.0, seq_len, dtype=F32)[:, None]
    bands = (HY_EMB_DIM - 1) // 2
    f = jnp.linspace(1e-4, bands - 1, bands, dtype=F32)[None, :]
    w = 2.0 * math.pi * jnp.arange(seq_len, dtype=F32)[:, None] / seq_len
    z = jnp.concatenate([t, jnp.cos(f * w), -jnp.sin(f * w)], axis=-1)
    z_ext = jnp.concatenate([z, jnp.zeros((1, HY_EMB_DIM), F32), jnp.flip(z[1:], axis=0)], axis=0)
    return jnp.pad(z_ext, ((0, 0), (0, LANES - HY_EMB_DIM)))


def _rope_tables(seq_len):
    quarter = HEAD_DIM // 4
    inv = ROPE_THETA ** (-jnp.arange(quarter, dtype=F32) / quarter)
    lane = jnp.arange(LANES)
    t = jnp.arange(seq_len)[:, None]
    pos = jnp.where((lane % HEAD_DIM < HEAD_DIM // 2)[None, :], t // GRID_W, t % GRID_W).astype(F32)
    ang = pos * jnp.tile(inv, LANES // quarter)[None, :]
    first = (lane % (2 * quarter) < quarter)[None, :]
    sin = jnp.sin(ang)
    return jnp.cos(ang), jnp.where(first, -sin, 0.0), jnp.where(first, 0.0, sin)


def _block_ones(width):
    h = jnp.arange(width) // HEAD_DIM
    return (h[:, None] == h[None, :]).astype(BF16)


def _stream_layer(x, p, wts, mod_l, batch, seq_len, rope_tabs, ctx, layer, hy_blk, consts):
    gq, gk, fwd, inv, z_ext = consts
    latent = ctx is not None
    outs = _token_a(x, mod_l, p["norm"], wts, layer, gq, gk, p["qg"], p["kg"], rope_tabs, seq_len)
    x1, u_pool, q, kd, vd, u_hy = outs[:6]
    k, v = (None, None) if latent else outs[6:]
    y_pool = _pool_mix(u_pool, p["pool_w"], p["pool_scale"], batch, seq_len)
    if latent:
        a = _attention(q, kd, vd, p["sink"], batch, seq_len, ctx[0], ctx[1], layer)
    else:
        a = _attention(q, kd, vd, p["sink"], batch, seq_len)
    kraw, ksum = _hyena_filter(z_ext, p["f_w1"], p["f_b1"], p["f_w2"], p["f_b2"], p["f_w3"], p["f_b3"],
                               p["f_freq"], p["decay"], seq_len)
    kf = _filter_spectrum(kraw, ksum, fwd, seq_len, hy_blk)
    z1 = _hyena_conv(u_hy, 0, True, u_hy, 1, p["short_w"], p["short_b"], p["hy_bias"][0:1], kf, 0, fwd, inv,
                     batch, seq_len, hy_blk, F32)
    y_hy = _hyena_conv(z1, 0, False, u_hy, 2, p["short_w"], p["short_b"], p["hy_bias"][1:2], kf, 1, fwd, inv,
                       batch, seq_len, hy_blk, BF16)
    x3 = _token_b(x1, y_pool, a, y_hy, mod_l, p["norm"], wts, layer, seq_len, latent)
    return x3, k, v


def kernel(x_prompt, x_sample, cache_k, cache_v, c, c_ctx, ada_w, ada_b, norm_w, ffn_wg, ffn_wu, ffn_wd, w_in, w_out, pool_w, pool_scale, q_norm, k_norm, attn_sink, hy_short_w, hy_short_b, hy_f_w1, hy_f_b1, hy_f_w2, hy_f_b2, hy_f_w3, hy_f_b3, hy_sin_freq, hy_decay, hy_bias):
    bp, lp, _ = x_prompt.shape
    bs, ls, _ = x_sample.shape
    lc = cache_k.shape[2]

    cond = jnp.concatenate([c_ctx[None, :], c, jnp.zeros((COND_ROWS - 1 - bs, D_MODEL), F32)], axis=0)
    mod = _ada_mod(cond, ada_w, ada_b).reshape(DEPTH, COND_ROWS, 1, N_MOD * D_MODEL)

    gq, gk = _block_ones(ATTN_WIDTH), _block_ones(KV_WIDTH)
    rope_tabs = _rope_tables(ls)
    blk_p, blk_s = min(lp, 512), min(ls, 512)
    consts_p = (gq, gk) + _dft_tables(blk_p) + (_filter_embedding(lp),)
    consts_s = (gq, gk) + _dft_tables(blk_s) + (_filter_embedding(ls),)
    ctx_k = cache_k.reshape(bs, DEPTH, lc, KV_WIDTH)
    ctx_v = cache_v.reshape(bs, DEPTH, lc, KV_WIDTH)

    wts = _cast_weights(ffn_wg, ffn_wu, ffn_wd, w_in, w_out)
    yp = x_prompt.reshape(bp * lp, D_MODEL)
    ys = x_sample.reshape(bs * ls, D_MODEL)
    ks, vs = [], []
    eye = jnp.eye(len(POOL_WINDOWS), dtype=F32)
    for l in range(DEPTH):
        pool_bd = (eye[:, None, :, None] * pool_w[l][:, :, None, :]).reshape(POOL_WIDTH, POOL_WIDTH)
        p = {
            "norm": norm_w[l],
            "pool_w": pool_bd.astype(BF16), "pool_scale": pool_scale[l][None, :],
            "qg": jnp.tile(q_norm[l], N_HEADS)[None, :], "kg": jnp.tile(k_norm[l], N_KV_HEADS)[None, :],
            "sink": attn_sink[l][None, :],
            "short_w": hy_short_w[l], "short_b": hy_short_b[l][None, :],
            "f_w1": jnp.pad(hy_f_w1[l], ((0, LANES - HY_EMB_DIM), (0, 0))), "f_b1": hy_f_b1[l][None, :],
            "f_w2": hy_f_w2[l], "f_b2": hy_f_b2[l][None, :],
            "f_w3": hy_f_w3[l], "f_b3": hy_f_b3[l][None, :],
            "f_freq": hy_sin_freq[l], "decay": hy_decay[l].reshape(1, HY_ORDER * HY_WIDTH),
            "hy_bias": hy_bias[l],
        }
        yp, k_l, v_l = _stream_layer(yp, p, wts, mod[l], bp, lp, None, None, l, blk_p, consts_p)
        ks.append(k_l.reshape(bp, lp, N_KV_HEADS, HEAD_DIM))
        vs.append(v_l.reshape(bp, lp, N_KV_HEADS, HEAD_DIM))
        ys, _, _ = _stream_layer(ys, p, wts, mod[l], bs, ls, rope_tabs, (ctx_k, ctx_v), l, blk_s, consts_s)
    return (yp.reshape(bp, lp, D_MODEL), ys.reshape(bs, ls, D_MODEL),
            jnp.stack(ks, axis=1), jnp.stack(vs, axis=1))
```

```python
import functools
import math

import jax
import jax.numpy as jnp
from jax import lax
from jax.experimental import pallas as pl
from jax.experimental.pallas import tpu as pltpu

F32 = jnp.float32
BF16 = jnp.bfloat16

D_MODEL = 1024
DEPTH = 2
GRID_W = 64
POOL_WINDOWS = (2, 4, 8, 16)
POOL_WIDTH = 256
POOL_GROUP = 64
HEAD_DIM = 64
N_HEADS = 8
N_KV_HEADS = 2
GQA_GROUP = 4
ATTN_WIDTH = 512
KV_WIDTH = 128
WINDOW = 128
BLOCK = 128
ROPE_THETA = 10000.0
HY_WIDTH = 256
HY_ORDER = 2
HY_EMB_DIM = 33
HY_FILTER_HIDDEN = 64
HY_MOD_SHIFT = 0.05
D_FF = 2816
IN_WIDTH = 1792
N_MOD = 9
NORM_EPS = 1e-6
NEG_INF = -1e30

LANES = 128
SUBLANES = 8
VMEM_LIMIT = 56 * 1024 * 1024

TOKEN_TILE = 512
MXU_DIM = 256
FF_CHUNKS = (1280, 1536)
SEQ_CHUNK = 512
SEQ_STEP_ROWS = 2048
FREQ_TILE = 128
MAC_ROWS = 32
ATTN_SUB = 4
CTX_SEQS = 2
COND_ROWS = 8


def _cparams(sem):
    return pltpu.CompilerParams(dimension_semantics=sem, vmem_limit_bytes=VMEM_LIMIT)


def _dot(a, b):
    return jnp.dot(a, b, preferred_element_type=F32)


def _cast_kernel(*refs):
    n = len(refs) // 2
    for i_ref, o_ref in zip(refs[:n], refs[n:]):
        o_ref[...] = i_ref[...].astype(o_ref.dtype)


def _cast_weights(ffn_wg, ffn_wu, ffn_wd, w_in, w_out):
    steps = 16
    arrs = {"wg": ffn_wg, "wu": ffn_wu, "wd": ffn_wd, "w_in": w_in, "w_out": w_out}
    flat = [a.reshape(-1, a.shape[-1]) for a in arrs.values()]
    specs = [pl.BlockSpec((a.shape[0] // steps, a.shape[1]), lambda i: (i, 0)) for a in flat]
    outs = pl.pallas_call(
        _cast_kernel,
        grid=(steps,),
        in_specs=specs,
        out_specs=specs,
        out_shape=[jax.ShapeDtypeStruct(a.shape, BF16) for a in flat],
        compiler_params=_cparams(("arbitrary",)),
        name="cast_weights",
    )(*flat)
    return {k: o.reshape(a.shape) for (k, a), o in zip(arrs.items(), outs)}


def _ada_kernel(c_ref, w_ref, b_ref, o_ref):
    c = c_ref[...]
    s = (c * jax.nn.sigmoid(c)).astype(BF16)
    o_ref[...] = _dot(s, w_ref[...].astype(BF16)) + b_ref[...]


def _ada_mod(cond, ada_w, ada_b):
    tn = 3072
    nw = N_MOD * D_MODEL
    return pl.pallas_call(
        _ada_kernel,
        grid=(DEPTH, nw // tn),
        in_specs=[
            pl.BlockSpec((COND_ROWS, D_MODEL), lambda l, j: (0, 0)),
            pl.BlockSpec((None, D_MODEL, tn), lambda l, j: (l, 0, j)),
            pl.BlockSpec((None, 1, tn), lambda l, j: (l, 0, j)),
        ],
        out_specs=pl.BlockSpec((None, COND_ROWS, tn), lambda l, j: (l, 0, j)),
        out_shape=jax.ShapeDtypeStruct((DEPTH, COND_ROWS, nw), F32),
        compiler_params=_cparams(("arbitrary", "arbitrary")),
        name="ada_mod",
    )(cond, ada_w, ada_b.reshape(DEPTH, 1, nw))


def _mod_slice(mod_ref, k):
    return mod_ref[:, k * D_MODEL:(k + 1) * D_MODEL]


def _rms_mod(x, gain, scale, shift):
    y = x * lax.rsqrt(jnp.mean(x * x, axis=-1, keepdims=True) + NORM_EPS)
    return (y * gain) * (1.0 + scale) + shift


def _swiglu(hb, wg_ref, wu_ref, wd_ref):
    acc = None
    lo = 0
    for width in FF_CHUNKS:
        sl = slice(lo, lo + width)
        lo += width
        g = _dot(hb, wg_ref[:, sl])
        u = _dot(hb, wu_ref[:, sl])
        a = ((g * jax.nn.sigmoid(g)) * u).astype(BF16)
        y = _dot(a, wd_ref[sl, :])
        acc = y if acc is None else acc + y
    return acc


def _head_norm(x, gmat, gain):
    ss = _dot((x * x).astype(BF16), gmat)
    return (x * lax.rsqrt(ss * (1.0 / HEAD_DIM) + NORM_EPS)) * gain


def _rope(x, cos, sa, sb):
    w = x.shape[1]
    xn = pltpu.roll(x, w - 16, axis=1)
    xp = pltpu.roll(x, 16, axis=1)
    return x * cos + xn * sa + xp * sb


def _token_a_kernel(*refs, rope):
    if rope:
        (x_ref, mod_ref, nrm_ref, wg_ref, wu_ref, wd_ref, win_ref, gq_ref, gk_ref, qg_ref, kg_ref,
         cos_ref, sa_ref, sb_ref, x1_ref, up_ref, q_ref, kd_ref, vd_ref, uh_ref) = refs
    else:
        (x_ref, mod_ref, nrm_ref, wg_ref, wu_ref, wd_ref, win_ref, gq_ref, gk_ref, qg_ref, kg_ref,
         x1_ref, up_ref, q_ref, kd_ref, vd_ref, uh_ref, k_ref, v_ref) = refs
    x = x_ref[...]
    sh1, sc1, g1 = _mod_slice(mod_ref, 0), _mod_slice(mod_ref, 1), _mod_slice(mod_ref, 2)
    sh2, sc2 = _mod_slice(mod_ref, 3), _mod_slice(mod_ref, 4)
    h = _rms_mod(x, nrm_ref[0:1, :], sc1, sh1).astype(BF16)
    x1 = x + (0.5 * g1) * _swiglu(h, wg_ref, wu_ref, wd_ref)
    x1_ref[...] = x1
    h2 = _rms_mod(x1, nrm_ref[1:2, :], sc2, sh2).astype(BF16)
    u = _dot(h2, win_ref[...])
    s1 = POOL_WIDTH
    s2 = s1 + ATTN_WIDTH
    s3 = s2 + KV_WIDTH
    s4 = s3 + KV_WIDTH
    up_ref[...] = u[:, :s1]
    q = _head_norm(u[:, s1:s2], gq_ref[...], qg_ref[...])
    k = _head_norm(u[:, s2:s3], gk_ref[...], kg_ref[...])
    if rope:
        cos, sa, sb = cos_ref[...], sa_ref[...], sb_ref[...]
        reps = ATTN_WIDTH // LANES
        q = _rope(q, jnp.concatenate([cos] * reps, axis=1), jnp.concatenate([sa] * reps, axis=1),
                  jnp.concatenate([sb] * reps, axis=1))
        k = _rope(k, cos, sa, sb)
    v = u[:, s3:s4]
    q_ref[...] = (q * (HEAD_DIM ** -0.5 * math.log2(math.e))).astype(BF16)
    kd_ref[...] = _dup_heads(k)
    vd_ref[...] = _dup_heads(v)
    uh_ref[...] = u[:, s4:]
    if not rope:
        k_ref[...] = k
        v_ref[...] = v


def _token_a(x, mod_l, nrm, wts, layer, gq, gk, qg, kg, rope_tabs, seq_len):
    n = x.shape[0]
    tm = TOKEN_TILE
    tiles_per_seq = max(seq_len // tm, 1)
    rope = rope_tabs is not None
    if rope:
        mod_row = lambda i: (1 + i // tiles_per_seq, 0, 0)
    else:
        mod_row = lambda i: (0, 0, 0)
    const = lambda i: (0, 0)
    row = lambda i: (i, 0)
    in_specs = [
        pl.BlockSpec((tm, D_MODEL), row),
        pl.BlockSpec((None, 1, N_MOD * D_MODEL), mod_row),
        pl.BlockSpec((3, D_MODEL), const),
        pl.BlockSpec((None, None, D_MODEL, D_FF), lambda i: (layer, 0, 0, 0)),
        pl.BlockSpec((None, None, D_MODEL, D_FF), lambda i: (layer, 0, 0, 0)),
        pl.BlockSpec((None, None, D_FF, D_MODEL), lambda i: (layer, 0, 0, 0)),
        pl.BlockSpec((None, D_MODEL, IN_WIDTH), lambda i: (layer, 0, 0)),
        pl.BlockSpec((ATTN_WIDTH, ATTN_WIDTH), const),
        pl.BlockSpec((KV_WIDTH, KV_WIDTH), const),
        pl.BlockSpec((1, ATTN_WIDTH), const),
        pl.BlockSpec((1, KV_WIDTH), const),
    ]
    args = [x, mod_l, nrm, wts["wg"], wts["wu"], wts["wd"], wts["w_in"], gq, gk, qg, kg]
    if rope:
        tab = pl.BlockSpec((tm, LANES), lambda i: (i % tiles_per_seq, 0))
        in_specs += [tab, tab, tab]
        args += list(rope_tabs)
    widths = (D_MODEL, POOL_WIDTH, ATTN_WIDTH, 2 * KV_WIDTH, 2 * KV_WIDTH, 3 * HY_WIDTH)
    dtypes = (F32, F32, BF16, BF16, BF16, F32)
    if not rope:
        widths += (KV_WIDTH, KV_WIDTH)
        dtypes += (F32, F32)
    return pl.pallas_call(
        functools.partial(_token_a_kernel, rope=rope),
        grid=(n // tm,),
        in_specs=in_specs,
        out_specs=[pl.BlockSpec((tm, w), row) for w in widths],
        out_shape=[jax.ShapeDtypeStruct((n, w), d) for w, d in zip(widths, dtypes)],
        compiler_params=_cparams(("arbitrary",)),
        name="token_a_rope" if rope else "token_a",
    )(*args)


def _token_b_kernel(x_ref, yp_ref, a_ref, yh_ref, mod_ref, nrm_ref, wo_ref, wg_ref, wu_ref, wd_ref, o_ref):
    x1 = x_ref[...]
    g2 = _mod_slice(mod_ref, 5)
    sh3, sc3, g3 = _mod_slice(mod_ref, 6), _mod_slice(mod_ref, 7), _mod_slice(mod_ref, 8)
    cat = jnp.concatenate([yp_ref[...], a_ref[...], yh_ref[...]], axis=1)
    x2 = x1 + g2 * _dot(cat, wo_ref[...])
    h3 = _rms_mod(x2, nrm_ref[2:3, :], sc3, sh3).astype(BF16)
    o_ref[...] = x2 + (0.5 * g3) * _swiglu(h3, wg_ref, wu_ref, wd_ref)


def _token_b(x1, y_pool, a, y_hy, mod_l, nrm, wts, layer, seq_len, per_seq_cond):
    n = x1.shape[0]
    tm = TOKEN_TILE
    tiles_per_seq = max(seq_len // tm, 1)
    if per_seq_cond:
        mod_row = lambda i: (1 + i // tiles_per_seq, 0, 0)
    else:
        mod_row = lambda i: (0, 0, 0)
    const = lambda i: (0, 0)
    row = lambda i: (i, 0)
    return pl.pallas_call(
        _token_b_kernel,
        grid=(n // tm,),
        in_specs=[
            pl.BlockSpec((tm, D_MODEL), row),
            pl.BlockSpec((tm, POOL_WIDTH), row),
            pl.BlockSpec((tm, ATTN_WIDTH), row),
            pl.BlockSpec((tm, HY_WIDTH), row),
            pl.BlockSpec((None, 1, N_MOD * D_MODEL), mod_row),
            pl.BlockSpec((3, D_MODEL), const),
            pl.BlockSpec((None, D_MODEL, D_MODEL), lambda i: (layer, 0, 0)),
            pl.BlockSpec((None, None, D_MODEL, D_FF), lambda i: (layer, 1, 0, 0)),
            pl.BlockSpec((None, None, D_MODEL, D_FF), lambda i: (layer, 1, 0, 0)),
            pl.BlockSpec((None, None, D_FF, D_MODEL), lambda i: (layer, 1, 0, 0)),
        ],
        out_specs=pl.BlockSpec((tm, D_MODEL), row),
        out_shape=jax.ShapeDtypeStruct((n, D_MODEL), F32),
        compiler_params=_cparams(("arbitrary",)),
        name="token_b",
    )(x1, y_pool, a, y_hy, mod_l, nrm, wts["w_out"], wts["wg"], wts["wu"], wts["wd"])


def _seqs_per_step(batch, seq_len):
    return min(batch, max(1, SEQ_STEP_ROWS // seq_len))


def _halo_rows(src_ref, base, r0, rows, seq_len):
    c = src_ref.shape[1]
    zero = jnp.zeros((SUBLANES, c), F32)
    lo = base + r0
    prev = src_ref[lo - SUBLANES:lo, :] if r0 > 0 else zero
    nxt = src_ref[lo + rows:lo + rows + SUBLANES, :] if r0 + rows < seq_len else zero
    return prev, nxt


def _short_conv_chunk(src_ref, base, r0, rows, seq_len, w, b):
    x = src_ref[base + r0:base + r0 + rows, :]
    prev, nxt = _halo_rows(src_ref, base, r0, rows, seq_len)
    ridx = lax.broadcasted_iota(jnp.int32, x.shape, 0)
    xp = jnp.where(ridx == 0, prev[SUBLANES - 1:SUBLANES, :], pltpu.roll(x, 1, axis=0))
    xn = jnp.where(ridx == rows - 1, nxt[0:1, :], pltpu.roll(x, rows - 1, axis=0))
    return xp * w[0:1, :] + x * w[1:2, :] + xn * w[2:3, :] + b


def _pool_kernel(u_ref, w_ref, scale_ref, o_ref, *, seq_len, rows, seqs):
    lane = lax.broadcasted_iota(jnp.int32, (rows, POOL_WIDTH), 1)
    grp = lane // POOL_GROUP
    half = jnp.where(grp == 0, 1, jnp.where(grp == 1, 2, jnp.where(grp == 2, 4, 8)))
    ext = rows + 2 * SUBLANES
    for base, r0 in [(sq * seq_len, r0) for sq in range(seqs) for r0 in range(0, seq_len, rows)]:
        x = u_ref[base + r0:base + r0 + rows, :]
        prev, nxt = _halo_rows(u_ref, base, r0, rows, seq_len)
        a = jnp.concatenate([prev, x, nxt], axis=0)
        back = lambda v, s: pltpu.roll(v, s, axis=0)
        fwd = lambda v, s: pltpu.roll(v, ext - s, axis=0)
        b1 = back(a, 1)
        b2 = b1 + back(b1, 1)
        b4 = b2 + back(b2, 2)
        b8 = b4 + back(b4, 4)
        f2 = a + fwd(a, 1)
        f4 = f2 + fwd(f2, 2)
        f8 = f4 + fwd(f4, 4)
        core = lambda v: v[SUBLANES:SUBLANES + rows, :]
        wsum = jnp.where(grp == 0, core(b1) + x,
                         jnp.where(grp == 1, core(b2) + core(f2),
                                   jnp.where(grp == 2, core(b4) + core(f4), core(b8) + core(f8))))
        t = r0 + lax.broadcasted_iota(jnp.int32, (rows, POOL_WIDTH), 0)
        cnt = jnp.minimum(t + half, seq_len) - jnp.maximum(t - half, 0)
        d = wsum / cnt.astype(F32) - x
        y = _dot(d.astype(BF16), w_ref[...]) * scale_ref[...]
        o_ref[base + r0:base + r0 + rows, :] = y.astype(o_ref.dtype)


def _pool_mix(u_pool, w_bd, scale, batch, seq_len):
    rows = min(SEQ_CHUNK // 2, seq_len)
    seqs = _seqs_per_step(batch, seq_len)
    return pl.pallas_call(
        functools.partial(_pool_kernel, seq_len=seq_len, rows=rows, seqs=seqs),
        grid=(batch // seqs,),
        in_specs=[
            pl.BlockSpec((seqs * seq_len, POOL_WIDTH), lambda b: (b, 0)),
            pl.BlockSpec((POOL_WIDTH, POOL_WIDTH), lambda b: (0, 0)),
            pl.BlockSpec((1, POOL_WIDTH), lambda b: (0, 0)),
        ],
        out_specs=pl.BlockSpec((seqs * seq_len, POOL_WIDTH), lambda b: (b, 0)),
        out_shape=jax.ShapeDtypeStruct((batch * seq_len, POOL_WIDTH), BF16),
        compiler_params=_cparams(("arbitrary",)),
        name="pool_mix",
    )(u_pool, w_bd, scale)


def _dup_heads(x):
    lane = lax.broadcasted_iota(jnp.int32, x.shape, 1)
    sw = pltpu.roll(x, HEAD_DIM, axis=1)
    lo = lane < HEAD_DIM
    return jnp.concatenate([jnp.where(lo, x, sw), jnp.where(lo, sw, x)], axis=1).astype(BF16)


def _attn_kernel(*refs, has_local, nblocks, sub, seqs):
    if has_local:
        sink_ref, q_ref, kp_ref, kc_ref, kn_ref, vp_ref, vc_ref, vn_ref, ck_ref, cv_ref, o_ref = refs
        kwin = jnp.concatenate([kp_ref[...], kc_ref[...], kn_ref[...]], axis=0)
        vwin = jnp.concatenate([vp_ref[...], vc_ref[...], vn_ref[...]], axis=0)
        kctx, vctx = _dup_heads(ck_ref[...]), _dup_heads(cv_ref[...])
        r = lax.broadcasted_iota(jnp.int32, (GQA_GROUP * BLOCK, BLOCK), 0) % BLOCK
        j = lax.broadcasted_iota(jnp.int32, (GQA_GROUP * BLOCK, BLOCK), 1)
    else:
        sink_ref, q_ref, kd_ref, vd_ref, o_ref = refs
    i = pl.program_id(1)
    lane_q = lax.broadcasted_iota(jnp.int32, (BLOCK, LANES), 1)
    log2e = math.log2(math.e)
    units = [(sb, kvh) for sb in range(seqs * sub) for kvh in range(N_KV_HEADS)]
    scores, values = [], []
    for sb, kvh in units:
        q = q_ref[sb * BLOCK:(sb + 1) * BLOCK, :]
        qparts = []
        for hd in range(kvh * GQA_GROUP, (kvh + 1) * GQA_GROUP):
            qp = q[:, (hd // 2) * LANES:(hd // 2 + 1) * LANES]
            keep = (lane_q < HEAD_DIM) == (hd % 2 == 0)
            qparts.append(jnp.where(keep, qp, jnp.zeros_like(qp)))
        qs = jnp.concatenate(qparts, axis=0)
        cols = slice(kvh * LANES, (kvh + 1) * LANES)
        if has_local:
            loc = slice(sb * BLOCK, (sb + 3) * BLOCK)
            kk = jnp.concatenate([kwin[loc, cols], kctx[:, cols]], axis=0)
            vv = jnp.concatenate([vwin[loc, cols], vctx[:, cols]], axis=0)
        else:
            own = slice((sb // sub) * sub * BLOCK, (sb // sub + 1) * sub * BLOCK)
            kk, vv = kd_ref[own, cols], vd_ref[own, cols]
        s = lax.dot_general(qs, kk, (((1,), (1,)), ((), ())), preferred_element_type=F32)
        if has_local:
            gb = i * sub + sb
            below = j >= r + jnp.where(gb >= 1, 0, BLOCK)
            above = j <= r - jnp.where(gb <= nblocks - 2, 0, BLOCK)
            s = jnp.concatenate([jnp.where(below, s[:, :BLOCK], NEG_INF), s[:, BLOCK:2 * BLOCK],
                                 jnp.where(above, s[:, 2 * BLOCK:3 * BLOCK], NEG_INF), s[:, 3 * BLOCK:]], axis=1)
        scores.append(s)
        values.append(jnp.concatenate([vv, jnp.ones_like(vv)], axis=1))

    def per_head(kvh, col, f):
        parts = [f(col[g * BLOCK:(g + 1) * BLOCK, :], sink_ref[0, kvh * GQA_GROUP + g] * log2e) for g in range(GQA_GROUP)]
        return jnp.concatenate(parts, axis=0)

    maxes = [per_head(kvh, jnp.max(s, axis=1, keepdims=True), jnp.maximum) for (_, kvh), s in zip(units, scores)]
    probs = [jnp.exp2(s - m).astype(BF16) for s, m in zip(scores, maxes)]
    sums = [_dot(e, v) for e, v in zip(probs, values)]
    outs = []
    for (_, kvh), ow, m in zip(units, sums, maxes):
        den = ow[:, LANES:] + per_head(kvh, m, lambda mm, sk: jnp.exp2(sk - mm))
        outs.append(ow[:, :LANES] / den)
    for sb in range(seqs * sub):
        heads = [outs[sb * N_KV_HEADS + kvh][g * BLOCK:(g + 1) * BLOCK, :]
                 for kvh in range(N_KV_HEADS) for g in range(GQA_GROUP)]
        blks = [jnp.where(lane_q < HEAD_DIM, heads[2 * p], heads[2 * p + 1]) for p in range(N_HEADS // 2)]
        o_ref[sb * BLOCK:(sb + 1) * BLOCK, :] = jnp.concatenate(blks, axis=1).astype(o_ref.dtype)


def _attention(q, kd, vd, sink, batch, seq_len, ctx_k=None, ctx_v=None, layer=0):
    nb = seq_len // BLOCK
    has_local = ctx_k is not None
    sub = min(ATTN_SUB, nb)
    steps = nb // sub
    seqs = 1
    if not has_local:
        assert steps == 1 and batch % CTX_SEQS == 0
        seqs = CTX_SEQS
        batch //= seqs
    qb = seqs * sub * BLOCK
    qspec = pl.BlockSpec((qb, ATTN_WIDTH), lambda b, i: (b * steps + i, 0))
    sspec = pl.BlockSpec(memory_space=pltpu.SMEM)
    if has_local:
        lc = ctx_k.shape[2]
        edge = lambda f: pl.BlockSpec((BLOCK, 2 * KV_WIDTH), f)
        prev = lambda b, i: (b * nb + jnp.maximum(i * sub - 1, 0), 0)
        nxt = lambda b, i: (b * nb + jnp.minimum((i + 1) * sub, nb - 1), 0)
        cur = pl.BlockSpec((qb, 2 * KV_WIDTH), lambda b, i: (b * steps + i, 0))
        cspec = pl.BlockSpec((None, None, lc, KV_WIDTH), lambda b, i: (b, layer, 0, 0))
        in_specs = [sspec, qspec, edge(prev), cur, edge(nxt), edge(prev), cur, edge(nxt), cspec, cspec]
        args = [sink, q, kd, kd, kd, vd, vd, vd, ctx_k, ctx_v]
    else:
        kv = pl.BlockSpec((qb, 2 * KV_WIDTH), lambda b, i: (b, 0))
        in_specs = [sspec, qspec, kv, kv]
        args = [sink, q, kd, vd]
    return pl.pallas_call(
        functools.partial(_attn_kernel, has_local=has_local, nblocks=nb, sub=sub, seqs=seqs),
        grid=(batch, steps),
        in_specs=in_specs,
        out_specs=pl.BlockSpec((qb, ATTN_WIDTH), lambda b, i: (b * steps + i, 0)),
        out_shape=jax.ShapeDtypeStruct(q.shape, BF16),
        compiler_params=_cparams(("arbitrary", "arbitrary")),
        name="attn_latent" if has_local else "attn_context",
    )(*args)


def _filter_kernel(z_ref, w1_ref, b1_ref, w2_ref, b2_ref, w3_ref, b3_ref, fr_ref, dl_ref, k_ref, sum_ref,
                   *, seq_len, rows):
    i = pl.program_id(0)
    z = z_ref[...]
    h = jnp.sin(fr_ref[0:1, :] * (_dot(z.astype(BF16), w1_ref[...].astype(BF16)) + b1_ref[...]))
    h = jnp.sin(fr_ref[1:2, :] * (_dot(h.astype(BF16), w2_ref[...].astype(BF16)) + b2_ref[...]))
    h3 = _dot(h.astype(BF16), w3_ref[...].astype(BF16)) + b3_ref[...]
    oc = HY_ORDER * HY_WIDTH
    row = i * rows + lax.broadcasted_iota(jnp.int32, (rows, oc), 0)
    t = z[:, 0:1]
    decay = jnp.exp(-t * jnp.abs(dl_ref[...]))
    sel = jnp.where(row < seq_len, h3[:, :oc], jnp.where(row > seq_len, h3[:, oc:], 0.0))
    k = sel * (decay + HY_MOD_SHIFT)
    k_ref[...] = k

    @pl.when(i == 0)
    def _():
        sum_ref[...] = jnp.zeros_like(sum_ref)

    sum_ref[...] += jnp.sum(jnp.abs(k), axis=0, keepdims=True)


def _hyena_filter(z_ext, w1p, b1, w2, b2, w3, b3, freq, deltas, seq_len):
    n = 2 * seq_len
    rows = min(1024, n)
    oc = HY_ORDER * HY_WIDTH
    const = lambda i: (0, 0)
    return pl.pallas_call(
        functools.partial(_filter_kernel, seq_len=seq_len, rows=rows),
        grid=(n // rows,),
        in_specs=[
            pl.BlockSpec((rows, LANES), lambda i: (i, 0)),
            pl.BlockSpec((LANES, HY_FILTER_HIDDEN), const),
            pl.BlockSpec((1, HY_FILTER_HIDDEN), const),
            pl.BlockSpec((HY_FILTER_HIDDEN, HY_FILTER_HIDDEN), const),
            pl.BlockSpec((1, HY_FILTER_HIDDEN), const),
            pl.BlockSpec((HY_FILTER_HIDDEN, 2 * oc), const),
            pl.BlockSpec((1, 2 * oc), const),
            pl.BlockSpec((2, HY_FILTER_HIDDEN), const),
            pl.BlockSpec((1, oc), const),
        ],
        out_specs=[pl.BlockSpec((rows, oc), lambda i: (i, 0)), pl.BlockSpec((1, oc), const)],
        out_shape=[jax.ShapeDtypeStruct((n, oc), F32), jax.ShapeDtypeStruct((1, oc), F32)],
        compiler_params=_cparams(("arbitrary",)),
        name="hyena_filter",
    )(z_ext, w1p, b1, w2, b2, w3, b3, freq, deltas)


def _spectrum_kernel(ka_ref, kb_ref, sum_ref, fwd_ref, kf_ref, *, blk):
    den = sum_ref[...] + 1e-6
    a = ka_ref[...] / den
    b = kb_ref[...] / den
    g0 = b[0:1, :]
    ha = _dot(fwd_ref[...], a.astype(BF16))
    hb = _dot(fwd_ref[...], b.astype(BF16))
    tf = FREQ_TILE
    par = lax.broadcasted_iota(jnp.int32, (tf, 1), 0) % 2
    sgn = (1 - 2 * par).astype(F32)
    for c in range(blk // tf):
        re = slice(2 * c * tf, (2 * c + 1) * tf)
        im = slice((2 * c + 1) * tf, (2 * c + 2) * tf)
        kf_ref[re, :] = ha[re, :] - sgn * hb[im, :]
        kf_ref[im, :] = ha[im, :] + sgn * (hb[re, :] - g0)


def _filter_spectrum(kraw, ksum, fwd, seq_len, blk):
    nb = seq_len // blk
    nlags = 2 * nb - 1
    oc = HY_ORDER * HY_WIDTH
    m = 2 * blk
    return pl.pallas_call(
        functools.partial(_spectrum_kernel, blk=blk),
        grid=(nlags,),
        in_specs=[
            pl.BlockSpec((blk, oc), lambda d: ((d + nb + 1) % (2 * nb), 0)),
            pl.BlockSpec((blk, oc), lambda d: ((d + nb) % (2 * nb), 0)),
            pl.BlockSpec((1, oc), lambda d: (0, 0)),
            pl.BlockSpec((m, blk), lambda d: (0, 0)),
        ],
        out_specs=pl.BlockSpec((None, m, oc), lambda d: (d, 0, 0)),
        out_shape=jax.ShapeDtypeStruct((nlags, m, oc), F32),
        compiler_params=_cparams(("arbitrary",)),
        name="filter_spectrum",
    )(kraw, kraw, ksum, fwd)


def _conv_kernel(z_ref, g_ref, swz_ref, sbz_ref, swg_ref, sbg_ref, bd_ref, kf_ref, fwd_ref, inv_ref, o_ref,
                 zb_ref, acc_ref, zf_ref, yf_ref, *, seq_len, blk, conv_z, seqs):
    fc = pl.program_id(1)
    nfc = pl.num_programs(1)
    nb = seq_len // blk
    tf = FREQ_TILE
    rows = min(SEQ_CHUNK, seq_len)
    chunks = [(sq * seq_len, r0) for sq in range(seqs) for r0 in range(0, seq_len, rows)]

    def z_chunk(base, r0):
        if conv_z:
            return _short_conv_chunk(z_ref, base, r0, rows, seq_len, swz_ref[...], sbz_ref[...])
        return z_ref[base + r0:base + r0 + rows, :]

    @pl.when(fc == 0)
    def _():
        for base, r0 in chunks:
            zb_ref[base + r0:base + r0 + rows, :] = z_chunk(base, r0).astype(BF16)
        acc_ref[...] = jnp.zeros_like(acc_ref)

    for g in range(seqs * nb):
        zf_ref[g] = _dot(fwd_ref[...], zb_ref[g * blk:(g + 1) * blk, :])

    for g0 in range(0, seqs * nb, nb):
        for bi in range(nb):
            for r in range(0, tf, MAC_ROWS):
                re = slice(r, r + MAC_ROWS)
                im = slice(tf + r, tf + r + MAC_ROWS)
                yr = jnp.zeros((MAC_ROWS, HY_WIDTH), F32)
                yi = jnp.zeros((MAC_ROWS, HY_WIDTH), F32)
                for bj in range(nb):
                    lag = bi - bj + nb - 1
                    kr, ki = kf_ref[lag, re, :], kf_ref[lag, im, :]
                    zr, zi = zf_ref[g0 + bj, re, :], zf_ref[g0 + bj, im, :]
                    yr = yr + (kr * zr - ki * zi)
                    yi = yi + (kr * zi + ki * zr)
                yf_ref[re, :] = yr.astype(BF16)
                yf_ref[im, :] = yi.astype(BF16)
            acc_ref[(g0 + bi) * blk:(g0 + bi + 1) * blk, :] += _dot(inv_ref[...], yf_ref[...])

    @pl.when(fc == nfc - 1)
    def _():
        for base, r0 in chunks:
            z = z_chunk(base, r0)
            gate = _short_conv_chunk(g_ref, base, r0, rows, seq_len, swg_ref[...], sbg_ref[...])
            y = gate * (acc_ref[base + r0:base + r0 + rows, :] + bd_ref[...] * z)
            o_ref[base + r0:base + r0 + rows, :] = y.astype(o_ref.dtype)


def _hyena_conv(z_src, z_col, conv_z, u_hy, g_col, short_w, short_b, bd, kf, order, fwd, inv,
                batch, seq_len, blk, out_dtype):
    nb = seq_len // blk
    nlags = 2 * nb - 1
    tf = FREQ_TILE
    nfc = blk // tf
    w = HY_WIDTH
    seqs = _seqs_per_step(batch, seq_len)
    step_rows = seqs * seq_len
    return pl.pallas_call(
        functools.partial(_conv_kernel, seq_len=seq_len, blk=blk, conv_z=conv_z, seqs=seqs),
        grid=(batch // seqs, nfc),
        in_specs=[
            pl.BlockSpec((step_rows, w), lambda b, f: (b, z_col)),
            pl.BlockSpec((step_rows, w), lambda b, f: (b, g_col)),
            pl.BlockSpec((3, w), lambda b, f: (0, z_col if conv_z else 0)),
            pl.BlockSpec((1, w), lambda b, f: (0, z_col if conv_z else 0)),
            pl.BlockSpec((3, w), lambda b, f: (0, g_col)),
            pl.BlockSpec((1, w), lambda b, f: (0, g_col)),
            pl.BlockSpec((1, w), lambda b, f: (0, 0)),
            pl.BlockSpec((nlags, 2 * tf, w), lambda b, f: (0, f, order)),
            pl.BlockSpec((2 * tf, blk), lambda b, f: (f, 0)),
            pl.BlockSpec((blk, 2 * tf), lambda b, f: (0, f)),
        ],
        out_specs=pl.BlockSpec((step_rows, w), lambda b, f: (b, 0)),
        out_shape=jax.ShapeDtypeStruct((batch * seq_len, w), out_dtype),
        scratch_shapes=[
            pltpu.VMEM((step_rows, w), BF16),
            pltpu.VMEM((step_rows, w), F32),
            pltpu.VMEM((seqs * nb, 2 * tf, w), F32),
            pltpu.VMEM((2 * tf, w), BF16),
        ],
        compiler_params=_cparams(("arbitrary", "arbitrary")),
        name="hyena_conv",
    )(z_src, u_hy, short_w, short_b, short_w, short_b, bd, kf, fwd, inv)


def _dft_tables(blk):
    m = 2 * blk
    tf = FREQ_TILE
    f = jnp.arange(blk, dtype=jnp.int32)
    n = jnp.arange(blk, dtype=jnp.int32)
    r = ((2 * f[:, None] + 1) * n[None, :]) % (2 * m)
    ang = r.astype(F32) * (math.pi / m)
    c, s = jnp.cos(ang), jnp.sin(ang)
    fwd = jnp.stack([c.reshape(blk // tf, tf, blk), (-s).reshape(blk // tf, tf, blk)], axis=1).reshape(m, blk)
    inv = fwd.T * (2.0 / m)
    return fwd.astype(BF16), inv.astype(BF16)


def _filter_embedding(seq_len):
    t = jnp.linspace(0.0, 1.0, seq_len, dtype=F32)[:, None]
    bands = (HY_EMB_DIM - 1) // 2
    f = jnp.linspace(1e-4, bands - 1, bands, dtype=F32)[None, :]
    w = 2.0 * math.pi * jnp.arange(seq_len, dtype=F32)[:, None] / seq_len
    z = jnp.concatenate([t, jnp.cos(f * w), -jnp.sin(f * w)], axis=-1)
    z_ext = jnp.concatenate([z, jnp.zeros((1, HY_EMB_DIM), F32), jnp.flip(z[1:], axis=0)], axis=0)
    return jnp.pad(z_ext, ((0, 0), (0, LANES - HY_EMB_DIM)))


def _rope_tables(seq_len):
    quarter = HEAD_DIM // 4
    inv = ROPE_THETA ** (-jnp.arange(quarter, dtype=F32) / quarter)
    lane = jnp.arange(LANES)
    t = jnp.arange(seq_len)[:, None]
    pos = jnp.where((lane % HEAD_DIM < HEAD_DIM // 2)[None, :], t // GRID_W, t % GRID_W).astype(F32)
    ang = pos * jnp.tile(inv, LANES // quarter)[None, :]
    first = (lane % (2 * quarter) < quarter)[None, :]
    sin = jnp.sin(ang)
    return jnp.cos(ang), jnp.where(first, -sin, 0.0), jnp.where(first, 0.0, sin)


def _block_ones(width):
    h = jnp.arange(width) // HEAD_DIM
    return (h[:, None] == h[None, :]).astype(BF16)


def _stream_layer(x, p, wts, mod_l, batch, seq_len, rope_tabs, ctx, layer, hy_blk, consts):
    gq, gk, fwd, inv, z_ext = consts
    latent = ctx is not None
    outs = _token_a(x, mod_l, p["norm"], wts, layer, gq, gk, p["qg"], p["kg"], rope_tabs, seq_len)
    x1, u_pool, q, kd, vd, u_hy = outs[:6]
    k, v = (None, None) if latent else outs[6:]
    y_pool = _pool_mix(u_pool, p["pool_w"], p["pool_scale"], batch, seq_len)
    if latent:
        a = _attention(q, kd, vd, p["sink"], batch, seq_len, ctx[0], ctx[1], layer)
    else:
        a = _attention(q, kd, vd, p["sink"], batch, seq_len)
    kraw, ksum = _hyena_filter(z_ext, p["f_w1"], p["f_b1"], p["f_w2"], p["f_b2"], p["f_w3"], p["f_b3"],
                               p["f_freq"], p["decay"], seq_len)
    kf = _filter_spectrum(kraw, ksum, fwd, seq_len, hy_blk)
    z1 = _hyena_conv(u_hy, 0, True, u_hy, 1, p["short_w"], p["short_b"], p["hy_bias"][0:1], kf, 0, fwd, inv,
                     batch, seq_len, hy_blk, F32)
    y_hy = _hyena_conv(z1, 0, False, u_hy, 2, p["short_w"], p["short_b"], p["hy_bias"][1:2], kf, 1, fwd, inv,
                       batch, seq_len, hy_blk, BF16)
    x3 = _token_b(x1, y_pool, a, y_hy, mod_l, p["norm"], wts, layer, seq_len, latent)
    return x3, k, v


def kernel(x_prompt, x_sample, cache_k, cache_v, c, c_ctx, ada_w, ada_b, norm_w, ffn_wg, ffn_wu, ffn_wd, w_in, w_out, pool_w, pool_scale, q_norm, k_norm, attn_sink, hy_short_w, hy_short_b, hy_f_w1, hy_f_b1, hy_f_w2, hy_f_b2, hy_f_w3, hy_f_b3, hy_sin_freq, hy_decay, hy_bias):
    bp, lp, _ = x_prompt.shape
    bs, ls, _ = x_sample.shape
    lc = cache_k.shape[2]

    cond = jnp.concatenate([c_ctx[None, :], c, jnp.zeros((COND_ROWS - 1 - bs, D_MODEL), F32)], axis=0)
    mod = _ada_mod(cond, ada_w, ada_b).reshape(DEPTH, COND_ROWS, 1, N_MOD * D_MODEL)

    gq, gk = _block_ones(ATTN_WIDTH), _block_ones(KV_WIDTH)
    rope_tabs = _rope_tables(ls)
    blk_p, blk_s = min(lp, 512), min(ls, 512)
    consts_p = (gq, gk) + _dft_tables(blk_p) + (_filter_embedding(lp),)
    consts_s = (gq, gk) + _dft_tables(blk_s) + (_filter_embedding(ls),)
    ctx_k = cache_k.reshape(bs, DEPTH, lc, KV_WIDTH)
    ctx_v = cache_v.reshape(bs, DEPTH, lc, KV_WIDTH)

    wts = _cast_weights(ffn_wg, ffn_wu, ffn_wd, w_in, w_out)
    yp = x_prompt.reshape(bp * lp, D_MODEL)
    ys = x_sample.reshape(bs * ls, D_MODEL)
    ks, vs = [], []
    eye = jnp.eye(len(POOL_WINDOWS), dtype=F32)
    for l in range(DEPTH):
        pool_bd = (eye[:, None, :, None] * pool_w[l][:, :, None, :]).reshape(POOL_WIDTH, POOL_WIDTH)
        p = {
            "norm": norm_w[l],
            "pool_w": pool_bd.astype(BF16), "pool_scale": pool_scale[l][None, :],
            "qg": jnp.tile(q_norm[l], N_HEADS)[None, :], "kg": jnp.tile(k_norm[l], N_KV_HEADS)[None, :],
            "sink": attn_sink[l][None, :],
            "short_w": hy_short_w[l], "short_b": hy_short_b[l][None, :],
            "f_w1": jnp.pad(hy_f_w1[l], ((0, LANES - HY_EMB_DIM), (0, 0))), "f_b1": hy_f_b1[l][None, :],
            "f_w2": hy_f_w2[l], "f_b2": hy_f_b2[l][None, :],
            "f_w3": hy_f_w3[l], "f_b3": hy_f_b3[l][None, :],
            "f_freq": hy_sin_freq[l], "decay": hy_decay[l].reshape(1, HY_ORDER * HY_WIDTH),
            "hy_bias": hy_bias[l],
        }
        yp, k_l, v_l = _stream_layer(yp, p, wts, mod[l], bp, lp, None, None, l, blk_p, consts_p)
        ks.append(k_l.reshape(bp, lp, N_KV_HEADS, HEAD_DIM))
        vs.append(v_l.reshape(bp, lp, N_KV_HEADS, HEAD_DIM))
        ys, _, _ = _stream_layer(ys, p, wts, mod[l], bs, ls, rope_tabs, (ctx_k, ctx_v), l, blk_s, consts_s)
    return (yp.reshape(bp, lp, D_MODEL), ys.reshape(bs, ls, D_MODEL),
            jnp.stack(ks, axis=1), jnp.stack(vs, axis=1))
```

```python
import functools
import math

import jax
import jax.numpy as jnp
from jax import lax
from jax.experimental import pallas as pl
from jax.experimental.pallas import tpu as pltpu

F32 = jnp.float32
BF16 = jnp.bfloat16

D_MODEL = 1024
DEPTH = 2
GRID_W = 64
POOL_WINDOWS = (2, 4, 8, 16)
POOL_WIDTH = 256
POOL_GROUP = 64
HEAD_DIM = 64
N_HEADS = 8
N_KV_HEADS = 2
GQA_GROUP = 4
ATTN_WIDTH = 512
KV_WIDTH = 128
WINDOW = 128
BLOCK = 128
ROPE_THETA = 10000.0
HY_WIDTH = 256
HY_ORDER = 2
HY_EMB_DIM = 33
HY_FILTER_HIDDEN = 64
HY_MOD_SHIFT = 0.05
D_FF = 2816
IN_WIDTH = 1792
N_MOD = 9
NORM_EPS = 1e-6
NEG_INF = -1e30

LANES = 128
SUBLANES = 8
VMEM_LIMIT = 56 * 1024 * 1024

TOKEN_TILE = 512
MXU_DIM = 256
FF_CHUNKS = (1280, 1536)
SEQ_CHUNK = 512
SEQ_STEP_ROWS = 2048
FREQ_TILE = 128
MAC_ROWS = 32
ATTN_SUB = 4
CTX_SEQS = 2
COND_ROWS = 8


def _cparams(sem):
    return pltpu.CompilerParams(dimension_semantics=sem, vmem_limit_bytes=VMEM_LIMIT)


def _dot(a, b):
    return jnp.dot(a, b, preferred_element_type=F32)


def _cast_kernel(*refs):
    n = len(refs) // 2
    for i_ref, o_ref in zip(refs[:n], refs[n:]):
        o_ref[...] = i_ref[...].astype(o_ref.dtype)


def _cast_weights(ffn_wg, ffn_wu, ffn_wd, w_in, w_out):
    steps = 16
    arrs = {"wg": ffn_wg, "wu": ffn_wu, "wd": ffn_wd, "w_in": w_in, "w_out": w_out}
    flat = [a.reshape(-1, a.shape[-1]) for a in arrs.values()]
    specs = [pl.BlockSpec((a.shape[0] // steps, a.shape[1]), lambda i: (i, 0)) for a in flat]
    outs = pl.pallas_call(
        _cast_kernel,
        grid=(steps,),
        in_specs=specs,
        out_specs=specs,
        out_shape=[jax.ShapeDtypeStruct(a.shape, BF16) for a in flat],
        compiler_params=_cparams(("arbitrary",)),
        name="cast_weights",
    )(*flat)
    return {k: o.reshape(a.shape) for (k, a), o in zip(arrs.items(), outs)}


def _ada_kernel(c_ref, w_ref, b_ref, o_ref):
    c = c_ref[...]
    s = (c * jax.nn.sigmoid(c)).astype(BF16)
    o_ref[...] = _dot(s, w_ref[...].astype(BF16)) + b_ref[...]


def _ada_mod(cond, ada_w, ada_b):
    tn = 3072
    nw = N_MOD * D_MODEL
    return pl.pallas_call(
        _ada_kernel,
        grid=(DEPTH, nw // tn),
        in_specs=[
            pl.BlockSpec((COND_ROWS, D_MODEL), lambda l, j: (0, 0)),
            pl.BlockSpec((None, D_MODEL, tn), lambda l, j: (l, 0, j)),
            pl.BlockSpec((None, 1, tn), lambda l, j: (l, 0, j)),
        ],
        out_specs=pl.BlockSpec((None, COND_ROWS, tn), lambda l, j: (l, 0, j)),
        out_shape=jax.ShapeDtypeStruct((DEPTH, COND_ROWS, nw), F32),
        compiler_params=_cparams(("arbitrary", "arbitrary")),
        name="ada_mod",
    )(cond, ada_w, ada_b.reshape(DEPTH, 1, nw))


def _mod_slice(mod_ref, k):
    return mod_ref[:, k * D_MODEL:(k + 1) * D_MODEL]


def _rms_mod(x, gain, scale, shift):
    y = x * lax.rsqrt(jnp.mean(x * x, axis=-1, keepdims=True) + NORM_EPS)
    return (y * gain) * (1.0 + scale) + shift


def _swiglu(hb, wg_ref, wu_ref, wd_ref):
    acc = None
    lo = 0
    for width in FF_CHUNKS:
        sl = slice(lo, lo + width)
        lo += width
        g = _dot(hb, wg_ref[:, sl])
        u = _dot(hb, wu_ref[:, sl])
        a = ((g * jax.nn.sigmoid(g)) * u).astype(BF16)
        y = _dot(a, wd_ref[sl, :])
        acc = y if acc is None else acc + y
    return acc


def _head_norm(x, gmat, gain):
    ss = _dot((x * x).astype(BF16), gmat)
    return (x * lax.rsqrt(ss * (1.0 / HEAD_DIM) + NORM_EPS)) * gain


def _rope(x, cos, sa, sb):
    w = x.shape[1]
    xn = pltpu.roll(x, w - 16, axis=1)
    xp = pltpu.roll(x, 16, axis=1)
    return x * cos + xn * sa + xp * sb


def _token_a_kernel(*refs, rope):
    if rope:
        (x_ref, mod_ref, nrm_ref, wg_ref, wu_ref, wd_ref, win_ref, gq_ref, gk_ref, qg_ref, kg_ref,
         cos_ref, sa_ref, sb_ref, x1_ref, up_ref, q_ref, kd_ref, vd_ref, uh_ref) = refs
    else:
        (x_ref, mod_ref, nrm_ref, wg_ref, wu_ref, wd_ref, win_ref, gq_ref, gk_ref, qg_ref, kg_ref,
         x1_ref, up_ref, q_ref, kd_ref, vd_ref, uh_ref, k_ref, v_ref) = refs
    x = x_ref[...]
    sh1, sc1, g1 = _mod_slice(mod_ref, 0), _mod_slice(mod_ref, 1), _mod_slice(mod_ref, 2)
    sh2, sc2 = _mod_slice(mod_ref, 3), _mod_slice(mod_ref, 4)
    h = _rms_mod(x, nrm_ref[0:1, :], sc1, sh1).astype(BF16)
    x1 = x + (0.5 * g1) * _swiglu(h, wg_ref, wu_ref, wd_ref)
    x1_ref[...] = x1
    h2 = _rms_mod(x1, nrm_ref[1:2, :], sc2, sh2).astype(BF16)
    s1 = POOL_WIDTH
    s2 = s1 + ATTN_WIDTH
    s3 = s2 + KV_WIDTH
    s4 = s3 + KV_WIDTH
    qkv = _dot(h2, win_ref[:, s1:s4])
    q = _head_norm(qkv[:, :ATTN_WIDTH], gq_ref[...], qg_ref[...])
    k = _head_norm(qkv[:, ATTN_WIDTH:ATTN_WIDTH + KV_WIDTH], gk_ref[...], kg_ref[...])
    v = qkv[:, ATTN_WIDTH + KV_WIDTH:]
    up_ref[...] = _dot(h2, win_ref[:, :s1])
    uh_ref[...] = _dot(h2, win_ref[:, s4:])
    if rope:
        cos, sa, sb = cos_ref[...], sa_ref[...], sb_ref[...]
        reps = ATTN_WIDTH // LANES
        q = _rope(q, jnp.concatenate([cos] * reps, axis=1), jnp.concatenate([sa] * reps, axis=1),
                  jnp.concatenate([sb] * reps, axis=1))
        k = _rope(k, cos, sa, sb)
    q_ref[...] = (q * (HEAD_DIM ** -0.5 * math.log2(math.e))).astype(BF16)
    kd_ref[...] = _dup_heads(k)
    vd_ref[...] = _dup_heads(v)
    if not rope:
        k_ref[...] = k
        v_ref[...] = v


def _token_a(x, mod_l, nrm, wts, layer, gq, gk, qg, kg, rope_tabs, seq_len):
    n = x.shape[0]
    tm = TOKEN_TILE
    tiles_per_seq = max(seq_len // tm, 1)
    rope = rope_tabs is not None
    if rope:
        mod_row = lambda i: (1 + i // tiles_per_seq, 0, 0)
    else:
        mod_row = lambda i: (0, 0, 0)
    const = lambda i: (0, 0)
    row = lambda i: (i, 0)
    in_specs = [
        pl.BlockSpec((tm, D_MODEL), row),
        pl.BlockSpec((None, 1, N_MOD * D_MODEL), mod_row),
        pl.BlockSpec((3, D_MODEL), const),
        pl.BlockSpec((None, None, D_MODEL, D_FF), lambda i: (layer, 0, 0, 0)),
        pl.BlockSpec((None, None, D_MODEL, D_FF), lambda i: (layer, 0, 0, 0)),
        pl.BlockSpec((None, None, D_FF, D_MODEL), lambda i: (layer, 0, 0, 0)),
        pl.BlockSpec((None, D_MODEL, IN_WIDTH), lambda i: (layer, 0, 0)),
        pl.BlockSpec((ATTN_WIDTH, ATTN_WIDTH), const),
        pl.BlockSpec((KV_WIDTH, KV_WIDTH), const),
        pl.BlockSpec((1, ATTN_WIDTH), const),
        pl.BlockSpec((1, KV_WIDTH), const),
    ]
    args = [x, mod_l, nrm, wts["wg"], wts["wu"], wts["wd"], wts["w_in"], gq, gk, qg, kg]
    if rope:
        tab = pl.BlockSpec((tm, LANES), lambda i: (i % tiles_per_seq, 0))
        in_specs += [tab, tab, tab]
        args += list(rope_tabs)
    widths = (D_MODEL, POOL_WIDTH, ATTN_WIDTH, 2 * KV_WIDTH, 2 * KV_WIDTH, 3 * HY_WIDTH)
    dtypes = (F32, F32, BF16, BF16, BF16, F32)
    if not rope:
        widths += (KV_WIDTH, KV_WIDTH)
        dtypes += (F32, F32)
    return pl.pallas_call(
        functools.partial(_token_a_kernel, rope=rope),
        grid=(n // tm,),
        in_specs=in_specs,
        out_specs=[pl.BlockSpec((tm, w), row) for w in widths],
        out_shape=[jax.ShapeDtypeStruct((n, w), d) for w, d in zip(widths, dtypes)],
        compiler_params=_cparams(("arbitrary",)),
        name="token_a_rope" if rope else "token_a",
    )(*args)


def _token_b_kernel(x_ref, yp_ref, a_ref, yh_ref, mod_ref, nrm_ref, wo_ref, wg_ref, wu_ref, wd_ref, o_ref):
    x1 = x_ref[...]
    g2 = _mod_slice(mod_ref, 5)
    sh3, sc3, g3 = _mod_slice(mod_ref, 6), _mod_slice(mod_ref, 7), _mod_slice(mod_ref, 8)
    cat = jnp.concatenate([yp_ref[...], a_ref[...], yh_ref[...]], axis=1)
    x2 = x1 + g2 * _dot(cat, wo_ref[...])
    h3 = _rms_mod(x2, nrm_ref[2:3, :], sc3, sh3).astype(BF16)
    o_ref[...] = x2 + (0.5 * g3) * _swiglu(h3, wg_ref, wu_ref, wd_ref)


def _token_b(x1, y_pool, a, y_hy, mod_l, nrm, wts, layer, seq_len, per_seq_cond):
    n = x1.shape[0]
    tm = TOKEN_TILE
    tiles_per_seq = max(seq_len // tm, 1)
    if per_seq_cond:
        mod_row = lambda i: (1 + i // tiles_per_seq, 0, 0)
    else:
        mod_row = lambda i: (0, 0, 0)
    const = lambda i: (0, 0)
    row = lambda i: (i, 0)
    return pl.pallas_call(
        _token_b_kernel,
        grid=(n // tm,),
        in_specs=[
            pl.BlockSpec((tm, D_MODEL), row),
            pl.BlockSpec((tm, POOL_WIDTH), row),
            pl.BlockSpec((tm, ATTN_WIDTH), row),
            pl.BlockSpec((tm, HY_WIDTH), row),
            pl.BlockSpec((None, 1, N_MOD * D_MODEL), mod_row),
            pl.BlockSpec((3, D_MODEL), const),
            pl.BlockSpec((None, D_MODEL, D_MODEL), lambda i: (layer, 0, 0)),
            pl.BlockSpec((None, None, D_MODEL, D_FF), lambda i: (layer, 1, 0, 0)),
            pl.BlockSpec((None, None, D_MODEL, D_FF), lambda i: (layer, 1, 0, 0)),
            pl.BlockSpec((None, None, D_FF, D_MODEL), lambda i: (layer, 1, 0, 0)),
        ],
        out_specs=pl.BlockSpec((tm, D_MODEL), row),
        out_shape=jax.ShapeDtypeStruct((n, D_MODEL), F32),
        compiler_params=_cparams(("arbitrary",)),
        name="token_b",
    )(x1, y_pool, a, y_hy, mod_l, nrm, wts["w_out"], wts["wg"], wts["wu"], wts["wd"])


def _seqs_per_step(batch, seq_len):
    return min(batch, max(1, SEQ_STEP_ROWS // seq_len))


def _halo_rows(src_ref, base, r0, rows, seq_len):
    c = src_ref.shape[1]
    zero = jnp.zeros((SUBLANES, c), F32)
    lo = base + r0
    prev = src_ref[lo - SUBLANES:lo, :] if r0 > 0 else zero
    nxt = src_ref[lo + rows:lo + rows + SUBLANES, :] if r0 + rows < seq_len else zero
    return prev, nxt


def _short_conv_chunk(src_ref, base, r0, rows, seq_len, w, b):
    x = src_ref[base + r0:base + r0 + rows, :]
    prev, nxt = _halo_rows(src_ref, base, r0, rows, seq_len)
    ridx = lax.broadcasted_iota(jnp.int32, x.shape, 0)
    xp = jnp.where(ridx == 0, prev[SUBLANES - 1:SUBLANES, :], pltpu.roll(x, 1, axis=0))
    xn = jnp.where(ridx == rows - 1, nxt[0:1, :], pltpu.roll(x, rows - 1, axis=0))
    return xp * w[0:1, :] + x * w[1:2, :] + xn * w[2:3, :] + b


def _pool_kernel(u_ref, w_ref, scale_ref, o_ref, *, seq_len, rows, seqs):
    lane = lax.broadcasted_iota(jnp.int32, (rows, POOL_WIDTH), 1)
    grp = lane // POOL_GROUP
    half = jnp.where(grp == 0, 1, jnp.where(grp == 1, 2, jnp.where(grp == 2, 4, 8)))
    ext = rows + 2 * SUBLANES
    for base, r0 in [(sq * seq_len, r0) for sq in range(seqs) for r0 in range(0, seq_len, rows)]:
        x = u_ref[base + r0:base + r0 + rows, :]
        prev, nxt = _halo_rows(u_ref, base, r0, rows, seq_len)
        a = jnp.concatenate([prev, x, nxt], axis=0)
        back = lambda v, s: pltpu.roll(v, s, axis=0)
        fwd = lambda v, s: pltpu.roll(v, ext - s, axis=0)
        b1 = back(a, 1)
        b2 = b1 + back(b1, 1)
        b4 = b2 + back(b2, 2)
        b8 = b4 + back(b4, 4)
        f2 = a + fwd(a, 1)
        f4 = f2 + fwd(f2, 2)
        f8 = f4 + fwd(f4, 4)
        core = lambda v: v[SUBLANES:SUBLANES + rows, :]
        wsum = jnp.where(grp == 0, core(b1) + x,
                         jnp.where(grp == 1, core(b2) + core(f2),
                                   jnp.where(grp == 2, core(b4) + core(f4), core(b8) + core(f8))))
        t = r0 + lax.broadcasted_iota(jnp.int32, (rows, POOL_WIDTH), 0)
        cnt = jnp.minimum(t + half, seq_len) - jnp.maximum(t - half, 0)
        d = wsum / cnt.astype(F32) - x
        y = _dot(d.astype(BF16), w_ref[...]) * scale_ref[...]
        o_ref[base + r0:base + r0 + rows, :] = y.astype(o_ref.dtype)


def _pool_mix(u_pool, w_bd, scale, batch, seq_len):
    rows = min(SEQ_CHUNK // 2, seq_len)
    seqs = _seqs_per_step(batch, seq_len)
    return pl.pallas_call(
        functools.partial(_pool_kernel, seq_len=seq_len, rows=rows, seqs=seqs),
        grid=(batch // seqs,),
        in_specs=[
            pl.BlockSpec((seqs * seq_len, POOL_WIDTH), lambda b: (b, 0)),
            pl.BlockSpec((POOL_WIDTH, POOL_WIDTH), lambda b: (0, 0)),
            pl.BlockSpec((1, POOL_WIDTH), lambda b: (0, 0)),
        ],
        out_specs=pl.BlockSpec((seqs * seq_len, POOL_WIDTH), lambda b: (b, 0)),
        out_shape=jax.ShapeDtypeStruct((batch * seq_len, POOL_WIDTH), BF16),
        compiler_params=_cparams(("arbitrary",)),
        name="pool_mix",
    )(u_pool, w_bd, scale)


def _dup_heads(x):
    lane = lax.broadcasted_iota(jnp.int32, x.shape, 1)
    sw = pltpu.roll(x, HEAD_DIM, axis=1)
    lo = lane < HEAD_DIM
    return jnp.concatenate([jnp.where(lo, x, sw), jnp.where(lo, sw, x)], axis=1).astype(BF16)


def _attn_kernel(*refs, has_local, nblocks, sub, seqs):
    if has_local:
        sink_ref, q_ref, kp_ref, kc_ref, kn_ref, vp_ref, vc_ref, vn_ref, ck_ref, cv_ref, o_ref = refs
        kwin = jnp.concatenate([kp_ref[...], kc_ref[...], kn_ref[...]], axis=0)
        vwin = jnp.concatenate([vp_ref[...], vc_ref[...], vn_ref[...]], axis=0)
        kctx, vctx = _dup_heads(ck_ref[...]), _dup_heads(cv_ref[...])
        r = lax.broadcasted_iota(jnp.int32, (GQA_GROUP * BLOCK, BLOCK), 0) % BLOCK
        j = lax.broadcasted_iota(jnp.int32, (GQA_GROUP * BLOCK, BLOCK), 1)
    else:
        sink_ref, q_ref, kd_ref, vd_ref, o_ref = refs
    i = pl.program_id(1)
    lane_q = lax.broadcasted_iota(jnp.int32, (BLOCK, LANES), 1)
    log2e = math.log2(math.e)
    units = [(sb, kvh) for sb in range(seqs * sub) for kvh in range(N_KV_HEADS)]
    scores, values = [], []
    for sb, kvh in units:
        q = q_ref[sb * BLOCK:(sb + 1) * BLOCK, :]
        qparts = []
        for hd in range(kvh * GQA_GROUP, (kvh + 1) * GQA_GROUP):
            qp = q[:, (hd // 2) * LANES:(hd // 2 + 1) * LANES]
            keep = (lane_q < HEAD_DIM) == (hd % 2 == 0)
            qparts.append(jnp.where(keep, qp, jnp.zeros_like(qp)))
        qs = jnp.concatenate(qparts, axis=0)
        cols = slice(kvh * LANES, (kvh + 1) * LANES)
        if has_local:
            loc = slice(sb * BLOCK, (sb + 3) * BLOCK)
            kk = jnp.concatenate([kwin[loc, cols], kctx[:, cols]], axis=0)
            vv = jnp.concatenate([vwin[loc, cols], vctx[:, cols]], axis=0)
        else:
            own = slice((sb // sub) * sub * BLOCK, (sb // sub + 1) * sub * BLOCK)
            kk, vv = kd_ref[own, cols], vd_ref[own, cols]
        s = lax.dot_general(qs, kk, (((1,), (1,)), ((), ())), preferred_element_type=F32)
        if has_local:
            gb = i * sub + sb
            below = j >= r + jnp.where(gb >= 1, 0, BLOCK)
            above = j <= r - jnp.where(gb <= nblocks - 2, 0, BLOCK)
            s = jnp.concatenate([jnp.where(below, s[:, :BLOCK], NEG_INF), s[:, BLOCK:2 * BLOCK],
                                 jnp.where(above, s[:, 2 * BLOCK:3 * BLOCK], NEG_INF), s[:, 3 * BLOCK:]], axis=1)
        scores.append(s)
        values.append(jnp.concatenate([vv, jnp.ones_like(vv)], axis=1))

    def per_head(kvh, col, f):
        parts = [f(col[g * BLOCK:(g + 1) * BLOCK, :], sink_ref[0, kvh * GQA_GROUP + g] * log2e) for g in range(GQA_GROUP)]
        return jnp.concatenate(parts, axis=0)

    maxes = [per_head(kvh, jnp.max(s, axis=1, keepdims=True), jnp.maximum) for (_, kvh), s in zip(units, scores)]
    probs = [jnp.exp2(s - m).astype(BF16) for s, m in zip(scores, maxes)]
    sums = [_dot(e, v) for e, v in zip(probs, values)]
    outs = []
    for (_, kvh), ow, m in zip(units, sums, maxes):
        den = ow[:, LANES:] + per_head(kvh, m, lambda mm, sk: jnp.exp2(sk - mm))
        outs.append(ow[:, :LANES] / den)
    for sb in range(seqs * sub):
        heads = [outs[sb * N_KV_HEADS + kvh][g * BLOCK:(g + 1) * BLOCK, :]
                 for kvh in range(N_KV_HEADS) for g in range(GQA_GROUP)]
        blks = [jnp.where(lane_q < HEAD_DIM, heads[2 * p], heads[2 * p + 1]) for p in range(N_HEADS // 2)]
        o_ref[sb * BLOCK:(sb + 1) * BLOCK, :] = jnp.concatenate(blks, axis=1).astype(o_ref.dtype)


def _attention(q, kd, vd, sink, batch, seq_len, ctx_k=None, ctx_v=None, layer=0):
    nb = seq_len // BLOCK
    has_local = ctx_k is not None
    sub = min(ATTN_SUB, nb)
    steps = nb // sub
    seqs = 1
    if not has_local:
        assert steps == 1 and batch % CTX_SEQS == 0
        seqs = CTX_SEQS
        batch //= seqs
    qb = seqs * sub * BLOCK
    qspec = pl.BlockSpec((qb, ATTN_WIDTH), lambda b, i: (b * steps + i, 0))
    sspec = pl.BlockSpec(memory_space=pltpu.SMEM)
    if has_local:
        lc = ctx_k.shape[2]
        edge = lambda f: pl.BlockSpec((BLOCK, 2 * KV_WIDTH), f)
        prev = lambda b, i: (b * nb + jnp.maximum(i * sub - 1, 0), 0)
        nxt = lambda b, i: (b * nb + jnp.minimum((i + 1) * sub, nb - 1), 0)
        cur = pl.BlockSpec((qb, 2 * KV_WIDTH), lambda b, i: (b * steps + i, 0))
        cspec = pl.BlockSpec((None, None, lc, KV_WIDTH), lambda b, i: (b, layer, 0, 0))
        in_specs = [sspec, qspec, edge(prev), cur, edge(nxt), edge(prev), cur, edge(nxt), cspec, cspec]
        args = [sink, q, kd, kd, kd, vd, vd, vd, ctx_k, ctx_v]
    else:
        kv = pl.BlockSpec((qb, 2 * KV_WIDTH), lambda b, i: (b, 0))
        in_specs = [sspec, qspec, kv, kv]
        args = [sink, q, kd, vd]
    return pl.pallas_call(
        functools.partial(_attn_kernel, has_local=has_local, nblocks=nb, sub=sub, seqs=seqs),
        grid=(batch, steps),
        in_specs=in_specs,
        out_specs=pl.BlockSpec((qb, ATTN_WIDTH), lambda b, i: (b * steps + i, 0)),
        out_shape=jax.ShapeDtypeStruct(q.shape, BF16),
        compiler_params=_cparams(("arbitrary", "arbitrary")),
        name="attn_latent" if has_local else "attn_context",
    )(*args)


def _filter_kernel(zt_ref, zb_ref, w1_ref, b1_ref, w2_ref, b2_ref, w3_ref, b3_ref, fr_ref, dl_ref, k_ref, sum_ref,
                   *, seq_len, rows):
    i = pl.program_id(0)
    half = rows // 2
    oc = HY_ORDER * HY_WIDTH
    z = jnp.concatenate([zt_ref[...], zb_ref[...]], axis=1)
    h = jnp.sin(fr_ref[0:1, :] * (_dot(z.astype(BF16), w1_ref[...].astype(BF16)) + b1_ref[...]))
    h = jnp.sin(fr_ref[1:2, :] * (_dot(h.astype(BF16), w2_ref[...].astype(BF16)) + b2_ref[...]))
    h3 = _dot(h.astype(BF16), w3_ref[...].astype(BF16)) + b3_ref[...]
    total = jnp.zeros((1, oc), F32)
    for part, z_ref in enumerate((zt_ref, zb_ref)):
        row = i * rows + part * half + lax.broadcasted_iota(jnp.int32, (half, oc), 0)
        t = z_ref[:, 0:1]
        decay = jnp.exp(-t * jnp.abs(dl_ref[...]))
        fwd_dir = h3[:, part * 2 * oc:part * 2 * oc + oc]
        bwd_dir = h3[:, part * 2 * oc + oc:(part + 1) * 2 * oc]
        sel = jnp.where(row < seq_len, fwd_dir, jnp.where(row > seq_len, bwd_dir, 0.0))
        k = sel * (decay + HY_MOD_SHIFT)
        k_ref[part * half:(part + 1) * half, :] = k
        total = total + jnp.sum(jnp.abs(k), axis=0, keepdims=True)

    @pl.when(i == 0)
    def _():
        sum_ref[...] = jnp.zeros_like(sum_ref)

    sum_ref[...] += total


def _block_diag2(w):
    z = jnp.zeros_like(w)
    return jnp.concatenate([jnp.concatenate([w, z], axis=1), jnp.concatenate([z, w], axis=1)], axis=0)


def _hyena_filter(z_ext, w1p, b1, w2, b2, w3, b3, freq, deltas, seq_len):
    n = 2 * seq_len
    rows = min(1024, n)
    half = rows // 2
    oc = HY_ORDER * HY_WIDTH
    hid = 2 * HY_FILTER_HIDDEN
    const = lambda i: (0, 0)
    two = lambda a: jnp.concatenate([a, a], axis=1)
    return pl.pallas_call(
        functools.partial(_filter_kernel, seq_len=seq_len, rows=rows),
        grid=(n // rows,),
        in_specs=[
            pl.BlockSpec((half, LANES), lambda i: (2 * i, 0)),
            pl.BlockSpec((half, LANES), lambda i: (2 * i + 1, 0)),
            pl.BlockSpec((2 * LANES, hid), const),
            pl.BlockSpec((1, hid), const),
            pl.BlockSpec((hid, hid), const),
            pl.BlockSpec((1, hid), const),
            pl.BlockSpec((hid, 4 * oc), const),
            pl.BlockSpec((1, 4 * oc), const),
            pl.BlockSpec((2, hid), const),
            pl.BlockSpec((1, oc), const),
        ],
        out_specs=[pl.BlockSpec((rows, oc), lambda i: (i, 0)), pl.BlockSpec((1, oc), const)],
        out_shape=[jax.ShapeDtypeStruct((n, oc), F32), jax.ShapeDtypeStruct((1, oc), F32)],
        compiler_params=_cparams(("arbitrary",)),
        name="hyena_filter",
    )(z_ext, z_ext, _block_diag2(w1p), two(b1), _block_diag2(w2), two(b2), _block_diag2(w3), two(b3), two(freq), deltas)


def _spectrum_kernel(k_ref, sum_ref, fwd_ref, kf_ref, hprev_ref, g0_ref, *, blk):
    t = pl.program_id(0)
    a = k_ref[...] / (sum_ref[...] + 1e-6)
    ha = _dot(fwd_ref[...], a.astype(BF16))
    tf = FREQ_TILE
    par = lax.broadcasted_iota(jnp.int32, (tf, 1), 0) % 2
    sgn = (1 - 2 * par).astype(F32)

    @pl.when(t > 0)
    def _():
        g0 = g0_ref[...]
        for c in range(blk // tf):
            re = slice(2 * c * tf, (2 * c + 1) * tf)
            im = slice((2 * c + 1) * tf, (2 * c + 2) * tf)
            kf_ref[re, :] = ha[re, :] - sgn * hprev_ref[im, :]
            kf_ref[im, :] = ha[im, :] + sgn * (hprev_ref[re, :] - g0)

    hprev_ref[...] = ha
    g0_ref[...] = a[0:1, :]


def _filter_spectrum(kraw, ksum, fwd, seq_len, blk):
    nb = seq_len // blk
    nlags = 2 * nb - 1
    oc = HY_ORDER * HY_WIDTH
    m = 2 * blk
    return pl.pallas_call(
        functools.partial(_spectrum_kernel, blk=blk),
        grid=(2 * nb,),
        in_specs=[
            pl.BlockSpec((blk, oc), lambda t: ((nb + t) % (2 * nb), 0)),
            pl.BlockSpec((1, oc), lambda t: (0, 0)),
            pl.BlockSpec((m, blk), lambda t: (0, 0)),
        ],
        out_specs=pl.BlockSpec((None, m, oc), lambda t: (jnp.maximum(t - 1, 0), 0, 0)),
        out_shape=jax.ShapeDtypeStruct((nlags, m, oc), F32),
        scratch_shapes=[pltpu.VMEM((m, oc), F32), pltpu.VMEM((1, oc), F32)],
        compiler_params=_cparams(("arbitrary",)),
        name="filter_spectrum",
    )(kraw, ksum, fwd)


def _conv_kernel(z_ref, g_ref, swz_ref, sbz_ref, swg_ref, sbg_ref, bd_ref, kf_ref, fwd_ref, inv_ref, o_ref,
                 zb_ref, zc_ref, zf_ref, yf_ref, *, seq_len, blk, conv_z, seqs):
    fc = pl.program_id(1)
    nfc = pl.num_programs(1)
    nb = seq_len // blk
    tf = FREQ_TILE
    blocks = [((g // nb) * seq_len, (g % nb) * blk) for g in range(seqs * nb)]

    @pl.when(fc == 0)
    def _():
        for base, r0 in blocks:
            rows = slice(base + r0, base + r0 + blk)
            if conv_z:
                zc_ref[rows, :] = _short_conv_chunk(z_ref, base, r0, blk, seq_len, swz_ref[...], sbz_ref[...])
                zb_ref[rows, :] = zc_ref[rows, :].astype(BF16)
            else:
                zb_ref[rows, :] = z_ref[rows, :].astype(BF16)

    for g in range(seqs * nb):
        zf_ref[g] = _dot(fwd_ref[...], zb_ref[g * blk:(g + 1) * blk, :])

    col = pl.multiple_of(fc * 2 * tf, 2 * tf)
    for g0 in range(0, seqs * nb, nb):
        for bi in range(nb):
            for r in range(0, tf, MAC_ROWS):
                re = slice(r, r + MAC_ROWS)
                im = slice(tf + r, tf + r + MAC_ROWS)
                yr = jnp.zeros((MAC_ROWS, HY_WIDTH), F32)
                yi = jnp.zeros((MAC_ROWS, HY_WIDTH), F32)
                for bj in range(nb):
                    lag = bi - bj + nb - 1
                    kr, ki = kf_ref[lag, re, :], kf_ref[lag, im, :]
                    zr, zi = zf_ref[g0 + bj, re, :], zf_ref[g0 + bj, im, :]
                    yr = yr + (kr * zr - ki * zi)
                    yi = yi + (kr * zi + ki * zr)
                yf_ref[g0 + bi, pl.ds(col + r, MAC_ROWS), :] = yr.astype(BF16)
                yf_ref[g0 + bi, pl.ds(col + tf + r, MAC_ROWS), :] = yi.astype(BF16)

    @pl.when(fc == nfc - 1)
    def _():
        for g, (base, r0) in enumerate(blocks):
            rows = slice(base + r0, base + r0 + blk)
            y = _dot(inv_ref[...], yf_ref[g])
            z = zc_ref[rows, :] if conv_z else z_ref[rows, :]
            gate = _short_conv_chunk(g_ref, base, r0, blk, seq_len, swg_ref[...], sbg_ref[...])
            o_ref[rows, :] = (gate * (y + bd_ref[...] * z)).astype(o_ref.dtype)


def _hyena_conv(z_src, z_col, conv_z, u_hy, g_col, short_w, short_b, bd, kf, order, fwd, inv,
                batch, seq_len, blk, out_dtype):
    nb = seq_len // blk
    nlags = 2 * nb - 1
    tf = FREQ_TILE
    nfc = blk // tf
    w = HY_WIDTH
    seqs = _seqs_per_step(batch, seq_len)
    step_rows = seqs * seq_len
    return pl.pallas_call(
        functools.partial(_conv_kernel, seq_len=seq_len, blk=blk, conv_z=conv_z, seqs=seqs),
        grid=(batch // seqs, nfc),
        in_specs=[
            pl.BlockSpec((step_rows, w), lambda b, f: (b, z_col)),
            pl.BlockSpec((step_rows, w), lambda b, f: (b, g_col)),
            pl.BlockSpec((3, w), lambda b, f: (0, z_col if conv_z else 0)),
            pl.BlockSpec((1, w), lambda b, f: (0, z_col if conv_z else 0)),
            pl.BlockSpec((3, w), lambda b, f: (0, g_col)),
            pl.BlockSpec((1, w), lambda b, f: (0, g_col)),
            pl.BlockSpec((1, w), lambda b, f: (0, 0)),
            pl.BlockSpec((nlags, 2 * tf, w), lambda b, f: (0, f, order)),
            pl.BlockSpec((2 * tf, blk), lambda b, f: (f, 0)),
            pl.BlockSpec((blk, 2 * blk), lambda b, f: (0, 0)),
        ],
        out_specs=pl.BlockSpec((step_rows, w), lambda b, f: (b, 0)),
        out_shape=jax.ShapeDtypeStruct((batch * seq_len, w), out_dtype),
        scratch_shapes=[
            pltpu.VMEM((step_rows, w), BF16),
            pltpu.VMEM((step_rows if conv_z else SUBLANES, w), F32),
            pltpu.VMEM((seqs * nb, 2 * tf, w), F32),
            pltpu.VMEM((seqs * nb, 2 * blk, w), BF16),
        ],
        compiler_params=_cparams(("arbitrary", "arbitrary")),
        name="hyena_conv",
    )(z_src, u_hy, short_w, short_b, short_w, short_b, bd, kf, fwd, inv)


def _dft_tables(blk):
    m = 2 * blk
    tf = FREQ_TILE
    f = jnp.arange(blk, dtype=jnp.int32)
    n = jnp.arange(blk, dtype=jnp.int32)
    r = ((2 * f[:, None] + 1) * n[None, :]) % (2 * m)
    ang = r.astype(F32) * (math.pi / m)
    c, s = jnp.cos(ang), jnp.sin(ang)
    fwd = jnp.stack([c.reshape(blk // tf, tf, blk), (-s).reshape(blk // tf, tf, blk)], axis=1).reshape(m, blk)
    inv = fwd.T * (2.0 / m)
    return fwd.astype(BF16), inv.astype(BF16)


def _filter_embedding(seq_len):
    t = jnp.linspace(0.0, 1.0, seq_len, dtype=F32)[:, None]
    bands = (HY_EMB_DIM - 1) // 2
    f = jnp.linspace(1e-4, bands - 1, bands, dtype=F32)[None, :]
    w = 2.0 * math.pi * jnp.arange(seq_len, dtype=F32)[:, None] / seq_len
    z = jnp.concatenate([t, jnp.cos(f * w), -jnp.sin(f * w)], axis=-1)
    z_ext = jnp.concatenate([z, jnp.zeros((1, HY_EMB_DIM), F32), jnp.flip(z[1:], axis=0)], axis=0)
    return jnp.pad(z_ext, ((0, 0), (0, LANES - HY_EMB_DIM)))


def _rope_tables(seq_len):
    quarter = HEAD_DIM // 4
    inv = ROPE_THETA ** (-jnp.arange(quarter, dtype=F32) / quarter)
    lane = jnp.arange(LANES)
    t = jnp.arange(seq_len)[:, None]
    pos = jnp.where((lane % HEAD_DIM < HEAD_DIM // 2)[None, :], t // GRID_W, t % GRID_W).astype(F32)
    ang = pos * jnp.tile(inv, LANES // quarter)[None, :]
    first = (lane % (2 * quarter) < quarter)[None, :]
    sin = jnp.sin(ang)
    return jnp.cos(ang), jnp.where(first, -sin, 0.0), jnp.where(first, 0.0, sin)


def _block_ones(width):
    h = jnp.arange(width) // HEAD_DIM
    return (h[:, None] == h[None, :]).astype(BF16)


def _stream_layer(x, p, wts, mod_l, batch, seq_len, rope_tabs, ctx, layer, hy_blk, consts):
    gq, gk, fwd, inv, z_ext = consts
    latent = ctx is not None
    outs = _token_a(x, mod_l, p["norm"], wts, layer, gq, gk, p["qg"], p["kg"], rope_tabs, seq_len)
    x1, u_pool, q, kd, vd, u_hy = outs[:6]
    k, v = (None, None) if latent else outs[6:]
    y_pool = _pool_mix(u_pool, p["pool_w"], p["pool_scale"], batch, seq_len)
    if latent:
        a = _attention(q, kd, vd, p["sink"], batch, seq_len, ctx[0], ctx[1], layer)
    else:
        a = _attention(q, kd, vd, p["sink"], batch, seq_len)
    kraw, ksum = _hyena_filter(z_ext, p["f_w1"], p["f_b1"], p["f_w2"], p["f_b2"], p["f_w3"], p["f_b3"],
                               p["f_freq"], p["decay"], seq_len)
    kf = _filter_spectrum(kraw, ksum, fwd, seq_len, hy_blk)
    z1 = _hyena_conv(u_hy, 0, True, u_hy, 1, p["short_w"], p["short_b"], p["hy_bias"][0:1], kf, 0, fwd, inv,
                     batch, seq_len, hy_blk, F32)
    y_hy = _hyena_conv(z1, 0, False, u_hy, 2, p["short_w"], p["short_b"], p["hy_bias"][1:2], kf, 1, fwd, inv,
                       batch, seq_len, hy_blk, BF16)
    x3 = _token_b(x1, y_pool, a, y_hy, mod_l, p["norm"], wts, layer, seq_len, latent)
    return x3, k, v


def kernel(x_prompt, x_sample, cache_k, cache_v, c, c_ctx, ada_w, ada_b, norm_w, ffn_wg, ffn_wu, ffn_wd, w_in, w_out, pool_w, pool_scale, q_norm, k_norm, attn_sink, hy_short_w, hy_short_b, hy_f_w1, hy_f_b1, hy_f_w2, hy_f_b2, hy_f_w3, hy_f_b3, hy_sin_freq, hy_decay, hy_bias):
    bp, lp, _ = x_prompt.shape
    bs, ls, _ = x_sample.shape
    lc = cache_k.shape[2]

    cond = jnp.concatenate([c_ctx[None, :], c, jnp.zeros((COND_ROWS - 1 - bs, D_MODEL), F32)], axis=0)
    mod = _ada_mod(cond, ada_w, ada_b).reshape(DEPTH, COND_ROWS, 1, N_MOD * D_MODEL)

    gq, gk = _block_ones(ATTN_WIDTH), _block_ones(KV_WIDTH)
    rope_tabs = _rope_tables(ls)
    blk_p, blk_s = min(lp, 512), min(ls, 512)
    consts_p = (gq, gk) + _dft_tables(blk_p) + (_filter_embedding(lp),)
    consts_s = (gq, gk) + _dft_tables(blk_s) + (_filter_embedding(ls),)
    ctx_k = cache_k.reshape(bs, DEPTH, lc, KV_WIDTH)
    ctx_v = cache_v.reshape(bs, DEPTH, lc, KV_WIDTH)

    wts = _cast_weights(ffn_wg, ffn_wu, ffn_wd, w_in, w_out)
    yp = x_prompt.reshape(bp * lp, D_MODEL)
    ys = x_sample.reshape(bs * ls, D_MODEL)
    ks, vs = [], []
    eye = jnp.eye(len(POOL_WINDOWS), dtype=F32)
    for l in range(DEPTH):
        pool_bd = (eye[:, None, :, None] * pool_w[l][:, :, None, :]).reshape(POOL_WIDTH, POOL_WIDTH)
        p = {
            "norm": norm_w[l],
            "pool_w": pool_bd.astype(BF16), "pool_scale": pool_scale[l][None, :],
            "qg": jnp.tile(q_norm[l], N_HEADS)[None, :], "kg": jnp.tile(k_norm[l], N_KV_HEADS)[None, :],
            "sink": attn_sink[l][None, :],
            "short_w": hy_short_w[l], "short_b": hy_short_b[l][None, :],
            "f_w1": jnp.pad(hy_f_w1[l], ((0, LANES - HY_EMB_DIM), (0, 0))), "f_b1": hy_f_b1[l][None, :],
            "f_w2": hy_f_w2[l], "f_b2": hy_f_b2[l][None, :],
            "f_w3": hy_f_w3[l], "f_b3": hy_f_b3[l][None, :],
            "f_freq": hy_sin_freq[l], "decay": hy_decay[l].reshape(1, HY_ORDER * HY_WIDTH),
            "hy_bias": hy_bias[l],
        }
        yp, k_l, v_l = _stream_layer(yp, p, wts, mod[l], bp, lp, None, None, l, blk_p, consts_p)
        ks.append(k_l.reshape(bp, lp, N_KV_HEADS, HEAD_DIM))
        vs.append(v_l.reshape(bp, lp, N_KV_HEADS, HEAD_DIM))
        ys, _, _ = _stream_layer(ys, p, wts, mod[l], bs, ls, rope_tabs, (ctx_k, ctx_v), l, blk_s, consts_s)
    return (yp.reshape(bp, lp, D_MODEL), ys.reshape(bs, ls, D_MODEL),
            jnp.stack(ks, axis=1), jnp.stack(vs, axis=1))
```

```python
import functools
import math

import jax
import jax.numpy as jnp
from jax import lax
from jax.experimental import pallas as pl
from jax.experimental.pallas import tpu as pltpu

F32 = jnp.float32
BF16 = jnp.bfloat16

D_MODEL = 1024
DEPTH = 2
GRID_W = 64
POOL_WINDOWS = (2, 4, 8, 16)
POOL_WIDTH = 256
POOL_GROUP = 64
HEAD_DIM = 64
N_HEADS = 8
N_KV_HEADS = 2
GQA_GROUP = 4
ATTN_WIDTH = 512
KV_WIDTH = 128
WINDOW = 128
BLOCK = 128
ROPE_THETA = 10000.0
HY_WIDTH = 256
HY_ORDER = 2
HY_EMB_DIM = 33
HY_FILTER_HIDDEN = 64
HY_MOD_SHIFT = 0.05
D_FF = 2816
IN_WIDTH = 1792
N_MOD = 9
NORM_EPS = 1e-6
NEG_INF = -1e30

LANES = 128
SUBLANES = 8
VMEM_LIMIT = 56 * 1024 * 1024

TOKEN_TILE = 512
MXU_DIM = 256
FF_CHUNKS = (1280, 1536)
SEQ_CHUNK = 512
SEQ_STEP_ROWS = 2048
HY_BLOCK = 1024
FREQ_TILE = 128
FREQ_SUB = 2
MAC_ROWS = 32
ATTN_SUB = 4
CTX_SEQS = 2
COND_ROWS = 8


def _cparams(sem):
    return pltpu.CompilerParams(dimension_semantics=sem, vmem_limit_bytes=VMEM_LIMIT)


def _dot(a, b):
    return jnp.dot(a, b, preferred_element_type=F32)


def _cast_kernel(*refs):
    n = len(refs) // 2
    for i_ref, o_ref in zip(refs[:n], refs[n:]):
        o_ref[...] = i_ref[...].astype(o_ref.dtype)


def _cast_weights(ffn_wg, ffn_wu, ffn_wd, w_in, w_out):
    steps = 16
    arrs = {"wg": ffn_wg, "wu": ffn_wu, "wd": ffn_wd, "w_in": w_in, "w_out": w_out}
    flat = [a.reshape(-1, a.shape[-1]) for a in arrs.values()]
    specs = [pl.BlockSpec((a.shape[0] // steps, a.shape[1]), lambda i: (i, 0)) for a in flat]
    outs = pl.pallas_call(
        _cast_kernel,
        grid=(steps,),
        in_specs=specs,
        out_specs=specs,
        out_shape=[jax.ShapeDtypeStruct(a.shape, BF16) for a in flat],
        compiler_params=_cparams(("arbitrary",)),
        name="cast_weights",
    )(*flat)
    return {k: o.reshape(a.shape) for (k, a), o in zip(arrs.items(), outs)}


def _ada_kernel(c_ref, w_ref, b_ref, o_ref):
    c = c_ref[...]
    s = (c * jax.nn.sigmoid(c)).astype(BF16)
    o_ref[...] = _dot(s, w_ref[...].astype(BF16)) + b_ref[...]


def _ada_mod(cond, ada_w, ada_b):
    tn = 3072
    nw = N_MOD * D_MODEL
    return pl.pallas_call(
        _ada_kernel,
        grid=(DEPTH, nw // tn),
        in_specs=[
            pl.BlockSpec((COND_ROWS, D_MODEL), lambda l, j: (0, 0)),
            pl.BlockSpec((None, D_MODEL, tn), lambda l, j: (l, 0, j)),
            pl.BlockSpec((None, 1, tn), lambda l, j: (l, 0, j)),
        ],
        out_specs=pl.BlockSpec((None, COND_ROWS, tn), lambda l, j: (l, 0, j)),
        out_shape=jax.ShapeDtypeStruct((DEPTH, COND_ROWS, nw), F32),
        compiler_params=_cparams(("arbitrary", "arbitrary")),
        name="ada_mod",
    )(cond, ada_w, ada_b.reshape(DEPTH, 1, nw))


def _mod_slice(mod_ref, k):
    return mod_ref[:, k * D_MODEL:(k + 1) * D_MODEL]


def _rms_mod(x, gain, scale, shift):
    y = x * lax.rsqrt(jnp.mean(x * x, axis=-1, keepdims=True) + NORM_EPS)
    return (y * gain) * (1.0 + scale) + shift


def _swiglu(hb, wg_ref, wu_ref, wd_ref):
    acc = None
    lo = 0
    for width in FF_CHUNKS:
        sl = slice(lo, lo + width)
        lo += width
        g = _dot(hb, wg_ref[:, sl])
        u = _dot(hb, wu_ref[:, sl])
        a = ((g * jax.nn.sigmoid(g)) * u).astype(BF16)
        y = _dot(a, wd_ref[sl, :])
        acc = y if acc is None else acc + y
    return acc


def _head_norm(x, gmat, gain):
    ss = _dot((x * x).astype(BF16), gmat)
    return (x * lax.rsqrt(ss * (1.0 / HEAD_DIM) + NORM_EPS)) * gain


def _rope(x, cos, sa, sb):
    w = x.shape[1]
    xn = pltpu.roll(x, w - 16, axis=1)
    xp = pltpu.roll(x, 16, axis=1)
    return x * cos + xn * sa + xp * sb


def _token_a_kernel(*refs, rope):
    if rope:
        (x_ref, mod_ref, nrm_ref, wg_ref, wu_ref, wd_ref, win_ref, gq_ref, gk_ref, qg_ref, kg_ref,
         cos_ref, sa_ref, sb_ref, x1_ref, up_ref, q_ref, kd_ref, vd_ref, hv_ref, hx1_ref, hx2_ref) = refs
    else:
        (x_ref, mod_ref, nrm_ref, wg_ref, wu_ref, wd_ref, win_ref, gq_ref, gk_ref, qg_ref, kg_ref,
         x1_ref, up_ref, q_ref, kd_ref, vd_ref, hv_ref, hx1_ref, hx2_ref, k_ref, v_ref) = refs
    x = x_ref[...]
    sh1, sc1, g1 = _mod_slice(mod_ref, 0), _mod_slice(mod_ref, 1), _mod_slice(mod_ref, 2)
    sh2, sc2 = _mod_slice(mod_ref, 3), _mod_slice(mod_ref, 4)
    h = _rms_mod(x, nrm_ref[0:1, :], sc1, sh1).astype(BF16)
    x1 = x + (0.5 * g1) * _swiglu(h, wg_ref, wu_ref, wd_ref)
    x1_ref[...] = x1
    h2 = _rms_mod(x1, nrm_ref[1:2, :], sc2, sh2).astype(BF16)
    s1 = POOL_WIDTH
    s2 = s1 + ATTN_WIDTH
    s3 = s2 + KV_WIDTH
    s4 = s3 + KV_WIDTH
    qkv = _dot(h2, win_ref[:, s1:s4])
    q = _head_norm(qkv[:, :ATTN_WIDTH], gq_ref[...], qg_ref[...])
    k = _head_norm(qkv[:, ATTN_WIDTH:ATTN_WIDTH + KV_WIDTH], gk_ref[...], kg_ref[...])
    v = qkv[:, ATTN_WIDTH + KV_WIDTH:]
    up_ref[...] = _dot(h2, win_ref[:, :s1])
    hy = _dot(h2, win_ref[:, s4:])
    for j, ref in enumerate((hv_ref, hx1_ref, hx2_ref)):
        ref[...] = hy[:, j * HY_WIDTH:(j + 1) * HY_WIDTH]
    if rope:
        cos, sa, sb = cos_ref[...], sa_ref[...], sb_ref[...]
        reps = ATTN_WIDTH // LANES
        q = _rope(q, jnp.concatenate([cos] * reps, axis=1), jnp.concatenate([sa] * reps, axis=1),
                  jnp.concatenate([sb] * reps, axis=1))
        k = _rope(k, cos, sa, sb)
    q_ref[...] = (q * (HEAD_DIM ** -0.5 * math.log2(math.e))).astype(BF16)
    kd_ref[...] = _dup_heads(k)
    vd_ref[...] = _dup_heads(v)
    if not rope:
        k_ref[...] = k
        v_ref[...] = v


def _token_a(x, mod_l, nrm, wts, layer, gq, gk, qg, kg, rope_tabs, seq_len):
    n = x.shape[0]
    tm = TOKEN_TILE
    tiles_per_seq = max(seq_len // tm, 1)
    rope = rope_tabs is not None
    if rope:
        mod_row = lambda i: (1 + i // tiles_per_seq, 0, 0)
    else:
        mod_row = lambda i: (0, 0, 0)
    const = lambda i: (0, 0)
    row = lambda i: (i, 0)
    in_specs = [
        pl.BlockSpec((tm, D_MODEL), row),
        pl.BlockSpec((None, 1, N_MOD * D_MODEL), mod_row),
        pl.BlockSpec((3, D_MODEL), const),
        pl.BlockSpec((None, None, D_MODEL, D_FF), lambda i: (layer, 0, 0, 0)),
        pl.BlockSpec((None, None, D_MODEL, D_FF), lambda i: (layer, 0, 0, 0)),
        pl.BlockSpec((None, None, D_FF, D_MODEL), lambda i: (layer, 0, 0, 0)),
        pl.BlockSpec((None, D_MODEL, IN_WIDTH), lambda i: (layer, 0, 0)),
        pl.BlockSpec((ATTN_WIDTH, ATTN_WIDTH), const),
        pl.BlockSpec((KV_WIDTH, KV_WIDTH), const),
        pl.BlockSpec((1, ATTN_WIDTH), const),
        pl.BlockSpec((1, KV_WIDTH), const),
    ]
    args = [x, mod_l, nrm, wts["wg"], wts["wu"], wts["wd"], wts["w_in"], gq, gk, qg, kg]
    if rope:
        tab = pl.BlockSpec((tm, LANES), lambda i: (i % tiles_per_seq, 0))
        in_specs += [tab, tab, tab]
        args += list(rope_tabs)
    widths = (D_MODEL, POOL_WIDTH, ATTN_WIDTH, 2 * KV_WIDTH, 2 * KV_WIDTH) + (HY_WIDTH,) * (HY_ORDER + 1)
    dtypes = (F32, F32, BF16, BF16, BF16) + (F32,) * (HY_ORDER + 1)
    if not rope:
        widths += (KV_WIDTH, KV_WIDTH)
        dtypes += (F32, F32)
    return pl.pallas_call(
        functools.partial(_token_a_kernel, rope=rope),
        grid=(n // tm,),
        in_specs=in_specs,
        out_specs=[pl.BlockSpec((tm, w), row) for w in widths],
        out_shape=[jax.ShapeDtypeStruct((n, w), d) for w, d in zip(widths, dtypes)],
        compiler_params=_cparams(("arbitrary",)),
        name="token_a_rope" if rope else "token_a",
    )(*args)


def _token_b_kernel(x_ref, yp_ref, a_ref, yh_ref, mod_ref, nrm_ref, wo_ref, wg_ref, wu_ref, wd_ref, o_ref):
    x1 = x_ref[...]
    g2 = _mod_slice(mod_ref, 5)
    sh3, sc3, g3 = _mod_slice(mod_ref, 6), _mod_slice(mod_ref, 7), _mod_slice(mod_ref, 8)
    cat = jnp.concatenate([yp_ref[...], a_ref[...], yh_ref[...]], axis=1)
    x2 = x1 + g2 * _dot(cat, wo_ref[...])
    h3 = _rms_mod(x2, nrm_ref[2:3, :], sc3, sh3).astype(BF16)
    o_ref[...] = x2 + (0.5 * g3) * _swiglu(h3, wg_ref, wu_ref, wd_ref)


def _token_b(x1, y_pool, a, y_hy, mod_l, nrm, wts, layer, seq_len, per_seq_cond):
    n = x1.shape[0]
    tm = TOKEN_TILE
    tiles_per_seq = max(seq_len // tm, 1)
    if per_seq_cond:
        mod_row = lambda i: (1 + i // tiles_per_seq, 0, 0)
    else:
        mod_row = lambda i: (0, 0, 0)
    const = lambda i: (0, 0)
    row = lambda i: (i, 0)
    return pl.pallas_call(
        _token_b_kernel,
        grid=(n // tm,),
        in_specs=[
            pl.BlockSpec((tm, D_MODEL), row),
            pl.BlockSpec((tm, POOL_WIDTH), row),
            pl.BlockSpec((tm, ATTN_WIDTH), row),
            pl.BlockSpec((tm, HY_WIDTH), row),
            pl.BlockSpec((None, 1, N_MOD * D_MODEL), mod_row),
            pl.BlockSpec((3, D_MODEL), const),
            pl.BlockSpec((None, D_MODEL, D_MODEL), lambda i: (layer, 0, 0)),
            pl.BlockSpec((None, None, D_MODEL, D_FF), lambda i: (layer, 1, 0, 0)),
            pl.BlockSpec((None, None, D_MODEL, D_FF), lambda i: (layer, 1, 0, 0)),
            pl.BlockSpec((None, None, D_FF, D_MODEL), lambda i: (layer, 1, 0, 0)),
        ],
        out_specs=pl.BlockSpec((tm, D_MODEL), row),
        out_shape=jax.ShapeDtypeStruct((n, D_MODEL), F32),
        compiler_params=_cparams(("arbitrary",)),
        name="token_b",
    )(x1, y_pool, a, y_hy, mod_l, nrm, wts["w_out"], wts["wg"], wts["wu"], wts["wd"])


def _seqs_per_step(batch, seq_len):
    return min(batch, max(1, SEQ_STEP_ROWS // seq_len))


def _halo_rows(src_ref, base, r0, rows, seq_len):
    c = src_ref.shape[1]
    zero = jnp.zeros((SUBLANES, c), F32)
    lo = base + r0
    prev = src_ref[lo - SUBLANES:lo, :] if r0 > 0 else zero
    nxt = src_ref[lo + rows:lo + rows + SUBLANES, :] if r0 + rows < seq_len else zero
    return prev, nxt


def _short_conv_chunk(src_ref, base, r0, rows, seq_len, w, b):
    x = src_ref[base + r0:base + r0 + rows, :]
    prev, nxt = _halo_rows(src_ref, base, r0, rows, seq_len)
    ridx = lax.broadcasted_iota(jnp.int32, x.shape, 0)
    xp = jnp.where(ridx == 0, prev[SUBLANES - 1:SUBLANES, :], pltpu.roll(x, 1, axis=0))
    xn = jnp.where(ridx == rows - 1, nxt[0:1, :], pltpu.roll(x, rows - 1, axis=0))
    return xp * w[0:1, :] + x * w[1:2, :] + xn * w[2:3, :] + b


def _pool_kernel(u_ref, w_ref, scale_ref, o_ref, *, seq_len, rows, seqs):
    lane = lax.broadcasted_iota(jnp.int32, (rows, POOL_WIDTH), 1)
    grp = lane // POOL_GROUP
    half = jnp.where(grp == 0, 1, jnp.where(grp == 1, 2, jnp.where(grp == 2, 4, 8)))
    ext = rows + 2 * SUBLANES
    for base, r0 in [(sq * seq_len, r0) for sq in range(seqs) for r0 in range(0, seq_len, rows)]:
        x = u_ref[base + r0:base + r0 + rows, :]
        prev, nxt = _halo_rows(u_ref, base, r0, rows, seq_len)
        a = jnp.concatenate([prev, x, nxt], axis=0)
        back = lambda v, s: pltpu.roll(v, s, axis=0)
        fwd = lambda v, s: pltpu.roll(v, ext - s, axis=0)
        b1 = back(a, 1)
        b2 = b1 + back(b1, 1)
        b4 = b2 + back(b2, 2)
        b8 = b4 + back(b4, 4)
        f2 = a + fwd(a, 1)
        f4 = f2 + fwd(f2, 2)
        f8 = f4 + fwd(f4, 4)
        core = lambda v: v[SUBLANES:SUBLANES + rows, :]
        wsum = jnp.where(grp == 0, core(b1) + x,
                         jnp.where(grp == 1, core(b2) + core(f2),
                                   jnp.where(grp == 2, core(b4) + core(f4), core(b8) + core(f8))))
        t = r0 + lax.broadcasted_iota(jnp.int32, (rows, POOL_WIDTH), 0)
        cnt = jnp.minimum(t + half, seq_len) - jnp.maximum(t - half, 0)
        d = wsum / cnt.astype(F32) - x
        y = _dot(d.astype(BF16), w_ref[...]) * scale_ref[...]
        o_ref[base + r0:base + r0 + rows, :] = y.astype(o_ref.dtype)


def _pool_mix(u_pool, w_bd, scale, batch, seq_len):
    rows = min(SEQ_CHUNK // 2, seq_len)
    seqs = _seqs_per_step(batch, seq_len)
    return pl.pallas_call(
        functools.partial(_pool_kernel, seq_len=seq_len, rows=rows, seqs=seqs),
        grid=(batch // seqs,),
        in_specs=[
            pl.BlockSpec((seqs * seq_len, POOL_WIDTH), lambda b: (b, 0)),
            pl.BlockSpec((POOL_WIDTH, POOL_WIDTH), lambda b: (0, 0)),
            pl.BlockSpec((1, POOL_WIDTH), lambda b: (0, 0)),
        ],
        out_specs=pl.BlockSpec((seqs * seq_len, POOL_WIDTH), lambda b: (b, 0)),
        out_shape=jax.ShapeDtypeStruct((batch * seq_len, POOL_WIDTH), BF16),
        compiler_params=_cparams(("arbitrary",)),
        name="pool_mix",
    )(u_pool, w_bd, scale)


def _dup_heads(x):
    lane = lax.broadcasted_iota(jnp.int32, x.shape, 1)
    sw = pltpu.roll(x, HEAD_DIM, axis=1)
    lo = lane < HEAD_DIM
    return jnp.concatenate([jnp.where(lo, x, sw), jnp.where(lo, sw, x)], axis=1).astype(BF16)


def _attn_kernel(*refs, has_local, nblocks, sub, seqs):
    if has_local:
        sink_ref, q_ref, kp_ref, kc_ref, kn_ref, vp_ref, vc_ref, vn_ref, ck_ref, cv_ref, o_ref = refs
        kwin = jnp.concatenate([kp_ref[...], kc_ref[...], kn_ref[...]], axis=0)
        vwin = jnp.concatenate([vp_ref[...], vc_ref[...], vn_ref[...]], axis=0)
        kctx, vctx = _dup_heads(ck_ref[...]), _dup_heads(cv_ref[...])
        r = lax.broadcasted_iota(jnp.int32, (GQA_GROUP * BLOCK, BLOCK), 0) % BLOCK
        j = lax.broadcasted_iota(jnp.int32, (GQA_GROUP * BLOCK, BLOCK), 1)
    else:
        sink_ref, q_ref, kd_ref, vd_ref, o_ref = refs
    i = pl.program_id(1)
    lane_q = lax.broadcasted_iota(jnp.int32, (BLOCK, LANES), 1)
    log2e = math.log2(math.e)
    units = [(sb, kvh) for sb in range(seqs * sub) for kvh in range(N_KV_HEADS)]
    scores, values = [], []
    for sb, kvh in units:
        q = q_ref[sb * BLOCK:(sb + 1) * BLOCK, :]
        qparts = []
        for hd in range(kvh * GQA_GROUP, (kvh + 1) * GQA_GROUP):
            qp = q[:, (hd // 2) * LANES:(hd // 2 + 1) * LANES]
            keep = (lane_q < HEAD_DIM) == (hd % 2 == 0)
            qparts.append(jnp.where(keep, qp, jnp.zeros_like(qp)))
        qs = jnp.concatenate(qparts, axis=0)
        cols = slice(kvh * LANES, (kvh + 1) * LANES)
        if has_local:
            loc = slice(sb * BLOCK, (sb + 3) * BLOCK)
            kk = jnp.concatenate([kwin[loc, cols], kctx[:, cols]], axis=0)
            vv = jnp.concatenate([vwin[loc, cols], vctx[:, cols]], axis=0)
        else:
            own = slice((sb // sub) * sub * BLOCK, (sb // sub + 1) * sub * BLOCK)
            kk, vv = kd_ref[own, cols], vd_ref[own, cols]
        s = lax.dot_general(qs, kk, (((1,), (1,)), ((), ())), preferred_element_type=F32)
        if has_local:
            gb = i * sub + sb
            below = j >= r + jnp.where(gb >= 1, 0, BLOCK)
            above = j <= r - jnp.where(gb <= nblocks - 2, 0, BLOCK)
            s = jnp.concatenate([jnp.where(below, s[:, :BLOCK], NEG_INF), s[:, BLOCK:2 * BLOCK],
                                 jnp.where(above, s[:, 2 * BLOCK:3 * BLOCK], NEG_INF), s[:, 3 * BLOCK:]], axis=1)
        scores.append(s)
        values.append(jnp.concatenate([vv, jnp.ones_like(vv)], axis=1))

    def per_head(kvh, col, f):
        parts = [f(col[g * BLOCK:(g + 1) * BLOCK, :], sink_ref[0, kvh * GQA_GROUP + g] * log2e) for g in range(GQA_GROUP)]
        return jnp.concatenate(parts, axis=0)

    maxes = [per_head(kvh, jnp.max(s, axis=1, keepdims=True), jnp.maximum) for (_, kvh), s in zip(units, scores)]
    probs = [jnp.exp2(s - m).astype(BF16) for s, m in zip(scores, maxes)]
    sums = [_dot(e, v) for e, v in zip(probs, values)]
    outs = []
    for (_, kvh), ow, m in zip(units, sums, maxes):
        den = ow[:, LANES:] + per_head(kvh, m, lambda mm, sk: jnp.exp2(sk - mm))
        outs.append(ow[:, :LANES] / den)
    for sb in range(seqs * sub):
        heads = [outs[sb * N_KV_HEADS + kvh][g * BLOCK:(g + 1) * BLOCK, :]
                 for kvh in range(N_KV_HEADS) for g in range(GQA_GROUP)]
        blks = [jnp.where(lane_q < HEAD_DIM, heads[2 * p], heads[2 * p + 1]) for p in range(N_HEADS // 2)]
        o_ref[sb * BLOCK:(sb + 1) * BLOCK, :] = jnp.concatenate(blks, axis=1).astype(o_ref.dtype)


def _attention(q, kd, vd, sink, batch, seq_len, ctx_k=None, ctx_v=None, layer=0):
    nb = seq_len // BLOCK
    has_local = ctx_k is not None
    sub = min(ATTN_SUB, nb)
    steps = nb // sub
    seqs = 1
    if not has_local:
        assert steps == 1 and batch % CTX_SEQS == 0
        seqs = CTX_SEQS
        batch //= seqs
    qb = seqs * sub * BLOCK
    qspec = pl.BlockSpec((qb, ATTN_WIDTH), lambda b, i: (b * steps + i, 0))
    sspec = pl.BlockSpec(memory_space=pltpu.SMEM)
    if has_local:
        lc = ctx_k.shape[2]
        edge = lambda f: pl.BlockSpec((BLOCK, 2 * KV_WIDTH), f)
        prev = lambda b, i: (b * nb + jnp.maximum(i * sub - 1, 0), 0)
        nxt = lambda b, i: (b * nb + jnp.minimum((i + 1) * sub, nb - 1), 0)
        cur = pl.BlockSpec((qb, 2 * KV_WIDTH), lambda b, i: (b * steps + i, 0))
        cspec = pl.BlockSpec((None, None, lc, KV_WIDTH), lambda b, i: (b, layer, 0, 0))
        in_specs = [sspec, qspec, edge(prev), cur, edge(nxt), edge(prev), cur, edge(nxt), cspec, cspec]
        args = [sink, q, kd, kd, kd, vd, vd, vd, ctx_k, ctx_v]
    else:
        kv = pl.BlockSpec((qb, 2 * KV_WIDTH), lambda b, i: (b, 0))
        in_specs = [sspec, qspec, kv, kv]
        args = [sink, q, kd, vd]
    return pl.pallas_call(
        functools.partial(_attn_kernel, has_local=has_local, nblocks=nb, sub=sub, seqs=seqs),
        grid=(batch, steps),
        in_specs=in_specs,
        out_specs=pl.BlockSpec((qb, ATTN_WIDTH), lambda b, i: (b * steps + i, 0)),
        out_shape=jax.ShapeDtypeStruct(q.shape, BF16),
        compiler_params=_cparams(("arbitrary", "arbitrary")),
        name="attn_latent" if has_local else "attn_context",
    )(*args)


def _filter_kernel(zt_ref, zb_ref, w1_ref, b1_ref, w2_ref, b2_ref, w3_ref, b3_ref, fr_ref, dl_ref, k_ref, sum_ref,
                   *, seq_len, rows):
    i = pl.program_id(0)
    half = rows // 2
    oc = HY_ORDER * HY_WIDTH
    z = jnp.concatenate([zt_ref[...], zb_ref[...]], axis=1)
    h = jnp.sin(fr_ref[0:1, :] * (_dot(z.astype(BF16), w1_ref[...].astype(BF16)) + b1_ref[...]))
    h = jnp.sin(fr_ref[1:2, :] * (_dot(h.astype(BF16), w2_ref[...].astype(BF16)) + b2_ref[...]))
    h3 = _dot(h.astype(BF16), w3_ref[...].astype(BF16)) + b3_ref[...]
    total = jnp.zeros((1, oc), F32)
    for part, z_ref in enumerate((zt_ref, zb_ref)):
        row = i * rows + part * half + lax.broadcasted_iota(jnp.int32, (half, oc), 0)
        t = z_ref[:, 0:1]
        decay = jnp.exp(-t * jnp.abs(dl_ref[...]))
        fwd_dir = h3[:, part * 2 * oc:part * 2 * oc + oc]
        bwd_dir = h3[:, part * 2 * oc + oc:(part + 1) * 2 * oc]
        sel = jnp.where(row < seq_len, fwd_dir, jnp.where(row > seq_len, bwd_dir, 0.0))
        k = sel * (decay + HY_MOD_SHIFT)
        k_ref[part * half:(part + 1) * half, :] = k
        total = total + jnp.sum(jnp.abs(k), axis=0, keepdims=True)

    @pl.when(i == 0)
    def _():
        sum_ref[...] = jnp.zeros_like(sum_ref)

    sum_ref[...] += total


def _block_diag2(w):
    z = jnp.zeros_like(w)
    return jnp.concatenate([jnp.concatenate([w, z], axis=1), jnp.concatenate([z, w], axis=1)], axis=0)


def _hyena_filter(z_ext, w1p, b1, w2, b2, w3, b3, freq, deltas, seq_len):
    n = 2 * seq_len
    rows = min(1024, n)
    half = rows // 2
    oc = HY_ORDER * HY_WIDTH
    hid = 2 * HY_FILTER_HIDDEN
    const = lambda i: (0, 0)
    two = lambda a: jnp.concatenate([a, a], axis=1)
    return pl.pallas_call(
        functools.partial(_filter_kernel, seq_len=seq_len, rows=rows),
        grid=(n // rows,),
        in_specs=[
            pl.BlockSpec((half, LANES), lambda i: (2 * i, 0)),
            pl.BlockSpec((half, LANES), lambda i: (2 * i + 1, 0)),
            pl.BlockSpec((2 * LANES, hid), const),
            pl.BlockSpec((1, hid), const),
            pl.BlockSpec((hid, hid), const),
            pl.BlockSpec((1, hid), const),
            pl.BlockSpec((hid, 4 * oc), const),
            pl.BlockSpec((1, 4 * oc), const),
            pl.BlockSpec((2, hid), const),
            pl.BlockSpec((1, oc), const),
        ],
        out_specs=[pl.BlockSpec((rows, oc), lambda i: (i, 0)), pl.BlockSpec((1, oc), const)],
        out_shape=[jax.ShapeDtypeStruct((n, oc), F32), jax.ShapeDtypeStruct((1, oc), F32)],
        compiler_params=_cparams(("arbitrary",)),
        name="hyena_filter",
    )(z_ext, z_ext, _block_diag2(w1p), two(b1), _block_diag2(w2), two(b2), _block_diag2(w3), two(b3), two(freq), deltas)


def _spectrum_kernel(k_ref, sum_ref, fwd_ref, kf_ref, hprev_ref, g0_ref, *, blk):
    t = pl.program_id(0)
    a = k_ref[...] / (sum_ref[...] + 1e-6)
    ha = _dot(fwd_ref[...], a.astype(BF16))
    tf = FREQ_TILE
    par = lax.broadcasted_iota(jnp.int32, (tf, 1), 0) % 2
    sgn = (1 - 2 * par).astype(F32)

    @pl.when(t > 0)
    def _():
        g0 = g0_ref[...]
        for c in range(blk // tf):
            re = slice(2 * c * tf, (2 * c + 1) * tf)
            im = slice((2 * c + 1) * tf, (2 * c + 2) * tf)
            kre = ha[re, :] - sgn * hprev_ref[im, :]
            kim = ha[im, :] + sgn * (hprev_ref[re, :] - g0)
            for o in range(HY_ORDER):
                lanes = slice(o * HY_WIDTH, (o + 1) * HY_WIDTH)
                kf_ref[o, c, 0:tf, :] = kre[:, lanes]
                kf_ref[o, c, tf:2 * tf, :] = kim[:, lanes]

    hprev_ref[...] = ha
    g0_ref[...] = a[0:1, :]


def _filter_spectrum(kraw, ksum, fwd, seq_len, blk):
    nb = seq_len // blk
    nlags = 2 * nb - 1
    oc = HY_ORDER * HY_WIDTH
    m = 2 * blk
    tf = FREQ_TILE
    nfc = blk // tf
    return pl.pallas_call(
        functools.partial(_spectrum_kernel, blk=blk),
        grid=(2 * nb,),
        in_specs=[
            pl.BlockSpec((blk, oc), lambda t: ((nb + t) % (2 * nb), 0)),
            pl.BlockSpec((1, oc), lambda t: (0, 0)),
            pl.BlockSpec((m, blk), lambda t: (0, 0)),
        ],
        out_specs=pl.BlockSpec((HY_ORDER, nfc, None, 2 * tf, HY_WIDTH), lambda t: (0, 0, jnp.maximum(t - 1, 0), 0, 0)),
        out_shape=jax.ShapeDtypeStruct((HY_ORDER, nfc, nlags, 2 * tf, HY_WIDTH), F32),
        scratch_shapes=[pltpu.VMEM((m, oc), F32), pltpu.VMEM((1, oc), F32)],
        compiler_params=_cparams(("arbitrary",)),
        name="filter_spectrum",
    )(kraw, ksum, fwd)


def _conv_kernel(z_ref, g_ref, swz_ref, sbz_ref, swg_ref, sbg_ref, bd_ref, kf_ref, fwd_ref, inv_ref, o_ref,
                 zb_ref, zc_ref, yf_ref, *, seq_len, blk, conv_z, seqs, sub):
    fc = pl.program_id(1)
    nfc = pl.num_programs(1)
    nb = seq_len // blk
    tf = FREQ_TILE
    blocks = [((g // nb) * seq_len, (g % nb) * blk) for g in range(seqs * nb)]

    @pl.when(fc == 0)
    def _():
        for base, r0 in blocks:
            rows = slice(base + r0, base + r0 + blk)
            if conv_z:
                zc_ref[rows, :] = _short_conv_chunk(z_ref, base, r0, blk, seq_len, swz_ref[...], sbz_ref[...])
                zb_ref[rows, :] = zc_ref[rows, :].astype(BF16)
            else:
                zb_ref[rows, :] = z_ref[rows, :].astype(BF16)

    zf = [[_dot(fwd_ref[s * 2 * tf:(s + 1) * 2 * tf, :], zb_ref[g * blk:(g + 1) * blk, :]) for g in range(seqs * nb)]
          for s in range(sub)]

    for s in range(sub):
        col = pl.multiple_of((fc * sub + s) * 2 * tf, 2 * tf)
        for g0 in range(0, seqs * nb, nb):
            for bi in range(nb):
                for r in range(0, tf, MAC_ROWS):
                    re = slice(r, r + MAC_ROWS)
                    im = slice(tf + r, tf + r + MAC_ROWS)
                    yr = jnp.zeros((MAC_ROWS, HY_WIDTH), F32)
                    yi = jnp.zeros((MAC_ROWS, HY_WIDTH), F32)
                    for bj in range(nb):
                        lag = bi - bj + nb - 1
                        kr, ki = kf_ref[s, lag, re, :], kf_ref[s, lag, im, :]
                        zr, zi = zf[s][g0 + bj][re, :], zf[s][g0 + bj][im, :]
                        yr = yr + (kr * zr - ki * zi)
                        yi = yi + (kr * zi + ki * zr)
                    yf_ref[g0 + bi, pl.ds(col + r, MAC_ROWS), :] = yr.astype(BF16)
                    yf_ref[g0 + bi, pl.ds(col + tf + r, MAC_ROWS), :] = yi.astype(BF16)

    @pl.when(fc == nfc - 1)
    def _():
        ys = [_dot(inv_ref[...], yf_ref[g]) for g in range(seqs * nb)]
        for (base, r0), y in zip(blocks, ys):
            rows = slice(base + r0, base + r0 + blk)
            z = zc_ref[rows, :] if conv_z else z_ref[rows, :]
            gate = _short_conv_chunk(g_ref, base, r0, blk, seq_len, swg_ref[...], sbg_ref[...])
            o_ref[rows, :] = (gate * (y + bd_ref[...] * z)).astype(o_ref.dtype)


def _hyena_conv(z_src, z_col, conv_z, g_src, g_col, short_w, short_b, bd, kf, order, fwd, inv,
                batch, seq_len, blk, out_dtype):
    nb = seq_len // blk
    nlags = 2 * nb - 1
    tf = FREQ_TILE
    sub = min(FREQ_SUB, blk // tf)
    nfc = blk // (sub * tf)
    w = HY_WIDTH
    seqs = _seqs_per_step(batch, seq_len)
    step_rows = seqs * seq_len
    return pl.pallas_call(
        functools.partial(_conv_kernel, seq_len=seq_len, blk=blk, conv_z=conv_z, seqs=seqs, sub=sub),
        grid=(batch // seqs, nfc),
        in_specs=[
            pl.BlockSpec((step_rows, w), lambda b, f: (b, 0)),
            pl.BlockSpec((step_rows, w), lambda b, f: (b, 0)),
            pl.BlockSpec((3, w), lambda b, f: (0, z_col if conv_z else 0)),
            pl.BlockSpec((1, w), lambda b, f: (0, z_col if conv_z else 0)),
            pl.BlockSpec((3, w), lambda b, f: (0, g_col)),
            pl.BlockSpec((1, w), lambda b, f: (0, g_col)),
            pl.BlockSpec((1, w), lambda b, f: (0, 0)),
            pl.BlockSpec((None, sub, nlags, 2 * tf, w), lambda b, f: (order, f, 0, 0, 0)),
            pl.BlockSpec((sub * 2 * tf, blk), lambda b, f: (f, 0)),
            pl.BlockSpec((blk, 2 * blk), lambda b, f: (0, 0)),
        ],
        out_specs=pl.BlockSpec((step_rows, w), lambda b, f: (b, 0)),
        out_shape=jax.ShapeDtypeStruct((batch * seq_len, w), out_dtype),
        scratch_shapes=[
            pltpu.VMEM((step_rows, w), BF16),
            pltpu.VMEM((step_rows if conv_z else SUBLANES, w), F32),
            pltpu.VMEM((seqs * nb, 2 * blk, w), BF16),
        ],
        compiler_params=_cparams(("arbitrary", "arbitrary")),
        name="hyena_conv",
    )(z_src, g_src, short_w, short_b, short_w, short_b, bd, kf, fwd, inv)


def _dft_tables(blk):
    m = 2 * blk
    tf = FREQ_TILE
    f = jnp.arange(blk, dtype=jnp.int32)
    n = jnp.arange(blk, dtype=jnp.int32)
    r = ((2 * f[:, None] + 1) * n[None, :]) % (2 * m)
    ang = r.astype(F32) * (math.pi / m)
    c, s = jnp.cos(ang), jnp.sin(ang)
    fwd = jnp.stack([c.reshape(blk // tf, tf, blk), (-s).reshape(blk // tf, tf, blk)], axis=1).reshape(m, blk)
    inv = fwd.T * (2.0 / m)
    return fwd.astype(BF16), inv.astype(BF16)


def _filter_embedding(seq_len):
    t = jnp.linspace(0.0, 1.0, seq_len, dtype=F32)[:, None]
    bands = (HY_EMB_DIM - 1) // 2
    f = jnp.linspace(1e-4, bands - 1, bands, dtype=F32)[None, :]
    w = 2.0 * math.pi * jnp.arange(seq_len, dtype=F32)[:, None] / seq_len
    z = jnp.concatenate([t, jnp.cos(f * w), -jnp.sin(f * w)], axis=-1)
    z_ext = jnp.concatenate([z, jnp.zeros((1, HY_EMB_DIM), F32), jnp.flip(z[1:], axis=0)], axis=0)
    return jnp.pad(z_ext, ((0, 0), (0, LANES - HY_EMB_DIM)))


def _rope_tables(seq_len):
    quarter = HEAD_DIM // 4
    inv = ROPE_THETA ** (-jnp.arange(quarter, dtype=F32) / quarter)
    lane = jnp.arange(LANES)
    t = jnp.arange(seq_len)[:, None]
    pos = jnp.where((lane % HEAD_DIM < HEAD_DIM // 2)[None, :], t // GRID_W, t % GRID_W).astype(F32)
    ang = pos * jnp.tile(inv, LANES // quarter)[None, :]
    first = (lane % (2 * quarter) < quarter)[None, :]
    sin = jnp.sin(ang)
    return jnp.cos(ang), jnp.where(first, -sin, 0.0), jnp.where(first, 0.0, sin)


def _block_ones(width):
    h = jnp.arange(width) // HEAD_DIM
    return (h[:, None] == h[None, :]).astype(BF16)


def _stream_layer(x, p, wts, mod_l, batch, seq_len, rope_tabs, ctx, layer, hy_blk, consts):
    gq, gk, fwd, inv, z_ext = consts
    latent = ctx is not None
    outs = _token_a(x, mod_l, p["norm"], wts, layer, gq, gk, p["qg"], p["kg"], rope_tabs, seq_len)
    x1, u_pool, q, kd, vd, hy_v, hy_x1, hy_x2 = outs[:8]
    k, v = (None, None) if latent else outs[8:]
    y_pool = _pool_mix(u_pool, p["pool_w"], p["pool_scale"], batch, seq_len)
    if latent:
        a = _attention(q, kd, vd, p["sink"], batch, seq_len, ctx[0], ctx[1], layer)
    else:
        a = _attention(q, kd, vd, p["sink"], batch, seq_len)
    kraw, ksum = _hyena_filter(z_ext, p["f_w1"], p["f_b1"], p["f_w2"], p["f_b2"], p["f_w3"], p["f_b3"],
                               p["f_freq"], p["decay"], seq_len)
    kf = _filter_spectrum(kraw, ksum, fwd, seq_len, hy_blk)
    z1 = _hyena_conv(hy_v, 0, True, hy_x1, 1, p["short_w"], p["short_b"], p["hy_bias"][0:1], kf, 0, fwd, inv,
                     batch, seq_len, hy_blk, F32)
    y_hy = _hyena_conv(z1, 0, False, hy_x2, 2, p["short_w"], p["short_b"], p["hy_bias"][1:2], kf, 1, fwd, inv,
                       batch, seq_len, hy_blk, BF16)
    x3 = _token_b(x1, y_pool, a, y_hy, mod_l, p["norm"], wts, layer, seq_len, latent)
    return x3, k, v


def kernel(x_prompt, x_sample, cache_k, cache_v, c, c_ctx, ada_w, ada_b, norm_w, ffn_wg, ffn_wu, ffn_wd, w_in, w_out, pool_w, pool_scale, q_norm, k_norm, attn_sink, hy_short_w, hy_short_b, hy_f_w1, hy_f_b1, hy_f_w2, hy_f_b2, hy_f_w3, hy_f_b3, hy_sin_freq, hy_decay, hy_bias):
    bp, lp, _ = x_prompt.shape
    bs, ls, _ = x_sample.shape
    lc = cache_k.shape[2]

    cond = jnp.concatenate([c_ctx[None, :], c, jnp.zeros((COND_ROWS - 1 - bs, D_MODEL), F32)], axis=0)
    mod = _ada_mod(cond, ada_w, ada_b).reshape(DEPTH, COND_ROWS, 1, N_MOD * D_MODEL)

    gq, gk = _block_ones(ATTN_WIDTH), _block_ones(KV_WIDTH)
    rope_tabs = _rope_tables(ls)
    blk_p, blk_s = min(lp, HY_BLOCK), min(ls, HY_BLOCK)
    consts_p = (gq, gk) + _dft_tables(blk_p) + (_filter_embedding(lp),)
    consts_s = (gq, gk) + _dft_tables(blk_s) + (_filter_embedding(ls),)
    ctx_k = cache_k.reshape(bs, DEPTH, lc, KV_WIDTH)
    ctx_v = cache_v.reshape(bs, DEPTH, lc, KV_WIDTH)

    wts = _cast_weights(ffn_wg, ffn_wu, ffn_wd, w_in, w_out)
    yp = x_prompt.reshape(bp * lp, D_MODEL)
    ys = x_sample.reshape(bs * ls, D_MODEL)
    ks, vs = [], []
    eye = jnp.eye(len(POOL_WINDOWS), dtype=F32)
    for l in range(DEPTH):
        pool_bd = (eye[:, None, :, None] * pool_w[l][:, :, None, :]).reshape(POOL_WIDTH, POOL_WIDTH)
        p = {
            "norm": norm_w[l],
            "pool_w": pool_bd.astype(BF16), "pool_scale": pool_scale[l][None, :],
            "qg": jnp.tile(q_norm[l], N_HEADS)[None, :], "kg": jnp.tile(k_norm[l], N_KV_HEADS)[None, :],
            "sink": attn_sink[l][None, :],
            "short_w": hy_short_w[l], "short_b": hy_short_b[l][None, :],
            "f_w1": jnp.pad(hy_f_w1[l], ((0, LANES - HY_EMB_DIM), (0, 0))), "f_b1": hy_f_b1[l][None, :],
            "f_w2": hy_f_w2[l], "f_b2": hy_f_b2[l][None, :],
            "f_w3": hy_f_w3[l], "f_b3": hy_f_b3[l][None, :],
            "f_freq": hy_sin_freq[l], "decay": hy_decay[l].reshape(1, HY_ORDER * HY_WIDTH),
            "hy_bias": hy_bias[l],
        }
        yp, k_l, v_l = _stream_layer(yp, p, wts, mod[l], bp, lp, None, None, l, blk_p, consts_p)
        ks.append(k_l.reshape(bp, lp, N_KV_HEADS, HEAD_DIM))
        vs.append(v_l.reshape(bp, lp, N_KV_HEADS, HEAD_DIM))
        ys, _, _ = _stream_layer(ys, p, wts, mod[l], bs, ls, rope_tabs, (ctx_k, ctx_v), l, blk_s, consts_s)
    return (yp.reshape(bp, lp, D_MODEL), ys.reshape(bs, ls, D_MODEL),
            jnp.stack(ks, axis=1), jnp.stack(vs, axis=1))
```

```python
import functools
import math

import jax
import jax.numpy as jnp
from jax import lax
from jax.experimental import pallas as pl
from jax.experimental.pallas import tpu as pltpu

F32 = jnp.float32
BF16 = jnp.bfloat16

D_MODEL = 1024
DEPTH = 2
GRID_W = 64
POOL_WINDOWS = (2, 4, 8, 16)
POOL_WIDTH = 256
POOL_GROUP = 64
HEAD_DIM = 64
N_HEADS = 8
N_KV_HEADS = 2
GQA_GROUP = 4
ATTN_WIDTH = 512
KV_WIDTH = 128
WINDOW = 128
BLOCK = 128
ROPE_THETA = 10000.0
HY_WIDTH = 256
HY_ORDER = 2
HY_EMB_DIM = 33
HY_FILTER_HIDDEN = 64
HY_MOD_SHIFT = 0.05
D_FF = 2816
IN_WIDTH = 1792
N_MOD = 9
NORM_EPS = 1e-6
NEG_INF = -1e30

LANES = 128
SUBLANES = 8
VMEM_LIMIT = 56 * 1024 * 1024

TOKEN_TILE = 512
TOKEN_TILE_B = 1024
MXU_DIM = 256
FF_CHUNKS = (1280, 1536)
SEQ_CHUNK = 512
SEQ_STEP_ROWS = 2048
HY_BLOCK = 1024
FREQ_TILE = 128
FREQ_SUB = 2
MAC_ROWS = 32
ATTN_SUB = 4
CTX_SEQS = 2
COND_ROWS = 8


def _cparams(sem):
    return pltpu.CompilerParams(dimension_semantics=sem, vmem_limit_bytes=VMEM_LIMIT)


def _dot(a, b):
    return jnp.dot(a, b, preferred_element_type=F32)


def _cast_kernel(*refs):
    n = len(refs) // 2
    for i_ref, o_ref in zip(refs[:n], refs[n:]):
        o_ref[...] = i_ref[...].astype(o_ref.dtype)


def _flat_weights(ffn_wg, ffn_wu, ffn_wd, w_in, w_out):
    arrs = {"wg": ffn_wg, "wu": ffn_wu, "wd": ffn_wd, "w_in": w_in, "w_out": w_out}
    return {k: a.reshape(-1, a.shape[-1]) for k, a in arrs.items()}


def _layer_cast_specs(flat, layer, steps):
    in_specs, out_specs, out_shapes = [], [], []
    for a in flat.values():
        rows = a.shape[0] // DEPTH
        blk = rows // steps
        in_specs.append(pl.BlockSpec((blk, a.shape[1]), lambda i, first=layer * steps: (first + i, 0)))
        out_specs.append(pl.BlockSpec((blk, a.shape[1]), lambda i: (i, 0)))
        out_shapes.append(jax.ShapeDtypeStruct((rows, a.shape[1]), BF16))
    return in_specs, out_specs, out_shapes


def _layer_weights(outs):
    wg, wu, wd, w_in, w_out = outs
    return {"wg": wg.reshape(2, D_MODEL, D_FF), "wu": wu.reshape(2, D_MODEL, D_FF),
            "wd": wd.reshape(2, D_FF, D_MODEL), "w_in": w_in, "w_out": w_out}


def _cast_layer(flat, layer):
    in_specs, out_specs, out_shapes = _layer_cast_specs(flat, layer, steps=16)
    outs = pl.pallas_call(
        _cast_kernel,
        grid=(16,),
        in_specs=in_specs,
        out_specs=out_specs,
        out_shape=out_shapes,
        compiler_params=_cparams(("arbitrary",)),
        name="cast_weights",
    )(*flat.values())
    return _layer_weights(outs)


def _ada_kernel(c_ref, w_ref, b_ref, o_ref):
    c = c_ref[...]
    s = (c * jax.nn.sigmoid(c)).astype(BF16)
    o_ref[...] = _dot(s, w_ref[...].astype(BF16)) + b_ref[...]


def _ada_mod(cond, ada_w, ada_b):
    tn = 3072
    nw = N_MOD * D_MODEL
    return pl.pallas_call(
        _ada_kernel,
        grid=(DEPTH, nw // tn),
        in_specs=[
            pl.BlockSpec((COND_ROWS, D_MODEL), lambda l, j: (0, 0)),
            pl.BlockSpec((None, D_MODEL, tn), lambda l, j: (l, 0, j)),
            pl.BlockSpec((None, 1, tn), lambda l, j: (l, 0, j)),
        ],
        out_specs=pl.BlockSpec((None, COND_ROWS, tn), lambda l, j: (l, 0, j)),
        out_shape=jax.ShapeDtypeStruct((DEPTH, COND_ROWS, nw), F32),
        compiler_params=_cparams(("arbitrary", "arbitrary")),
        name="ada_mod",
    )(cond, ada_w, ada_b.reshape(DEPTH, 1, nw))


def _mod_slice(mod_ref, k):
    return mod_ref[:, k * D_MODEL:(k + 1) * D_MODEL]


def _rms_mod(x, gain, scale, shift):
    y = x * lax.rsqrt(jnp.mean(x * x, axis=-1, keepdims=True) + NORM_EPS)
    return (y * gain) * (1.0 + scale) + shift


def _swiglu(hb, wg_ref, wu_ref, wd_ref):
    acc = None
    lo = 0
    for width in FF_CHUNKS:
        sl = slice(lo, lo + width)
        lo += width
        g = _dot(hb, wg_ref[:, sl])
        u = _dot(hb, wu_ref[:, sl])
        a = ((g * jax.nn.sigmoid(g)) * u).astype(BF16)
        y = _dot(a, wd_ref[sl, :])
        acc = y if acc is None else acc + y
    return acc


def _head_norm(x, gmat, gain):
    ss = _dot((x * x).astype(BF16), gmat)
    return (x * lax.rsqrt(ss * (1.0 / HEAD_DIM) + NORM_EPS)) * gain


def _rope(x, cos, sa, sb):
    w = x.shape[1]
    xn = pltpu.roll(x, w - 16, axis=1)
    xp = pltpu.roll(x, 16, axis=1)
    return x * cos + xn * sa + xp * sb


def _token_a_kernel(*refs, rope, n_cast):
    n_in = 11 + (3 if rope else 0) + n_cast
    ins, outs = refs[:n_in], refs[n_in:]
    x_ref, mod_ref, nrm_ref, wg_ref, wu_ref, wd_ref, win_ref, gq_ref, gk_ref, qg_ref, kg_ref = ins[:11]
    if rope:
        cos_ref, sa_ref, sb_ref = ins[11:14]
    x1_ref, up_ref, q_ref, kd_ref, vd_ref, hv_ref, hx1_ref, hx2_ref = outs[:8]
    if not rope:
        k_ref, v_ref = outs[8:10]
    for i_ref, o_ref in zip(ins[n_in - n_cast:], outs[len(outs) - n_cast:]):
        o_ref[...] = i_ref[...].astype(o_ref.dtype)
    x = x_ref[...]
    sh1, sc1, g1 = _mod_slice(mod_ref, 0), _mod_slice(mod_ref, 1), _mod_slice(mod_ref, 2)
    sh2, sc2 = _mod_slice(mod_ref, 3), _mod_slice(mod_ref, 4)
    h = _rms_mod(x, nrm_ref[0:1, :], sc1, sh1).astype(BF16)
    x1 = x + (0.5 * g1) * _swiglu(h, wg_ref, wu_ref, wd_ref)
    x1_ref[...] = x1
    h2 = _rms_mod(x1, nrm_ref[1:2, :], sc2, sh2).astype(BF16)
    s1 = POOL_WIDTH
    s2 = s1 + ATTN_WIDTH
    s3 = s2 + KV_WIDTH
    s4 = s3 + KV_WIDTH
    qkv = _dot(h2, win_ref[:, s1:s4])
    q = _head_norm(qkv[:, :ATTN_WIDTH], gq_ref[...], qg_ref[...])
    k = _head_norm(qkv[:, ATTN_WIDTH:ATTN_WIDTH + KV_WIDTH], gk_ref[...], kg_ref[...])
    v = qkv[:, ATTN_WIDTH + KV_WIDTH:]
    up_ref[...] = _dot(h2, win_ref[:, :s1])
    hy = _dot(h2, win_ref[:, s4:])
    for j, ref in enumerate((hv_ref, hx1_ref, hx2_ref)):
        ref[...] = hy[:, j * HY_WIDTH:(j + 1) * HY_WIDTH]
    if rope:
        cos, sa, sb = cos_ref[...], sa_ref[...], sb_ref[...]
        reps = ATTN_WIDTH // LANES
        q = _rope(q, jnp.concatenate([cos] * reps, axis=1), jnp.concatenate([sa] * reps, axis=1),
                  jnp.concatenate([sb] * reps, axis=1))
        k = _rope(k, cos, sa, sb)
    q_ref[...] = (q * (HEAD_DIM ** -0.5 * math.log2(math.e))).astype(BF16)
    kd_ref[...] = _dup_heads(k)
    vd_ref[...] = _dup_heads(v)
    if not rope:
        k_ref[...] = k
        v_ref[...] = v


def _token_a(x, mod_l, nrm, wts, gq, gk, qg, kg, rope_tabs, seq_len, cast_next=None):
    n = x.shape[0]
    tm = TOKEN_TILE
    tiles_per_seq = max(seq_len // tm, 1)
    rope = rope_tabs is not None
    if rope:
        mod_row = lambda i: (1 + i // tiles_per_seq, 0, 0)
    else:
        mod_row = lambda i: (0, 0, 0)
    const = lambda i: (0, 0)
    row = lambda i: (i, 0)
    in_specs = [
        pl.BlockSpec((tm, D_MODEL), row),
        pl.BlockSpec((None, 1, N_MOD * D_MODEL), mod_row),
        pl.BlockSpec((3, D_MODEL), const),
        pl.BlockSpec((None, D_MODEL, D_FF), lambda i: (0, 0, 0)),
        pl.BlockSpec((None, D_MODEL, D_FF), lambda i: (0, 0, 0)),
        pl.BlockSpec((None, D_FF, D_MODEL), lambda i: (0, 0, 0)),
        pl.BlockSpec((D_MODEL, IN_WIDTH), const),
        pl.BlockSpec((ATTN_WIDTH, ATTN_WIDTH), const),
        pl.BlockSpec((KV_WIDTH, KV_WIDTH), const),
        pl.BlockSpec((1, ATTN_WIDTH), const),
        pl.BlockSpec((1, KV_WIDTH), const),
    ]
    args = [x, mod_l, nrm, wts["wg"], wts["wu"], wts["wd"], wts["w_in"], gq, gk, qg, kg]
    if rope:
        tab = pl.BlockSpec((tm, LANES), lambda i: (i % tiles_per_seq, 0))
        in_specs += [tab, tab, tab]
        args += list(rope_tabs)
    widths = (D_MODEL, POOL_WIDTH, ATTN_WIDTH, 2 * KV_WIDTH, 2 * KV_WIDTH) + (HY_WIDTH,) * (HY_ORDER + 1)
    dtypes = (F32, F32, BF16, BF16, BF16) + (F32,) * (HY_ORDER + 1)
    if not rope:
        widths += (KV_WIDTH, KV_WIDTH)
        dtypes += (F32, F32)
    out_specs = [pl.BlockSpec((tm, w), row) for w in widths]
    out_shapes = [jax.ShapeDtypeStruct((n, w), d) for w, d in zip(widths, dtypes)]
    n_cast = 0
    if cast_next is not None:
        flat, next_layer = cast_next
        c_in, c_out, c_shapes = _layer_cast_specs(flat, next_layer, steps=n // tm)
        in_specs += c_in
        args += list(flat.values())
        out_specs += c_out
        out_shapes += c_shapes
        n_cast = len(c_in)
    outs = pl.pallas_call(
        functools.partial(_token_a_kernel, rope=rope, n_cast=n_cast),
        grid=(n // tm,),
        in_specs=in_specs,
        out_specs=out_specs,
        out_shape=out_shapes,
        compiler_params=_cparams(("arbitrary",)),
        name="token_a_rope" if rope else "token_a",
    )(*args)
    if n_cast:
        return outs[:len(outs) - n_cast], _layer_weights(outs[len(outs) - n_cast:])
    return outs, None


def _token_b_kernel(x_ref, yp_ref, a_ref, yh_ref, mod_ref, nrm_ref, wo_ref, wg_ref, wu_ref, wd_ref, o_ref):
    x1 = x_ref[...]
    g2 = _mod_slice(mod_ref, 5)
    sh3, sc3, g3 = _mod_slice(mod_ref, 6), _mod_slice(mod_ref, 7), _mod_slice(mod_ref, 8)
    cat = jnp.concatenate([yp_ref[...], a_ref[...], yh_ref[...]], axis=1)
    x2 = x1 + g2 * _dot(cat, wo_ref[...])
    h3 = _rms_mod(x2, nrm_ref[2:3, :], sc3, sh3).astype(BF16)
    o_ref[...] = x2 + (0.5 * g3) * _swiglu(h3, wg_ref, wu_ref, wd_ref)


def _token_b(x1, y_pool, a, y_hy, mod_l, nrm, wts, seq_len, per_seq_cond):
    n = x1.shape[0]
    tm = min(TOKEN_TILE_B, n)
    tiles_per_seq = max(seq_len // tm, 1)
    if per_seq_cond:
        mod_row = lambda i: (1 + i // tiles_per_seq, 0, 0)
    else:
        mod_row = lambda i: (0, 0, 0)
    const = lambda i: (0, 0)
    row = lambda i: (i, 0)
    return pl.pallas_call(
        _token_b_kernel,
        grid=(n // tm,),
        in_specs=[
            pl.BlockSpec((tm, D_MODEL), row),
            pl.BlockSpec((tm, POOL_WIDTH), row),
            pl.BlockSpec((tm, ATTN_WIDTH), row),
            pl.BlockSpec((tm, HY_WIDTH), row),
            pl.BlockSpec((None, 1, N_MOD * D_MODEL), mod_row),
            pl.BlockSpec((3, D_MODEL), const),
            pl.BlockSpec((D_MODEL, D_MODEL), const),
            pl.BlockSpec((None, D_MODEL, D_FF), lambda i: (1, 0, 0)),
            pl.BlockSpec((None, D_MODEL, D_FF), lambda i: (1, 0, 0)),
            pl.BlockSpec((None, D_FF, D_MODEL), lambda i: (1, 0, 0)),
        ],
        out_specs=pl.BlockSpec((tm, D_MODEL), row),
        out_shape=jax.ShapeDtypeStruct((n, D_MODEL), F32),
        compiler_params=_cparams(("arbitrary",)),
        name="token_b",
    )(x1, y_pool, a, y_hy, mod_l, nrm, wts["w_out"], wts["wg"], wts["wu"], wts["wd"])


def _seqs_per_step(batch, seq_len):
    return min(batch, max(1, SEQ_STEP_ROWS // seq_len))


def _halo_rows(src_ref, base, r0, rows, seq_len):
    c = src_ref.shape[1]
    zero = jnp.zeros((SUBLANES, c), F32)
    lo = base + r0
    prev = src_ref[lo - SUBLANES:lo, :] if r0 > 0 else zero
    nxt = src_ref[lo + rows:lo + rows + SUBLANES, :] if r0 + rows < seq_len else zero
    return prev, nxt


def _short_conv_chunk(src_ref, base, r0, rows, seq_len, w, b):
    x = src_ref[base + r0:base + r0 + rows, :]
    prev, nxt = _halo_rows(src_ref, base, r0, rows, seq_len)
    ridx = lax.broadcasted_iota(jnp.int32, x.shape, 0)
    xp = jnp.where(ridx == 0, prev[SUBLANES - 1:SUBLANES, :], pltpu.roll(x, 1, axis=0))
    xn = jnp.where(ridx == rows - 1, nxt[0:1, :], pltpu.roll(x, rows - 1, axis=0))
    return xp * w[0:1, :] + x * w[1:2, :] + xn * w[2:3, :] + b


def _pool_kernel(u_ref, w_ref, scale_ref, o_ref, *, seq_len, rows, seqs):
    lane = lax.broadcasted_iota(jnp.int32, (rows, POOL_WIDTH), 1)
    grp = lane // POOL_GROUP
    half = jnp.where(grp == 0, 1, jnp.where(grp == 1, 2, jnp.where(grp == 2, 4, 8)))
    ext = rows + 2 * SUBLANES
    for base, r0 in [(sq * seq_len, r0) for sq in range(seqs) for r0 in range(0, seq_len, rows)]:
        x = u_ref[base + r0:base + r0 + rows, :]
        prev, nxt = _halo_rows(u_ref, base, r0, rows, seq_len)
        a = jnp.concatenate([prev, x, nxt], axis=0)
        back = lambda v, s: pltpu.roll(v, s, axis=0)
        fwd = lambda v, s: pltpu.roll(v, ext - s, axis=0)
        b1 = back(a, 1)
        b2 = b1 + back(b1, 1)
        b4 = b2 + back(b2, 2)
        b8 = b4 + back(b4, 4)
        f2 = a + fwd(a, 1)
        f4 = f2 + fwd(f2, 2)
        f8 = f4 + fwd(f4, 4)
        core = lambda v: v[SUBLANES:SUBLANES + rows, :]
        wsum = jnp.where(grp == 0, core(b1) + x,
                         jnp.where(grp == 1, core(b2) + core(f2),
                                   jnp.where(grp == 2, core(b4) + core(f4), core(b8) + core(f8))))
        t = r0 + lax.broadcasted_iota(jnp.int32, (rows, POOL_WIDTH), 0)
        cnt = jnp.minimum(t + half, seq_len) - jnp.maximum(t - half, 0)
        d = wsum / cnt.astype(F32) - x
        y = _dot(d.astype(BF16), w_ref[...]) * scale_ref[...]
        o_ref[base + r0:base + r0 + rows, :] = y.astype(o_ref.dtype)


def _pool_mix(u_pool, w_bd, scale, batch, seq_len):
    rows = min(SEQ_CHUNK // 2, seq_len)
    seqs = _seqs_per_step(batch, seq_len)
    return pl.pallas_call(
        functools.partial(_pool_kernel, seq_len=seq_len, rows=rows, seqs=seqs),
        grid=(batch // seqs,),
        in_specs=[
            pl.BlockSpec((seqs * seq_len, POOL_WIDTH), lambda b: (b, 0)),
            pl.BlockSpec((POOL_WIDTH, POOL_WIDTH), lambda b: (0, 0)),
            pl.BlockSpec((1, POOL_WIDTH), lambda b: (0, 0)),
        ],
        out_specs=pl.BlockSpec((seqs * seq_len, POOL_WIDTH), lambda b: (b, 0)),
        out_shape=jax.ShapeDtypeStruct((batch * seq_len, POOL_WIDTH), BF16),
        compiler_params=_cparams(("arbitrary",)),
        name="pool_mix",
    )(u_pool, w_bd, scale)


def _dup_heads(x):
    lane = lax.broadcasted_iota(jnp.int32, x.shape, 1)
    sw = pltpu.roll(x, HEAD_DIM, axis=1)
    lo = lane < HEAD_DIM
    return jnp.concatenate([jnp.where(lo, x, sw), jnp.where(lo, sw, x)], axis=1).astype(BF16)


def _attn_kernel(*refs, has_local, nblocks, sub, seqs):
    if has_local:
        sink_ref, q_ref, kp_ref, kc_ref, kn_ref, vp_ref, vc_ref, vn_ref, ck_ref, cv_ref, o_ref = refs
        kwin = jnp.concatenate([kp_ref[...], kc_ref[...], kn_ref[...]], axis=0)
        vwin = jnp.concatenate([vp_ref[...], vc_ref[...], vn_ref[...]], axis=0)
        kctx, vctx = _dup_heads(ck_ref[...]), _dup_heads(cv_ref[...])
        r = lax.broadcasted_iota(jnp.int32, (GQA_GROUP * BLOCK, BLOCK), 0) % BLOCK
        j = lax.broadcasted_iota(jnp.int32, (GQA_GROUP * BLOCK, BLOCK), 1)
    else:
        sink_ref, q_ref, kd_ref, vd_ref, o_ref = refs
    i = pl.program_id(1)
    lane_q = lax.broadcasted_iota(jnp.int32, (BLOCK, LANES), 1)
    log2e = math.log2(math.e)
    units = [(sb, kvh) for sb in range(seqs * sub) for kvh in range(N_KV_HEADS)]
    scores, values = [], []
    for sb, kvh in units:
        q = q_ref[sb * BLOCK:(sb + 1) * BLOCK, :]
        qparts = []
        for hd in range(kvh * GQA_GROUP, (kvh + 1) * GQA_GROUP):
            qp = q[:, (hd // 2) * LANES:(hd // 2 + 1) * LANES]
            keep = (lane_q < HEAD_DIM) == (hd % 2 == 0)
            qparts.append(jnp.where(keep, qp, jnp.zeros_like(qp)))
        qs = jnp.concatenate(qparts, axis=0)
        cols = slice(kvh * LANES, (kvh + 1) * LANES)
        if has_local:
            loc = slice(sb * BLOCK, (sb + 3) * BLOCK)
            kk = jnp.concatenate([kwin[loc, cols], kctx[:, cols]], axis=0)
            vv = jnp.concatenate([vwin[loc, cols], vctx[:, cols]], axis=0)
        else:
            own = slice((sb // sub) * sub * BLOCK, (sb // sub + 1) * sub * BLOCK)
            kk, vv = kd_ref[own, cols], vd_ref[own, cols]
        s = lax.dot_general(qs, kk, (((1,), (1,)), ((), ())), preferred_element_type=F32)
        if has_local:
            gb = i * sub + sb
            below = j >= r + jnp.where(gb >= 1, 0, BLOCK)
            above = j <= r - jnp.where(gb <= nblocks - 2, 0, BLOCK)
            s = jnp.concatenate([jnp.where(below, s[:, :BLOCK], NEG_INF), s[:, BLOCK:2 * BLOCK],
                                 jnp.where(above, s[:, 2 * BLOCK:3 * BLOCK], NEG_INF), s[:, 3 * BLOCK:]], axis=1)
        scores.append(s)
        values.append(jnp.concatenate([vv, jnp.ones_like(vv)], axis=1))

    def per_head(kvh, col, f):
        parts = [f(col[g * BLOCK:(g + 1) * BLOCK, :], sink_ref[0, kvh * GQA_GROUP + g] * log2e) for g in range(GQA_GROUP)]
        return jnp.concatenate(parts, axis=0)

    maxes = [per_head(kvh, jnp.max(s, axis=1, keepdims=True), jnp.maximum) for (_, kvh), s in zip(units, scores)]
    probs = [jnp.exp2(s - m).astype(BF16) for s, m in zip(scores, maxes)]
    sums = [_dot(e, v) for e, v in zip(probs, values)]
    outs = []
    for (_, kvh), ow, m in zip(units, sums, maxes):
        den = ow[:, LANES:] + per_head(kvh, m, lambda mm, sk: jnp.exp2(sk - mm))
        outs.append(ow[:, :LANES] / den)
    for sb in range(seqs * sub):
        heads = [outs[sb * N_KV_HEADS + kvh][g * BLOCK:(g + 1) * BLOCK, :]
                 for kvh in range(N_KV_HEADS) for g in range(GQA_GROUP)]
        blks = [jnp.where(lane_q < HEAD_DIM, heads[2 * p], heads[2 * p + 1]) for p in range(N_HEADS // 2)]
        o_ref[sb * BLOCK:(sb + 1) * BLOCK, :] = jnp.concatenate(blks, axis=1).astype(o_ref.dtype)


def _attention(q, kd, vd, sink, batch, seq_len, ctx_k=None, ctx_v=None, layer=0):
    nb = seq_len // BLOCK
    has_local = ctx_k is not None
    sub = min(ATTN_SUB, nb)
    steps = nb // sub
    seqs = 1
    if not has_local:
        assert steps == 1 and batch % CTX_SEQS == 0
        seqs = CTX_SEQS
        batch //= seqs
    qb = seqs * sub * BLOCK
    qspec = pl.BlockSpec((qb, ATTN_WIDTH), lambda b, i: (b * steps + i, 0))
    sspec = pl.BlockSpec(memory_space=pltpu.SMEM)
    if has_local:
        lc = ctx_k.shape[2]
        edge = lambda f: pl.BlockSpec((BLOCK, 2 * KV_WIDTH), f)
        prev = lambda b, i: (b * nb + jnp.maximum(i * sub - 1, 0), 0)
        nxt = lambda b, i: (b * nb + jnp.minimum((i + 1) * sub, nb - 1), 0)
        cur = pl.BlockSpec((qb, 2 * KV_WIDTH), lambda b, i: (b * steps + i, 0))
        cspec = pl.BlockSpec((None, None, lc, KV_WIDTH), lambda b, i: (b, layer, 0, 0))
        in_specs = [sspec, qspec, edge(prev), cur, edge(nxt), edge(prev), cur, edge(nxt), cspec, cspec]
        args = [sink, q, kd, kd, kd, vd, vd, vd, ctx_k, ctx_v]
    else:
        kv = pl.BlockSpec((qb, 2 * KV_WIDTH), lambda b, i: (b, 0))
        in_specs = [sspec, qspec, kv, kv]
        args = [sink, q, kd, vd]
    return pl.pallas_call(
        functools.partial(_attn_kernel, has_local=has_local, nblocks=nb, sub=sub, seqs=seqs),
        grid=(batch, steps),
        in_specs=in_specs,
        out_specs=pl.BlockSpec((qb, ATTN_WIDTH), lambda b, i: (b * steps + i, 0)),
        out_shape=jax.ShapeDtypeStruct(q.shape, BF16),
        compiler_params=_cparams(("arbitrary", "arbitrary")),
        name="attn_latent" if has_local else "attn_context",
    )(*args)


def _filter_kernel(zt_ref, zb_ref, w1_ref, b1_ref, w2_ref, b2_ref, w3_ref, b3_ref, fr_ref, dl_ref, k_ref, sum_ref,
                   *, seq_len, rows):
    i = pl.program_id(0)
    half = rows // 2
    oc = HY_ORDER * HY_WIDTH
    z = jnp.concatenate([zt_ref[...], zb_ref[...]], axis=1)
    h = jnp.sin(fr_ref[0:1, :] * (_dot(z.astype(BF16), w1_ref[...].astype(BF16)) + b1_ref[...]))
    h = jnp.sin(fr_ref[1:2, :] * (_dot(h.astype(BF16), w2_ref[...].astype(BF16)) + b2_ref[...]))
    h3 = _dot(h.astype(BF16), w3_ref[...].astype(BF16)) + b3_ref[...]
    total = jnp.zeros((1, oc), F32)
    for part, z_ref in enumerate((zt_ref, zb_ref)):
        row = i * rows + part * half + lax.broadcasted_iota(jnp.int32, (half, oc), 0)
        t = z_ref[:, 0:1]
        decay = jnp.exp(-t * jnp.abs(dl_ref[...]))
        fwd_dir = h3[:, part * 2 * oc:part * 2 * oc + oc]
        bwd_dir = h3[:, part * 2 * oc + oc:(part + 1) * 2 * oc]
        sel = jnp.where(row < seq_len, fwd_dir, jnp.where(row > seq_len, bwd_dir, 0.0))
        k = sel * (decay + HY_MOD_SHIFT)
        k_ref[part * half:(part + 1) * half, :] = k
        total = total + jnp.sum(jnp.abs(k), axis=0, keepdims=True)

    @pl.when(i == 0)
    def _():
        sum_ref[...] = jnp.zeros_like(sum_ref)

    sum_ref[...] += total


def _block_diag2(w):
    z = jnp.zeros_like(w)
    return jnp.concatenate([jnp.concatenate([w, z], axis=1), jnp.concatenate([z, w], axis=1)], axis=0)


def _hyena_filter(z_ext, w1p, b1, w2, b2, w3, b3, freq, deltas, seq_len):
    n = 2 * seq_len
    rows = min(1024, n)
    half = rows // 2
    oc = HY_ORDER * HY_WIDTH
    hid = 2 * HY_FILTER_HIDDEN
    const = lambda i: (0, 0)
    two = lambda a: jnp.concatenate([a, a], axis=1)
    return pl.pallas_call(
        functools.partial(_filter_kernel, seq_len=seq_len, rows=rows),
        grid=(n // rows,),
        in_specs=[
            pl.BlockSpec((half, LANES), lambda i: (2 * i, 0)),
            pl.BlockSpec((half, LANES), lambda i: (2 * i + 1, 0)),
            pl.BlockSpec((2 * LANES, hid), const),
            pl.BlockSpec((1, hid), const),
            pl.BlockSpec((hid, hid), const),
            pl.BlockSpec((1, hid), const),
            pl.BlockSpec((hid, 4 * oc), const),
            pl.BlockSpec((1, 4 * oc), const),
            pl.BlockSpec((2, hid), const),
            pl.BlockSpec((1, oc), const),
        ],
        out_specs=[pl.BlockSpec((rows, oc), lambda i: (i, 0)), pl.BlockSpec((1, oc), const)],
        out_shape=[jax.ShapeDtypeStruct((n, oc), F32), jax.ShapeDtypeStruct((1, oc), F32)],
        compiler_params=_cparams(("arbitrary",)),
        name="hyena_filter",
    )(z_ext, z_ext, _block_diag2(w1p), two(b1), _block_diag2(w2), two(b2), _block_diag2(w3), two(b3), two(freq), deltas)


def _spectrum_kernel(k_ref, sum_ref, fwd_ref, kf_ref, hprev_ref, g0_ref, *, blk):
    t = pl.program_id(0)
    a = k_ref[...] / (sum_ref[...] + 1e-6)
    ha = _dot(fwd_ref[...], a.astype(BF16))
    tf = FREQ_TILE
    par = lax.broadcasted_iota(jnp.int32, (tf, 1), 0) % 2
    sgn = (1 - 2 * par).astype(F32)

    @pl.when(t > 0)
    def _():
        g0 = g0_ref[...]
        for c in range(blk // tf):
            re = slice(2 * c * tf, (2 * c + 1) * tf)
            im = slice((2 * c + 1) * tf, (2 * c + 2) * tf)
            kre = ha[re, :] - sgn * hprev_ref[im, :]
            kim = ha[im, :] + sgn * (hprev_ref[re, :] - g0)
            for o in range(HY_ORDER):
                lanes = slice(o * HY_WIDTH, (o + 1) * HY_WIDTH)
                kf_ref[o, c, 0:tf, :] = kre[:, lanes]
                kf_ref[o, c, tf:2 * tf, :] = kim[:, lanes]

    hprev_ref[...] = ha
    g0_ref[...] = a[0:1, :]


def _filter_spectrum(kraw, ksum, fwd, seq_len, blk):
    nb = seq_len // blk
    nlags = 2 * nb - 1
    oc = HY_ORDER * HY_WIDTH
    m = 2 * blk
    tf = FREQ_TILE
    nfc = blk // tf
    return pl.pallas_call(
        functools.partial(_spectrum_kernel, blk=blk),
        grid=(2 * nb,),
        in_specs=[
            pl.BlockSpec((blk, oc), lambda t: ((nb + t) % (2 * nb), 0)),
            pl.BlockSpec((1, oc), lambda t: (0, 0)),
            pl.BlockSpec((m, blk), lambda t: (0, 0)),
        ],
        out_specs=pl.BlockSpec((HY_ORDER, nfc, None, 2 * tf, HY_WIDTH), lambda t: (0, 0, jnp.maximum(t - 1, 0), 0, 0)),
        out_shape=jax.ShapeDtypeStruct((HY_ORDER, nfc, nlags, 2 * tf, HY_WIDTH), F32),
        scratch_shapes=[pltpu.VMEM((m, oc), F32), pltpu.VMEM((1, oc), F32)],
        compiler_params=_cparams(("arbitrary",)),
        name="filter_spectrum",
    )(kraw, ksum, fwd)


def _conv_kernel(z_ref, g_ref, swz_ref, sbz_ref, swg_ref, sbg_ref, bd_ref, kf_ref, fwd_ref, inv_ref, o_ref,
                 zb_ref, zc_ref, yf_ref, *, seq_len, blk, conv_z, seqs, sub):
    fc = pl.program_id(1)
    nfc = pl.num_programs(1)
    nb = seq_len // blk
    tf = FREQ_TILE
    blocks = [((g // nb) * seq_len, (g % nb) * blk) for g in range(seqs * nb)]

    @pl.when(fc == 0)
    def _():
        for base, r0 in blocks:
            rows = slice(base + r0, base + r0 + blk)
            if conv_z:
                zc_ref[rows, :] = _short_conv_chunk(z_ref, base, r0, blk, seq_len, swz_ref[...], sbz_ref[...])
                zb_ref[rows, :] = zc_ref[rows, :].astype(BF16)
            else:
                zb_ref[rows, :] = z_ref[rows, :].astype(BF16)

    zf = [[_dot(fwd_ref[s * 2 * tf:(s + 1) * 2 * tf, :], zb_ref[g * blk:(g + 1) * blk, :]) for g in range(seqs * nb)]
          for s in range(sub)]

    for s in range(sub):
        col = pl.multiple_of((fc * sub + s) * 2 * tf, 2 * tf)
        for g0 in range(0, seqs * nb, nb):
            for bi in range(nb):
                for r in range(0, tf, MAC_ROWS):
                    re = slice(r, r + MAC_ROWS)
                    im = slice(tf + r, tf + r + MAC_ROWS)
                    yr = jnp.zeros((MAC_ROWS, HY_WIDTH), F32)
                    yi = jnp.zeros((MAC_ROWS, HY_WIDTH), F32)
                    for bj in range(nb):
                        lag = bi - bj + nb - 1
                        kr, ki = kf_ref[s, lag, re, :], kf_ref[s, lag, im, :]
                        zr, zi = zf[s][g0 + bj][re, :], zf[s][g0 + bj][im, :]
                        yr = yr + (kr * zr - ki * zi)
                        yi = yi + (kr * zi + ki * zr)
                    yf_ref[g0 + bi, pl.ds(col + r, MAC_ROWS), :] = yr.astype(BF16)
                    yf_ref[g0 + bi, pl.ds(col + tf + r, MAC_ROWS), :] = yi.astype(BF16)

    @pl.when(fc == nfc - 1)
    def _():
        ys = [_dot(inv_ref[...], yf_ref[g]) for g in range(seqs * nb)]
        for (base, r0), y in zip(blocks, ys):
            rows = slice(base + r0, base + r0 + blk)
            z = zc_ref[rows, :] if conv_z else z_ref[rows, :]
            gate = _short_conv_chunk(g_ref, base, r0, blk, seq_len, swg_ref[...], sbg_ref[...])
            o_ref[rows, :] = (gate * (y + bd_ref[...] * z)).astype(o_ref.dtype)


def _hyena_conv(z_src, z_col, conv_z, g_src, g_col, short_w, short_b, bd, kf, order, fwd, inv,
                batch, seq_len, blk, out_dtype):
    nb = seq_len // blk
    nlags = 2 * nb - 1
    tf = FREQ_TILE
    sub = min(FREQ_SUB, blk // tf)
    nfc = blk // (sub * tf)
    w = HY_WIDTH
    seqs = _seqs_per_step(batch, seq_len)
    step_rows = seqs * seq_len
    return pl.pallas_call(
        functools.partial(_conv_kernel, seq_len=seq_len, blk=blk, conv_z=conv_z, seqs=seqs, sub=sub),
        grid=(batch // seqs, nfc),
        in_specs=[
            pl.BlockSpec((step_rows, w), lambda b, f: (b, 0)),
            pl.BlockSpec((step_rows, w), lambda b, f: (b, 0)),
            pl.BlockSpec((3, w), lambda b, f: (0, z_col if conv_z else 0)),
            pl.BlockSpec((1, w), lambda b, f: (0, z_col if conv_z else 0)),
            pl.BlockSpec((3, w), lambda b, f: (0, g_col)),
            pl.BlockSpec((1, w), lambda b, f: (0, g_col)),
            pl.BlockSpec((1, w), lambda b, f: (0, 0)),
            pl.BlockSpec((None, sub, nlags, 2 * tf, w), lambda b, f: (order, f, 0, 0, 0)),
            pl.BlockSpec((sub * 2 * tf, blk), lambda b, f: (f, 0)),
            pl.BlockSpec((blk, 2 * blk), lambda b, f: (0, 0)),
        ],
        out_specs=pl.BlockSpec((step_rows, w), lambda b, f: (b, 0)),
        out_shape=jax.ShapeDtypeStruct((batch * seq_len, w), out_dtype),
        scratch_shapes=[
            pltpu.VMEM((step_rows, w), BF16),
            pltpu.VMEM((step_rows if conv_z else SUBLANES, w), F32),
            pltpu.VMEM((seqs * nb, 2 * blk, w), BF16),
        ],
        compiler_params=_cparams(("arbitrary", "arbitrary")),
        name="hyena_conv",
    )(z_src, g_src, short_w, short_b, short_w, short_b, bd, kf, fwd, inv)


def _dft_tables(blk):
    m = 2 * blk
    tf = FREQ_TILE
    nt = blk // tf
    q = 32
    f2 = 2 * jnp.arange(blk, dtype=jnp.int32) + 1

    def cos_sin(times):
        r = (f2[:, None] * times[None, :]) % (2 * m)
        ang = r.astype(F32) * (math.pi / m)
        return jnp.cos(ang), jnp.sin(ang)

    ch, sh = cos_sin(jnp.arange(0, blk, q, dtype=jnp.int32))
    cl, sl = cos_sin(jnp.arange(q, dtype=jnp.int32))
    rep = lambda a: jnp.repeat(a, q, axis=1)
    til = lambda a: jnp.tile(a, (1, blk // q))
    c = rep(ch) * til(cl) - rep(sh) * til(sl)
    s = rep(sh) * til(cl) + rep(ch) * til(sl)
    fwd = jnp.stack([c.reshape(nt, tf, blk), (-s).reshape(nt, tf, blk)], axis=1).reshape(m, blk)
    rep_t = lambda a: jnp.repeat(a.T, q, axis=0)
    til_t = lambda a: jnp.tile(a.T, (blk // q, 1))
    ct = rep_t(ch) * til_t(cl) - rep_t(sh) * til_t(sl)
    st = rep_t(sh) * til_t(cl) + rep_t(ch) * til_t(sl)
    inv = jnp.stack([ct.reshape(blk, nt, tf), (-st).reshape(blk, nt, tf)], axis=2).reshape(blk, m) * (2.0 / m)
    return fwd.astype(BF16), inv.astype(BF16)


def _filter_embedding(seq_len):
    t = jnp.linspace(0.0, 1.0, seq_len, dtype=F32)[:, None]
    bands = (HY_EMB_DIM - 1) // 2
    f = jnp.linspace(1e-4, bands - 1, bands, dtype=F32)[None, :]
    w = 2.0 * math.pi * jnp.arange(seq_len, dtype=F32)[:, None] / seq_len
    z = jnp.concatenate([t, jnp.cos(f * w), -jnp.sin(f * w)], axis=-1)
    z_ext = jnp.concatenate([z, jnp.zeros((1, HY_EMB_DIM), F32), jnp.flip(z[1:], axis=0)], axis=0)
    return jnp.pad(z_ext, ((0, 0), (0, LANES - HY_EMB_DIM)))


def _rope_tables(seq_len):
    quarter = HEAD_DIM // 4
    inv = jnp.tile(ROPE_THETA ** (-jnp.arange(quarter, dtype=F32) / quarter), LANES // quarter)
    lane = jnp.arange(LANES)
    by_row = (lane % HEAD_DIM < HEAD_DIM // 2)[None, None, :]
    first = (lane % (2 * quarter) < quarter)[None, :]

    def cos_sin(npos):
        ang = jnp.arange(npos).astype(F32)[:, None] * inv[None, :]
        return jnp.cos(ang), jnp.sin(ang)

    (cr, sr), (cc, sc) = cos_sin(seq_len // GRID_W), cos_sin(GRID_W)
    lay = lambda r, c: jnp.where(by_row, r[:, None, :], c[None, :, :]).reshape(seq_len, LANES)
    cos, sin = lay(cr, cc), lay(sr, sc)
    return cos, jnp.where(first, -sin, 0.0), jnp.where(first, 0.0, sin)


def _block_ones(width):
    h = jnp.arange(width) // HEAD_DIM
    return (h[:, None] == h[None, :]).astype(BF16)


def _stream_layer(x, p, wts, mod_l, batch, seq_len, rope_tabs, ctx, layer, hy_blk, consts, cast_next=None):
    gq, gk, fwd, inv, z_ext = consts
    latent = ctx is not None
    outs, next_wts = _token_a(x, mod_l, p["norm"], wts, gq, gk, p["qg"], p["kg"], rope_tabs, seq_len, cast_next)
    x1, u_pool, q, kd, vd, hy_v, hy_x1, hy_x2 = outs[:8]
    k, v = (None, None) if latent else outs[8:]
    y_pool = _pool_mix(u_pool, p["pool_w"], p["pool_scale"], batch, seq_len)
    if latent:
        a = _attention(q, kd, vd, p["sink"], batch, seq_len, ctx[0], ctx[1], layer)
    else:
        a = _attention(q, kd, vd, p["sink"], batch, seq_len)
    kraw, ksum = _hyena_filter(z_ext, p["f_w1"], p["f_b1"], p["f_w2"], p["f_b2"], p["f_w3"], p["f_b3"],
                               p["f_freq"], p["decay"], seq_len)
    kf = _filter_spectrum(kraw, ksum, fwd, seq_len, hy_blk)
    z1 = _hyena_conv(hy_v, 0, True, hy_x1, 1, p["short_w"], p["short_b"], p["hy_bias"][0:1], kf, 0, fwd, inv,
                     batch, seq_len, hy_blk, F32)
    y_hy = _hyena_conv(z1, 0, False, hy_x2, 2, p["short_w"], p["short_b"], p["hy_bias"][1:2], kf, 1, fwd, inv,
                       batch, seq_len, hy_blk, BF16)
    x3 = _token_b(x1, y_pool, a, y_hy, mod_l, p["norm"], wts, seq_len, latent)
    return x3, k, v, next_wts


def kernel(x_prompt, x_sample, cache_k, cache_v, c, c_ctx, ada_w, ada_b, norm_w, ffn_wg, ffn_wu, ffn_wd, w_in, w_out, pool_w, pool_scale, q_norm, k_norm, attn_sink, hy_short_w, hy_short_b, hy_f_w1, hy_f_b1, hy_f_w2, hy_f_b2, hy_f_w3, hy_f_b3, hy_sin_freq, hy_decay, hy_bias):
    bp, lp, _ = x_prompt.shape
    bs, ls, _ = x_sample.shape
    lc = cache_k.shape[2]

    cond = jnp.concatenate([c_ctx[None, :], c, jnp.zeros((COND_ROWS - 1 - bs, D_MODEL), F32)], axis=0)
    mod = _ada_mod(cond, ada_w, ada_b).reshape(DEPTH, COND_ROWS, 1, N_MOD * D_MODEL)

    gq, gk = _block_ones(ATTN_WIDTH), _block_ones(KV_WIDTH)
    rope_tabs = _rope_tables(ls)
    blk_p, blk_s = min(lp, HY_BLOCK), min(ls, HY_BLOCK)
    consts_p = (gq, gk) + _dft_tables(blk_p) + (_filter_embedding(lp),)
    consts_s = (gq, gk) + _dft_tables(blk_s) + (_filter_embedding(ls),)
    ctx_k = cache_k.reshape(bs, DEPTH, lc, KV_WIDTH)
    ctx_v = cache_v.reshape(bs, DEPTH, lc, KV_WIDTH)

    flat = _flat_weights(ffn_wg, ffn_wu, ffn_wd, w_in, w_out)
    wts = _cast_layer(flat, 0)
    yp = x_prompt.reshape(bp * lp, D_MODEL)
    ys = x_sample.reshape(bs * ls, D_MODEL)
    ks, vs = [], []
    eye = jnp.eye(len(POOL_WINDOWS), dtype=F32)
    for l in range(DEPTH):
        pool_bd = (eye[:, None, :, None] * pool_w[l][:, :, None, :]).reshape(POOL_WIDTH, POOL_WIDTH)
        p = {
            "norm": norm_w[l],
            "pool_w": pool_bd.astype(BF16), "pool_scale": pool_scale[l][None, :],
            "qg": jnp.tile(q_norm[l], N_HEADS)[None, :], "kg": jnp.tile(k_norm[l], N_KV_HEADS)[None, :],
            "sink": attn_sink[l][None, :],
            "short_w": hy_short_w[l], "short_b": hy_short_b[l][None, :],
            "f_w1": jnp.pad(hy_f_w1[l], ((0, LANES - HY_EMB_DIM), (0, 0))), "f_b1": hy_f_b1[l][None, :],
            "f_w2": hy_f_w2[l], "f_b2": hy_f_b2[l][None, :],
            "f_w3": hy_f_w3[l], "f_b3": hy_f_b3[l][None, :],
            "f_freq": hy_sin_freq[l], "decay": hy_decay[l].reshape(1, HY_ORDER * HY_WIDTH),
            "hy_bias": hy_bias[l],
        }
        yp, k_l, v_l, _ = _stream_layer(yp, p, wts, mod[l], bp, lp, None, None, l, blk_p, consts_p)
        ks.append(k_l.reshape(bp, lp, N_KV_HEADS, HEAD_DIM))
        vs.append(v_l.reshape(bp, lp, N_KV_HEADS, HEAD_DIM))
        cast_next = (flat, l + 1) if l + 1 < DEPTH else None
        ys, _, _, wts = _stream_layer(ys, p, wts, mod[l], bs, ls, rope_tabs, (ctx_k, ctx_v), l, blk_s, consts_s, cast_next)
    return (yp.reshape(bp, lp, D_MODEL), ys.reshape(bs, ls, D_MODEL),
            jnp.stack(ks, axis=1), jnp.stack(vs, axis=1))
```

```python
import functools
import math

import jax
import jax.numpy as jnp
from jax import lax
from jax.experimental import pallas as pl
from jax.experimental.pallas import tpu as pltpu

F32 = jnp.float32
BF16 = jnp.bfloat16

D_MODEL = 1024
DEPTH = 2
GRID_W = 64
POOL_WINDOWS = (2, 4, 8, 16)
POOL_WIDTH = 256
POOL_GROUP = 64
HEAD_DIM = 64
N_HEADS = 8
N_KV_HEADS = 2
GQA_GROUP = 4
ATTN_WIDTH = 512
KV_WIDTH = 128
WINDOW = 128
BLOCK = 128
ROPE_THETA = 10000.0
HY_WIDTH = 256
HY_ORDER = 2
HY_EMB_DIM = 33
HY_FILTER_HIDDEN = 64
HY_MOD_SHIFT = 0.05
D_FF = 2816
IN_WIDTH = 1792
N_MOD = 9
NORM_EPS = 1e-6
NEG_INF = -1e30

LANES = 128
SUBLANES = 8
VMEM_LIMIT = 56 * 1024 * 1024

TOKEN_TILE = 512
TOKEN_TILE_B = 1024
MXU_DIM = 256
FF_CHUNKS = (1280, 1536)
SEQ_CHUNK = 512
SEQ_STEP_ROWS = 2048
HY_BLOCK = 1024
FREQ_TILE = 128
FREQ_SUB = 2
MAC_ROWS = 32
ATTN_SUB = 4
CTX_SEQS = 2
COND_ROWS = 8


def _cparams(sem):
    return pltpu.CompilerParams(dimension_semantics=sem, vmem_limit_bytes=VMEM_LIMIT)


def _dot(a, b):
    return jnp.dot(a, b, preferred_element_type=F32)


def _cast_kernel(*refs):
    n = len(refs) // 2
    for i_ref, o_ref in zip(refs[:n], refs[n:]):
        o_ref[...] = i_ref[...].astype(o_ref.dtype)


def _flat_weights(ffn_wg, ffn_wu, ffn_wd, w_in, w_out):
    arrs = {"wg": ffn_wg, "wu": ffn_wu, "wd": ffn_wd, "w_in": w_in, "w_out": w_out}
    return {k: a.reshape(-1, a.shape[-1]) for k, a in arrs.items()}


def _layer_cast_specs(flat, layer, steps):
    in_specs, out_specs, out_shapes = [], [], []
    for a in flat.values():
        rows = a.shape[0] // DEPTH
        blk = rows // steps
        in_specs.append(pl.BlockSpec((blk, a.shape[1]), lambda i, first=layer * steps: (first + i, 0)))
        out_specs.append(pl.BlockSpec((blk, a.shape[1]), lambda i: (i, 0)))
        out_shapes.append(jax.ShapeDtypeStruct((rows, a.shape[1]), BF16))
    return in_specs, out_specs, out_shapes


def _layer_weights(outs):
    wg, wu, wd, w_in, w_out = outs
    return {"wg": wg.reshape(2, D_MODEL, D_FF), "wu": wu.reshape(2, D_MODEL, D_FF),
            "wd": wd.reshape(2, D_FF, D_MODEL), "w_in": w_in, "w_out": w_out}


def _cast_layer(flat, layer):
    in_specs, out_specs, out_shapes = _layer_cast_specs(flat, layer, steps=16)
    outs = pl.pallas_call(
        _cast_kernel,
        grid=(16,),
        in_specs=in_specs,
        out_specs=out_specs,
        out_shape=out_shapes,
        compiler_params=_cparams(("arbitrary",)),
        name="cast_weights",
    )(*flat.values())
    return _layer_weights(outs)


def _ada_kernel(c_ref, w_ref, b_ref, o_ref):
    c = c_ref[...]
    s = (c * jax.nn.sigmoid(c)).astype(BF16)
    o_ref[...] = _dot(s, w_ref[...].astype(BF16)) + b_ref[...]


def _ada_mod(cond, ada_w, ada_b):
    tn = 3072
    nw = N_MOD * D_MODEL
    return pl.pallas_call(
        _ada_kernel,
        grid=(DEPTH, nw // tn),
        in_specs=[
            pl.BlockSpec((COND_ROWS, D_MODEL), lambda l, j: (0, 0)),
            pl.BlockSpec((None, D_MODEL, tn), lambda l, j: (l, 0, j)),
            pl.BlockSpec((None, 1, tn), lambda l, j: (l, 0, j)),
        ],
        out_specs=pl.BlockSpec((None, COND_ROWS, tn), lambda l, j: (l, 0, j)),
        out_shape=jax.ShapeDtypeStruct((DEPTH, COND_ROWS, nw), F32),
        compiler_params=_cparams(("arbitrary", "arbitrary")),
        name="ada_mod",
    )(cond, ada_w, ada_b.reshape(DEPTH, 1, nw))


def _mod_slice(mod_ref, k):
    return mod_ref[:, k * D_MODEL:(k + 1) * D_MODEL]


def _rms_mod(x, gain, scale, shift):
    y = x * lax.rsqrt(jnp.mean(x * x, axis=-1, keepdims=True) + NORM_EPS)
    return (y * gain) * (1.0 + scale) + shift


def _swiglu(hb, wg_ref, wu_ref, wd_ref):
    acc = None
    lo = 0
    for width in FF_CHUNKS:
        sl = slice(lo, lo + width)
        lo += width
        g = _dot(hb, wg_ref[:, sl])
        u = _dot(hb, wu_ref[:, sl])
        a = ((g * jax.nn.sigmoid(g)) * u).astype(BF16)
        y = _dot(a, wd_ref[sl, :])
        acc = y if acc is None else acc + y
    return acc


def _head_norm(x, gmat, gain):
    ss = _dot((x * x).astype(BF16), gmat)
    return (x * lax.rsqrt(ss * (1.0 / HEAD_DIM) + NORM_EPS)) * gain


def _rope(x, cos, sa, sb):
    w = x.shape[1]
    xn = pltpu.roll(x, w - 16, axis=1)
    xp = pltpu.roll(x, 16, axis=1)
    return x * cos + xn * sa + xp * sb


def _token_a_kernel(*refs, rope, n_cast):
    n_in = 11 + (3 if rope else 0) + n_cast
    ins, outs = refs[:n_in], refs[n_in:]
    x_ref, mod_ref, nrm_ref, wg_ref, wu_ref, wd_ref, win_ref, gq_ref, gk_ref, qg_ref, kg_ref = ins[:11]
    if rope:
        cos_ref, sa_ref, sb_ref = ins[11:14]
    x1_ref, up_ref, q_ref, kd_ref, vd_ref, hv_ref, hx1_ref, hx2_ref = outs[:8]
    if not rope:
        k_ref, v_ref = outs[8:10]
    for i_ref, o_ref in zip(ins[n_in - n_cast:], outs[len(outs) - n_cast:]):
        o_ref[...] = i_ref[...].astype(o_ref.dtype)
    x = x_ref[...]
    sh1, sc1, g1 = _mod_slice(mod_ref, 0), _mod_slice(mod_ref, 1), _mod_slice(mod_ref, 2)
    sh2, sc2 = _mod_slice(mod_ref, 3), _mod_slice(mod_ref, 4)
    h = _rms_mod(x, nrm_ref[0:1, :], sc1, sh1).astype(BF16)
    x1 = x + (0.5 * g1) * _swiglu(h, wg_ref, wu_ref, wd_ref)
    x1_ref[...] = x1
    h2 = _rms_mod(x1, nrm_ref[1:2, :], sc2, sh2).astype(BF16)
    s1 = POOL_WIDTH
    s2 = s1 + ATTN_WIDTH
    s3 = s2 + KV_WIDTH
    s4 = s3 + KV_WIDTH
    qkv = _dot(h2, win_ref[:, s1:s4])
    q = _head_norm(qkv[:, :ATTN_WIDTH], gq_ref[...], qg_ref[...])
    k = _head_norm(qkv[:, ATTN_WIDTH:ATTN_WIDTH + KV_WIDTH], gk_ref[...], kg_ref[...])
    v = qkv[:, ATTN_WIDTH + KV_WIDTH:]
    up_ref[...] = _dot(h2, win_ref[:, :s1])
    hy = _dot(h2, win_ref[:, s4:])
    for j, ref in enumerate((hv_ref, hx1_ref, hx2_ref)):
        ref[...] = hy[:, j * HY_WIDTH:(j + 1) * HY_WIDTH]
    if rope:
        cos, sa, sb = cos_ref[...], sa_ref[...], sb_ref[...]
        reps = ATTN_WIDTH // LANES
        q = _rope(q, jnp.concatenate([cos] * reps, axis=1), jnp.concatenate([sa] * reps, axis=1),
                  jnp.concatenate([sb] * reps, axis=1))
        k = _rope(k, cos, sa, sb)
    q_ref[...] = (q * (HEAD_DIM ** -0.5 * math.log2(math.e))).astype(BF16)
    kd_ref[...] = _dup_heads(k)
    vd_ref[...] = _dup_heads(v)
    if not rope:
        k_ref[...] = k
        v_ref[...] = v


def _token_a(x, mod_l, nrm, wts, gq, gk, qg, kg, rope_tabs, seq_len, cast_next=None):
    n = x.shape[0]
    tm = TOKEN_TILE
    tiles_per_seq = max(seq_len // tm, 1)
    rope = rope_tabs is not None
    if rope:
        mod_row = lambda i: (1 + i // tiles_per_seq, 0, 0)
    else:
        mod_row = lambda i: (0, 0, 0)
    const = lambda i: (0, 0)
    row = lambda i: (i, 0)
    in_specs = [
        pl.BlockSpec((tm, D_MODEL), row),
        pl.BlockSpec((None, 1, N_MOD * D_MODEL), mod_row),
        pl.BlockSpec((3, D_MODEL), const),
        pl.BlockSpec((None, D_MODEL, D_FF), lambda i: (0, 0, 0)),
        pl.BlockSpec((None, D_MODEL, D_FF), lambda i: (0, 0, 0)),
        pl.BlockSpec((None, D_FF, D_MODEL), lambda i: (0, 0, 0)),
        pl.BlockSpec((D_MODEL, IN_WIDTH), const),
        pl.BlockSpec((ATTN_WIDTH, ATTN_WIDTH), const),
        pl.BlockSpec((KV_WIDTH, KV_WIDTH), const),
        pl.BlockSpec((1, ATTN_WIDTH), const),
        pl.BlockSpec((1, KV_WIDTH), const),
    ]
    args = [x, mod_l, nrm, wts["wg"], wts["wu"], wts["wd"], wts["w_in"], gq, gk, qg, kg]
    if rope:
        tab = pl.BlockSpec((tm, LANES), lambda i: (i % tiles_per_seq, 0))
        in_specs += [tab, tab, tab]
        args += list(rope_tabs)
    widths = (D_MODEL, POOL_WIDTH, ATTN_WIDTH, 2 * KV_WIDTH, 2 * KV_WIDTH) + (HY_WIDTH,) * (HY_ORDER + 1)
    dtypes = (F32, F32, BF16, BF16, BF16) + (F32,) * (HY_ORDER + 1)
    if not rope:
        widths += (KV_WIDTH, KV_WIDTH)
        dtypes += (F32, F32)
    out_specs = [pl.BlockSpec((tm, w), row) for w in widths]
    out_shapes = [jax.ShapeDtypeStruct((n, w), d) for w, d in zip(widths, dtypes)]
    n_cast = 0
    if cast_next is not None:
        flat, next_layer = cast_next
        c_in, c_out, c_shapes = _layer_cast_specs(flat, next_layer, steps=n // tm)
        in_specs += c_in
        args += list(flat.values())
        out_specs += c_out
        out_shapes += c_shapes
        n_cast = len(c_in)
    outs = pl.pallas_call(
        functools.partial(_token_a_kernel, rope=rope, n_cast=n_cast),
        grid=(n // tm,),
        in_specs=in_specs,
        out_specs=out_specs,
        out_shape=out_shapes,
        compiler_params=_cparams(("arbitrary",)),
        name="token_a_rope" if rope else "token_a",
    )(*args)
    if n_cast:
        return outs[:len(outs) - n_cast], _layer_weights(outs[len(outs) - n_cast:])
    return outs, None


def _token_b_kernel(x_ref, yp_ref, a_ref, yh_ref, mod_ref, nrm_ref, wo_ref, wg_ref, wu_ref, wd_ref, o_ref):
    x1 = x_ref[...]
    g2 = _mod_slice(mod_ref, 5)
    sh3, sc3, g3 = _mod_slice(mod_ref, 6), _mod_slice(mod_ref, 7), _mod_slice(mod_ref, 8)
    cat = jnp.concatenate([yp_ref[...], a_ref[...], yh_ref[...]], axis=1)
    x2 = x1 + g2 * _dot(cat, wo_ref[...])
    h3 = _rms_mod(x2, nrm_ref[2:3, :], sc3, sh3).astype(BF16)
    o_ref[...] = x2 + (0.5 * g3) * _swiglu(h3, wg_ref, wu_ref, wd_ref)


def _token_b(x1, y_pool, a, y_hy, mod_l, nrm, wts, seq_len, per_seq_cond):
    n = x1.shape[0]
    tm = min(TOKEN_TILE_B, n)
    tiles_per_seq = max(seq_len // tm, 1)
    if per_seq_cond:
        mod_row = lambda i: (1 + i // tiles_per_seq, 0, 0)
    else:
        mod_row = lambda i: (0, 0, 0)
    const = lambda i: (0, 0)
    row = lambda i: (i, 0)
    return pl.pallas_call(
        _token_b_kernel,
        grid=(n // tm,),
        in_specs=[
            pl.BlockSpec((tm, D_MODEL), row),
            pl.BlockSpec((tm, POOL_WIDTH), row),
            pl.BlockSpec((tm, ATTN_WIDTH), row),
            pl.BlockSpec((tm, HY_WIDTH), row),
            pl.BlockSpec((None, 1, N_MOD * D_MODEL), mod_row),
            pl.BlockSpec((3, D_MODEL), const),
            pl.BlockSpec((D_MODEL, D_MODEL), const),
            pl.BlockSpec((None, D_MODEL, D_FF), lambda i: (1, 0, 0)),
            pl.BlockSpec((None, D_MODEL, D_FF), lambda i: (1, 0, 0)),
            pl.BlockSpec((None, D_FF, D_MODEL), lambda i: (1, 0, 0)),
        ],
        out_specs=pl.BlockSpec((tm, D_MODEL), row),
        out_shape=jax.ShapeDtypeStruct((n, D_MODEL), F32),
        compiler_params=_cparams(("arbitrary",)),
        name="token_b",
    )(x1, y_pool, a, y_hy, mod_l, nrm, wts["w_out"], wts["wg"], wts["wu"], wts["wd"])


def _seqs_per_step(batch, seq_len):
    return min(batch, max(1, SEQ_STEP_ROWS // seq_len))


def _halo_rows(src_ref, base, r0, rows, seq_len):
    c = src_ref.shape[1]
    zero = jnp.zeros((SUBLANES, c), F32)
    lo = base + r0
    prev = src_ref[lo - SUBLANES:lo, :] if r0 > 0 else zero
    nxt = src_ref[lo + rows:lo + rows + SUBLANES, :] if r0 + rows < seq_len else zero
    return prev, nxt


def _short_conv_chunk(src_ref, base, r0, rows, seq_len, w, b):
    x = src_ref[base + r0:base + r0 + rows, :]
    prev, nxt = _halo_rows(src_ref, base, r0, rows, seq_len)
    ridx = lax.broadcasted_iota(jnp.int32, x.shape, 0)
    xp = jnp.where(ridx == 0, prev[SUBLANES - 1:SUBLANES, :], pltpu.roll(x, 1, axis=0))
    xn = jnp.where(ridx == rows - 1, nxt[0:1, :], pltpu.roll(x, rows - 1, axis=0))
    return xp * w[0:1, :] + x * w[1:2, :] + xn * w[2:3, :] + b


def _pool_kernel(u_ref, w_ref, scale_ref, o_ref, *, seq_len, rows, seqs):
    lane = lax.broadcasted_iota(jnp.int32, (rows, POOL_WIDTH), 1)
    grp = lane // POOL_GROUP
    half = jnp.where(grp == 0, 1, jnp.where(grp == 1, 2, jnp.where(grp == 2, 4, 8)))
    odd_grp = lax.broadcasted_iota(jnp.int32, (rows, LANES), 1) >= POOL_GROUP
    ext = rows + 2 * SUBLANES
    back = lambda v, s: pltpu.roll(v, s, axis=0)
    fwd = lambda v, s: pltpu.roll(v, ext - s, axis=0)
    core = lambda v: v[SUBLANES:SUBLANES + rows, :]
    for base, r0 in [(sq * seq_len, r0) for sq in range(seqs) for r0 in range(0, seq_len, rows)]:
        x = u_ref[base + r0:base + r0 + rows, :]
        prev, nxt = _halo_rows(u_ref, base, r0, rows, seq_len)
        a = jnp.concatenate([prev, x, nxt], axis=0)
        lo, hi = a[:, :LANES], a[:, LANES:]
        b1 = back(lo, 1)
        b2 = b1 + back(b1, 1)
        f2 = lo + fwd(lo, 1)
        sum_lo = jnp.where(odd_grp, core(b2) + core(f2), core(b1) + core(lo))
        c1 = back(hi, 1)
        c2 = c1 + back(c1, 1)
        c4 = c2 + back(c2, 2)
        c8 = c4 + back(c4, 4)
        g2 = hi + fwd(hi, 1)
        g4 = g2 + fwd(g2, 2)
        g8 = g4 + fwd(g4, 4)
        sum_hi = jnp.where(odd_grp, core(c8) + core(g8), core(c4) + core(g4))
        wsum = jnp.concatenate([sum_lo, sum_hi], axis=1)
        t = r0 + lax.broadcasted_iota(jnp.int32, (rows, POOL_WIDTH), 0)
        cnt = jnp.minimum(t + half, seq_len) - jnp.maximum(t - half, 0)
        d = wsum / cnt.astype(F32) - x
        y = _dot(d.astype(BF16), w_ref[...]) * scale_ref[...]
        o_ref[base + r0:base + r0 + rows, :] = y.astype(o_ref.dtype)


def _pool_mix(u_pool, w_bd, scale, batch, seq_len):
    rows = min(SEQ_CHUNK // 2, seq_len)
    seqs = _seqs_per_step(batch, seq_len)
    return pl.pallas_call(
        functools.partial(_pool_kernel, seq_len=seq_len, rows=rows, seqs=seqs),
        grid=(batch // seqs,),
        in_specs=[
            pl.BlockSpec((seqs * seq_len, POOL_WIDTH), lambda b: (b, 0)),
            pl.BlockSpec((POOL_WIDTH, POOL_WIDTH), lambda b: (0, 0)),
            pl.BlockSpec((1, POOL_WIDTH), lambda b: (0, 0)),
        ],
        out_specs=pl.BlockSpec((seqs * seq_len, POOL_WIDTH), lambda b: (b, 0)),
        out_shape=jax.ShapeDtypeStruct((batch * seq_len, POOL_WIDTH), BF16),
        compiler_params=_cparams(("arbitrary",)),
        name="pool_mix",
    )(u_pool, w_bd, scale)


def _dup_heads(x):
    lane = lax.broadcasted_iota(jnp.int32, x.shape, 1)
    sw = pltpu.roll(x, HEAD_DIM, axis=1)
    lo = lane < HEAD_DIM
    return jnp.concatenate([jnp.where(lo, x, sw), jnp.where(lo, sw, x)], axis=1).astype(BF16)


def _attn_kernel(*refs, has_local, nblocks, sub, seqs):
    if has_local:
        sink_ref, q_ref, kp_ref, kc_ref, kn_ref, vp_ref, vc_ref, vn_ref, ck_ref, cv_ref, o_ref = refs
        kwin = jnp.concatenate([kp_ref[...], kc_ref[...], kn_ref[...]], axis=0)
        vwin = jnp.concatenate([vp_ref[...], vc_ref[...], vn_ref[...]], axis=0)
        kctx, vctx = _dup_heads(ck_ref[...]), _dup_heads(cv_ref[...])
        r = lax.broadcasted_iota(jnp.int32, (GQA_GROUP * BLOCK, BLOCK), 0) % BLOCK
        j = lax.broadcasted_iota(jnp.int32, (GQA_GROUP * BLOCK, BLOCK), 1)
    else:
        sink_ref, q_ref, kd_ref, vd_ref, o_ref = refs
    i = pl.program_id(1)
    lane_q = lax.broadcasted_iota(jnp.int32, (BLOCK, LANES), 1)
    log2e = math.log2(math.e)
    units = [(sb, kvh) for sb in range(seqs * sub) for kvh in range(N_KV_HEADS)]
    scores, values = [], []
    for sb, kvh in units:
        q = q_ref[sb * BLOCK:(sb + 1) * BLOCK, :]
        qparts = []
        for hd in range(kvh * GQA_GROUP, (kvh + 1) * GQA_GROUP):
            qp = q[:, (hd // 2) * LANES:(hd // 2 + 1) * LANES]
            keep = (lane_q < HEAD_DIM) == (hd % 2 == 0)
            qparts.append(jnp.where(keep, qp, jnp.zeros_like(qp)))
        qs = jnp.concatenate(qparts, axis=0)
        cols = slice(kvh * LANES, (kvh + 1) * LANES)
        if has_local:
            loc = slice(sb * BLOCK, (sb + 3) * BLOCK)
            kk = jnp.concatenate([kwin[loc, cols], kctx[:, cols]], axis=0)
            vv = jnp.concatenate([vwin[loc, cols], vctx[:, cols]], axis=0)
        else:
            own = slice((sb // sub) * sub * BLOCK, (sb // sub + 1) * sub * BLOCK)
            kk, vv = kd_ref[own, cols], vd_ref[own, cols]
        s = lax.dot_general(qs, kk, (((1,), (1,)), ((), ())), preferred_element_type=F32)
        if has_local:
            gb = i * sub + sb
            below = j >= r + jnp.where(gb >= 1, 0, BLOCK)
            above = j <= r - jnp.where(gb <= nblocks - 2, 0, BLOCK)
            s = jnp.concatenate([jnp.where(below, s[:, :BLOCK], NEG_INF), s[:, BLOCK:2 * BLOCK],
                                 jnp.where(above, s[:, 2 * BLOCK:3 * BLOCK], NEG_INF), s[:, 3 * BLOCK:]], axis=1)
        scores.append(s)
        values.append(jnp.concatenate([vv, jnp.ones_like(vv)], axis=1))

    def per_head(kvh, col, f):
        parts = [f(col[g * BLOCK:(g + 1) * BLOCK, :], sink_ref[0, kvh * GQA_GROUP + g] * log2e) for g in range(GQA_GROUP)]
        return jnp.concatenate(parts, axis=0)

    maxes = [per_head(kvh, jnp.max(s, axis=1, keepdims=True), jnp.maximum) for (_, kvh), s in zip(units, scores)]
    probs = [jnp.exp2(s - m).astype(BF16) for s, m in zip(scores, maxes)]
    sums = [_dot(e, v) for e, v in zip(probs, values)]
    outs = []
    for (_, kvh), ow, m in zip(units, sums, maxes):
        den = ow[:, LANES:] + per_head(kvh, m, lambda mm, sk: jnp.exp2(sk - mm))
        outs.append(ow[:, :LANES] / den)
    for sb in range(seqs * sub):
        heads = [outs[sb * N_KV_HEADS + kvh][g * BLOCK:(g + 1) * BLOCK, :]
                 for kvh in range(N_KV_HEADS) for g in range(GQA_GROUP)]
        blks = [jnp.where(lane_q < HEAD_DIM, heads[2 * p], heads[2 * p + 1]) for p in range(N_HEADS // 2)]
        o_ref[sb * BLOCK:(sb + 1) * BLOCK, :] = jnp.concatenate(blks, axis=1).astype(o_ref.dtype)


def _attention(q, kd, vd, sink, batch, seq_len, ctx_k=None, ctx_v=None, layer=0):
    nb = seq_len // BLOCK
    has_local = ctx_k is not None
    sub = min(ATTN_SUB, nb)
    steps = nb // sub
    seqs = 1
    if not has_local:
        assert steps == 1 and batch % CTX_SEQS == 0
        seqs = CTX_SEQS
        batch //= seqs
    qb = seqs * sub * BLOCK
    qspec = pl.BlockSpec((qb, ATTN_WIDTH), lambda b, i: (b * steps + i, 0))
    sspec = pl.BlockSpec(memory_space=pltpu.SMEM)
    if has_local:
        lc = ctx_k.shape[2]
        edge = lambda f: pl.BlockSpec((BLOCK, 2 * KV_WIDTH), f)
        prev = lambda b, i: (b * nb + jnp.maximum(i * sub - 1, 0), 0)
        nxt = lambda b, i: (b * nb + jnp.minimum((i + 1) * sub, nb - 1), 0)
        cur = pl.BlockSpec((qb, 2 * KV_WIDTH), lambda b, i: (b * steps + i, 0))
        cspec = pl.BlockSpec((None, None, lc, KV_WIDTH), lambda b, i: (b, layer, 0, 0))
        in_specs = [sspec, qspec, edge(prev), cur, edge(nxt), edge(prev), cur, edge(nxt), cspec, cspec]
        args = [sink, q, kd, kd, kd, vd, vd, vd, ctx_k, ctx_v]
    else:
        kv = pl.BlockSpec((qb, 2 * KV_WIDTH), lambda b, i: (b, 0))
        in_specs = [sspec, qspec, kv, kv]
        args = [sink, q, kd, vd]
    return pl.pallas_call(
        functools.partial(_attn_kernel, has_local=has_local, nblocks=nb, sub=sub, seqs=seqs),
        grid=(batch, steps),
        in_specs=in_specs,
        out_specs=pl.BlockSpec((qb, ATTN_WIDTH), lambda b, i: (b * steps + i, 0)),
        out_shape=jax.ShapeDtypeStruct(q.shape, BF16),
        compiler_params=_cparams(("arbitrary", "arbitrary")),
        name="attn_latent" if has_local else "attn_context",
    )(*args)


def _filter_kernel(zt_ref, zb_ref, w1_ref, b1_ref, w2_ref, b2_ref, w3_ref, b3_ref, fr_ref, dl_ref, k_ref, sum_ref,
                   *, seq_len, rows):
    i = pl.program_id(0)
    half = rows // 2
    oc = HY_ORDER * HY_WIDTH
    z = jnp.concatenate([zt_ref[...], zb_ref[...]], axis=1)
    h = jnp.sin(fr_ref[0:1, :] * (_dot(z.astype(BF16), w1_ref[...].astype(BF16)) + b1_ref[...]))
    h = jnp.sin(fr_ref[1:2, :] * (_dot(h.astype(BF16), w2_ref[...].astype(BF16)) + b2_ref[...]))
    h3 = _dot(h.astype(BF16), w3_ref[...].astype(BF16)) + b3_ref[...]
    total = jnp.zeros((1, oc), F32)
    for part, z_ref in enumerate((zt_ref, zb_ref)):
        row = i * rows + part * half + lax.broadcasted_iota(jnp.int32, (half, oc), 0)
        t = z_ref[:, 0:1]
        decay = jnp.exp(-t * jnp.abs(dl_ref[...]))
        fwd_dir = h3[:, part * 2 * oc:part * 2 * oc + oc]
        bwd_dir = h3[:, part * 2 * oc + oc:(part + 1) * 2 * oc]
        sel = jnp.where(row < seq_len, fwd_dir, jnp.where(row > seq_len, bwd_dir, 0.0))
        k = sel * (decay + HY_MOD_SHIFT)
        k_ref[part * half:(part + 1) * half, :] = k
        total = total + jnp.sum(jnp.abs(k), axis=0, keepdims=True)

    @pl.when(i == 0)
    def _():
        sum_ref[...] = jnp.zeros_like(sum_ref)

    sum_ref[...] += total


def _block_diag2(w):
    z = jnp.zeros_like(w)
    return jnp.concatenate([jnp.concatenate([w, z], axis=1), jnp.concatenate([z, w], axis=1)], axis=0)


def _hyena_filter(z_ext, w1p, b1, w2, b2, w3, b3, freq, deltas, seq_len):
    n = 2 * seq_len
    rows = min(1024, n)
    half = rows // 2
    oc = HY_ORDER * HY_WIDTH
    hid = 2 * HY_FILTER_HIDDEN
    const = lambda i: (0, 0)
    two = lambda a: jnp.concatenate([a, a], axis=1)
    return pl.pallas_call(
        functools.partial(_filter_kernel, seq_len=seq_len, rows=rows),
        grid=(n // rows,),
        in_specs=[
            pl.BlockSpec((half, LANES), lambda i: (2 * i, 0)),
            pl.BlockSpec((half, LANES), lambda i: (2 * i + 1, 0)),
            pl.BlockSpec((2 * LANES, hid), const),
            pl.BlockSpec((1, hid), const),
            pl.BlockSpec((hid, hid), const),
            pl.BlockSpec((1, hid), const),
            pl.BlockSpec((hid, 4 * oc), const),
            pl.BlockSpec((1, 4 * oc), const),
            pl.BlockSpec((2, hid), const),
            pl.BlockSpec((1, oc), const),
        ],
        out_specs=[pl.BlockSpec((rows, oc), lambda i: (i, 0)), pl.BlockSpec((1, oc), const)],
        out_shape=[jax.ShapeDtypeStruct((n, oc), F32), jax.ShapeDtypeStruct((1, oc), F32)],
        compiler_params=_cparams(("arbitrary",)),
        name="hyena_filter",
    )(z_ext, z_ext, _block_diag2(w1p), two(b1), _block_diag2(w2), two(b2), _block_diag2(w3), two(b3), two(freq), deltas)


def _spectrum_kernel(k_ref, sum_ref, fwd_ref, kf_ref, hprev_ref, g0_ref, *, blk):
    t = pl.program_id(0)
    a = k_ref[...] / (sum_ref[...] + 1e-6)
    ha = _dot(fwd_ref[...], a.astype(BF16))
    tf = FREQ_TILE
    par = lax.broadcasted_iota(jnp.int32, (tf, 1), 0) % 2
    sgn = (1 - 2 * par).astype(F32)

    @pl.when(t > 0)
    def _():
        g0 = g0_ref[...]
        for c in range(blk // tf):
            re = slice(2 * c * tf, (2 * c + 1) * tf)
            im = slice((2 * c + 1) * tf, (2 * c + 2) * tf)
            kre = ha[re, :] - sgn * hprev_ref[im, :]
            kim = ha[im, :] + sgn * (hprev_ref[re, :] - g0)
            for o in range(HY_ORDER):
                lanes = slice(o * HY_WIDTH, (o + 1) * HY_WIDTH)
                kf_ref[o, c, 0:tf, :] = kre[:, lanes]
                kf_ref[o, c, tf:2 * tf, :] = kim[:, lanes]

    hprev_ref[...] = ha
    g0_ref[...] = a[0:1, :]


def _filter_spectrum(kraw, ksum, fwd, seq_len, blk):
    nb = seq_len // blk
    nlags = 2 * nb - 1
    oc = HY_ORDER * HY_WIDTH
    m = 2 * blk
    tf = FREQ_TILE
    nfc = blk // tf
    return pl.pallas_call(
        functools.partial(_spectrum_kernel, blk=blk),
        grid=(2 * nb,),
        in_specs=[
            pl.BlockSpec((blk, oc), lambda t: ((nb + t) % (2 * nb), 0)),
            pl.BlockSpec((1, oc), lambda t: (0, 0)),
            pl.BlockSpec((m, blk), lambda t: (0, 0)),
        ],
        out_specs=pl.BlockSpec((HY_ORDER, nfc, None, 2 * tf, HY_WIDTH), lambda t: (0, 0, jnp.maximum(t - 1, 0), 0, 0)),
        out_shape=jax.ShapeDtypeStruct((HY_ORDER, nfc, nlags, 2 * tf, HY_WIDTH), F32),
        scratch_shapes=[pltpu.VMEM((m, oc), F32), pltpu.VMEM((1, oc), F32)],
        compiler_params=_cparams(("arbitrary",)),
        name="filter_spectrum",
    )(kraw, ksum, fwd)


def _conv_kernel(z_ref, g_ref, swz_ref, sbz_ref, swg_ref, sbg_ref, bd_ref, kf_ref, fwd_ref, inv_ref, o_ref,
                 zb_ref, zc_ref, yf_ref, *, seq_len, blk, conv_z, seqs, sub):
    fc = pl.program_id(1)
    nfc = pl.num_programs(1)
    nb = seq_len // blk
    tf = FREQ_TILE
    blocks = [((g // nb) * seq_len, (g % nb) * blk) for g in range(seqs * nb)]

    @pl.when(fc == 0)
    def _():
        for base, r0 in blocks:
            rows = slice(base + r0, base + r0 + blk)
            if conv_z:
                zc_ref[rows, :] = _short_conv_chunk(z_ref, base, r0, blk, seq_len, swz_ref[...], sbz_ref[...])
                zb_ref[rows, :] = zc_ref[rows, :].astype(BF16)
            else:
                zb_ref[rows, :] = z_ref[rows, :].astype(BF16)

    zf = [[_dot(fwd_ref[s * 2 * tf:(s + 1) * 2 * tf, :], zb_ref[g * blk:(g + 1) * blk, :]) for g in range(seqs * nb)]
          for s in range(sub)]

    for s in range(sub):
        col = pl.multiple_of((fc * sub + s) * 2 * tf, 2 * tf)
        for g0 in range(0, seqs * nb, nb):
            for bi in range(nb):
                for r in range(0, tf, MAC_ROWS):
                    re = slice(r, r + MAC_ROWS)
                    im = slice(tf + r, tf + r + MAC_ROWS)
                    yr = jnp.zeros((MAC_ROWS, HY_WIDTH), F32)
                    yi = jnp.zeros((MAC_ROWS, HY_WIDTH), F32)
                    for bj in range(nb):
                        lag = bi - bj + nb - 1
                        kr, ki = kf_ref[s, lag, re, :], kf_ref[s, lag, im, :]
                        zr, zi = zf[s][g0 + bj][re, :], zf[s][g0 + bj][im, :]
                        yr = yr + (kr * zr - ki * zi)
                        yi = yi + (kr * zi + ki * zr)
                    yf_ref[g0 + bi, pl.ds(col + r, MAC_ROWS), :] = yr.astype(BF16)
                    yf_ref[g0 + bi, pl.ds(col + tf + r, MAC_ROWS), :] = yi.astype(BF16)

    @pl.when(fc == nfc - 1)
    def _():
        ys, gates = [], []
        for g, (base, r0) in enumerate(blocks):
            ys.append(_dot(inv_ref[...], yf_ref[g]))
            gates.append(_short_conv_chunk(g_ref, base, r0, blk, seq_len, swg_ref[...], sbg_ref[...]))
        for (base, r0), y, gate in zip(blocks, ys, gates):
            rows = slice(base + r0, base + r0 + blk)
            z = zc_ref[rows, :] if conv_z else z_ref[rows, :]
            o_ref[rows, :] = (gate * (y + bd_ref[...] * z)).astype(o_ref.dtype)


def _hyena_conv(z_src, z_col, conv_z, g_src, g_col, short_w, short_b, bd, kf, order, fwd, inv,
                batch, seq_len, blk, out_dtype):
    nb = seq_len // blk
    nlags = 2 * nb - 1
    tf = FREQ_TILE
    sub = min(FREQ_SUB, blk // tf)
    nfc = blk // (sub * tf)
    w = HY_WIDTH
    seqs = _seqs_per_step(batch, seq_len)
    step_rows = seqs * seq_len
    return pl.pallas_call(
        functools.partial(_conv_kernel, seq_len=seq_len, blk=blk, conv_z=conv_z, seqs=seqs, sub=sub),
        grid=(batch // seqs, nfc),
        in_specs=[
            pl.BlockSpec((step_rows, w), lambda b, f: (b, 0)),
            pl.BlockSpec((step_rows, w), lambda b, f: (b, 0)),
            pl.BlockSpec((3, w), lambda b, f: (0, z_col if conv_z else 0)),
            pl.BlockSpec((1, w), lambda b, f: (0, z_col if conv_z else 0)),
            pl.BlockSpec((3, w), lambda b, f: (0, g_col)),
            pl.BlockSpec((1, w), lambda b, f: (0, g_col)),
            pl.BlockSpec((1, w), lambda b, f: (0, 0)),
            pl.BlockSpec((None, sub, nlags, 2 * tf, w), lambda b, f: (order, f, 0, 0, 0)),
            pl.BlockSpec((sub * 2 * tf, blk), lambda b, f: (f, 0)),
            pl.BlockSpec((blk, 2 * blk), lambda b, f: (0, 0)),
        ],
        out_specs=pl.BlockSpec((step_rows, w), lambda b, f: (b, 0)),
        out_shape=jax.ShapeDtypeStruct((batch * seq_len, w), out_dtype),
        scratch_shapes=[
            pltpu.VMEM((step_rows, w), BF16),
            pltpu.VMEM((step_rows if conv_z else SUBLANES, w), F32),
            pltpu.VMEM((seqs * nb, 2 * blk, w), BF16),
        ],
        compiler_params=_cparams(("arbitrary", "arbitrary")),
        name="hyena_conv",
    )(z_src, g_src, short_w, short_b, short_w, short_b, bd, kf, fwd, inv)


def _dft_tables(blk):
    m = 2 * blk
    tf = FREQ_TILE
    nt = blk // tf
    q = 32
    f2 = 2 * jnp.arange(blk, dtype=jnp.int32) + 1

    def cos_sin(times):
        r = (f2[:, None] * times[None, :]) % (2 * m)
        ang = r.astype(F32) * (math.pi / m)
        return jnp.cos(ang), jnp.sin(ang)

    ch, sh = cos_sin(jnp.arange(0, blk, q, dtype=jnp.int32))
    cl, sl = cos_sin(jnp.arange(q, dtype=jnp.int32))
    rep = lambda a: jnp.repeat(a, q, axis=1)
    til = lambda a: jnp.tile(a, (1, blk // q))
    c = rep(ch) * til(cl) - rep(sh) * til(sl)
    s = rep(sh) * til(cl) + rep(ch) * til(sl)
    tiles = [slice(t * tf, (t + 1) * tf) for t in range(nt)]
    fwd = jnp.concatenate([part[t, :] for t in tiles for part in (c, -s)], axis=0)
    rep_t = lambda a: jnp.repeat(a.T, q, axis=0)
    til_t = lambda a: jnp.tile(a.T, (blk // q, 1))
    ct = rep_t(ch) * til_t(cl) - rep_t(sh) * til_t(sl)
    st = rep_t(sh) * til_t(cl) + rep_t(ch) * til_t(sl)
    inv = jnp.concatenate([part[:, t] for t in tiles for part in (ct, -st)], axis=1) * (2.0 / m)
    return fwd.astype(BF16), inv.astype(BF16)


def _filter_embedding(seq_len):
    t = jnp.linspace(0.0, 1.0, seq_len, dtype=F32)[:, None]
    bands = (HY_EMB_DIM - 1) // 2
    f = jnp.linspace(1e-4, bands - 1, bands, dtype=F32)[None, :]
    w = 2.0 * math.pi * jnp.arange(seq_len, dtype=F32)[:, None] / seq_len
    z = jnp.concatenate([t, jnp.cos(f * w), -jnp.sin(f * w)], axis=-1)
    z_ext = jnp.concatenate([z, jnp.zeros((1, HY_EMB_DIM), F32), jnp.flip(z[1:], axis=0)], axis=0)
    return jnp.pad(z_ext, ((0, 0), (0, LANES - HY_EMB_DIM)))


def _rope_tables(seq_len):
    quarter = HEAD_DIM // 4
    inv = jnp.tile(ROPE_THETA ** (-jnp.arange(quarter, dtype=F32) / quarter), LANES // quarter)
    lane = jnp.arange(LANES)
    by_row = (lane % HEAD_DIM < HEAD_DIM // 2)[None, None, :]
    first = (lane % (2 * quarter) < quarter)[None, :]

    def cos_sin(npos):
        ang = jnp.arange(npos).astype(F32)[:, None] * inv[None, :]
        return jnp.cos(ang), jnp.sin(ang)

    (cr, sr), (cc, sc) = cos_sin(seq_len // GRID_W), cos_sin(GRID_W)
    lay = lambda r, c: jnp.where(by_row, r[:, None, :], c[None, :, :]).reshape(seq_len, LANES)
    cos, sin = lay(cr, cc), lay(sr, sc)
    return cos, jnp.where(first, -sin, 0.0), jnp.where(first, 0.0, sin)


def _block_ones(width):
    h = jnp.arange(width) // HEAD_DIM
    return (h[:, None] == h[None, :]).astype(BF16)


def _stream_layer(x, p, wts, mod_l, batch, seq_len, rope_tabs, ctx, layer, hy_blk, consts, cast_next=None):
    gq, gk, fwd, inv, z_ext = consts
    latent = ctx is not None
    outs, next_wts = _token_a(x, mod_l, p["norm"], wts, gq, gk, p["qg"], p["kg"], rope_tabs, seq_len, cast_next)
    x1, u_pool, q, kd, vd, hy_v, hy_x1, hy_x2 = outs[:8]
    k, v = (None, None) if latent else outs[8:]
    y_pool = _pool_mix(u_pool, p["pool_w"], p["pool_scale"], batch, seq_len)
    if latent:
        a = _attention(q, kd, vd, p["sink"], batch, seq_len, ctx[0], ctx[1], layer)
    else:
        a = _attention(q, kd, vd, p["sink"], batch, seq_len)
    kraw, ksum = _hyena_filter(z_ext, p["f_w1"], p["f_b1"], p["f_w2"], p["f_b2"], p["f_w3"], p["f_b3"],
                               p["f_freq"], p["decay"], seq_len)
    kf = _filter_spectrum(kraw, ksum, fwd, seq_len, hy_blk)
    z1 = _hyena_conv(hy_v, 0, True, hy_x1, 1, p["short_w"], p["short_b"], p["hy_bias"][0:1], kf, 0, fwd, inv,
                     batch, seq_len, hy_blk, F32)
    y_hy = _hyena_conv(z1, 0, False, hy_x2, 2, p["short_w"], p["short_b"], p["hy_bias"][1:2], kf, 1, fwd, inv,
                       batch, seq_len, hy_blk, BF16)
    x3 = _token_b(x1, y_pool, a, y_hy, mod_l, p["norm"], wts, seq_len, latent)
    return x3, k, v, next_wts


def kernel(x_prompt, x_sample, cache_k, cache_v, c, c_ctx, ada_w, ada_b, norm_w, ffn_wg, ffn_wu, ffn_wd, w_in, w_out, pool_w, pool_scale, q_norm, k_norm, attn_sink, hy_short_w, hy_short_b, hy_f_w1, hy_f_b1, hy_f_w2, hy_f_b2, hy_f_w3, hy_f_b3, hy_sin_freq, hy_decay, hy_bias):
    bp, lp, _ = x_prompt.shape
    bs, ls, _ = x_sample.shape
    lc = cache_k.shape[2]

    cond = jnp.concatenate([c_ctx[None, :], c, jnp.zeros((COND_ROWS - 1 - bs, D_MODEL), F32)], axis=0)
    mod = _ada_mod(cond, ada_w, ada_b).reshape(DEPTH, COND_ROWS, 1, N_MOD * D_MODEL)

    gq, gk = _block_ones(ATTN_WIDTH), _block_ones(KV_WIDTH)
    rope_tabs = _rope_tables(ls)
    blk_p, blk_s = min(lp, HY_BLOCK), min(ls, HY_BLOCK)
    consts_p = (gq, gk) + _dft_tables(blk_p) + (_filter_embedding(lp),)
    consts_s = (gq, gk) + _dft_tables(blk_s) + (_filter_embedding(ls),)
    ctx_k = cache_k.reshape(bs, DEPTH, lc, KV_WIDTH)
    ctx_v = cache_v.reshape(bs, DEPTH, lc, KV_WIDTH)

    flat = _flat_weights(ffn_wg, ffn_wu, ffn_wd, w_in, w_out)
    wts = _cast_layer(flat, 0)
    yp = x_prompt.reshape(bp * lp, D_MODEL)
    ys = x_sample.reshape(bs * ls, D_MODEL)
    ks, vs = [], []
    eye = jnp.eye(len(POOL_WINDOWS), dtype=F32)
    for l in range(DEPTH):
        pool_bd = (eye[:, None, :, None] * pool_w[l][:, :, None, :]).reshape(POOL_WIDTH, POOL_WIDTH)
        p = {
            "norm": norm_w[l],
            "pool_w": pool_bd.astype(BF16), "pool_scale": pool_scale[l][None, :],
            "qg": jnp.tile(q_norm[l], N_HEADS)[None, :], "kg": jnp.tile(k_norm[l], N_KV_HEADS)[None, :],
            "sink": attn_sink[l][None, :],
            "short_w": hy_short_w[l], "short_b": hy_short_b[l][None, :],
            "f_w1": jnp.pad(hy_f_w1[l], ((0, LANES - HY_EMB_DIM), (0, 0))), "f_b1": hy_f_b1[l][None, :],
            "f_w2": hy_f_w2[l], "f_b2": hy_f_b2[l][None, :],
            "f_w3": hy_f_w3[l], "f_b3": hy_f_b3[l][None, :],
            "f_freq": hy_sin_freq[l], "decay": hy_decay[l].reshape(1, HY_ORDER * HY_WIDTH),
            "hy_bias": hy_bias[l],
        }
        yp, k_l, v_l, _ = _stream_layer(yp, p, wts, mod[l], bp, lp, None, None, l, blk_p, consts_p)
        ks.append(k_l.reshape(bp, lp, N_KV_HEADS, HEAD_DIM))
        vs.append(v_l.reshape(bp, lp, N_KV_HEADS, HEAD_DIM))
        cast_next = (flat, l + 1) if l + 1 < DEPTH else None
        ys, _, _, wts = _stream_layer(ys, p, wts, mod[l], bs, ls, rope_tabs, (ctx_k, ctx_v), l, blk_s, consts_s, cast_next)
    return (yp.reshape(bp, lp, D_MODEL), ys.reshape(bs, ls, D_MODEL),
            jnp.stack(ks, axis=1), jnp.stack(vs, axis=1))
```

```python
import functools
import math

import jax
import jax.numpy as jnp
from jax import lax
from jax.experimental import pallas as pl
from jax.experimental.pallas import tpu as pltpu

F32 = jnp.float32
BF16 = jnp.bfloat16

D_MODEL = 1024
DEPTH = 2
GRID_W = 64
POOL_WINDOWS = (2, 4, 8, 16)
POOL_WIDTH = 256
POOL_GROUP = 64
HEAD_DIM = 64
N_HEADS = 8
N_KV_HEADS = 2
GQA_GROUP = 4
ATTN_WIDTH = 512
KV_WIDTH = 128
WINDOW = 128
BLOCK = 128
ROPE_THETA = 10000.0
HY_WIDTH = 256
HY_ORDER = 2
HY_EMB_DIM = 33
HY_FILTER_HIDDEN = 64
HY_MOD_SHIFT = 0.05
D_FF = 2816
IN_WIDTH = 1792
N_MOD = 9
NORM_EPS = 1e-6
NEG_INF = -1e30

LANES = 128
SUBLANES = 8
VMEM_LIMIT = 56 * 1024 * 1024

TOKEN_TILE = 512
TOKEN_TILE_B = 1024
MXU_DIM = 256
FF_CHUNKS = (1280, 1536)
assert sum(FF_CHUNKS) == D_FF and all(w % MXU_DIM == 0 for w in FF_CHUNKS)
SEQ_CHUNK = 512
SEQ_STEP_ROWS = 2048
HY_BLOCK = 1024
FREQ_TILE = 128
FREQ_SUB = 2
MAC_ROWS = 32
ATTN_SUB = 8
CTX_SEQS = 4
COND_ROWS = 8


def _cparams(sem):
    return pltpu.CompilerParams(dimension_semantics=sem, vmem_limit_bytes=VMEM_LIMIT)


def _dot(a, b):
    return jnp.dot(a, b, preferred_element_type=F32)


def _layer_block(tail, layer):
    return pl.BlockSpec((None,) + tuple(tail), lambda *_: (layer,) + (0,) * len(tail))


def _cast_kernel(*refs):
    n = len(refs) // 2
    for i_ref, o_ref in zip(refs[:n], refs[n:]):
        o_ref[...] = i_ref[...].astype(o_ref.dtype)


def _flat_weights(ffn_wg, ffn_wu, ffn_wd, w_in, w_out):
    arrs = {"wg": ffn_wg, "wu": ffn_wu, "wd": ffn_wd, "w_in": w_in, "w_out": w_out}
    return {k: a.reshape(-1, a.shape[-1]) for k, a in arrs.items()}


def _layer_cast_specs(flat, layer, steps):
    in_specs, out_specs, out_shapes = [], [], []
    for a in flat.values():
        rows = a.shape[0] // DEPTH
        blk = rows // steps
        in_specs.append(pl.BlockSpec((blk, a.shape[1]), lambda i, first=layer * steps: (first + i, 0)))
        out_specs.append(pl.BlockSpec((blk, a.shape[1]), lambda i: (i, 0)))
        out_shapes.append(jax.ShapeDtypeStruct((rows, a.shape[1]), BF16))
    return in_specs, out_specs, out_shapes


def _layer_weights(outs):
    wg, wu, wd, w_in, w_out = outs
    return {"wg": wg.reshape(2, D_MODEL, D_FF), "wu": wu.reshape(2, D_MODEL, D_FF),
            "wd": wd.reshape(2, D_FF, D_MODEL), "w_in": w_in, "w_out": w_out}


def _cast_layer(flat, layer):
    in_specs, out_specs, out_shapes = _layer_cast_specs(flat, layer, steps=16)
    outs = pl.pallas_call(
        _cast_kernel,
        grid=(16,),
        in_specs=in_specs,
        out_specs=out_specs,
        out_shape=out_shapes,
        compiler_params=_cparams(("arbitrary",)),
        name="cast_weights",
    )(*flat.values())
    return _layer_weights(outs)


def _ada_kernel(c_ref, w_ref, b_ref, o_ref):
    c = c_ref[...]
    s = (c * jax.nn.sigmoid(c)).astype(BF16)
    o_ref[...] = _dot(s, w_ref[...].astype(BF16)) + b_ref[...]


def _ada_mod(cond, ada_w, ada_b):
    tn = 3072
    nw = N_MOD * D_MODEL
    return pl.pallas_call(
        _ada_kernel,
        grid=(DEPTH, nw // tn),
        in_specs=[
            pl.BlockSpec((COND_ROWS, D_MODEL), lambda l, j: (0, 0)),
            pl.BlockSpec((None, D_MODEL, tn), lambda l, j: (l, 0, j)),
            pl.BlockSpec((None, 1, tn), lambda l, j: (l, 0, j)),
        ],
        out_specs=pl.BlockSpec((None, COND_ROWS, tn), lambda l, j: (l, 0, j)),
        out_shape=jax.ShapeDtypeStruct((DEPTH, COND_ROWS, nw), F32),
        compiler_params=_cparams(("arbitrary", "arbitrary")),
        name="ada_mod",
    )(cond, ada_w, ada_b.reshape(DEPTH, 1, nw))


def _mod_slice(mod_ref, k):
    return mod_ref[:, k * D_MODEL:(k + 1) * D_MODEL]


def _rms_mod(x, gain, scale, shift):
    y = x * lax.rsqrt(jnp.mean(x * x, axis=-1, keepdims=True) + NORM_EPS)
    return (y * gain) * (1.0 + scale) + shift


def _swiglu(hb, wg_ref, wu_ref, wd_ref):
    acc = None
    lo = 0
    for width in FF_CHUNKS:
        sl = slice(lo, lo + width)
        lo += width
        g = _dot(hb, wg_ref[:, sl])
        u = _dot(hb, wu_ref[:, sl])
        a = ((g * jax.nn.sigmoid(g)) * u).astype(BF16)
        y = _dot(a, wd_ref[sl, :])
        acc = y if acc is None else acc + y
    return acc


def _head_norm(x, gmat, gain):
    ss = _dot((x * x).astype(BF16), gmat)
    return (x * lax.rsqrt(ss * (1.0 / HEAD_DIM) + NORM_EPS)) * gain


def _rope(x, cos, sa, sb):
    w = x.shape[1]
    xn = pltpu.roll(x, w - 16, axis=1)
    xp = pltpu.roll(x, 16, axis=1)
    return x * cos + xn * sa + xp * sb


def _token_a_kernel(*refs, rope, n_cast):
    n_in = 11 + (3 if rope else 0) + n_cast
    ins, outs = refs[:n_in], refs[n_in:]
    x_ref, mod_ref, nrm_ref, wg_ref, wu_ref, wd_ref, win_ref, gq_ref, gk_ref, qg_ref, kg_ref = ins[:11]
    if rope:
        cos_ref, sa_ref, sb_ref = ins[11:14]
    x1_ref, up_ref, q_ref, kd_ref, vd_ref, hv_ref, hx1_ref, hx2_ref = outs[:8]
    if not rope:
        k_ref, v_ref = outs[8:10]
    for i_ref, o_ref in zip(ins[n_in - n_cast:], outs[len(outs) - n_cast:]):
        o_ref[...] = i_ref[...].astype(o_ref.dtype)
    x = x_ref[...]
    sh1, sc1, g1 = _mod_slice(mod_ref, 0), _mod_slice(mod_ref, 1), _mod_slice(mod_ref, 2)
    sh2, sc2 = _mod_slice(mod_ref, 3), _mod_slice(mod_ref, 4)
    h = _rms_mod(x, nrm_ref[0:1, :], sc1, sh1).astype(BF16)
    x1 = x + (0.5 * g1) * _swiglu(h, wg_ref, wu_ref, wd_ref)
    x1_ref[...] = x1
    h2 = _rms_mod(x1, nrm_ref[1:2, :], sc2, sh2).astype(BF16)
    s1 = POOL_WIDTH
    s2 = s1 + ATTN_WIDTH
    s3 = s2 + KV_WIDTH
    s4 = s3 + KV_WIDTH
    qkv = _dot(h2, win_ref[:, s1:s4])
    q = _head_norm(qkv[:, :ATTN_WIDTH], gq_ref[...], qg_ref[...])
    k = _head_norm(qkv[:, ATTN_WIDTH:ATTN_WIDTH + KV_WIDTH], gk_ref[...], kg_ref[...])
    v = qkv[:, ATTN_WIDTH + KV_WIDTH:]
    up_ref[...] = _dot(h2, win_ref[:, :s1])
    hy = _dot(h2, win_ref[:, s4:])
    for j, ref in enumerate((hv_ref, hx1_ref, hx2_ref)):
        ref[...] = hy[:, j * HY_WIDTH:(j + 1) * HY_WIDTH]
    if rope:
        cos, sa, sb = cos_ref[...], sa_ref[...], sb_ref[...]
        reps = ATTN_WIDTH // LANES
        q = _rope(q, jnp.concatenate([cos] * reps, axis=1), jnp.concatenate([sa] * reps, axis=1),
                  jnp.concatenate([sb] * reps, axis=1))
        k = _rope(k, cos, sa, sb)
    q_ref[...] = (q * (HEAD_DIM ** -0.5 * math.log2(math.e))).astype(BF16)
    kd_ref[...] = _dup_heads(k)
    vd_ref[...] = _dup_heads(v)
    if not rope:
        k_ref[...] = k
        v_ref[...] = v


def _token_a(x, mod, nrm, wts, gq, gk, qg, kg, rope_tabs, seq_len, layer, cast_next=None):
    n = x.shape[0]
    tm = TOKEN_TILE
    tiles_per_seq = max(seq_len // tm, 1)
    rope = rope_tabs is not None
    if rope:
        mod_row = lambda i: (layer, 1 + i // tiles_per_seq, 0, 0)
    else:
        mod_row = lambda i: (layer, 0, 0, 0)
    const = lambda i: (0, 0)
    row = lambda i: (i, 0)
    in_specs = [
        pl.BlockSpec((tm, D_MODEL), row),
        pl.BlockSpec((None, None, 1, N_MOD * D_MODEL), mod_row),
        _layer_block((3, D_MODEL), layer),
        pl.BlockSpec((None, D_MODEL, D_FF), lambda i: (0, 0, 0)),
        pl.BlockSpec((None, D_MODEL, D_FF), lambda i: (0, 0, 0)),
        pl.BlockSpec((None, D_FF, D_MODEL), lambda i: (0, 0, 0)),
        pl.BlockSpec((D_MODEL, IN_WIDTH), const),
        pl.BlockSpec((ATTN_WIDTH, ATTN_WIDTH), const),
        pl.BlockSpec((KV_WIDTH, KV_WIDTH), const),
        _layer_block((1, ATTN_WIDTH), layer),
        _layer_block((1, KV_WIDTH), layer),
    ]
    args = [x, mod, nrm, wts["wg"], wts["wu"], wts["wd"], wts["w_in"], gq, gk, qg, kg]
    if rope:
        tab = pl.BlockSpec((tm, LANES), lambda i: (i % tiles_per_seq, 0))
        in_specs += [tab, tab, tab]
        args += list(rope_tabs)
    widths = (D_MODEL, POOL_WIDTH, ATTN_WIDTH, 2 * KV_WIDTH, 2 * KV_WIDTH) + (HY_WIDTH,) * (HY_ORDER + 1)
    dtypes = (F32, F32, BF16, BF16, BF16) + (F32,) * (HY_ORDER + 1)
    if not rope:
        widths += (KV_WIDTH, KV_WIDTH)
        dtypes += (F32, F32)
    out_specs = [pl.BlockSpec((tm, w), row) for w in widths]
    out_shapes = [jax.ShapeDtypeStruct((n, w), d) for w, d in zip(widths, dtypes)]
    n_cast = 0
    if cast_next is not None:
        flat, next_layer = cast_next
        c_in, c_out, c_shapes = _layer_cast_specs(flat, next_layer, steps=n // tm)
        in_specs += c_in
        args += list(flat.values())
        out_specs += c_out
        out_shapes += c_shapes
        n_cast = len(c_in)
    outs = pl.pallas_call(
        functools.partial(_token_a_kernel, rope=rope, n_cast=n_cast),
        grid=(n // tm,),
        in_specs=in_specs,
        out_specs=out_specs,
        out_shape=out_shapes,
        compiler_params=_cparams(("arbitrary",)),
        name="token_a_rope" if rope else "token_a",
    )(*args)
    if n_cast:
        return outs[:len(outs) - n_cast], _layer_weights(outs[len(outs) - n_cast:])
    return outs, None


def _token_b_kernel(x_ref, yp_ref, a_ref, yh_ref, mod_ref, nrm_ref, wo_ref, wg_ref, wu_ref, wd_ref, o_ref):
    x1 = x_ref[...]
    g2 = _mod_slice(mod_ref, 5)
    sh3, sc3, g3 = _mod_slice(mod_ref, 6), _mod_slice(mod_ref, 7), _mod_slice(mod_ref, 8)
    cat = jnp.concatenate([yp_ref[...], a_ref[...], yh_ref[...]], axis=1)
    x2 = x1 + g2 * _dot(cat, wo_ref[...])
    h3 = _rms_mod(x2, nrm_ref[2:3, :], sc3, sh3).astype(BF16)
    o_ref[...] = x2 + (0.5 * g3) * _swiglu(h3, wg_ref, wu_ref, wd_ref)


def _token_b(x1, y_pool, a, y_hy, mod, nrm, wts, seq_len, layer, per_seq_cond):
    n = x1.shape[0]
    tm = min(TOKEN_TILE_B, n)
    tiles_per_seq = max(seq_len // tm, 1)
    if per_seq_cond:
        mod_row = lambda i: (layer, 1 + i // tiles_per_seq, 0, 0)
    else:
        mod_row = lambda i: (layer, 0, 0, 0)
    const = lambda i: (0, 0)
    row = lambda i: (i, 0)
    return pl.pallas_call(
        _token_b_kernel,
        grid=(n // tm,),
        in_specs=[
            pl.BlockSpec((tm, D_MODEL), row),
            pl.BlockSpec((tm, POOL_WIDTH), row),
            pl.BlockSpec((tm, ATTN_WIDTH), row),
            pl.BlockSpec((tm, HY_WIDTH), row),
            pl.BlockSpec((None, None, 1, N_MOD * D_MODEL), mod_row),
            _layer_block((3, D_MODEL), layer),
            pl.BlockSpec((D_MODEL, D_MODEL), const),
            pl.BlockSpec((None, D_MODEL, D_FF), lambda i: (1, 0, 0)),
            pl.BlockSpec((None, D_MODEL, D_FF), lambda i: (1, 0, 0)),
            pl.BlockSpec((None, D_FF, D_MODEL), lambda i: (1, 0, 0)),
        ],
        out_specs=pl.BlockSpec((tm, D_MODEL), row),
        out_shape=jax.ShapeDtypeStruct((n, D_MODEL), F32),
        compiler_params=_cparams(("arbitrary",)),
        name="token_b",
    )(x1, y_pool, a, y_hy, mod, nrm, wts["w_out"], wts["wg"], wts["wu"], wts["wd"])


def _seqs_per_step(batch, seq_len):
    return min(batch, max(1, SEQ_STEP_ROWS // seq_len))


def _halo_rows(src_ref, base, r0, rows, seq_len):
    c = src_ref.shape[1]
    zero = jnp.zeros((SUBLANES, c), F32)
    lo = base + r0
    prev = src_ref[lo - SUBLANES:lo, :] if r0 > 0 else zero
    nxt = src_ref[lo + rows:lo + rows + SUBLANES, :] if r0 + rows < seq_len else zero
    return prev, nxt


def _short_conv_chunk(src_ref, base, r0, rows, seq_len, w, b):
    x = src_ref[base + r0:base + r0 + rows, :]
    prev, nxt = _halo_rows(src_ref, base, r0, rows, seq_len)
    ridx = lax.broadcasted_iota(jnp.int32, x.shape, 0)
    xp = jnp.where(ridx == 0, prev[SUBLANES - 1:SUBLANES, :], pltpu.roll(x, 1, axis=0))
    xn = jnp.where(ridx == rows - 1, nxt[0:1, :], pltpu.roll(x, rows - 1, axis=0))
    return xp * w[0:1, :] + x * w[1:2, :] + xn * w[2:3, :] + b


def _pool_kernel(u_ref, w_ref, scale_ref, o_ref, *, seq_len, rows, seqs):
    lane = lax.broadcasted_iota(jnp.int32, (rows, POOL_WIDTH), 1)
    grp = lane // POOL_GROUP
    half = jnp.where(grp == 0, 1, jnp.where(grp == 1, 2, jnp.where(grp == 2, 4, 8)))
    odd_grp = lax.broadcasted_iota(jnp.int32, (rows, LANES), 1) >= POOL_GROUP
    ext = rows + 2 * SUBLANES
    back = lambda v, s: pltpu.roll(v, s, axis=0)
    fwd = lambda v, s: pltpu.roll(v, ext - s, axis=0)
    core = lambda v: v[SUBLANES:SUBLANES + rows, :]
    for base, r0 in [(sq * seq_len, r0) for sq in range(seqs) for r0 in range(0, seq_len, rows)]:
        x = u_ref[base + r0:base + r0 + rows, :]
        prev, nxt = _halo_rows(u_ref, base, r0, rows, seq_len)
        a = jnp.concatenate([prev, x, nxt], axis=0)
        lo, hi = a[:, :LANES], a[:, LANES:]
        b1 = back(lo, 1)
        b2 = b1 + back(b1, 1)
        f2 = lo + fwd(lo, 1)
        sum_lo = jnp.where(odd_grp, core(b2) + core(f2), core(b1) + core(lo))
        c1 = back(hi, 1)
        c2 = c1 + back(c1, 1)
        c4 = c2 + back(c2, 2)
        c8 = c4 + back(c4, 4)
        g2 = hi + fwd(hi, 1)
        g4 = g2 + fwd(g2, 2)
        g8 = g4 + fwd(g4, 4)
        sum_hi = jnp.where(odd_grp, core(c8) + core(g8), core(c4) + core(g4))
        wsum = jnp.concatenate([sum_lo, sum_hi], axis=1)
        t = r0 + lax.broadcasted_iota(jnp.int32, (rows, POOL_WIDTH), 0)
        cnt = jnp.minimum(t + half, seq_len) - jnp.maximum(t - half, 0)
        d = wsum / cnt.astype(F32) - x
        y = _dot(d.astype(BF16), w_ref[...]) * scale_ref[...]
        o_ref[base + r0:base + r0 + rows, :] = y.astype(o_ref.dtype)


def _pool_mix(u_pool, w_bd, scale, batch, seq_len, layer):
    rows = min(SEQ_CHUNK // 2, seq_len)
    seqs = _seqs_per_step(batch, seq_len)
    return pl.pallas_call(
        functools.partial(_pool_kernel, seq_len=seq_len, rows=rows, seqs=seqs),
        grid=(batch // seqs,),
        in_specs=[
            pl.BlockSpec((seqs * seq_len, POOL_WIDTH), lambda b: (b, 0)),
            _layer_block((POOL_WIDTH, POOL_WIDTH), layer),
            _layer_block((1, POOL_WIDTH), layer),
        ],
        out_specs=pl.BlockSpec((seqs * seq_len, POOL_WIDTH), lambda b: (b, 0)),
        out_shape=jax.ShapeDtypeStruct((batch * seq_len, POOL_WIDTH), BF16),
        compiler_params=_cparams(("arbitrary",)),
        name="pool_mix",
    )(u_pool, w_bd, scale)


def _dup_heads(x):
    lane = lax.broadcasted_iota(jnp.int32, x.shape, 1)
    sw = pltpu.roll(x, HEAD_DIM, axis=1)
    lo = lane < HEAD_DIM
    return jnp.concatenate([jnp.where(lo, x, sw), jnp.where(lo, sw, x)], axis=1).astype(BF16)


def _attn_kernel(*refs, has_local, nblocks, sub, seqs, layer):
    if has_local:
        sink_ref, q_ref, kp_ref, kc_ref, kn_ref, vp_ref, vc_ref, vn_ref, ck_ref, cv_ref, o_ref = refs
        kwin = jnp.concatenate([kp_ref[...], kc_ref[...], kn_ref[...]], axis=0)
        vwin = jnp.concatenate([vp_ref[...], vc_ref[...], vn_ref[...]], axis=0)
        kctx, vctx = _dup_heads(ck_ref[...]), _dup_heads(cv_ref[...])
        r = lax.broadcasted_iota(jnp.int32, (GQA_GROUP * BLOCK, BLOCK), 0) % BLOCK
        j = lax.broadcasted_iota(jnp.int32, (GQA_GROUP * BLOCK, BLOCK), 1)
    else:
        sink_ref, q_ref, kd_ref, vd_ref, o_ref = refs
    i = pl.program_id(1)
    lane_q = lax.broadcasted_iota(jnp.int32, (BLOCK, LANES), 1)
    log2e = math.log2(math.e)
    units = [(sb, kvh) for sb in range(seqs * sub) for kvh in range(N_KV_HEADS)]
    scores, values = [], []
    for sb, kvh in units:
        q = q_ref[sb * BLOCK:(sb + 1) * BLOCK, :]
        qparts = []
        for hd in range(kvh * GQA_GROUP, (kvh + 1) * GQA_GROUP):
            qp = q[:, (hd // 2) * LANES:(hd // 2 + 1) * LANES]
            keep = (lane_q < HEAD_DIM) == (hd % 2 == 0)
            qparts.append(jnp.where(keep, qp, jnp.zeros_like(qp)))
        qs = jnp.concatenate(qparts, axis=0)
        cols = slice(kvh * LANES, (kvh + 1) * LANES)
        if has_local:
            loc = slice(sb * BLOCK, (sb + 3) * BLOCK)
            kk = jnp.concatenate([kwin[loc, cols], kctx[:, cols]], axis=0)
            vv = jnp.concatenate([vwin[loc, cols], vctx[:, cols]], axis=0)
        else:
            own = slice((sb // sub) * sub * BLOCK, (sb // sub + 1) * sub * BLOCK)
            kk, vv = kd_ref[own, cols], vd_ref[own, cols]
        s = lax.dot_general(qs, kk, (((1,), (1,)), ((), ())), preferred_element_type=F32)
        if has_local:
            gb = i * sub + sb
            below = j >= r + jnp.where(gb >= 1, 0, BLOCK)
            above = j <= r - jnp.where(gb <= nblocks - 2, 0, BLOCK)
            s = jnp.concatenate([jnp.where(below, s[:, :BLOCK], NEG_INF), s[:, BLOCK:2 * BLOCK],
                                 jnp.where(above, s[:, 2 * BLOCK:3 * BLOCK], NEG_INF), s[:, 3 * BLOCK:]], axis=1)
        scores.append(s)
        values.append(jnp.concatenate([vv, jnp.ones_like(vv)], axis=1))

    def per_head(kvh, col, f):
        parts = [f(col[g * BLOCK:(g + 1) * BLOCK, :], sink_ref[layer, kvh * GQA_GROUP + g] * log2e) for g in range(GQA_GROUP)]
        return jnp.concatenate(parts, axis=0)

    maxes = [per_head(kvh, jnp.max(s, axis=1, keepdims=True), jnp.maximum) for (_, kvh), s in zip(units, scores)]
    probs = [jnp.exp2(s - m).astype(BF16) for s, m in zip(scores, maxes)]
    sums = [_dot(e, v) for e, v in zip(probs, values)]
    outs = []
    for (_, kvh), ow, m in zip(units, sums, maxes):
        den = ow[:, LANES:] + per_head(kvh, m, lambda mm, sk: jnp.exp2(sk - mm))
        outs.append(ow[:, :LANES] / den)
    for sb in range(seqs * sub):
        heads = [outs[sb * N_KV_HEADS + kvh][g * BLOCK:(g + 1) * BLOCK, :]
                 for kvh in range(N_KV_HEADS) for g in range(GQA_GROUP)]
        blks = [jnp.where(lane_q < HEAD_DIM, heads[2 * p], heads[2 * p + 1]) for p in range(N_HEADS // 2)]
        o_ref[sb * BLOCK:(sb + 1) * BLOCK, :] = jnp.concatenate(blks, axis=1).astype(o_ref.dtype)


def _attention(q, kd, vd, sink, batch, seq_len, ctx_k=None, ctx_v=None, layer=0):
    nb = seq_len // BLOCK
    has_local = ctx_k is not None
    sub = min(ATTN_SUB, nb)
    steps = nb // sub
    seqs = 1
    if not has_local:
        assert steps == 1
        seqs = math.gcd(CTX_SEQS, batch)
        batch //= seqs
    qb = seqs * sub * BLOCK
    qspec = pl.BlockSpec((qb, ATTN_WIDTH), lambda b, i: (b * steps + i, 0))
    sspec = pl.BlockSpec(memory_space=pltpu.SMEM)
    if has_local:
        lc = ctx_k.shape[2]
        edge = lambda f: pl.BlockSpec((BLOCK, 2 * KV_WIDTH), f)
        prev = lambda b, i: (b * nb + jnp.maximum(i * sub - 1, 0), 0)
        nxt = lambda b, i: (b * nb + jnp.minimum((i + 1) * sub, nb - 1), 0)
        cur = pl.BlockSpec((qb, 2 * KV_WIDTH), lambda b, i: (b * steps + i, 0))
        cspec = pl.BlockSpec((None, None, lc, KV_WIDTH), lambda b, i: (b, layer, 0, 0))
        in_specs = [sspec, qspec, edge(prev), cur, edge(nxt), edge(prev), cur, edge(nxt), cspec, cspec]
        args = [sink, q, kd, kd, kd, vd, vd, vd, ctx_k, ctx_v]
    else:
        kv = pl.BlockSpec((qb, 2 * KV_WIDTH), lambda b, i: (b, 0))
        in_specs = [sspec, qspec, kv, kv]
        args = [sink, q, kd, vd]
    return pl.pallas_call(
        functools.partial(_attn_kernel, has_local=has_local, nblocks=nb, sub=sub, seqs=seqs, layer=layer),
        grid=(batch, steps),
        in_specs=in_specs,
        out_specs=pl.BlockSpec((qb, ATTN_WIDTH), lambda b, i: (b * steps + i, 0)),
        out_shape=jax.ShapeDtypeStruct(q.shape, BF16),
        compiler_params=_cparams(("arbitrary", "arbitrary")),
        name="attn_latent" if has_local else "attn_context",
    )(*args)


def _filter_kernel(zt_ref, zb_ref, w1_ref, b1_ref, w2_ref, b2_ref, w3_ref, b3_ref, fr_ref, dl_ref, k_ref, sum_ref,
                   *, seq_len, rows):
    i = pl.program_id(0)
    half = rows // 2
    oc = HY_ORDER * HY_WIDTH
    z = jnp.concatenate([zt_ref[...], zb_ref[...]], axis=1)
    h = jnp.sin(fr_ref[0:1, :] * (_dot(z.astype(BF16), w1_ref[...].astype(BF16)) + b1_ref[...]))
    h = jnp.sin(fr_ref[1:2, :] * (_dot(h.astype(BF16), w2_ref[...].astype(BF16)) + b2_ref[...]))
    h3 = _dot(h.astype(BF16), w3_ref[...].astype(BF16)) + b3_ref[...]
    total = jnp.zeros((1, oc), F32)
    for part, z_ref in enumerate((zt_ref, zb_ref)):
        row = i * rows + part * half + lax.broadcasted_iota(jnp.int32, (half, oc), 0)
        t = z_ref[:, 0:1]
        decay = jnp.exp(-t * jnp.abs(dl_ref[...]))
        fwd_dir = h3[:, part * 2 * oc:part * 2 * oc + oc]
        bwd_dir = h3[:, part * 2 * oc + oc:(part + 1) * 2 * oc]
        sel = jnp.where(row < seq_len, fwd_dir, jnp.where(row > seq_len, bwd_dir, 0.0))
        k = sel * (decay + HY_MOD_SHIFT)
        k_ref[part * half:(part + 1) * half, :] = k
        total = total + jnp.sum(jnp.abs(k), axis=0, keepdims=True)

    @pl.when(i == 0)
    def _():
        sum_ref[...] = jnp.zeros_like(sum_ref)

    sum_ref[...] += total


def _block_diag2(w):
    z = jnp.zeros_like(w)
    return jnp.concatenate([jnp.concatenate([w, z], axis=-1), jnp.concatenate([z, w], axis=-1)], axis=-2)


def _filter_weights(w1, b1, w2, b2, w3, b3, freq, deltas):
    two = lambda a: jnp.concatenate([a, a], axis=-1)
    w1p = jnp.pad(w1, ((0, 0), (0, LANES - HY_EMB_DIM), (0, 0)))
    return (_block_diag2(w1p), two(b1)[:, None, :], _block_diag2(w2), two(b2)[:, None, :],
            _block_diag2(w3), two(b3)[:, None, :], two(freq), deltas.reshape(DEPTH, 1, HY_ORDER * HY_WIDTH))


def _hyena_filter(z_ext, fw, seq_len, layer):
    n = 2 * seq_len
    rows = min(1024, n)
    half = rows // 2
    oc = HY_ORDER * HY_WIDTH
    hid = 2 * HY_FILTER_HIDDEN
    const = lambda i: (0, 0)
    return pl.pallas_call(
        functools.partial(_filter_kernel, seq_len=seq_len, rows=rows),
        grid=(n // rows,),
        in_specs=[
            pl.BlockSpec((half, LANES), lambda i: (2 * i, 0)),
            pl.BlockSpec((half, LANES), lambda i: (2 * i + 1, 0)),
            _layer_block((2 * LANES, hid), layer),
            _layer_block((1, hid), layer),
            _layer_block((hid, hid), layer),
            _layer_block((1, hid), layer),
            _layer_block((hid, 4 * oc), layer),
            _layer_block((1, 4 * oc), layer),
            _layer_block((2, hid), layer),
            _layer_block((1, oc), layer),
        ],
        out_specs=[pl.BlockSpec((rows, oc), lambda i: (i, 0)), pl.BlockSpec((1, oc), const)],
        out_shape=[jax.ShapeDtypeStruct((n, oc), F32), jax.ShapeDtypeStruct((1, oc), F32)],
        compiler_params=_cparams(("arbitrary",)),
        name="hyena_filter",
    )(z_ext, z_ext, *fw)


def _spectrum_kernel(k_ref, sum_ref, fwd_ref, kf_ref, hprev_ref, g0_ref, *, blk):
    t = pl.program_id(0)
    a = k_ref[...] / (sum_ref[...] + 1e-6)
    ha = _dot(fwd_ref[...], a.astype(BF16))
    tf = FREQ_TILE
    par = lax.broadcasted_iota(jnp.int32, (tf, 1), 0) % 2
    sgn = (1 - 2 * par).astype(F32)

    @pl.when(t > 0)
    def _():
        g0 = g0_ref[...]
        for c in range(blk // tf):
            re = slice(2 * c * tf, (2 * c + 1) * tf)
            im = slice((2 * c + 1) * tf, (2 * c + 2) * tf)
            kre = ha[re, :] - sgn * hprev_ref[im, :]
            kim = ha[im, :] + sgn * (hprev_ref[re, :] - g0)
            for o in range(HY_ORDER):
                lanes = slice(o * HY_WIDTH, (o + 1) * HY_WIDTH)
                kf_ref[o, c, 0:tf, :] = kre[:, lanes]
                kf_ref[o, c, tf:2 * tf, :] = kim[:, lanes]

    hprev_ref[...] = ha
    g0_ref[...] = a[0:1, :]


def _filter_spectrum(kraw, ksum, fwd, seq_len, blk):
    nb = seq_len // blk
    nlags = 2 * nb - 1
    oc = HY_ORDER * HY_WIDTH
    m = 2 * blk
    tf = FREQ_TILE
    nfc = blk // tf
    return pl.pallas_call(
        functools.partial(_spectrum_kernel, blk=blk),
        grid=(2 * nb,),
        in_specs=[
            pl.BlockSpec((blk, oc), lambda t: ((nb + t) % (2 * nb), 0)),
            pl.BlockSpec((1, oc), lambda t: (0, 0)),
            pl.BlockSpec((m, blk), lambda t: (0, 0)),
        ],
        out_specs=pl.BlockSpec((HY_ORDER, nfc, None, 2 * tf, HY_WIDTH), lambda t: (0, 0, jnp.maximum(t - 1, 0), 0, 0)),
        out_shape=jax.ShapeDtypeStruct((HY_ORDER, nfc, nlags, 2 * tf, HY_WIDTH), F32),
        scratch_shapes=[pltpu.VMEM((m, oc), F32), pltpu.VMEM((1, oc), F32)],
        compiler_params=_cparams(("arbitrary",)),
        name="filter_spectrum",
    )(kraw, ksum, fwd)


def _conv_kernel(z_ref, g_ref, swz_ref, sbz_ref, swg_ref, sbg_ref, bd_ref, kf_ref, fwd_ref, inv_ref, o_ref,
                 zb_ref, zc_ref, yf_ref, *, seq_len, blk, conv_z, seqs, sub):
    fc = pl.program_id(1)
    nfc = pl.num_programs(1)
    nb = seq_len // blk
    tf = FREQ_TILE
    blocks = [((g // nb) * seq_len, (g % nb) * blk) for g in range(seqs * nb)]

    @pl.when(fc == 0)
    def _():
        for base, r0 in blocks:
            rows = slice(base + r0, base + r0 + blk)
            if conv_z:
                zc_ref[rows, :] = _short_conv_chunk(z_ref, base, r0, blk, seq_len, swz_ref[...], sbz_ref[...])
                zb_ref[rows, :] = zc_ref[rows, :].astype(BF16)
            else:
                zb_ref[rows, :] = z_ref[rows, :].astype(BF16)

    zf = [[_dot(fwd_ref[s * 2 * tf:(s + 1) * 2 * tf, :], zb_ref[g * blk:(g + 1) * blk, :]) for g in range(seqs * nb)]
          for s in range(sub)]

    for s in range(sub):
        col = pl.multiple_of((fc * sub + s) * 2 * tf, 2 * tf)
        for g0 in range(0, seqs * nb, nb):
            for bi in range(nb):
                for r in range(0, tf, MAC_ROWS):
                    re = slice(r, r + MAC_ROWS)
                    im = slice(tf + r, tf + r + MAC_ROWS)
                    yr = jnp.zeros((MAC_ROWS, HY_WIDTH), F32)
                    yi = jnp.zeros((MAC_ROWS, HY_WIDTH), F32)
                    for bj in range(nb):
                        lag = bi - bj + nb - 1
                        kr, ki = kf_ref[s, lag, re, :], kf_ref[s, lag, im, :]
                        zr, zi = zf[s][g0 + bj][re, :], zf[s][g0 + bj][im, :]
                        yr = yr + (kr * zr - ki * zi)
                        yi = yi + (kr * zi + ki * zr)
                    yf_ref[g0 + bi, pl.ds(col + r, MAC_ROWS), :] = yr.astype(BF16)
                    yf_ref[g0 + bi, pl.ds(col + tf + r, MAC_ROWS), :] = yi.astype(BF16)

    @pl.when(fc == nfc - 1)
    def _():
        ys, gates = [], []
        for g, (base, r0) in enumerate(blocks):
            ys.append(_dot(inv_ref[...], yf_ref[g]))
            gates.append(_short_conv_chunk(g_ref, base, r0, blk, seq_len, swg_ref[...], sbg_ref[...]))
        for (base, r0), y, gate in zip(blocks, ys, gates):
            rows = slice(base + r0, base + r0 + blk)
            z = zc_ref[rows, :] if conv_z else z_ref[rows, :]
            o_ref[rows, :] = (gate * (y + bd_ref[...] * z)).astype(o_ref.dtype)


def _hyena_conv(z_src, z_col, conv_z, g_src, g_col, short_w, short_b, bd, kf, order, fwd, inv,
                batch, seq_len, blk, out_dtype, layer):
    nb = seq_len // blk
    nlags = 2 * nb - 1
    tf = FREQ_TILE
    sub = min(FREQ_SUB, blk // tf)
    nfc = blk // (sub * tf)
    w = HY_WIDTH
    seqs = _seqs_per_step(batch, seq_len)
    step_rows = seqs * seq_len
    return pl.pallas_call(
        functools.partial(_conv_kernel, seq_len=seq_len, blk=blk, conv_z=conv_z, seqs=seqs, sub=sub),
        grid=(batch // seqs, nfc),
        in_specs=[
            pl.BlockSpec((step_rows, w), lambda b, f: (b, 0)),
            pl.BlockSpec((step_rows, w), lambda b, f: (b, 0)),
            pl.BlockSpec((None, 3, w), lambda b, f: (layer, 0, z_col if conv_z else 0)),
            pl.BlockSpec((None, 1, w), lambda b, f: (layer, 0, z_col if conv_z else 0)),
            pl.BlockSpec((None, 3, w), lambda b, f: (layer, 0, g_col)),
            pl.BlockSpec((None, 1, w), lambda b, f: (layer, 0, g_col)),
            pl.BlockSpec((None, 1, w), lambda b, f: (layer * HY_ORDER + order, 0, 0)),
            pl.BlockSpec((None, sub, nlags, 2 * tf, w), lambda b, f: (order, f, 0, 0, 0)),
            pl.BlockSpec((sub * 2 * tf, blk), lambda b, f: (f, 0)),
            pl.BlockSpec((blk, 2 * blk), lambda b, f: (0, 0)),
        ],
        out_specs=pl.BlockSpec((step_rows, w), lambda b, f: (b, 0)),
        out_shape=jax.ShapeDtypeStruct((batch * seq_len, w), out_dtype),
        scratch_shapes=[
            pltpu.VMEM((step_rows, w), BF16),
            pltpu.VMEM((step_rows if conv_z else SUBLANES, w), F32),
            pltpu.VMEM((seqs * nb, 2 * blk, w), BF16),
        ],
        compiler_params=_cparams(("arbitrary", "arbitrary")),
        name="hyena_conv",
    )(z_src, g_src, short_w, short_b, short_w, short_b, bd, kf, fwd, inv)


def _dft_tables(blk):
    m = 2 * blk
    tf = FREQ_TILE
    nt = blk // tf
    q = 32
    f2 = 2 * jnp.arange(blk, dtype=jnp.int32) + 1

    def cos_sin(times):
        r = (f2[:, None] * times[None, :]) % (2 * m)
        ang = r.astype(F32) * (math.pi / m)
        return jnp.cos(ang), jnp.sin(ang)

    ch, sh = cos_sin(jnp.arange(0, blk, q, dtype=jnp.int32))
    cl, sl = cos_sin(jnp.arange(q, dtype=jnp.int32))
    rep = lambda a: jnp.repeat(a, q, axis=1)
    til = lambda a: jnp.tile(a, (1, blk // q))
    c = rep(ch) * til(cl) - rep(sh) * til(sl)
    s = rep(sh) * til(cl) + rep(ch) * til(sl)
    tiles = [slice(t * tf, (t + 1) * tf) for t in range(nt)]
    fwd = jnp.concatenate([part[t, :] for t in tiles for part in (c, -s)], axis=0)
    rep_t = lambda a: jnp.repeat(a.T, q, axis=0)
    til_t = lambda a: jnp.tile(a.T, (blk // q, 1))
    ct = rep_t(ch) * til_t(cl) - rep_t(sh) * til_t(sl)
    st = rep_t(sh) * til_t(cl) + rep_t(ch) * til_t(sl)
    inv = jnp.concatenate([part[:, t] for t in tiles for part in (ct, -st)], axis=1) * (2.0 / m)
    return fwd.astype(BF16), inv.astype(BF16)


def _filter_embedding(seq_len):
    t = jnp.linspace(0.0, 1.0, seq_len, dtype=F32)[:, None]
    bands = (HY_EMB_DIM - 1) // 2
    f = jnp.linspace(1e-4, bands - 1, bands, dtype=F32)[None, :]
    w = 2.0 * math.pi * jnp.arange(seq_len, dtype=F32)[:, None] / seq_len
    z = jnp.concatenate([t, jnp.cos(f * w), -jnp.sin(f * w)], axis=-1)
    z_ext = jnp.concatenate([z, jnp.zeros((1, HY_EMB_DIM), F32), jnp.flip(z[1:], axis=0)], axis=0)
    return jnp.pad(z_ext, ((0, 0), (0, LANES - HY_EMB_DIM)))


def _rope_tables(seq_len):
    quarter = HEAD_DIM // 4
    inv = jnp.tile(ROPE_THETA ** (-jnp.arange(quarter, dtype=F32) / quarter), LANES // quarter)
    lane = jnp.arange(LANES)
    by_row = (lane % HEAD_DIM < HEAD_DIM // 2)[None, None, :]
    first = (lane % (2 * quarter) < quarter)[None, :]

    def cos_sin(npos):
        ang = jnp.arange(npos).astype(F32)[:, None] * inv[None, :]
        return jnp.cos(ang), jnp.sin(ang)

    (cr, sr), (cc, sc) = cos_sin(seq_len // GRID_W), cos_sin(GRID_W)
    lay = lambda r, c: jnp.where(by_row, r[:, None, :], c[None, :, :]).reshape(seq_len, LANES)
    cos, sin = lay(cr, cc), lay(sr, sc)
    return cos, jnp.where(first, -sin, 0.0), jnp.where(first, 0.0, sin)


def _block_ones(width):
    h = jnp.arange(width) // HEAD_DIM
    return (h[:, None] == h[None, :]).astype(BF16)


def _stream_layer(x, p, wts, mod, batch, seq_len, rope_tabs, ctx, layer, hy_blk, consts, cast_next=None):
    gq, gk, fwd, inv, z_ext = consts
    latent = ctx is not None
    outs, next_wts = _token_a(x, mod, p["norm"], wts, gq, gk, p["qg"], p["kg"], rope_tabs, seq_len, layer, cast_next)
    x1, u_pool, q, kd, vd, hy_v, hy_x1, hy_x2 = outs[:8]
    k, v = (None, None) if latent else outs[8:]
    y_pool = _pool_mix(u_pool, p["pool_w"], p["pool_scale"], batch, seq_len, layer)
    if latent:
        a = _attention(q, kd, vd, p["sink"], batch, seq_len, ctx[0], ctx[1], layer)
    else:
        a = _attention(q, kd, vd, p["sink"], batch, seq_len, layer=layer)
    kraw, ksum = _hyena_filter(z_ext, p["filter"], seq_len, layer)
    kf = _filter_spectrum(kraw, ksum, fwd, seq_len, hy_blk)
    z1 = _hyena_conv(hy_v, 0, True, hy_x1, 1, p["short_w"], p["short_b"], p["hy_bias"], kf, 0, fwd, inv,
                     batch, seq_len, hy_blk, F32, layer)
    y_hy = _hyena_conv(z1, 0, False, hy_x2, 2, p["short_w"], p["short_b"], p["hy_bias"], kf, 1, fwd, inv,
                       batch, seq_len, hy_blk, BF16, layer)
    x3 = _token_b(x1, y_pool, a, y_hy, mod, p["norm"], wts, seq_len, layer, latent)
    return x3, k, v, next_wts


def _layer_params(norm_w, pool_w, pool_scale, q_norm, k_norm, attn_sink, hy_short_w, hy_short_b, hy_f_w1, hy_f_b1,
                  hy_f_w2, hy_f_b2, hy_f_w3, hy_f_b3, hy_sin_freq, hy_decay, hy_bias):
    eye = jnp.eye(len(POOL_WINDOWS), dtype=F32)
    pool_bd = (eye[None, :, None, :, None] * pool_w[:, :, :, None, :]).reshape(DEPTH, POOL_WIDTH, POOL_WIDTH)
    return {
        "norm": norm_w,
        "pool_w": pool_bd.astype(BF16), "pool_scale": pool_scale[:, None, :],
        "qg": jnp.tile(q_norm, (1, N_HEADS))[:, None, :], "kg": jnp.tile(k_norm, (1, N_KV_HEADS))[:, None, :],
        "sink": attn_sink,
        "short_w": hy_short_w, "short_b": hy_short_b[:, None, :],
        "filter": _filter_weights(hy_f_w1, hy_f_b1, hy_f_w2, hy_f_b2, hy_f_w3, hy_f_b3, hy_sin_freq, hy_decay),
        "hy_bias": hy_bias.reshape(DEPTH * HY_ORDER, 1, HY_WIDTH),
    }


def kernel(x_prompt, x_sample, cache_k, cache_v, c, c_ctx, ada_w, ada_b, norm_w, ffn_wg, ffn_wu, ffn_wd, w_in, w_out, pool_w, pool_scale, q_norm, k_norm, attn_sink, hy_short_w, hy_short_b, hy_f_w1, hy_f_b1, hy_f_w2, hy_f_b2, hy_f_w3, hy_f_b3, hy_sin_freq, hy_decay, hy_bias):
    bp, lp, _ = x_prompt.shape
    bs, ls, _ = x_sample.shape
    lc = cache_k.shape[2]

    cond = jnp.concatenate([c_ctx[None, :], c, jnp.zeros((COND_ROWS - 1 - bs, D_MODEL), F32)], axis=0)
    mod = _ada_mod(cond, ada_w, ada_b).reshape(DEPTH, COND_ROWS, 1, N_MOD * D_MODEL)

    gq, gk = _block_ones(ATTN_WIDTH), _block_ones(KV_WIDTH)
    rope_tabs = _rope_tables(ls)
    blk_p, blk_s = min(lp, HY_BLOCK), min(ls, HY_BLOCK)
    consts_p = (gq, gk) + _dft_tables(blk_p) + (_filter_embedding(lp),)
    consts_s = (gq, gk) + _dft_tables(blk_s) + (_filter_embedding(ls),)
    ctx_k = cache_k.reshape(bs, DEPTH, lc, KV_WIDTH)
    ctx_v = cache_v.reshape(bs, DEPTH, lc, KV_WIDTH)

    flat = _flat_weights(ffn_wg, ffn_wu, ffn_wd, w_in, w_out)
    wts = _cast_layer(flat, 0)
    yp = x_prompt.reshape(bp * lp, D_MODEL)
    ys = x_sample.reshape(bs * ls, D_MODEL)
    p = _layer_params(norm_w, pool_w, pool_scale, q_norm, k_norm, attn_sink, hy_short_w, hy_short_b, hy_f_w1, hy_f_b1,
                      hy_f_w2, hy_f_b2, hy_f_w3, hy_f_b3, hy_sin_freq, hy_decay, hy_bias)
    ks, vs = [], []
    for l in range(DEPTH):
        yp, k_l, v_l, _ = _stream_layer(yp, p, wts, mod, bp, lp, None, None, l, blk_p, consts_p)
        ks.append(k_l.reshape(bp, lp, N_KV_HEADS, HEAD_DIM))
        vs.append(v_l.reshape(bp, lp, N_KV_HEADS, HEAD_DIM))
        cast_next = (flat, l + 1) if l + 1 < DEPTH else None
        ys, _, _, wts = _stream_layer(ys, p, wts, mod, bs, ls, rope_tabs, (ctx_k, ctx_v), l, blk_s, consts_s, cast_next)
    return (yp.reshape(bp, lp, D_MODEL), ys.reshape(bs, ls, D_MODEL),
            jnp.stack(ks, axis=1), jnp.stack(vs, axis=1))
```

```python
import functools
import math

import jax
import jax.numpy as jnp
from jax import lax
from jax.experimental import pallas as pl
from jax.experimental.pallas import tpu as pltpu

F32 = jnp.float32
BF16 = jnp.bfloat16

D_MODEL = 1024
DEPTH = 2
GRID_W = 64
POOL_WINDOWS = (2, 4, 8, 16)
POOL_WIDTH = 256
POOL_GROUP = 64
HEAD_DIM = 64
N_HEADS = 8
N_KV_HEADS = 2
GQA_GROUP = 4
ATTN_WIDTH = 512
KV_WIDTH = 128
WINDOW = 128
BLOCK = 128
ROPE_THETA = 10000.0
HY_WIDTH = 256
HY_ORDER = 2
HY_EMB_DIM = 33
HY_FILTER_HIDDEN = 64
HY_MOD_SHIFT = 0.05
D_FF = 2816
IN_WIDTH = 1792
N_MOD = 9
NORM_EPS = 1e-6
NEG_INF = -1e30

LANES = 128
SUBLANES = 8
VMEM_LIMIT = 56 * 1024 * 1024

TOKEN_TILE = 512
TOKEN_TILE_B = 1024
MXU_DIM = 256
FF_CHUNKS = (1280, 1536)
assert sum(FF_CHUNKS) == D_FF and all(w % MXU_DIM == 0 for w in FF_CHUNKS)
SEQ_CHUNK = 512
SEQ_STEP_ROWS = 2048
HY_BLOCK = 1024
FREQ_TILE = 128
FREQ_SUB = 2
MAC_ROWS = 32
ATTN_SUB = 8
CTX_SEQS = 4
COND_ROWS = 8


def _cparams(sem):
    return pltpu.CompilerParams(dimension_semantics=sem, vmem_limit_bytes=VMEM_LIMIT)


def _dot(a, b):
    return jnp.dot(a, b, preferred_element_type=F32)


def _layer_block(tail, layer):
    return pl.BlockSpec((None,) + tuple(tail), lambda *_: (layer,) + (0,) * len(tail))


def _cast_kernel(*refs):
    n = len(refs) // 2
    for i_ref, o_ref in zip(refs[:n], refs[n:]):
        o_ref[...] = i_ref[...].astype(o_ref.dtype)


def _flat_weights(ffn_wg, ffn_wu, ffn_wd, w_in, w_out):
    arrs = {"wg": ffn_wg, "wu": ffn_wu, "wd": ffn_wd, "w_in": w_in, "w_out": w_out}
    return {k: a.reshape(-1, a.shape[-1]) for k, a in arrs.items()}


def _layer_cast_specs(flat, layer, steps):
    in_specs, out_specs, out_shapes = [], [], []
    for a in flat.values():
        rows = a.shape[0] // DEPTH
        blk = rows // steps
        in_specs.append(pl.BlockSpec((blk, a.shape[1]), lambda i, first=layer * steps: (first + i, 0)))
        out_specs.append(pl.BlockSpec((blk, a.shape[1]), lambda i: (i, 0)))
        out_shapes.append(jax.ShapeDtypeStruct((rows, a.shape[1]), BF16))
    return in_specs, out_specs, out_shapes


def _layer_weights(outs):
    wg, wu, wd, w_in, w_out = outs
    return {"wg": wg.reshape(2, D_MODEL, D_FF), "wu": wu.reshape(2, D_MODEL, D_FF),
            "wd": wd.reshape(2, D_FF, D_MODEL), "w_in": w_in, "w_out": w_out}


def _cast_layer(flat, layer):
    in_specs, out_specs, out_shapes = _layer_cast_specs(flat, layer, steps=16)
    outs = pl.pallas_call(
        _cast_kernel,
        grid=(16,),
        in_specs=in_specs,
        out_specs=out_specs,
        out_shape=out_shapes,
        compiler_params=_cparams(("arbitrary",)),
        name="cast_weights",
    )(*flat.values())
    return _layer_weights(outs)


def _ada_kernel(c_ref, w_ref, b_ref, o_ref):
    c = c_ref[...]
    s = (c * jax.nn.sigmoid(c)).astype(BF16)
    o_ref[...] = _dot(s, w_ref[...].astype(BF16)) + b_ref[...]


def _ada_mod(cond, ada_w, ada_b):
    tn = 3072
    nw = N_MOD * D_MODEL
    return pl.pallas_call(
        _ada_kernel,
        grid=(DEPTH, nw // tn),
        in_specs=[
            pl.BlockSpec((COND_ROWS, D_MODEL), lambda l, j: (0, 0)),
            pl.BlockSpec((None, D_MODEL, tn), lambda l, j: (l, 0, j)),
            pl.BlockSpec((None, 1, tn), lambda l, j: (l, 0, j)),
        ],
        out_specs=pl.BlockSpec((None, COND_ROWS, tn), lambda l, j: (l, 0, j)),
        out_shape=jax.ShapeDtypeStruct((DEPTH, COND_ROWS, nw), F32),
        compiler_params=_cparams(("arbitrary", "arbitrary")),
        name="ada_mod",
    )(cond, ada_w, ada_b.reshape(DEPTH, 1, nw))


def _mod_slice(mod_ref, k):
    return mod_ref[:, k * D_MODEL:(k + 1) * D_MODEL]


def _rms_mod(x, gain, scale, shift):
    y = x * lax.rsqrt(jnp.mean(x * x, axis=-1, keepdims=True) + NORM_EPS)
    return (y * gain) * (1.0 + scale) + shift


def _swiglu(hb, wg_ref, wu_ref, wd_ref):
    acc = None
    lo = 0
    for width in FF_CHUNKS:
        sl = slice(lo, lo + width)
        lo += width
        g = _dot(hb, wg_ref[:, sl])
        u = _dot(hb, wu_ref[:, sl])
        a = ((g * jax.nn.sigmoid(g)) * u).astype(BF16)
        y = _dot(a, wd_ref[sl, :])
        acc = y if acc is None else acc + y
    return acc


def _head_norm(x, gmat, gain):
    ss = _dot((x * x).astype(BF16), gmat)
    return (x * lax.rsqrt(ss * (1.0 / HEAD_DIM) + NORM_EPS)) * gain


def _rope(x, cos, sa, sb):
    w = x.shape[1]
    xn = pltpu.roll(x, w - 16, axis=1)
    xp = pltpu.roll(x, 16, axis=1)
    return x * cos + xn * sa + xp * sb


def _token_a_kernel(*refs, rope, n_cast):
    n_in = 11 + (3 if rope else 0) + n_cast
    ins, outs = refs[:n_in], refs[n_in:]
    x_ref, mod_ref, nrm_ref, wg_ref, wu_ref, wd_ref, win_ref, gq_ref, gk_ref, qg_ref, kg_ref = ins[:11]
    if rope:
        cos_ref, sa_ref, sb_ref = ins[11:14]
    x1_ref, up_ref, q_ref, kd_ref, vd_ref, hv_ref, hx1_ref, hx2_ref = outs[:8]
    if not rope:
        k_ref, v_ref = outs[8:10]
    for i_ref, o_ref in zip(ins[n_in - n_cast:], outs[len(outs) - n_cast:]):
        o_ref[...] = i_ref[...].astype(o_ref.dtype)
    x = x_ref[...]
    sh1, sc1, g1 = _mod_slice(mod_ref, 0), _mod_slice(mod_ref, 1), _mod_slice(mod_ref, 2)
    sh2, sc2 = _mod_slice(mod_ref, 3), _mod_slice(mod_ref, 4)
    h = _rms_mod(x, nrm_ref[0:1, :], sc1, sh1).astype(BF16)
    x1 = x + (0.5 * g1) * _swiglu(h, wg_ref, wu_ref, wd_ref)
    x1_ref[...] = x1
    h2 = _rms_mod(x1, nrm_ref[1:2, :], sc2, sh2).astype(BF16)
    s1 = POOL_WIDTH
    s2 = s1 + ATTN_WIDTH
    s3 = s2 + KV_WIDTH
    s4 = s3 + KV_WIDTH
    qkv = _dot(h2, win_ref[:, s1:s4])
    q = _head_norm(qkv[:, :ATTN_WIDTH], gq_ref[...], qg_ref[...])
    k = _head_norm(qkv[:, ATTN_WIDTH:ATTN_WIDTH + KV_WIDTH], gk_ref[...], kg_ref[...])
    v = qkv[:, ATTN_WIDTH + KV_WIDTH:]
    up_ref[...] = _dot(h2, win_ref[:, :s1])
    hy = _dot(h2, win_ref[:, s4:])
    for j, ref in enumerate((hv_ref, hx1_ref, hx2_ref)):
        ref[...] = hy[:, j * HY_WIDTH:(j + 1) * HY_WIDTH]
    if rope:
        cos, sa, sb = cos_ref[...], sa_ref[...], sb_ref[...]
        reps = ATTN_WIDTH // LANES
        q = _rope(q, jnp.concatenate([cos] * reps, axis=1), jnp.concatenate([sa] * reps, axis=1),
                  jnp.concatenate([sb] * reps, axis=1))
        k = _rope(k, cos, sa, sb)
    q_ref[...] = (q * (HEAD_DIM ** -0.5 * math.log2(math.e))).astype(BF16)
    kd_ref[...] = _dup_heads(k)
    vd_ref[...] = _dup_heads(v)
    if not rope:
        k_ref[...] = k
        v_ref[...] = v


def _token_a(x, mod, nrm, wts, gq, gk, qg, kg, rope_tabs, seq_len, layer, cast_next=None):
    n = x.shape[0]
    tm = TOKEN_TILE
    tiles_per_seq = max(seq_len // tm, 1)
    rope = rope_tabs is not None
    if rope:
        mod_row = lambda i: (layer, 1 + i // tiles_per_seq, 0, 0)
    else:
        mod_row = lambda i: (layer, 0, 0, 0)
    const = lambda i: (0, 0)
    row = lambda i: (i, 0)
    in_specs = [
        pl.BlockSpec((tm, D_MODEL), row),
        pl.BlockSpec((None, None, 1, N_MOD * D_MODEL), mod_row),
        _layer_block((3, D_MODEL), layer),
        pl.BlockSpec((None, D_MODEL, D_FF), lambda i: (0, 0, 0)),
        pl.BlockSpec((None, D_MODEL, D_FF), lambda i: (0, 0, 0)),
        pl.BlockSpec((None, D_FF, D_MODEL), lambda i: (0, 0, 0)),
        pl.BlockSpec((D_MODEL, IN_WIDTH), const),
        pl.BlockSpec((ATTN_WIDTH, ATTN_WIDTH), const),
        pl.BlockSpec((KV_WIDTH, KV_WIDTH), const),
        _layer_block((1, ATTN_WIDTH), layer),
        _layer_block((1, KV_WIDTH), layer),
    ]
    args = [x, mod, nrm, wts["wg"], wts["wu"], wts["wd"], wts["w_in"], gq, gk, qg, kg]
    if rope:
        tab = pl.BlockSpec((tm, LANES), lambda i: (i % tiles_per_seq, 0))
        in_specs += [tab, tab, tab]
        args += list(rope_tabs)
    widths = (D_MODEL, POOL_WIDTH, ATTN_WIDTH, 2 * KV_WIDTH, 2 * KV_WIDTH) + (HY_WIDTH,) * (HY_ORDER + 1)
    dtypes = (F32, F32, BF16, BF16, BF16) + (F32,) * (HY_ORDER + 1)
    if not rope:
        widths += (KV_WIDTH, KV_WIDTH)
        dtypes += (F32, F32)
    out_specs = [pl.BlockSpec((tm, w), row) for w in widths]
    out_shapes = [jax.ShapeDtypeStruct((n, w), d) for w, d in zip(widths, dtypes)]
    n_cast = 0
    if cast_next is not None:
        flat, next_layer = cast_next
        c_in, c_out, c_shapes = _layer_cast_specs(flat, next_layer, steps=n // tm)
        in_specs += c_in
        args += list(flat.values())
        out_specs += c_out
        out_shapes += c_shapes
        n_cast = len(c_in)
    outs = pl.pallas_call(
        functools.partial(_token_a_kernel, rope=rope, n_cast=n_cast),
        grid=(n // tm,),
        in_specs=in_specs,
        out_specs=out_specs,
        out_shape=out_shapes,
        compiler_params=_cparams(("arbitrary",)),
        name="token_a_rope" if rope else "token_a",
    )(*args)
    if n_cast:
        return outs[:len(outs) - n_cast], _layer_weights(outs[len(outs) - n_cast:])
    return outs, None


def _token_b_kernel(x_ref, yp_ref, a_ref, yh_ref, mod_ref, nrm_ref, wo_ref, wg_ref, wu_ref, wd_ref, o_ref):
    x1 = x_ref[...]
    g2 = _mod_slice(mod_ref, 5)
    sh3, sc3, g3 = _mod_slice(mod_ref, 6), _mod_slice(mod_ref, 7), _mod_slice(mod_ref, 8)
    cat = jnp.concatenate([yp_ref[...], a_ref[...], yh_ref[...]], axis=1)
    x2 = x1 + g2 * _dot(cat, wo_ref[...])
    h3 = _rms_mod(x2, nrm_ref[2:3, :], sc3, sh3).astype(BF16)
    o_ref[...] = x2 + (0.5 * g3) * _swiglu(h3, wg_ref, wu_ref, wd_ref)


def _token_b(x1, y_pool, a, y_hy, mod, nrm, wts, seq_len, layer, per_seq_cond):
    n = x1.shape[0]
    tm = min(TOKEN_TILE_B, n)
    tiles_per_seq = max(seq_len // tm, 1)
    if per_seq_cond:
        mod_row = lambda i: (layer, 1 + i // tiles_per_seq, 0, 0)
    else:
        mod_row = lambda i: (layer, 0, 0, 0)
    const = lambda i: (0, 0)
    row = lambda i: (i, 0)
    return pl.pallas_call(
        _token_b_kernel,
        grid=(n // tm,),
        in_specs=[
            pl.BlockSpec((tm, D_MODEL), row),
            pl.BlockSpec((tm, POOL_WIDTH), row),
            pl.BlockSpec((tm, ATTN_WIDTH), row),
            pl.BlockSpec((tm, HY_WIDTH), row),
            pl.BlockSpec((None, None, 1, N_MOD * D_MODEL), mod_row),
            _layer_block((3, D_MODEL), layer),
            pl.BlockSpec((D_MODEL, D_MODEL), const),
            pl.BlockSpec((None, D_MODEL, D_FF), lambda i: (1, 0, 0)),
            pl.BlockSpec((None, D_MODEL, D_FF), lambda i: (1, 0, 0)),
            pl.BlockSpec((None, D_FF, D_MODEL), lambda i: (1, 0, 0)),
        ],
        out_specs=pl.BlockSpec((tm, D_MODEL), row),
        out_shape=jax.ShapeDtypeStruct((n, D_MODEL), F32),
        compiler_params=_cparams(("arbitrary",)),
        name="token_b",
    )(x1, y_pool, a, y_hy, mod, nrm, wts["w_out"], wts["wg"], wts["wu"], wts["wd"])


def _seqs_per_step(batch, seq_len):
    return min(batch, max(1, SEQ_STEP_ROWS // seq_len))


def _halo_rows(src_ref, base, r0, rows, seq_len):
    c = src_ref.shape[1]
    zero = jnp.zeros((SUBLANES, c), F32)
    lo = base + r0
    prev = src_ref[lo - SUBLANES:lo, :] if r0 > 0 else zero
    nxt = src_ref[lo + rows:lo + rows + SUBLANES, :] if r0 + rows < seq_len else zero
    return prev, nxt


def _short_conv_chunk(src_ref, base, r0, rows, seq_len, w, b):
    x = src_ref[base + r0:base + r0 + rows, :]
    prev, nxt = _halo_rows(src_ref, base, r0, rows, seq_len)
    ridx = lax.broadcasted_iota(jnp.int32, x.shape, 0)
    xp = jnp.where(ridx == 0, prev[SUBLANES - 1:SUBLANES, :], pltpu.roll(x, 1, axis=0))
    xn = jnp.where(ridx == rows - 1, nxt[0:1, :], pltpu.roll(x, rows - 1, axis=0))
    return xp * w[0:1, :] + x * w[1:2, :] + xn * w[2:3, :] + b


def _pool_kernel(u_ref, w_ref, scale_ref, o_ref, *, seq_len, rows, seqs):
    lane = lax.broadcasted_iota(jnp.int32, (rows, POOL_WIDTH), 1)
    grp = lane // POOL_GROUP
    half = jnp.where(grp == 0, 1, jnp.where(grp == 1, 2, jnp.where(grp == 2, 4, 8)))
    odd_grp = lax.broadcasted_iota(jnp.int32, (rows, LANES), 1) >= POOL_GROUP
    ext = rows + 2 * SUBLANES
    back = lambda v, s: pltpu.roll(v, s, axis=0)
    fwd = lambda v, s: pltpu.roll(v, ext - s, axis=0)
    core = lambda v: v[SUBLANES:SUBLANES + rows, :]
    for base, r0 in [(sq * seq_len, r0) for sq in range(seqs) for r0 in range(0, seq_len, rows)]:
        x = u_ref[base + r0:base + r0 + rows, :]
        prev, nxt = _halo_rows(u_ref, base, r0, rows, seq_len)
        a = jnp.concatenate([prev, x, nxt], axis=0)
        lo, hi = a[:, :LANES], a[:, LANES:]
        b1 = back(lo, 1)
        b2 = b1 + back(b1, 1)
        f2 = lo + fwd(lo, 1)
        sum_lo = jnp.where(odd_grp, core(b2) + core(f2), core(b1) + core(lo))
        c1 = back(hi, 1)
        c2 = c1 + back(c1, 1)
        c4 = c2 + back(c2, 2)
        c8 = c4 + back(c4, 4)
        g2 = hi + fwd(hi, 1)
        g4 = g2 + fwd(g2, 2)
        g8 = g4 + fwd(g4, 4)
        sum_hi = jnp.where(odd_grp, core(c8) + core(g8), core(c4) + core(g4))
        wsum = jnp.concatenate([sum_lo, sum_hi], axis=1)
        t = r0 + lax.broadcasted_iota(jnp.int32, (rows, POOL_WIDTH), 0)
        cnt = jnp.minimum(t + half, seq_len) - jnp.maximum(t - half, 0)
        d = wsum / cnt.astype(F32) - x
        y = _dot(d.astype(BF16), w_ref[...]) * scale_ref[...]
        o_ref[base + r0:base + r0 + rows, :] = y.astype(o_ref.dtype)


def _pool_mix(u_pool, w_bd, scale, batch, seq_len, layer):
    rows = min(SEQ_CHUNK // 2, seq_len)
    seqs = _seqs_per_step(batch, seq_len)
    return pl.pallas_call(
        functools.partial(_pool_kernel, seq_len=seq_len, rows=rows, seqs=seqs),
        grid=(batch // seqs,),
        in_specs=[
            pl.BlockSpec((seqs * seq_len, POOL_WIDTH), lambda b: (b, 0)),
            _layer_block((POOL_WIDTH, POOL_WIDTH), layer),
            _layer_block((1, POOL_WIDTH), layer),
        ],
        out_specs=pl.BlockSpec((seqs * seq_len, POOL_WIDTH), lambda b: (b, 0)),
        out_shape=jax.ShapeDtypeStruct((batch * seq_len, POOL_WIDTH), BF16),
        compiler_params=_cparams(("arbitrary",)),
        name="pool_mix",
    )(u_pool, w_bd, scale)


def _dup_heads(x):
    lane = lax.broadcasted_iota(jnp.int32, x.shape, 1)
    sw = pltpu.roll(x, HEAD_DIM, axis=1)
    lo = lane < HEAD_DIM
    return jnp.concatenate([jnp.where(lo, x, sw), jnp.where(lo, sw, x)], axis=1).astype(BF16)


def _attn_kernel(*refs, has_local, nblocks, sub, seqs, layer):
    if has_local:
        sink_ref, q_ref, kp_ref, kc_ref, kn_ref, vp_ref, vc_ref, vn_ref, ck_ref, cv_ref, o_ref = refs
        kwin = jnp.concatenate([kp_ref[...], kc_ref[...], kn_ref[...]], axis=0)
        vwin = jnp.concatenate([vp_ref[...], vc_ref[...], vn_ref[...]], axis=0)
        kctx, vctx = _dup_heads(ck_ref[...]), _dup_heads(cv_ref[...])
        r = lax.broadcasted_iota(jnp.int32, (GQA_GROUP * BLOCK, BLOCK), 0) % BLOCK
        j = lax.broadcasted_iota(jnp.int32, (GQA_GROUP * BLOCK, BLOCK), 1)
    else:
        sink_ref, q_ref, kd_ref, vd_ref, o_ref = refs
    i = pl.program_id(1)
    lane_q = lax.broadcasted_iota(jnp.int32, (BLOCK, LANES), 1)
    log2e = math.log2(math.e)
    units = [(sb, kvh) for sb in range(seqs * sub) for kvh in range(N_KV_HEADS)]
    scores, values = [], []
    for sb, kvh in units:
        q = q_ref[sb * BLOCK:(sb + 1) * BLOCK, :]
        qparts = []
        for hd in range(kvh * GQA_GROUP, (kvh + 1) * GQA_GROUP):
            qp = q[:, (hd // 2) * LANES:(hd // 2 + 1) * LANES]
            keep = (lane_q < HEAD_DIM) == (hd % 2 == 0)
            qparts.append(jnp.where(keep, qp, jnp.zeros_like(qp)))
        qs = jnp.concatenate(qparts, axis=0)
        cols = slice(kvh * LANES, (kvh + 1) * LANES)
        if has_local:
            loc = slice(sb * BLOCK, (sb + 3) * BLOCK)
            kk = jnp.concatenate([kwin[loc, cols], kctx[:, cols]], axis=0)
            vv = jnp.concatenate([vwin[loc, cols], vctx[:, cols]], axis=0)
        else:
            own = slice((sb // sub) * sub * BLOCK, (sb // sub + 1) * sub * BLOCK)
            kk, vv = kd_ref[own, cols], vd_ref[own, cols]
        s = lax.dot_general(qs, kk, (((1,), (1,)), ((), ())), preferred_element_type=F32)
        if has_local:
            gb = i * sub + sb
            below = j >= r + jnp.where(gb >= 1, 0, BLOCK)
            above = j <= r - jnp.where(gb <= nblocks - 2, 0, BLOCK)
            s = jnp.concatenate([jnp.where(below, s[:, :BLOCK], NEG_INF), s[:, BLOCK:2 * BLOCK],
                                 jnp.where(above, s[:, 2 * BLOCK:3 * BLOCK], NEG_INF), s[:, 3 * BLOCK:]], axis=1)
        scores.append(s)
        values.append(jnp.concatenate([vv, jnp.ones_like(vv)], axis=1))

    def per_head(kvh, col, f):
        parts = [f(col[g * BLOCK:(g + 1) * BLOCK, :], sink_ref[layer, kvh * GQA_GROUP + g] * log2e) for g in range(GQA_GROUP)]
        return jnp.concatenate(parts, axis=0)

    maxes = [per_head(kvh, jnp.max(s, axis=1, keepdims=True), jnp.maximum) for (_, kvh), s in zip(units, scores)]
    probs = [jnp.exp2(s - m).astype(BF16) for s, m in zip(scores, maxes)]
    sums = [_dot(e, v) for e, v in zip(probs, values)]
    outs = []
    for (_, kvh), ow, m in zip(units, sums, maxes):
        den = ow[:, LANES:] + per_head(kvh, m, lambda mm, sk: jnp.exp2(sk - mm))
        outs.append(ow[:, :LANES] / den)
    for sb in range(seqs * sub):
        heads = [outs[sb * N_KV_HEADS + kvh][g * BLOCK:(g + 1) * BLOCK, :]
                 for kvh in range(N_KV_HEADS) for g in range(GQA_GROUP)]
        blks = [jnp.where(lane_q < HEAD_DIM, heads[2 * p], heads[2 * p + 1]) for p in range(N_HEADS // 2)]
        o_ref[sb * BLOCK:(sb + 1) * BLOCK, :] = jnp.concatenate(blks, axis=1).astype(o_ref.dtype)


def _attention(q, kd, vd, sink, batch, seq_len, ctx_k=None, ctx_v=None, layer=0):
    assert WINDOW == BLOCK
    nb = seq_len // BLOCK
    has_local = ctx_k is not None
    sub = min(ATTN_SUB, nb)
    steps = nb // sub
    seqs = 1
    if not has_local:
        assert steps == 1
        seqs = math.gcd(CTX_SEQS, batch)
        batch //= seqs
    qb = seqs * sub * BLOCK
    qspec = pl.BlockSpec((qb, ATTN_WIDTH), lambda b, i: (b * steps + i, 0))
    sspec = pl.BlockSpec(memory_space=pltpu.SMEM)
    if has_local:
        lc = ctx_k.shape[2]
        edge = lambda f: pl.BlockSpec((BLOCK, 2 * KV_WIDTH), f)
        prev = lambda b, i: (b * nb + jnp.maximum(i * sub - 1, 0), 0)
        nxt = lambda b, i: (b * nb + jnp.minimum((i + 1) * sub, nb - 1), 0)
        cur = pl.BlockSpec((qb, 2 * KV_WIDTH), lambda b, i: (b * steps + i, 0))
        cspec = pl.BlockSpec((None, None, lc, KV_WIDTH), lambda b, i: (b, layer, 0, 0))
        in_specs = [sspec, qspec, edge(prev), cur, edge(nxt), edge(prev), cur, edge(nxt), cspec, cspec]
        args = [sink, q, kd, kd, kd, vd, vd, vd, ctx_k, ctx_v]
    else:
        kv = pl.BlockSpec((qb, 2 * KV_WIDTH), lambda b, i: (b, 0))
        in_specs = [sspec, qspec, kv, kv]
        args = [sink, q, kd, vd]
    return pl.pallas_call(
        functools.partial(_attn_kernel, has_local=has_local, nblocks=nb, sub=sub, seqs=seqs, layer=layer),
        grid=(batch, steps),
        in_specs=in_specs,
        out_specs=pl.BlockSpec((qb, ATTN_WIDTH), lambda b, i: (b * steps + i, 0)),
        out_shape=jax.ShapeDtypeStruct(q.shape, BF16),
        compiler_params=_cparams(("arbitrary", "arbitrary")),
        name="attn_latent" if has_local else "attn_context",
    )(*args)


def _filter_kernel(zt_ref, zb_ref, w1_ref, b1_ref, w2_ref, b2_ref, w3_ref, b3_ref, fr_ref, dl_ref, k_ref, sum_ref,
                   *, seq_len, rows):
    i = pl.program_id(0)
    half = rows // 2
    oc = HY_ORDER * HY_WIDTH
    z = jnp.concatenate([zt_ref[...], zb_ref[...]], axis=1)
    h = jnp.sin(fr_ref[0:1, :] * (_dot(z.astype(BF16), w1_ref[...].astype(BF16)) + b1_ref[...]))
    h = jnp.sin(fr_ref[1:2, :] * (_dot(h.astype(BF16), w2_ref[...].astype(BF16)) + b2_ref[...]))
    h3 = _dot(h.astype(BF16), w3_ref[...].astype(BF16)) + b3_ref[...]
    total = jnp.zeros((1, oc), F32)
    for part, z_ref in enumerate((zt_ref, zb_ref)):
        row = i * rows + part * half + lax.broadcasted_iota(jnp.int32, (half, oc), 0)
        t = z_ref[:, 0:1]
        decay = jnp.exp(-t * jnp.abs(dl_ref[...]))
        fwd_dir = h3[:, part * 2 * oc:part * 2 * oc + oc]
        bwd_dir = h3[:, part * 2 * oc + oc:(part + 1) * 2 * oc]
        sel = jnp.where(row < seq_len, fwd_dir, jnp.where(row > seq_len, bwd_dir, 0.0))
        k = sel * (decay + HY_MOD_SHIFT)
        k_ref[part * half:(part + 1) * half, :] = k
        total = total + jnp.sum(jnp.abs(k), axis=0, keepdims=True)

    @pl.when(i == 0)
    def _():
        sum_ref[...] = jnp.zeros_like(sum_ref)

    sum_ref[...] += total


def _block_diag2(w):
    z = jnp.zeros_like(w)
    return jnp.concatenate([jnp.concatenate([w, z], axis=-1), jnp.concatenate([z, w], axis=-1)], axis=-2)


def _filter_weights(w1, b1, w2, b2, w3, b3, freq, deltas):
    two = lambda a: jnp.concatenate([a, a], axis=-1)
    w1p = jnp.pad(w1, ((0, 0), (0, LANES - HY_EMB_DIM), (0, 0)))
    return (_block_diag2(w1p), two(b1)[:, None, :], _block_diag2(w2), two(b2)[:, None, :],
            _block_diag2(w3), two(b3)[:, None, :], two(freq), deltas.reshape(DEPTH, 1, HY_ORDER * HY_WIDTH))


def _hyena_filter(z_ext, fw, seq_len, layer):
    n = 2 * seq_len
    rows = min(1024, n)
    half = rows // 2
    oc = HY_ORDER * HY_WIDTH
    hid = 2 * HY_FILTER_HIDDEN
    const = lambda i: (0, 0)
    return pl.pallas_call(
        functools.partial(_filter_kernel, seq_len=seq_len, rows=rows),
        grid=(n // rows,),
        in_specs=[
            pl.BlockSpec((half, LANES), lambda i: (2 * i, 0)),
            pl.BlockSpec((half, LANES), lambda i: (2 * i + 1, 0)),
            _layer_block((2 * LANES, hid), layer),
            _layer_block((1, hid), layer),
            _layer_block((hid, hid), layer),
            _layer_block((1, hid), layer),
            _layer_block((hid, 4 * oc), layer),
            _layer_block((1, 4 * oc), layer),
            _layer_block((2, hid), layer),
            _layer_block((1, oc), layer),
        ],
        out_specs=[pl.BlockSpec((rows, oc), lambda i: (i, 0)), pl.BlockSpec((1, oc), const)],
        out_shape=[jax.ShapeDtypeStruct((n, oc), F32), jax.ShapeDtypeStruct((1, oc), F32)],
        compiler_params=_cparams(("arbitrary",)),
        name="hyena_filter",
    )(z_ext, z_ext, *fw)


def _spectrum_kernel(k_ref, sum_ref, fwd_ref, kf_ref, hprev_ref, g0_ref, *, blk):
    t = pl.program_id(0)
    a = k_ref[...] / (sum_ref[...] + 1e-6)
    ha = _dot(fwd_ref[...], a.astype(BF16))
    tf = FREQ_TILE
    par = lax.broadcasted_iota(jnp.int32, (tf, 1), 0) % 2
    sgn = (1 - 2 * par).astype(F32)

    @pl.when(t > 0)
    def _():
        g0 = g0_ref[...]
        for c in range(blk // tf):
            re = slice(2 * c * tf, (2 * c + 1) * tf)
            im = slice((2 * c + 1) * tf, (2 * c + 2) * tf)
            kre = ha[re, :] - sgn * hprev_ref[im, :]
            kim = ha[im, :] + sgn * (hprev_ref[re, :] - g0)
            for o in range(HY_ORDER):
                lanes = slice(o * HY_WIDTH, (o + 1) * HY_WIDTH)
                kf_ref[o, c, 0:tf, :] = kre[:, lanes]
                kf_ref[o, c, tf:2 * tf, :] = kim[:, lanes]

    hprev_ref[...] = ha
    g0_ref[...] = a[0:1, :]


def _filter_spectrum(kraw, ksum, fwd, seq_len, blk):
    nb = seq_len // blk
    nlags = 2 * nb - 1
    oc = HY_ORDER * HY_WIDTH
    m = 2 * blk
    tf = FREQ_TILE
    nfc = blk // tf
    return pl.pallas_call(
        functools.partial(_spectrum_kernel, blk=blk),
        grid=(2 * nb,),
        in_specs=[
            pl.BlockSpec((blk, oc), lambda t: ((nb + t) % (2 * nb), 0)),
            pl.BlockSpec((1, oc), lambda t: (0, 0)),
            pl.BlockSpec((m, blk), lambda t: (0, 0)),
        ],
        out_specs=pl.BlockSpec((HY_ORDER, nfc, None, 2 * tf, HY_WIDTH), lambda t: (0, 0, jnp.maximum(t - 1, 0), 0, 0)),
        out_shape=jax.ShapeDtypeStruct((HY_ORDER, nfc, nlags, 2 * tf, HY_WIDTH), F32),
        scratch_shapes=[pltpu.VMEM((m, oc), F32), pltpu.VMEM((1, oc), F32)],
        compiler_params=_cparams(("arbitrary",)),
        name="filter_spectrum",
    )(kraw, ksum, fwd)


def _conv_kernel(z_ref, g_ref, swz_ref, sbz_ref, swg_ref, sbg_ref, bd_ref, kf_ref, fwd_ref, inv_ref, o_ref,
                 zb_ref, zc_ref, yf_ref, *, seq_len, blk, conv_z, seqs, sub):
    fc = pl.program_id(1)
    nfc = pl.num_programs(1)
    nb = seq_len // blk
    tf = FREQ_TILE
    blocks = [((g // nb) * seq_len, (g % nb) * blk) for g in range(seqs * nb)]

    @pl.when(fc == 0)
    def _():
        for base, r0 in blocks:
            rows = slice(base + r0, base + r0 + blk)
            if conv_z:
                zc_ref[rows, :] = _short_conv_chunk(z_ref, base, r0, blk, seq_len, swz_ref[...], sbz_ref[...])
                zb_ref[rows, :] = zc_ref[rows, :].astype(BF16)
            else:
                zb_ref[rows, :] = z_ref[rows, :].astype(BF16)

    zf = [[_dot(fwd_ref[s * 2 * tf:(s + 1) * 2 * tf, :], zb_ref[g * blk:(g + 1) * blk, :]) for g in range(seqs * nb)]
          for s in range(sub)]

    for s in range(sub):
        col = pl.multiple_of((fc * sub + s) * 2 * tf, 2 * tf)
        for g0 in range(0, seqs * nb, nb):
            for bi in range(nb):
                for r in range(0, tf, MAC_ROWS):
                    re = slice(r, r + MAC_ROWS)
                    im = slice(tf + r, tf + r + MAC_ROWS)
                    yr = jnp.zeros((MAC_ROWS, HY_WIDTH), F32)
                    yi = jnp.zeros((MAC_ROWS, HY_WIDTH), F32)
                    for bj in range(nb):
                        lag = bi - bj + nb - 1
                        kr, ki = kf_ref[s, lag, re, :], kf_ref[s, lag, im, :]
                        zr, zi = zf[s][g0 + bj][re, :], zf[s][g0 + bj][im, :]
                        yr = yr + (kr * zr - ki * zi)
                        yi = yi + (kr * zi + ki * zr)
                    yf_ref[g0 + bi, pl.ds(col + r, MAC_ROWS), :] = yr.astype(BF16)
                    yf_ref[g0 + bi, pl.ds(col + tf + r, MAC_ROWS), :] = yi.astype(BF16)

    @pl.when(fc == nfc - 1)
    def _():
        ys, gates = [], []
        for g, (base, r0) in enumerate(blocks):
            ys.append(_dot(inv_ref[...], yf_ref[g]))
            gates.append(_short_conv_chunk(g_ref, base, r0, blk, seq_len, swg_ref[...], sbg_ref[...]))
        for (base, r0), y, gate in zip(blocks, ys, gates):
            rows = slice(base + r0, base + r0 + blk)
            z = zc_ref[rows, :] if conv_z else z_ref[rows, :]
            o_ref[rows, :] = (gate * (y + bd_ref[...] * z)).astype(o_ref.dtype)


def _hyena_conv(z_src, z_col, conv_z, g_src, g_col, short_w, short_b, bd, kf, order, fwd, inv,
                batch, seq_len, blk, out_dtype, layer):
    nb = seq_len // blk
    nlags = 2 * nb - 1
    tf = FREQ_TILE
    sub = min(FREQ_SUB, blk // tf)
    nfc = blk // (sub * tf)
    w = HY_WIDTH
    seqs = _seqs_per_step(batch, seq_len)
    step_rows = seqs * seq_len
    return pl.pallas_call(
        functools.partial(_conv_kernel, seq_len=seq_len, blk=blk, conv_z=conv_z, seqs=seqs, sub=sub),
        grid=(batch // seqs, nfc),
        in_specs=[
            pl.BlockSpec((step_rows, w), lambda b, f: (b, 0)),
            pl.BlockSpec((step_rows, w), lambda b, f: (b, 0)),
            pl.BlockSpec((None, 3, w), lambda b, f: (layer, 0, z_col if conv_z else 0)),
            pl.BlockSpec((None, 1, w), lambda b, f: (layer, 0, z_col if conv_z else 0)),
            pl.BlockSpec((None, 3, w), lambda b, f: (layer, 0, g_col)),
            pl.BlockSpec((None, 1, w), lambda b, f: (layer, 0, g_col)),
            pl.BlockSpec((None, 1, w), lambda b, f: (layer * HY_ORDER + order, 0, 0)),
            pl.BlockSpec((None, sub, nlags, 2 * tf, w), lambda b, f: (order, f, 0, 0, 0)),
            pl.BlockSpec((sub * 2 * tf, blk), lambda b, f: (f, 0)),
            pl.BlockSpec((blk, 2 * blk), lambda b, f: (0, 0)),
        ],
        out_specs=pl.BlockSpec((step_rows, w), lambda b, f: (b, 0)),
        out_shape=jax.ShapeDtypeStruct((batch * seq_len, w), out_dtype),
        scratch_shapes=[
            pltpu.VMEM((step_rows, w), BF16),
            pltpu.VMEM((step_rows if conv_z else SUBLANES, w), F32),
            pltpu.VMEM((seqs * nb, 2 * blk, w), BF16),
        ],
        compiler_params=_cparams(("arbitrary", "arbitrary")),
        name="hyena_conv",
    )(z_src, g_src, short_w, short_b, short_w, short_b, bd, kf, fwd, inv)


def _dft_tables(blk):
    m = 2 * blk
    tf = FREQ_TILE
    nt = blk // tf
    q = 32
    f2 = 2 * jnp.arange(blk, dtype=jnp.int32) + 1

    def cos_sin(times):
        r = (f2[:, None] * times[None, :]) % (2 * m)
        ang = r.astype(F32) * (math.pi / m)
        return jnp.cos(ang), jnp.sin(ang)

    ch, sh = cos_sin(jnp.arange(0, blk, q, dtype=jnp.int32))
    cl, sl = cos_sin(jnp.arange(q, dtype=jnp.int32))
    rep = lambda a: jnp.repeat(a, q, axis=1)
    til = lambda a: jnp.tile(a, (1, blk // q))
    c = rep(ch) * til(cl) - rep(sh) * til(sl)
    s = rep(sh) * til(cl) + rep(ch) * til(sl)
    tiles = [slice(t * tf, (t + 1) * tf) for t in range(nt)]
    fwd = jnp.concatenate([part[t, :] for t in tiles for part in (c, -s)], axis=0).astype(BF16)
    inv = fwd.T * (2.0 / m)
    return fwd, inv.astype(BF16)


def _filter_embedding(seq_len):
    t = jnp.linspace(0.0, 1.0, seq_len, dtype=F32)[:, None]
    bands = (HY_EMB_DIM - 1) // 2
    f = jnp.linspace(1e-4, bands - 1, bands, dtype=F32)[None, :]
    w = 2.0 * math.pi * jnp.arange(seq_len, dtype=F32)[:, None] / seq_len
    z = jnp.concatenate([t, jnp.cos(f * w), -jnp.sin(f * w)], axis=-1)
    z_ext = jnp.concatenate([z, jnp.zeros((1, HY_EMB_DIM), F32), jnp.flip(z[1:], axis=0)], axis=0)
    return jnp.pad(z_ext, ((0, 0), (0, LANES - HY_EMB_DIM)))


def _rope_tables(seq_len):
    quarter = HEAD_DIM // 4
    inv = jnp.tile(ROPE_THETA ** (-jnp.arange(quarter, dtype=F32) / quarter), LANES // quarter)
    lane = jnp.arange(LANES)
    by_row = (lane % HEAD_DIM < HEAD_DIM // 2)[None, None, :]
    first = (lane % (2 * quarter) < quarter)[None, :]

    def cos_sin(npos):
        ang = jnp.arange(npos).astype(F32)[:, None] * inv[None, :]
        return jnp.cos(ang), jnp.sin(ang)

    (cr, sr), (cc, sc) = cos_sin(seq_len // GRID_W), cos_sin(GRID_W)
    lay = lambda r, c: jnp.where(by_row, r[:, None, :], c[None, :, :]).reshape(seq_len, LANES)
    cos, sin = lay(cr, cc), lay(sr, sc)
    return cos, jnp.where(first, -sin, 0.0), jnp.where(first, 0.0, sin)


def _block_ones(width):
    h = jnp.arange(width) // HEAD_DIM
    return (h[:, None] == h[None, :]).astype(BF16)


def _stream_layer(x, p, wts, mod, batch, seq_len, rope_tabs, ctx, layer, hy_blk, consts, cast_next=None):
    gq, gk, fwd, inv, z_ext = consts
    latent = ctx is not None
    outs, next_wts = _token_a(x, mod, p["norm"], wts, gq, gk, p["qg"], p["kg"], rope_tabs, seq_len, layer, cast_next)
    x1, u_pool, q, kd, vd, hy_v, hy_x1, hy_x2 = outs[:8]
    k, v = (None, None) if latent else outs[8:]
    y_pool = _pool_mix(u_pool, p["pool_w"], p["pool_scale"], batch, seq_len, layer)
    if latent:
        a = _attention(q, kd, vd, p["sink"], batch, seq_len, ctx[0], ctx[1], layer)
    else:
        a = _attention(q, kd, vd, p["sink"], batch, seq_len, layer=layer)
    kraw, ksum = _hyena_filter(z_ext, p["filter"], seq_len, layer)
    kf = _filter_spectrum(kraw, ksum, fwd, seq_len, hy_blk)
    z1 = _hyena_conv(hy_v, 0, True, hy_x1, 1, p["short_w"], p["short_b"], p["hy_bias"], kf, 0, fwd, inv,
                     batch, seq_len, hy_blk, F32, layer)
    y_hy = _hyena_conv(z1, 0, False, hy_x2, 2, p["short_w"], p["short_b"], p["hy_bias"], kf, 1, fwd, inv,
                       batch, seq_len, hy_blk, BF16, layer)
    x3 = _token_b(x1, y_pool, a, y_hy, mod, p["norm"], wts, seq_len, layer, latent)
    return x3, k, v, next_wts


def _layer_params(norm_w, pool_w, pool_scale, q_norm, k_norm, attn_sink, hy_short_w, hy_short_b, hy_f_w1, hy_f_b1,
                  hy_f_w2, hy_f_b2, hy_f_w3, hy_f_b3, hy_sin_freq, hy_decay, hy_bias):
    eye = jnp.eye(len(POOL_WINDOWS), dtype=F32)
    pool_bd = (eye[None, :, None, :, None] * pool_w[:, :, :, None, :]).reshape(DEPTH, POOL_WIDTH, POOL_WIDTH)
    return {
        "norm": norm_w,
        "pool_w": pool_bd.astype(BF16), "pool_scale": pool_scale[:, None, :],
        "qg": jnp.tile(q_norm, (1, N_HEADS))[:, None, :], "kg": jnp.tile(k_norm, (1, N_KV_HEADS))[:, None, :],
        "sink": attn_sink,
        "short_w": hy_short_w, "short_b": hy_short_b[:, None, :],
        "filter": _filter_weights(hy_f_w1, hy_f_b1, hy_f_w2, hy_f_b2, hy_f_w3, hy_f_b3, hy_sin_freq, hy_decay),
        "hy_bias": hy_bias.reshape(DEPTH * HY_ORDER, 1, HY_WIDTH),
    }


def kernel(x_prompt, x_sample, cache_k, cache_v, c, c_ctx, ada_w, ada_b, norm_w, ffn_wg, ffn_wu, ffn_wd, w_in, w_out, pool_w, pool_scale, q_norm, k_norm, attn_sink, hy_short_w, hy_short_b, hy_f_w1, hy_f_b1, hy_f_w2, hy_f_b2, hy_f_w3, hy_f_b3, hy_sin_freq, hy_decay, hy_bias):
    bp, lp, _ = x_prompt.shape
    bs, ls, _ = x_sample.shape
    lc = cache_k.shape[2]

    cond = jnp.concatenate([c_ctx[None, :], c, jnp.zeros((COND_ROWS - 1 - bs, D_MODEL), F32)], axis=0)
    mod = _ada_mod(cond, ada_w, ada_b).reshape(DEPTH, COND_ROWS, 1, N_MOD * D_MODEL)

    gq, gk = _block_ones(ATTN_WIDTH), _block_ones(KV_WIDTH)
    rope_tabs = _rope_tables(ls)
    blk_p, blk_s = min(lp, HY_BLOCK), min(ls, HY_BLOCK)
    consts_p = (gq, gk) + _dft_tables(blk_p) + (_filter_embedding(lp),)
    consts_s = (gq, gk) + _dft_tables(blk_s) + (_filter_embedding(ls),)
    ctx_k = cache_k.reshape(bs, DEPTH, lc, KV_WIDTH)
    ctx_v = cache_v.reshape(bs, DEPTH, lc, KV_WIDTH)

    flat = _flat_weights(ffn_wg, ffn_wu, ffn_wd, w_in, w_out)
    wts = _cast_layer(flat, 0)
    yp = x_prompt.reshape(bp * lp, D_MODEL)
    ys = x_sample.reshape(bs * ls, D_MODEL)
    p = _layer_params(norm_w, pool_w, pool_scale, q_norm, k_norm, attn_sink, hy_short_w, hy_short_b, hy_f_w1, hy_f_b1,
                      hy_f_w2, hy_f_b2, hy_f_w3, hy_f_b3, hy_sin_freq, hy_decay, hy_bias)
    ks, vs = [], []
    for l in range(DEPTH):
        yp, k_l, v_l, _ = _stream_layer(yp, p, wts, mod, bp, lp, None, None, l, blk_p, consts_p)
        ks.append(k_l.reshape(bp, lp, N_KV_HEADS, HEAD_DIM))
        vs.append(v_l.reshape(bp, lp, N_KV_HEADS, HEAD_DIM))
        cast_next = (flat, l + 1) if l + 1 < DEPTH else None
        ys, _, _, wts = _stream_layer(ys, p, wts, mod, bs, ls, rope_tabs, (ctx_k, ctx_v), l, blk_s, consts_s, cast_next)
    return (yp.reshape(bp, lp, D_MODEL), ys.reshape(bs, ls, D_MODEL),
            jnp.stack(ks, axis=1), jnp.stack(vs, axis=1))
```

```python
import functools
import math

import jax
import jax.numpy as jnp
from jax import lax
from jax.experimental import pallas as pl
from jax.experimental.pallas import tpu as pltpu

F32 = jnp.float32
BF16 = jnp.bfloat16

D_MODEL = 1024
DEPTH = 2
GRID_W = 64
POOL_WINDOWS = (2, 4, 8, 16)
POOL_WIDTH = 256
POOL_GROUP = 64
HEAD_DIM = 64
N_HEADS = 8
N_KV_HEADS = 2
GQA_GROUP = 4
ATTN_WIDTH = 512
KV_WIDTH = 128
WINDOW = 128
BLOCK = 128
ROPE_THETA = 10000.0
HY_WIDTH = 256
HY_ORDER = 2
HY_EMB_DIM = 33
HY_FILTER_HIDDEN = 64
HY_MOD_SHIFT = 0.05
D_FF = 2816
IN_WIDTH = 1792
N_MOD = 9
NORM_EPS = 1e-6
NEG_INF = -1e30

LANES = 128
SUBLANES = 8
VMEM_LIMIT = 56 * 1024 * 1024

TOKEN_TILE = 512
TOKEN_TILE_B = 1024
MXU_DIM = 256
FF_CHUNKS = (1280, 1536)
assert sum(FF_CHUNKS) == D_FF and all(w % MXU_DIM == 0 for w in FF_CHUNKS)
SEQ_CHUNK = 512
SEQ_STEP_ROWS = 2048
HY_BLOCK = 1024
FREQ_TILE = 128
FREQ_SUB = 2
MAC_ROWS = 32
ATTN_SUB = 8
CTX_SEQS = 4
COND_ROWS = 8


def _cparams(sem):
    return pltpu.CompilerParams(dimension_semantics=sem, vmem_limit_bytes=VMEM_LIMIT)


def _dot(a, b):
    return jnp.dot(a, b, preferred_element_type=F32)


def _layer_block(tail, layer):
    return pl.BlockSpec((None,) + tuple(tail), lambda *_: (layer,) + (0,) * len(tail))


def _cast_kernel(*refs):
    n = len(refs) // 2
    for i_ref, o_ref in zip(refs[:n], refs[n:]):
        o_ref[...] = i_ref[...].astype(o_ref.dtype)


def _flat_weights(ffn_wg, ffn_wu, ffn_wd, w_in, w_out):
    arrs = {"wg": ffn_wg, "wu": ffn_wu, "wd": ffn_wd, "w_in": w_in, "w_out": w_out}
    return {k: a.reshape(-1, a.shape[-1]) for k, a in arrs.items()}


def _layer_cast_specs(flat, layer, steps):
    in_specs, out_specs, out_shapes = [], [], []
    for a in flat.values():
        rows = a.shape[0] // DEPTH
        blk = rows // steps
        in_specs.append(pl.BlockSpec((blk, a.shape[1]), lambda i, first=layer * steps: (first + i, 0)))
        out_specs.append(pl.BlockSpec((blk, a.shape[1]), lambda i: (i, 0)))
        out_shapes.append(jax.ShapeDtypeStruct((rows, a.shape[1]), BF16))
    return in_specs, out_specs, out_shapes


def _layer_weights(outs):
    wg, wu, wd, w_in, w_out = outs
    return {"wg": wg.reshape(2, D_MODEL, D_FF), "wu": wu.reshape(2, D_MODEL, D_FF),
            "wd": wd.reshape(2, D_FF, D_MODEL), "w_in": w_in, "w_out": w_out}


def _cast_layer(flat, layer):
    in_specs, out_specs, out_shapes = _layer_cast_specs(flat, layer, steps=16)
    outs = pl.pallas_call(
        _cast_kernel,
        grid=(16,),
        in_specs=in_specs,
        out_specs=out_specs,
        out_shape=out_shapes,
        compiler_params=_cparams(("arbitrary",)),
        name="cast_weights",
    )(*flat.values())
    return _layer_weights(outs)


def _ada_kernel(c_ref, w_ref, b_ref, o_ref):
    c = c_ref[...]
    s = (c * jax.nn.sigmoid(c)).astype(BF16)
    o_ref[...] = _dot(s, w_ref[...].astype(BF16)) + b_ref[...]


def _ada_specs(layer, tn):
    in_specs = [
        pl.BlockSpec((COND_ROWS, D_MODEL), lambda j: (0, 0)),
        pl.BlockSpec((None, D_MODEL, tn), lambda j: (layer, 0, j)),
        pl.BlockSpec((None, 1, tn), lambda j: (layer, 0, j)),
    ]
    return in_specs, pl.BlockSpec((COND_ROWS, tn), lambda j: (0, j))


def _ada_mod(cond, ada_w, ada_b, layer):
    tn = 3072
    nw = N_MOD * D_MODEL
    in_specs, out_spec = _ada_specs(layer, tn)
    return pl.pallas_call(
        _ada_kernel,
        grid=(nw // tn,),
        in_specs=in_specs,
        out_specs=out_spec,
        out_shape=jax.ShapeDtypeStruct((COND_ROWS, nw), F32),
        compiler_params=_cparams(("arbitrary",)),
        name="ada_mod",
    )(cond, ada_w, ada_b)


def _mod_slice(mod_ref, k):
    return mod_ref[:, k * D_MODEL:(k + 1) * D_MODEL]


def _rms_mod(x, gain, scale, shift):
    y = x * lax.rsqrt(jnp.mean(x * x, axis=-1, keepdims=True) + NORM_EPS)
    return (y * gain) * (1.0 + scale) + shift


def _swiglu(hb, wg_ref, wu_ref, wd_ref):
    acc = None
    lo = 0
    for width in FF_CHUNKS:
        sl = slice(lo, lo + width)
        lo += width
        g = _dot(hb, wg_ref[:, sl])
        u = _dot(hb, wu_ref[:, sl])
        a = ((g * jax.nn.sigmoid(g)) * u).astype(BF16)
        y = _dot(a, wd_ref[sl, :])
        acc = y if acc is None else acc + y
    return acc


def _head_norm(x, gmat, gain):
    ss = _dot((x * x).astype(BF16), gmat)
    return (x * lax.rsqrt(ss * (1.0 / HEAD_DIM) + NORM_EPS)) * gain


def _rope(x, cos, sa, sb):
    w = x.shape[1]
    xn = pltpu.roll(x, w - 16, axis=1)
    xp = pltpu.roll(x, 16, axis=1)
    return x * cos + xn * sa + xp * sb


def _token_a_kernel(*refs, rope, n_cast, n_ada):
    n_in = 11 + (3 if rope else 0) + n_cast + n_ada
    ins, outs = refs[:n_in], refs[n_in:]
    x_ref, mod_ref, nrm_ref, wg_ref, wu_ref, wd_ref, win_ref, gq_ref, gk_ref, qg_ref, kg_ref = ins[:11]
    if rope:
        cos_ref, sa_ref, sb_ref = ins[11:14]
    x1_ref, up_ref, q_ref, kd_ref, vd_ref, hv_ref, hx1_ref, hx2_ref = outs[:8]
    if not rope:
        k_ref, v_ref = outs[8:10]
    n_side_out = n_cast + (1 if n_ada else 0)
    for i_ref, o_ref in zip(ins[n_in - n_cast - n_ada:n_in - n_ada], outs[len(outs) - n_side_out:]):
        o_ref[...] = i_ref[...].astype(o_ref.dtype)
    if n_ada:
        _ada_kernel(*ins[n_in - n_ada:], outs[-1])
    x = x_ref[...]
    sh1, sc1, g1 = _mod_slice(mod_ref, 0), _mod_slice(mod_ref, 1), _mod_slice(mod_ref, 2)
    sh2, sc2 = _mod_slice(mod_ref, 3), _mod_slice(mod_ref, 4)
    h = _rms_mod(x, nrm_ref[0:1, :], sc1, sh1).astype(BF16)
    x1 = x + (0.5 * g1) * _swiglu(h, wg_ref, wu_ref, wd_ref)
    x1_ref[...] = x1
    h2 = _rms_mod(x1, nrm_ref[1:2, :], sc2, sh2).astype(BF16)
    s1 = POOL_WIDTH
    s2 = s1 + ATTN_WIDTH
    s3 = s2 + KV_WIDTH
    s4 = s3 + KV_WIDTH
    qkv = _dot(h2, win_ref[:, s1:s4])
    q = _head_norm(qkv[:, :ATTN_WIDTH], gq_ref[...], qg_ref[...])
    k = _head_norm(qkv[:, ATTN_WIDTH:ATTN_WIDTH + KV_WIDTH], gk_ref[...], kg_ref[...])
    v = qkv[:, ATTN_WIDTH + KV_WIDTH:]
    up_ref[...] = _dot(h2, win_ref[:, :s1])
    hy = _dot(h2, win_ref[:, s4:])
    for j, ref in enumerate((hv_ref, hx1_ref, hx2_ref)):
        ref[...] = hy[:, j * HY_WIDTH:(j + 1) * HY_WIDTH]
    if rope:
        cos, sa, sb = cos_ref[...], sa_ref[...], sb_ref[...]
        reps = ATTN_WIDTH // LANES
        q = _rope(q, jnp.concatenate([cos] * reps, axis=1), jnp.concatenate([sa] * reps, axis=1),
                  jnp.concatenate([sb] * reps, axis=1))
        k = _rope(k, cos, sa, sb)
    q_ref[...] = (q * (HEAD_DIM ** -0.5 * math.log2(math.e))).astype(BF16)
    kd_ref[...] = _dup_heads(k)
    vd_ref[...] = _dup_heads(v)
    if not rope:
        k_ref[...] = k
        v_ref[...] = v


def _token_a(x, mod_l, nrm, wts, gq, gk, qg, kg, rope_tabs, seq_len, layer, cast_next=None, ada_next=None):
    n = x.shape[0]
    tm = TOKEN_TILE
    tiles_per_seq = max(seq_len // tm, 1)
    rope = rope_tabs is not None
    if rope:
        mod_row = lambda i: (1 + i // tiles_per_seq, 0, 0)
    else:
        mod_row = lambda i: (0, 0, 0)
    const = lambda i: (0, 0)
    row = lambda i: (i, 0)
    in_specs = [
        pl.BlockSpec((tm, D_MODEL), row),
        pl.BlockSpec((None, 1, N_MOD * D_MODEL), mod_row),
        _layer_block((3, D_MODEL), layer),
        pl.BlockSpec((None, D_MODEL, D_FF), lambda i: (0, 0, 0)),
        pl.BlockSpec((None, D_MODEL, D_FF), lambda i: (0, 0, 0)),
        pl.BlockSpec((None, D_FF, D_MODEL), lambda i: (0, 0, 0)),
        pl.BlockSpec((D_MODEL, IN_WIDTH), const),
        pl.BlockSpec((ATTN_WIDTH, ATTN_WIDTH), const),
        pl.BlockSpec((KV_WIDTH, KV_WIDTH), const),
        _layer_block((1, ATTN_WIDTH), layer),
        _layer_block((1, KV_WIDTH), layer),
    ]
    args = [x, mod_l, nrm, wts["wg"], wts["wu"], wts["wd"], wts["w_in"], gq, gk, qg, kg]
    if rope:
        tab = pl.BlockSpec((tm, LANES), lambda i: (i % tiles_per_seq, 0))
        in_specs += [tab, tab, tab]
        args += list(rope_tabs)
    widths = (D_MODEL, POOL_WIDTH, ATTN_WIDTH, 2 * KV_WIDTH, 2 * KV_WIDTH) + (HY_WIDTH,) * (HY_ORDER + 1)
    dtypes = (F32, F32, BF16, BF16, BF16) + (F32,) * (HY_ORDER + 1)
    if not rope:
        widths += (KV_WIDTH, KV_WIDTH)
        dtypes += (F32, F32)
    out_specs = [pl.BlockSpec((tm, w), row) for w in widths]
    out_shapes = [jax.ShapeDtypeStruct((n, w), d) for w, d in zip(widths, dtypes)]
    n_cast = 0
    if cast_next is not None:
        flat, next_layer = cast_next
        c_in, c_out, c_shapes = _layer_cast_specs(flat, next_layer, steps=n // tm)
        in_specs += c_in
        args += list(flat.values())
        out_specs += c_out
        out_shapes += c_shapes
        n_cast = len(c_in)
    n_ada = 0
    if ada_next is not None:
        cond, ada_w, ada_b, next_layer = ada_next
        a_in, a_out = _ada_specs(next_layer, N_MOD * D_MODEL // (n // tm))
        in_specs += a_in
        args += [cond, ada_w, ada_b]
        out_specs.append(a_out)
        out_shapes.append(jax.ShapeDtypeStruct((COND_ROWS, N_MOD * D_MODEL), F32))
        n_ada = len(a_in)
    outs = pl.pallas_call(
        functools.partial(_token_a_kernel, rope=rope, n_cast=n_cast, n_ada=n_ada),
        grid=(n // tm,),
        in_specs=in_specs,
        out_specs=out_specs,
        out_shape=out_shapes,
        compiler_params=_cparams(("arbitrary",)),
        name="token_a_rope" if rope else "token_a",
    )(*args)
    outs = list(outs)
    next_mod = outs.pop() if n_ada else None
    next_wts = _layer_weights(outs[len(outs) - n_cast:]) if n_cast else None
    return outs[:len(outs) - n_cast], next_wts, next_mod


def _token_b_kernel(x_ref, yp_ref, a_ref, yh_ref, mod_ref, nrm_ref, wo_ref, wg_ref, wu_ref, wd_ref, o_ref):
    x1 = x_ref[...]
    g2 = _mod_slice(mod_ref, 5)
    sh3, sc3, g3 = _mod_slice(mod_ref, 6), _mod_slice(mod_ref, 7), _mod_slice(mod_ref, 8)
    cat = jnp.concatenate([yp_ref[...], a_ref[...], yh_ref[...]], axis=1)
    x2 = x1 + g2 * _dot(cat, wo_ref[...])
    h3 = _rms_mod(x2, nrm_ref[2:3, :], sc3, sh3).astype(BF16)
    o_ref[...] = x2 + (0.5 * g3) * _swiglu(h3, wg_ref, wu_ref, wd_ref)


def _token_b(x1, y_pool, a, y_hy, mod_l, nrm, wts, seq_len, layer, per_seq_cond):
    n = x1.shape[0]
    tm = min(TOKEN_TILE_B, n)
    tiles_per_seq = max(seq_len // tm, 1)
    if per_seq_cond:
        mod_row = lambda i: (1 + i // tiles_per_seq, 0, 0)
    else:
        mod_row = lambda i: (0, 0, 0)
    const = lambda i: (0, 0)
    row = lambda i: (i, 0)
    return pl.pallas_call(
        _token_b_kernel,
        grid=(n // tm,),
        in_specs=[
            pl.BlockSpec((tm, D_MODEL), row),
            pl.BlockSpec((tm, POOL_WIDTH), row),
            pl.BlockSpec((tm, ATTN_WIDTH), row),
            pl.BlockSpec((tm, HY_WIDTH), row),
            pl.BlockSpec((None, 1, N_MOD * D_MODEL), mod_row),
            _layer_block((3, D_MODEL), layer),
            pl.BlockSpec((D_MODEL, D_MODEL), const),
            pl.BlockSpec((None, D_MODEL, D_FF), lambda i: (1, 0, 0)),
            pl.BlockSpec((None, D_MODEL, D_FF), lambda i: (1, 0, 0)),
            pl.BlockSpec((None, D_FF, D_MODEL), lambda i: (1, 0, 0)),
        ],
        out_specs=pl.BlockSpec((tm, D_MODEL), row),
        out_shape=jax.ShapeDtypeStruct((n, D_MODEL), F32),
        compiler_params=_cparams(("arbitrary",)),
        name="token_b",
    )(x1, y_pool, a, y_hy, mod_l, nrm, wts["w_out"], wts["wg"], wts["wu"], wts["wd"])


def _seqs_per_step(batch, seq_len):
    return min(batch, max(1, SEQ_STEP_ROWS // seq_len))


def _halo_rows(src_ref, base, r0, rows, seq_len):
    c = src_ref.shape[1]
    zero = jnp.zeros((SUBLANES, c), F32)
    lo = base + r0
    prev = src_ref[lo - SUBLANES:lo, :] if r0 > 0 else zero
    nxt = src_ref[lo + rows:lo + rows + SUBLANES, :] if r0 + rows < seq_len else zero
    return prev, nxt


def _short_conv_chunk(src_ref, base, r0, rows, seq_len, w, b):
    x = src_ref[base + r0:base + r0 + rows, :]
    prev, nxt = _halo_rows(src_ref, base, r0, rows, seq_len)
    ridx = lax.broadcasted_iota(jnp.int32, x.shape, 0)
    xp = jnp.where(ridx == 0, prev[SUBLANES - 1:SUBLANES, :], pltpu.roll(x, 1, axis=0))
    xn = jnp.where(ridx == rows - 1, nxt[0:1, :], pltpu.roll(x, rows - 1, axis=0))
    return xp * w[0:1, :] + x * w[1:2, :] + xn * w[2:3, :] + b


def _pool_kernel(u_ref, w_ref, scale_ref, o_ref, *, seq_len, rows, seqs):
    lane = lax.broadcasted_iota(jnp.int32, (rows, POOL_WIDTH), 1)
    grp = lane // POOL_GROUP
    half = jnp.where(grp == 0, 1, jnp.where(grp == 1, 2, jnp.where(grp == 2, 4, 8)))
    odd_grp = lax.broadcasted_iota(jnp.int32, (rows, LANES), 1) >= POOL_GROUP
    ext = rows + 2 * SUBLANES
    back = lambda v, s: pltpu.roll(v, s, axis=0)
    fwd = lambda v, s: pltpu.roll(v, ext - s, axis=0)
    core = lambda v: v[SUBLANES:SUBLANES + rows, :]
    for base, r0 in [(sq * seq_len, r0) for sq in range(seqs) for r0 in range(0, seq_len, rows)]:
        x = u_ref[base + r0:base + r0 + rows, :]
        prev, nxt = _halo_rows(u_ref, base, r0, rows, seq_len)
        a = jnp.concatenate([prev, x, nxt], axis=0)
        lo, hi = a[:, :LANES], a[:, LANES:]
        b1 = back(lo, 1)
        b2 = b1 + back(b1, 1)
        f2 = lo + fwd(lo, 1)
        sum_lo = jnp.where(odd_grp, core(b2) + core(f2), core(b1) + core(lo))
        c1 = back(hi, 1)
        c2 = c1 + back(c1, 1)
        c4 = c2 + back(c2, 2)
        c8 = c4 + back(c4, 4)
        g2 = hi + fwd(hi, 1)
        g4 = g2 + fwd(g2, 2)
        g8 = g4 + fwd(g4, 4)
        sum_hi = jnp.where(odd_grp, core(c8) + core(g8), core(c4) + core(g4))
        wsum = jnp.concatenate([sum_lo, sum_hi], axis=1)
        t = r0 + lax.broadcasted_iota(jnp.int32, (rows, POOL_WIDTH), 0)
        cnt = jnp.minimum(t + half, seq_len) - jnp.maximum(t - half, 0)
        d = wsum / cnt.astype(F32) - x
        y = _dot(d.astype(BF16), w_ref[...]) * scale_ref[...]
        o_ref[base + r0:base + r0 + rows, :] = y.astype(o_ref.dtype)


def _pool_mix(u_pool, w_bd, scale, batch, seq_len, layer):
    rows = min(SEQ_CHUNK // 2, seq_len)
    seqs = _seqs_per_step(batch, seq_len)
    return pl.pallas_call(
        functools.partial(_pool_kernel, seq_len=seq_len, rows=rows, seqs=seqs),
        grid=(batch // seqs,),
        in_specs=[
            pl.BlockSpec((seqs * seq_len, POOL_WIDTH), lambda b: (b, 0)),
            _layer_block((POOL_WIDTH, POOL_WIDTH), layer),
            _layer_block((1, POOL_WIDTH), layer),
        ],
        out_specs=pl.BlockSpec((seqs * seq_len, POOL_WIDTH), lambda b: (b, 0)),
        out_shape=jax.ShapeDtypeStruct((batch * seq_len, POOL_WIDTH), BF16),
        compiler_params=_cparams(("arbitrary",)),
        name="pool_mix",
    )(u_pool, w_bd, scale)


def _dup_heads(x):
    lane = lax.broadcasted_iota(jnp.int32, x.shape, 1)
    sw = pltpu.roll(x, HEAD_DIM, axis=1)
    lo = lane < HEAD_DIM
    return jnp.concatenate([jnp.where(lo, x, sw), jnp.where(lo, sw, x)], axis=1).astype(BF16)


def _attn_kernel(*refs, has_local, nblocks, sub, seqs, layer):
    if has_local:
        sink_ref, q_ref, kp_ref, kc_ref, kn_ref, vp_ref, vc_ref, vn_ref, ck_ref, cv_ref, o_ref = refs
        kwin = jnp.concatenate([kp_ref[...], kc_ref[...], kn_ref[...]], axis=0)
        vwin = jnp.concatenate([vp_ref[...], vc_ref[...], vn_ref[...]], axis=0)
        kctx, vctx = _dup_heads(ck_ref[...]), _dup_heads(cv_ref[...])
        r = lax.broadcasted_iota(jnp.int32, (GQA_GROUP * BLOCK, BLOCK), 0) % BLOCK
        j = lax.broadcasted_iota(jnp.int32, (GQA_GROUP * BLOCK, BLOCK), 1)
    else:
        sink_ref, q_ref, kd_ref, vd_ref, o_ref = refs
    i = pl.program_id(1)
    lane_q = lax.broadcasted_iota(jnp.int32, (BLOCK, LANES), 1)
    log2e = math.log2(math.e)
    units = [(sb, kvh) for sb in range(seqs * sub) for kvh in range(N_KV_HEADS)]
    scores, values = [], []
    for sb, kvh in units:
        q = q_ref[sb * BLOCK:(sb + 1) * BLOCK, :]
        qparts = []
        for hd in range(kvh * GQA_GROUP, (kvh + 1) * GQA_GROUP):
            qp = q[:, (hd // 2) * LANES:(hd // 2 + 1) * LANES]
            keep = (lane_q < HEAD_DIM) == (hd % 2 == 0)
            qparts.append(jnp.where(keep, qp, jnp.zeros_like(qp)))
        qs = jnp.concatenate(qparts, axis=0)
        cols = slice(kvh * LANES, (kvh + 1) * LANES)
        if has_local:
            loc = slice(sb * BLOCK, (sb + 3) * BLOCK)
            kk = jnp.concatenate([kwin[loc, cols], kctx[:, cols]], axis=0)
            vv = jnp.concatenate([vwin[loc, cols], vctx[:, cols]], axis=0)
        else:
            own = slice((sb // sub) * sub * BLOCK, (sb // sub + 1) * sub * BLOCK)
            kk, vv = kd_ref[own, cols], vd_ref[own, cols]
        s = lax.dot_general(qs, kk, (((1,), (1,)), ((), ())), preferred_element_type=F32)
        if has_local:
            gb = i * sub + sb
            below = j >= r + jnp.where(gb >= 1, 0, BLOCK)
            above = j <= r - jnp.where(gb <= nblocks - 2, 0, BLOCK)
            s = jnp.concatenate([jnp.where(below, s[:, :BLOCK], NEG_INF), s[:, BLOCK:2 * BLOCK],
                                 jnp.where(above, s[:, 2 * BLOCK:3 * BLOCK], NEG_INF), s[:, 3 * BLOCK:]], axis=1)
        scores.append(s)
        values.append(jnp.concatenate([vv, jnp.ones_like(vv)], axis=1))

    def per_head(kvh, col, f):
        parts = [f(col[g * BLOCK:(g + 1) * BLOCK, :], sink_ref[layer, kvh * GQA_GROUP + g] * log2e) for g in range(GQA_GROUP)]
        return jnp.concatenate(parts, axis=0)

    maxes = [per_head(kvh, jnp.max(s, axis=1, keepdims=True), jnp.maximum) for (_, kvh), s in zip(units, scores)]
    probs = [jnp.exp2(s - m).astype(BF16) for s, m in zip(scores, maxes)]
    sums = [_dot(e, v) for e, v in zip(probs, values)]
    outs = []
    for (_, kvh), ow, m in zip(units, sums, maxes):
        den = ow[:, LANES:] + per_head(kvh, m, lambda mm, sk: jnp.exp2(sk - mm))
        outs.append(ow[:, :LANES] / den)
    for sb in range(seqs * sub):
        heads = [outs[sb * N_KV_HEADS + kvh][g * BLOCK:(g + 1) * BLOCK, :]
                 for kvh in range(N_KV_HEADS) for g in range(GQA_GROUP)]
        blks = [jnp.where(lane_q < HEAD_DIM, heads[2 * p], heads[2 * p + 1]) for p in range(N_HEADS // 2)]
        o_ref[sb * BLOCK:(sb + 1) * BLOCK, :] = jnp.concatenate(blks, axis=1).astype(o_ref.dtype)


def _attention(q, kd, vd, sink, batch, seq_len, ctx_k=None, ctx_v=None, layer=0):
    assert WINDOW == BLOCK
    nb = seq_len // BLOCK
    has_local = ctx_k is not None
    sub = min(ATTN_SUB, nb)
    steps = nb // sub
    seqs = 1
    if not has_local:
        assert steps == 1
        seqs = math.gcd(CTX_SEQS, batch)
        batch //= seqs
    qb = seqs * sub * BLOCK
    qspec = pl.BlockSpec((qb, ATTN_WIDTH), lambda b, i: (b * steps + i, 0))
    sspec = pl.BlockSpec(memory_space=pltpu.SMEM)
    if has_local:
        lc = ctx_k.shape[2]
        edge = lambda f: pl.BlockSpec((BLOCK, 2 * KV_WIDTH), f)
        prev = lambda b, i: (b * nb + jnp.maximum(i * sub - 1, 0), 0)
        nxt = lambda b, i: (b * nb + jnp.minimum((i + 1) * sub, nb - 1), 0)
        cur = pl.BlockSpec((qb, 2 * KV_WIDTH), lambda b, i: (b * steps + i, 0))
        cspec = pl.BlockSpec((None, None, lc, KV_WIDTH), lambda b, i: (b, layer, 0, 0))
        in_specs = [sspec, qspec, edge(prev), cur, edge(nxt), edge(prev), cur, edge(nxt), cspec, cspec]
        args = [sink, q, kd, kd, kd, vd, vd, vd, ctx_k, ctx_v]
    else:
        kv = pl.BlockSpec((qb, 2 * KV_WIDTH), lambda b, i: (b, 0))
        in_specs = [sspec, qspec, kv, kv]
        args = [sink, q, kd, vd]
    return pl.pallas_call(
        functools.partial(_attn_kernel, has_local=has_local, nblocks=nb, sub=sub, seqs=seqs, layer=layer),
        grid=(batch, steps),
        in_specs=in_specs,
        out_specs=pl.BlockSpec((qb, ATTN_WIDTH), lambda b, i: (b * steps + i, 0)),
        out_shape=jax.ShapeDtypeStruct(q.shape, BF16),
        compiler_params=_cparams(("arbitrary", "arbitrary")),
        name="attn_latent" if has_local else "attn_context",
    )(*args)


def _filter_kernel(zt_ref, zb_ref, w1_ref, b1_ref, w2_ref, b2_ref, w3_ref, b3_ref, fr_ref, dl_ref, k_ref, sum_ref,
                   *, seq_len, rows):
    i = pl.program_id(0)
    half = rows // 2
    oc = HY_ORDER * HY_WIDTH
    z = jnp.concatenate([zt_ref[...], zb_ref[...]], axis=1)
    h = jnp.sin(fr_ref[0:1, :] * (_dot(z.astype(BF16), w1_ref[...].astype(BF16)) + b1_ref[...]))
    h = jnp.sin(fr_ref[1:2, :] * (_dot(h.astype(BF16), w2_ref[...].astype(BF16)) + b2_ref[...]))
    h3 = _dot(h.astype(BF16), w3_ref[...].astype(BF16)) + b3_ref[...]
    total = jnp.zeros((1, oc), F32)
    for part, z_ref in enumerate((zt_ref, zb_ref)):
        row = i * rows + part * half + lax.broadcasted_iota(jnp.int32, (half, oc), 0)
        t = z_ref[:, 0:1]
        decay = jnp.exp(-t * jnp.abs(dl_ref[...]))
        fwd_dir = h3[:, part * 2 * oc:part * 2 * oc + oc]
        bwd_dir = h3[:, part * 2 * oc + oc:(part + 1) * 2 * oc]
        sel = jnp.where(row < seq_len, fwd_dir, jnp.where(row > seq_len, bwd_dir, 0.0))
        k = sel * (decay + HY_MOD_SHIFT)
        k_ref[part * half:(part + 1) * half, :] = k
        total = total + jnp.sum(jnp.abs(k), axis=0, keepdims=True)

    @pl.when(i == 0)
    def _():
        sum_ref[...] = jnp.zeros_like(sum_ref)

    sum_ref[...] += total


def _block_diag2(w):
    z = jnp.zeros_like(w)
    return jnp.concatenate([jnp.concatenate([w, z], axis=-1), jnp.concatenate([z, w], axis=-1)], axis=-2)


def _filter_weights(w1, b1, w2, b2, w3, b3, freq, deltas):
    two = lambda a: jnp.concatenate([a, a], axis=-1)
    w1p = jnp.pad(w1, ((0, 0), (0, LANES - HY_EMB_DIM), (0, 0)))
    return (_block_diag2(w1p), two(b1)[:, None, :], _block_diag2(w2), two(b2)[:, None, :],
            _block_diag2(w3), two(b3)[:, None, :], two(freq), deltas.reshape(DEPTH, 1, HY_ORDER * HY_WIDTH))


def _hyena_filter(z_ext, fw, seq_len, layer):
    n = 2 * seq_len
    rows = min(1024, n)
    half = rows // 2
    oc = HY_ORDER * HY_WIDTH
    hid = 2 * HY_FILTER_HIDDEN
    const = lambda i: (0, 0)
    return pl.pallas_call(
        functools.partial(_filter_kernel, seq_len=seq_len, rows=rows),
        grid=(n // rows,),
        in_specs=[
            pl.BlockSpec((half, LANES), lambda i: (2 * i, 0)),
            pl.BlockSpec((half, LANES), lambda i: (2 * i + 1, 0)),
            _layer_block((2 * LANES, hid), layer),
            _layer_block((1, hid), layer),
            _layer_block((hid, hid), layer),
            _layer_block((1, hid), layer),
            _layer_block((hid, 4 * oc), layer),
            _layer_block((1, 4 * oc), layer),
            _layer_block((2, hid), layer),
            _layer_block((1, oc), layer),
        ],
        out_specs=[pl.BlockSpec((rows, oc), lambda i: (i, 0)), pl.BlockSpec((1, oc), const)],
        out_shape=[jax.ShapeDtypeStruct((n, oc), F32), jax.ShapeDtypeStruct((1, oc), F32)],
        compiler_params=_cparams(("arbitrary",)),
        name="hyena_filter",
    )(z_ext, z_ext, *fw)


def _spectrum_kernel(k_ref, sum_ref, fwd_ref, kf_ref, hprev_ref, g0_ref, *, blk):
    t = pl.program_id(0)
    a = k_ref[...] / (sum_ref[...] + 1e-6)
    ha = _dot(fwd_ref[...], a.astype(BF16))
    tf = FREQ_TILE
    par = lax.broadcasted_iota(jnp.int32, (tf, 1), 0) % 2
    sgn = (1 - 2 * par).astype(F32)

    @pl.when(t > 0)
    def _():
        g0 = g0_ref[...]
        for c in range(blk // tf):
            re = slice(2 * c * tf, (2 * c + 1) * tf)
            im = slice((2 * c + 1) * tf, (2 * c + 2) * tf)
            kre = ha[re, :] - sgn * hprev_ref[im, :]
            kim = ha[im, :] + sgn * (hprev_ref[re, :] - g0)
            for o in range(HY_ORDER):
                lanes = slice(o * HY_WIDTH, (o + 1) * HY_WIDTH)
                kf_ref[o, c, 0:tf, :] = kre[:, lanes]
                kf_ref[o, c, tf:2 * tf, :] = kim[:, lanes]

    hprev_ref[...] = ha
    g0_ref[...] = a[0:1, :]


def _filter_spectrum(kraw, ksum, fwd, seq_len, blk):
    nb = seq_len // blk
    nlags = 2 * nb - 1
    oc = HY_ORDER * HY_WIDTH
    m = 2 * blk
    tf = FREQ_TILE
    nfc = blk // tf
    return pl.pallas_call(
        functools.partial(_spectrum_kernel, blk=blk),
        grid=(2 * nb,),
        in_specs=[
            pl.BlockSpec((blk, oc), lambda t: ((nb + t) % (2 * nb), 0)),
            pl.BlockSpec((1, oc), lambda t: (0, 0)),
            pl.BlockSpec((m, blk), lambda t: (0, 0)),
        ],
        out_specs=pl.BlockSpec((HY_ORDER, nfc, None, 2 * tf, HY_WIDTH), lambda t: (0, 0, jnp.maximum(t - 1, 0), 0, 0)),
        out_shape=jax.ShapeDtypeStruct((HY_ORDER, nfc, nlags, 2 * tf, HY_WIDTH), F32),
        scratch_shapes=[pltpu.VMEM((m, oc), F32), pltpu.VMEM((1, oc), F32)],
        compiler_params=_cparams(("arbitrary",)),
        name="filter_spectrum",
    )(kraw, ksum, fwd)


def _conv_kernel(z_ref, g_ref, swz_ref, sbz_ref, swg_ref, sbg_ref, bd_ref, kf_ref, fwd_ref, inv_ref, o_ref,
                 zb_ref, zc_ref, yf_ref, *, seq_len, blk, conv_z, seqs, sub):
    fc = pl.program_id(1)
    nfc = pl.num_programs(1)
    nb = seq_len // blk
    tf = FREQ_TILE
    blocks = [((g // nb) * seq_len, (g % nb) * blk) for g in range(seqs * nb)]

    @pl.when(fc == 0)
    def _():
        for base, r0 in blocks:
            rows = slice(base + r0, base + r0 + blk)
            if conv_z:
                zc_ref[rows, :] = _short_conv_chunk(z_ref, base, r0, blk, seq_len, swz_ref[...], sbz_ref[...])
                zb_ref[rows, :] = zc_ref[rows, :].astype(BF16)
            else:
                zb_ref[rows, :] = z_ref[rows, :].astype(BF16)

    zf = [[_dot(fwd_ref[s * 2 * tf:(s + 1) * 2 * tf, :], zb_ref[g * blk:(g + 1) * blk, :]) for g in range(seqs * nb)]
          for s in range(sub)]

    for s in range(sub):
        col = pl.multiple_of((fc * sub + s) * 2 * tf, 2 * tf)
        for g0 in range(0, seqs * nb, nb):
            for bi in range(nb):
                for r in range(0, tf, MAC_ROWS):
                    re = slice(r, r + MAC_ROWS)
                    im = slice(tf + r, tf + r + MAC_ROWS)
                    yr = jnp.zeros((MAC_ROWS, HY_WIDTH), F32)
                    yi = jnp.zeros((MAC_ROWS, HY_WIDTH), F32)
                    for bj in range(nb):
                        lag = bi - bj + nb - 1
                        kr, ki = kf_ref[s, lag, re, :], kf_ref[s, lag, im, :]
                        zr, zi = zf[s][g0 + bj][re, :], zf[s][g0 + bj][im, :]
                        yr = yr + (kr * zr - ki * zi)
                        yi = yi + (kr * zi + ki * zr)
                    yf_ref[g0 + bi, pl.ds(col + r, MAC_ROWS), :] = yr.astype(BF16)
                    yf_ref[g0 + bi, pl.ds(col + tf + r, MAC_ROWS), :] = yi.astype(BF16)

    @pl.when(fc == nfc - 1)
    def _():
        ys, gates = [], []
        for g, (base, r0) in enumerate(blocks):
            ys.append(_dot(inv_ref[...], yf_ref[g]))
            gates.append(_short_conv_chunk(g_ref, base, r0, blk, seq_len, swg_ref[...], sbg_ref[...]))
        for (base, r0), y, gate in zip(blocks, ys, gates):
            rows = slice(base + r0, base + r0 + blk)
            z = zc_ref[rows, :] if conv_z else z_ref[rows, :]
            o_ref[rows, :] = (gate * (y + bd_ref[...] * z)).astype(o_ref.dtype)


def _hyena_conv(z_src, z_col, conv_z, g_src, g_col, short_w, short_b, bd, kf, order, fwd, inv,
                batch, seq_len, blk, out_dtype, layer):
    nb = seq_len // blk
    nlags = 2 * nb - 1
    tf = FREQ_TILE
    sub = min(FREQ_SUB, blk // tf)
    nfc = blk // (sub * tf)
    w = HY_WIDTH
    seqs = _seqs_per_step(batch, seq_len)
    step_rows = seqs * seq_len
    return pl.pallas_call(
        functools.partial(_conv_kernel, seq_len=seq_len, blk=blk, conv_z=conv_z, seqs=seqs, sub=sub),
        grid=(batch // seqs, nfc),
        in_specs=[
            pl.BlockSpec((step_rows, w), lambda b, f: (b, 0)),
            pl.BlockSpec((step_rows, w), lambda b, f: (b, 0)),
            pl.BlockSpec((None, 3, w), lambda b, f: (layer, 0, z_col if conv_z else 0)),
            pl.BlockSpec((None, 1, w), lambda b, f: (layer, 0, z_col if conv_z else 0)),
            pl.BlockSpec((None, 3, w), lambda b, f: (layer, 0, g_col)),
            pl.BlockSpec((None, 1, w), lambda b, f: (layer, 0, g_col)),
            pl.BlockSpec((None, 1, w), lambda b, f: (layer * HY_ORDER + order, 0, 0)),
            pl.BlockSpec((None, sub, nlags, 2 * tf, w), lambda b, f: (order, f, 0, 0, 0)),
            pl.BlockSpec((sub * 2 * tf, blk), lambda b, f: (f, 0)),
            pl.BlockSpec((blk, 2 * blk), lambda b, f: (0, 0)),
        ],
        out_specs=pl.BlockSpec((step_rows, w), lambda b, f: (b, 0)),
        out_shape=jax.ShapeDtypeStruct((batch * seq_len, w), out_dtype),
        scratch_shapes=[
            pltpu.VMEM((step_rows, w), BF16),
            pltpu.VMEM((step_rows if conv_z else SUBLANES, w), F32),
            pltpu.VMEM((seqs * nb, 2 * blk, w), BF16),
        ],
        compiler_params=_cparams(("arbitrary", "arbitrary")),
        name="hyena_conv",
    )(z_src, g_src, short_w, short_b, short_w, short_b, bd, kf, fwd, inv)


def _dft_tables(blk):
    m = 2 * blk
    tf = FREQ_TILE
    nt = blk // tf
    q = 32
    f2 = 2 * jnp.arange(blk, dtype=jnp.int32) + 1

    def cos_sin(times):
        r = (f2[:, None] * times[None, :]) % (2 * m)
        ang = r.astype(F32) * (math.pi / m)
        return jnp.cos(ang), jnp.sin(ang)

    ch, sh = cos_sin(jnp.arange(0, blk, q, dtype=jnp.int32))
    cl, sl = cos_sin(jnp.arange(q, dtype=jnp.int32))
    rep = lambda a: jnp.repeat(a, q, axis=1)
    til = lambda a: jnp.tile(a, (1, blk // q))
    c = rep(ch) * til(cl) - rep(sh) * til(sl)
    s = rep(sh) * til(cl) + rep(ch) * til(sl)
    tiles = [slice(t * tf, (t + 1) * tf) for t in range(nt)]
    fwd = jnp.concatenate([part[t, :] for t in tiles for part in (c, -s)], axis=0).astype(BF16)
    inv = fwd.T * (2.0 / m)
    return fwd, inv.astype(BF16)


def _filter_embedding(seq_len):
    t = jnp.linspace(0.0, 1.0, seq_len, dtype=F32)[:, None]
    bands = (HY_EMB_DIM - 1) // 2
    f = jnp.linspace(1e-4, bands - 1, bands, dtype=F32)[None, :]
    w = 2.0 * math.pi * jnp.arange(seq_len, dtype=F32)[:, None] / seq_len
    z = jnp.concatenate([t, jnp.cos(f * w), -jnp.sin(f * w)], axis=-1)
    z_ext = jnp.concatenate([z, jnp.zeros((1, HY_EMB_DIM), F32), jnp.flip(z[1:], axis=0)], axis=0)
    return jnp.pad(z_ext, ((0, 0), (0, LANES - HY_EMB_DIM)))


def _rope_tables(seq_len):
    quarter = HEAD_DIM // 4
    inv = jnp.tile(ROPE_THETA ** (-jnp.arange(quarter, dtype=F32) / quarter), LANES // quarter)
    lane = jnp.arange(LANES)
    by_row = (lane % HEAD_DIM < HEAD_DIM // 2)[None, None, :]
    first = (lane % (2 * quarter) < quarter)[None, :]

    def cos_sin(npos):
        ang = jnp.arange(npos).astype(F32)[:, None] * inv[None, :]
        return jnp.cos(ang), jnp.sin(ang)

    (cr, sr), (cc, sc) = cos_sin(seq_len // GRID_W), cos_sin(GRID_W)
    lay = lambda r, c: jnp.where(by_row, r[:, None, :], c[None, :, :]).reshape(seq_len, LANES)
    cos, sin = lay(cr, cc), lay(sr, sc)
    return cos, jnp.where(first, -sin, 0.0), jnp.where(first, 0.0, sin)


def _block_ones(width):
    h = jnp.arange(width) // HEAD_DIM
    return (h[:, None] == h[None, :]).astype(BF16)


def _stream_layer(x, p, wts, mod_l, batch, seq_len, rope_tabs, ctx, layer, hy_blk, consts, cast_next=None, ada_next=None):
    gq, gk, fwd, inv, z_ext = consts
    latent = ctx is not None
    outs, next_wts, next_mod = _token_a(x, mod_l, p["norm"], wts, gq, gk, p["qg"], p["kg"], rope_tabs, seq_len, layer,
                                        cast_next, ada_next)
    x1, u_pool, q, kd, vd, hy_v, hy_x1, hy_x2 = outs[:8]
    k, v = (None, None) if latent else outs[8:]
    y_pool = _pool_mix(u_pool, p["pool_w"], p["pool_scale"], batch, seq_len, layer)
    if latent:
        a = _attention(q, kd, vd, p["sink"], batch, seq_len, ctx[0], ctx[1], layer)
    else:
        a = _attention(q, kd, vd, p["sink"], batch, seq_len, layer=layer)
    kraw, ksum = _hyena_filter(z_ext, p["filter"], seq_len, layer)
    kf = _filter_spectrum(kraw, ksum, fwd, seq_len, hy_blk)
    z1 = _hyena_conv(hy_v, 0, True, hy_x1, 1, p["short_w"], p["short_b"], p["hy_bias"], kf, 0, fwd, inv,
                     batch, seq_len, hy_blk, F32, layer)
    y_hy = _hyena_conv(z1, 0, False, hy_x2, 2, p["short_w"], p["short_b"], p["hy_bias"], kf, 1, fwd, inv,
                       batch, seq_len, hy_blk, BF16, layer)
    x3 = _token_b(x1, y_pool, a, y_hy, mod_l, p["norm"], wts, seq_len, layer, latent)
    return x3, k, v, next_wts, next_mod


def _layer_params(norm_w, pool_w, pool_scale, q_norm, k_norm, attn_sink, hy_short_w, hy_short_b, hy_f_w1, hy_f_b1,
                  hy_f_w2, hy_f_b2, hy_f_w3, hy_f_b3, hy_sin_freq, hy_decay, hy_bias):
    eye = jnp.eye(len(POOL_WINDOWS), dtype=F32)
    pool_bd = (eye[None, :, None, :, None] * pool_w[:, :, :, None, :]).reshape(DEPTH, POOL_WIDTH, POOL_WIDTH)
    return {
        "norm": norm_w,
        "pool_w": pool_bd.astype(BF16), "pool_scale": pool_scale[:, None, :],
        "qg": jnp.tile(q_norm, (1, N_HEADS))[:, None, :], "kg": jnp.tile(k_norm, (1, N_KV_HEADS))[:, None, :],
        "sink": attn_sink,
        "short_w": hy_short_w, "short_b": hy_short_b[:, None, :],
        "filter": _filter_weights(hy_f_w1, hy_f_b1, hy_f_w2, hy_f_b2, hy_f_w3, hy_f_b3, hy_sin_freq, hy_decay),
        "hy_bias": hy_bias.reshape(DEPTH * HY_ORDER, 1, HY_WIDTH),
    }


def kernel(x_prompt, x_sample, cache_k, cache_v, c, c_ctx, ada_w, ada_b, norm_w, ffn_wg, ffn_wu, ffn_wd, w_in, w_out, pool_w, pool_scale, q_norm, k_norm, attn_sink, hy_short_w, hy_short_b, hy_f_w1, hy_f_b1, hy_f_w2, hy_f_b2, hy_f_w3, hy_f_b3, hy_sin_freq, hy_decay, hy_bias):
    bp, lp, _ = x_prompt.shape
    bs, ls, _ = x_sample.shape
    lc = cache_k.shape[2]

    cond = jnp.concatenate([c_ctx[None, :], c, jnp.zeros((COND_ROWS - 1 - bs, D_MODEL), F32)], axis=0)
    ada_b3 = ada_b[:, None, :]
    as_rows = lambda m: m.reshape(COND_ROWS, 1, N_MOD * D_MODEL)
    mod_l = as_rows(_ada_mod(cond, ada_w, ada_b3, 0))

    gq, gk = _block_ones(ATTN_WIDTH), _block_ones(KV_WIDTH)
    rope_tabs = _rope_tables(ls)
    blk_p, blk_s = min(lp, HY_BLOCK), min(ls, HY_BLOCK)
    consts_p = (gq, gk) + _dft_tables(blk_p) + (_filter_embedding(lp),)
    consts_s = (gq, gk) + _dft_tables(blk_s) + (_filter_embedding(ls),)
    ctx_k = cache_k.reshape(bs, DEPTH, lc, KV_WIDTH)
    ctx_v = cache_v.reshape(bs, DEPTH, lc, KV_WIDTH)

    flat = _flat_weights(ffn_wg, ffn_wu, ffn_wd, w_in, w_out)
    wts = _cast_layer(flat, 0)
    yp = x_prompt.reshape(bp * lp, D_MODEL)
    ys = x_sample.reshape(bs * ls, D_MODEL)
    p = _layer_params(norm_w, pool_w, pool_scale, q_norm, k_norm, attn_sink, hy_short_w, hy_short_b, hy_f_w1, hy_f_b1,
                      hy_f_w2, hy_f_b2, hy_f_w3, hy_f_b3, hy_sin_freq, hy_decay, hy_bias)
    ks, vs = [], []
    for l in range(DEPTH):
        more = l + 1 < DEPTH
        ada_next = (cond, ada_w, ada_b3, l + 1) if more else None
        yp, k_l, v_l, _, next_mod = _stream_layer(yp, p, wts, mod_l, bp, lp, None, None, l, blk_p, consts_p, None, ada_next)
        ks.append(k_l.reshape(bp, lp, N_KV_HEADS, HEAD_DIM))
        vs.append(v_l.reshape(bp, lp, N_KV_HEADS, HEAD_DIM))
        cast_next = (flat, l + 1) if more else None
        ys, _, _, wts, _ = _stream_layer(ys, p, wts, mod_l, bs, ls, rope_tabs, (ctx_k, ctx_v), l, blk_s, consts_s, cast_next)
        mod_l = as_rows(next_mod) if more else None
    return (yp.reshape(bp, lp, D_MODEL), ys.reshape(bs, ls, D_MODEL),
            jnp.stack(ks, axis=1), jnp.stack(vs, axis=1))
```

```python
import functools
import math

import jax
import jax.numpy as jnp
from jax import lax
from jax.experimental import pallas as pl
from jax.experimental.pallas import tpu as pltpu

F32 = jnp.float32
BF16 = jnp.bfloat16

D_MODEL = 1024
DEPTH = 2
GRID_W = 64
POOL_WINDOWS = (2, 4, 8, 16)
POOL_WIDTH = 256
POOL_GROUP = 64
HEAD_DIM = 64
N_HEADS = 8
N_KV_HEADS = 2
GQA_GROUP = 4
ATTN_WIDTH = 512
KV_WIDTH = 128
WINDOW = 128
BLOCK = 128
ROPE_THETA = 10000.0
HY_WIDTH = 256
HY_ORDER = 2
HY_EMB_DIM = 33
HY_FILTER_HIDDEN = 64
HY_MOD_SHIFT = 0.05
D_FF = 2816
IN_WIDTH = 1792
N_MOD = 9
NORM_EPS = 1e-6
NEG_INF = -1e30

LANES = 128
SUBLANES = 8
VMEM_LIMIT = 56 * 1024 * 1024

TOKEN_TILE = 512
TOKEN_TILE_B = 1024
MXU_DIM = 256
FF_CHUNKS = (1280, 1536)
assert sum(FF_CHUNKS) == D_FF and all(w % MXU_DIM == 0 for w in FF_CHUNKS)
SEQ_CHUNK = 512
SEQ_STEP_ROWS = 2048
HY_BLOCK = 1024
FREQ_TILE = 128
FREQ_SUB = 2
MAC_ROWS = 32
ATTN_SUB = 8
CTX_SEQS = 4
COND_ROWS = 8


def _cparams(sem):
    return pltpu.CompilerParams(dimension_semantics=sem, vmem_limit_bytes=VMEM_LIMIT)


def _dot(a, b):
    return jnp.dot(a, b, preferred_element_type=F32)


def _layer_block(tail, layer):
    return pl.BlockSpec((None,) + tuple(tail), lambda *_: (layer,) + (0,) * len(tail))


def _cast_kernel(*refs):
    n = len(refs) // 2
    for i_ref, o_ref in zip(refs[:n], refs[n:]):
        o_ref[...] = i_ref[...].astype(o_ref.dtype)


def _flat_weights(ffn_wg, ffn_wu, ffn_wd, w_in, w_out):
    arrs = {"wg": ffn_wg, "wu": ffn_wu, "wd": ffn_wd, "w_in": w_in, "w_out": w_out}
    return {k: a.reshape(-1, a.shape[-1]) for k, a in arrs.items()}


def _layer_cast_specs(flat, layer, steps):
    in_specs, out_specs, out_shapes = [], [], []
    for a in flat.values():
        rows = a.shape[0] // DEPTH
        blk = rows // steps
        in_specs.append(pl.BlockSpec((blk, a.shape[1]), lambda i, first=layer * steps: (first + i, 0)))
        out_specs.append(pl.BlockSpec((blk, a.shape[1]), lambda i: (i, 0)))
        out_shapes.append(jax.ShapeDtypeStruct((rows, a.shape[1]), BF16))
    return in_specs, out_specs, out_shapes


def _layer_weights(outs):
    wg, wu, wd, w_in, w_out = outs
    return {"wg": wg.reshape(2, D_MODEL, D_FF), "wu": wu.reshape(2, D_MODEL, D_FF),
            "wd": wd.reshape(2, D_FF, D_MODEL), "w_in": w_in, "w_out": w_out}


def _cast_layer(flat, layer):
    in_specs, out_specs, out_shapes = _layer_cast_specs(flat, layer, steps=16)
    outs = pl.pallas_call(
        _cast_kernel,
        grid=(16,),
        in_specs=in_specs,
        out_specs=out_specs,
        out_shape=out_shapes,
        compiler_params=_cparams(("arbitrary",)),
        name="cast_weights",
    )(*flat.values())
    return _layer_weights(outs)


def _ada_kernel(c_ref, w_ref, b_ref, o_ref):
    c = c_ref[...]
    s = (c * jax.nn.sigmoid(c)).astype(BF16)
    o_ref[...] = _dot(s, w_ref[...].astype(BF16)) + b_ref[...]


def _ada_mod(cond, ada_w, ada_b):
    tn = 3072
    nw = N_MOD * D_MODEL
    return pl.pallas_call(
        _ada_kernel,
        grid=(DEPTH, nw // tn),
        in_specs=[
            pl.BlockSpec((COND_ROWS, D_MODEL), lambda l, j: (0, 0)),
            pl.BlockSpec((None, D_MODEL, tn), lambda l, j: (l, 0, j)),
            pl.BlockSpec((None, 1, tn), lambda l, j: (l, 0, j)),
        ],
        out_specs=pl.BlockSpec((None, COND_ROWS, tn), lambda l, j: (l, 0, j)),
        out_shape=jax.ShapeDtypeStruct((DEPTH, COND_ROWS, nw), F32),
        compiler_params=_cparams(("arbitrary", "arbitrary")),
        name="ada_mod",
    )(cond, ada_w, ada_b.reshape(DEPTH, 1, nw))


def _mod_slice(mod_ref, k):
    return mod_ref[:, k * D_MODEL:(k + 1) * D_MODEL]


def _rms_mod(x, gain, scale, shift):
    y = x * lax.rsqrt(jnp.mean(x * x, axis=-1, keepdims=True) + NORM_EPS)
    return (y * gain) * (1.0 + scale) + shift


def _swiglu(hb, wg_ref, wu_ref, wd_ref):
    acc = None
    lo = 0
    for width in FF_CHUNKS:
        sl = slice(lo, lo + width)
        lo += width
        g = _dot(hb, wg_ref[:, sl])
        u = _dot(hb, wu_ref[:, sl])
        a = ((g * jax.nn.sigmoid(g)) * u).astype(BF16)
        y = _dot(a, wd_ref[sl, :])
        acc = y if acc is None else acc + y
    return acc


def _head_norm(x, gmat, gain):
    ss = _dot((x * x).astype(BF16), gmat)
    return (x * lax.rsqrt(ss * (1.0 / HEAD_DIM) + NORM_EPS)) * gain


def _rope(x, cos, sa, sb):
    w = x.shape[1]
    xn = pltpu.roll(x, w - 16, axis=1)
    xp = pltpu.roll(x, 16, axis=1)
    return x * cos + xn * sa + xp * sb


def _token_a_kernel(*refs, rope, n_cast):
    n_in = 11 + (3 if rope else 0) + n_cast
    ins, outs = refs[:n_in], refs[n_in:]
    x_ref, mod_ref, nrm_ref, wg_ref, wu_ref, wd_ref, win_ref, gq_ref, gk_ref, qg_ref, kg_ref = ins[:11]
    if rope:
        cos_ref, sa_ref, sb_ref = ins[11:14]
    x1_ref, up_ref, q_ref, kd_ref, vd_ref, hv_ref, hx1_ref, hx2_ref = outs[:8]
    if not rope:
        k_ref, v_ref = outs[8:10]
    for i_ref, o_ref in zip(ins[n_in - n_cast:], outs[len(outs) - n_cast:]):
        o_ref[...] = i_ref[...].astype(o_ref.dtype)
    x = x_ref[...]
    sh1, sc1, g1 = _mod_slice(mod_ref, 0), _mod_slice(mod_ref, 1), _mod_slice(mod_ref, 2)
    sh2, sc2 = _mod_slice(mod_ref, 3), _mod_slice(mod_ref, 4)
    h = _rms_mod(x, nrm_ref[0:1, :], sc1, sh1).astype(BF16)
    x1 = x + (0.5 * g1) * _swiglu(h, wg_ref, wu_ref, wd_ref)
    x1_ref[...] = x1
    h2 = _rms_mod(x1, nrm_ref[1:2, :], sc2, sh2).astype(BF16)
    s1 = POOL_WIDTH
    s2 = s1 + ATTN_WIDTH
    s3 = s2 + KV_WIDTH
    s4 = s3 + KV_WIDTH
    qkv = _dot(h2, win_ref[:, s1:s4])
    q = _head_norm(qkv[:, :ATTN_WIDTH], gq_ref[...], qg_ref[...])
    k = _head_norm(qkv[:, ATTN_WIDTH:ATTN_WIDTH + KV_WIDTH], gk_ref[...], kg_ref[...])
    v = qkv[:, ATTN_WIDTH + KV_WIDTH:]
    up_ref[...] = _dot(h2, win_ref[:, :s1])
    hy = _dot(h2, win_ref[:, s4:])
    for j, ref in enumerate((hv_ref, hx1_ref, hx2_ref)):
        ref[...] = hy[:, j * HY_WIDTH:(j + 1) * HY_WIDTH]
    if rope:
        cos, sa, sb = cos_ref[...], sa_ref[...], sb_ref[...]
        reps = ATTN_WIDTH // LANES
        q = _rope(q, jnp.concatenate([cos] * reps, axis=1), jnp.concatenate([sa] * reps, axis=1),
                  jnp.concatenate([sb] * reps, axis=1))
        k = _rope(k, cos, sa, sb)
    q_ref[...] = (q * (HEAD_DIM ** -0.5 * math.log2(math.e))).astype(BF16)
    kd_ref[...] = _dup_heads(k)
    vd_ref[...] = _dup_heads(v)
    if not rope:
        k_ref[...] = k
        v_ref[...] = v


def _token_a(x, mod, nrm, wts, gq, gk, qg, kg, rope_tabs, seq_len, layer, cast_next=None):
    n = x.shape[0]
    tm = TOKEN_TILE
    tiles_per_seq = max(seq_len // tm, 1)
    rope = rope_tabs is not None
    if rope:
        mod_row = lambda i: (layer, 1 + i // tiles_per_seq, 0, 0)
    else:
        mod_row = lambda i: (layer, 0, 0, 0)
    const = lambda i: (0, 0)
    row = lambda i: (i, 0)
    in_specs = [
        pl.BlockSpec((tm, D_MODEL), row),
        pl.BlockSpec((None, None, 1, N_MOD * D_MODEL), mod_row),
        _layer_block((3, D_MODEL), layer),
        pl.BlockSpec((None, D_MODEL, D_FF), lambda i: (0, 0, 0)),
        pl.BlockSpec((None, D_MODEL, D_FF), lambda i: (0, 0, 0)),
        pl.BlockSpec((None, D_FF, D_MODEL), lambda i: (0, 0, 0)),
        pl.BlockSpec((D_MODEL, IN_WIDTH), const),
        pl.BlockSpec((ATTN_WIDTH, ATTN_WIDTH), const),
        pl.BlockSpec((KV_WIDTH, KV_WIDTH), const),
        _layer_block((1, ATTN_WIDTH), layer),
        _layer_block((1, KV_WIDTH), layer),
    ]
    args = [x, mod, nrm, wts["wg"], wts["wu"], wts["wd"], wts["w_in"], gq, gk, qg, kg]
    if rope:
        tab = pl.BlockSpec((tm, LANES), lambda i: (i % tiles_per_seq, 0))
        in_specs += [tab, tab, tab]
        args += list(rope_tabs)
    widths = (D_MODEL, POOL_WIDTH, ATTN_WIDTH, 2 * KV_WIDTH, 2 * KV_WIDTH) + (HY_WIDTH,) * (HY_ORDER + 1)
    dtypes = (F32, F32, BF16, BF16, BF16) + (F32,) * (HY_ORDER + 1)
    if not rope:
        widths += (KV_WIDTH, KV_WIDTH)
        dtypes += (F32, F32)
    out_specs = [pl.BlockSpec((tm, w), row) for w in widths]
    out_shapes = [jax.ShapeDtypeStruct((n, w), d) for w, d in zip(widths, dtypes)]
    n_cast = 0
    if cast_next is not None:
        flat, next_layer = cast_next
        c_in, c_out, c_shapes = _layer_cast_specs(flat, next_layer, steps=n // tm)
        in_specs += c_in
        args += list(flat.values())
        out_specs += c_out
        out_shapes += c_shapes
        n_cast = len(c_in)
    outs = pl.pallas_call(
        functools.partial(_token_a_kernel, rope=rope, n_cast=n_cast),
        grid=(n // tm,),
        in_specs=in_specs,
        out_specs=out_specs,
        out_shape=out_shapes,
        compiler_params=_cparams(("arbitrary",)),
        name="token_a_rope" if rope else "token_a",
    )(*args)
    if n_cast:
        return outs[:len(outs) - n_cast], _layer_weights(outs[len(outs) - n_cast:])
    return outs, None


def _token_b_kernel(x_ref, yp_ref, a_ref, yh_ref, mod_ref, nrm_ref, wo_ref, wg_ref, wu_ref, wd_ref, o_ref):
    x1 = x_ref[...]
    g2 = _mod_slice(mod_ref, 5)
    sh3, sc3, g3 = _mod_slice(mod_ref, 6), _mod_slice(mod_ref, 7), _mod_slice(mod_ref, 8)
    cat = jnp.concatenate([yp_ref[...], a_ref[...], yh_ref[...]], axis=1)
    x2 = x1 + g2 * _dot(cat, wo_ref[...])
    h3 = _rms_mod(x2, nrm_ref[2:3, :], sc3, sh3).astype(BF16)
    o_ref[...] = x2 + (0.5 * g3) * _swiglu(h3, wg_ref, wu_ref, wd_ref)


def _token_b(x1, y_pool, a, y_hy, mod, nrm, wts, seq_len, layer, per_seq_cond):
    n = x1.shape[0]
    tm = min(TOKEN_TILE_B, n)
    tiles_per_seq = max(seq_len // tm, 1)
    if per_seq_cond:
        mod_row = lambda i: (layer, 1 + i // tiles_per_seq, 0, 0)
    else:
        mod_row = lambda i: (layer, 0, 0, 0)
    const = lambda i: (0, 0)
    row = lambda i: (i, 0)
    return pl.pallas_call(
        _token_b_kernel,
        grid=(n // tm,),
        in_specs=[
            pl.BlockSpec((tm, D_MODEL), row),
            pl.BlockSpec((tm, POOL_WIDTH), row),
            pl.BlockSpec((tm, ATTN_WIDTH), row),
            pl.BlockSpec((tm, HY_WIDTH), row),
            pl.BlockSpec((None, None, 1, N_MOD * D_MODEL), mod_row),
            _layer_block((3, D_MODEL), layer),
            pl.BlockSpec((D_MODEL, D_MODEL), const),
            pl.BlockSpec((None, D_MODEL, D_FF), lambda i: (1, 0, 0)),
            pl.BlockSpec((None, D_MODEL, D_FF), lambda i: (1, 0, 0)),
            pl.BlockSpec((None, D_FF, D_MODEL), lambda i: (1, 0, 0)),
        ],
        out_specs=pl.BlockSpec((tm, D_MODEL), row),
        out_shape=jax.ShapeDtypeStruct((n, D_MODEL), F32),
        compiler_params=_cparams(("arbitrary",)),
        name="token_b",
    )(x1, y_pool, a, y_hy, mod, nrm, wts["w_out"], wts["wg"], wts["wu"], wts["wd"])


def _seqs_per_step(batch, seq_len):
    return min(batch, max(1, SEQ_STEP_ROWS // seq_len))


def _halo_rows(src_ref, base, r0, rows, seq_len):
    c = src_ref.shape[1]
    zero = jnp.zeros((SUBLANES, c), F32)
    lo = base + r0
    prev = src_ref[lo - SUBLANES:lo, :] if r0 > 0 else zero
    nxt = src_ref[lo + rows:lo + rows + SUBLANES, :] if r0 + rows < seq_len else zero
    return prev, nxt


def _short_conv_chunk(src_ref, base, r0, rows, seq_len, w, b):
    x = src_ref[base + r0:base + r0 + rows, :]
    prev, nxt = _halo_rows(src_ref, base, r0, rows, seq_len)
    ridx = lax.broadcasted_iota(jnp.int32, x.shape, 0)
    xp = jnp.where(ridx == 0, prev[SUBLANES - 1:SUBLANES, :], pltpu.roll(x, 1, axis=0))
    xn = jnp.where(ridx == rows - 1, nxt[0:1, :], pltpu.roll(x, rows - 1, axis=0))
    return xp * w[0:1, :] + x * w[1:2, :] + xn * w[2:3, :] + b


def _pool_kernel(u_ref, w_ref, scale_ref, o_ref, *, seq_len, rows, seqs):
    lane = lax.broadcasted_iota(jnp.int32, (rows, POOL_WIDTH), 1)
    grp = lane // POOL_GROUP
    half = jnp.where(grp == 0, 1, jnp.where(grp == 1, 2, jnp.where(grp == 2, 4, 8)))
    odd_grp = lax.broadcasted_iota(jnp.int32, (rows, LANES), 1) >= POOL_GROUP
    ext = rows + 2 * SUBLANES
    back = lambda v, s: pltpu.roll(v, s, axis=0)
    fwd = lambda v, s: pltpu.roll(v, ext - s, axis=0)
    core = lambda v: v[SUBLANES:SUBLANES + rows, :]
    for base, r0 in [(sq * seq_len, r0) for sq in range(seqs) for r0 in range(0, seq_len, rows)]:
        x = u_ref[base + r0:base + r0 + rows, :]
        prev, nxt = _halo_rows(u_ref, base, r0, rows, seq_len)
        a = jnp.concatenate([prev, x, nxt], axis=0)
        lo, hi = a[:, :LANES], a[:, LANES:]
        b1 = back(lo, 1)
        b2 = b1 + back(b1, 1)
        f2 = lo + fwd(lo, 1)
        sum_lo = jnp.where(odd_grp, core(b2) + core(f2), core(b1) + core(lo))
        c1 = back(hi, 1)
        c2 = c1 + back(c1, 1)
        c4 = c2 + back(c2, 2)
        c8 = c4 + back(c4, 4)
        g2 = hi + fwd(hi, 1)
        g4 = g2 + fwd(g2, 2)
        g8 = g4 + fwd(g4, 4)
        sum_hi = jnp.where(odd_grp, core(c8) + core(g8), core(c4) + core(g4))
        wsum = jnp.concatenate([sum_lo, sum_hi], axis=1)
        t = r0 + lax.broadcasted_iota(jnp.int32, (rows, POOL_WIDTH), 0)
        cnt = jnp.minimum(t + half, seq_len) - jnp.maximum(t - half, 0)
        d = wsum / cnt.astype(F32) - x
        y = _dot(d.astype(BF16), w_ref[...]) * scale_ref[...]
        o_ref[base + r0:base + r0 + rows, :] = y.astype(o_ref.dtype)


def _pool_mix(u_pool, w_bd, scale, batch, seq_len, layer):
    rows = min(SEQ_CHUNK // 2, seq_len)
    seqs = _seqs_per_step(batch, seq_len)
    return pl.pallas_call(
        functools.partial(_pool_kernel, seq_len=seq_len, rows=rows, seqs=seqs),
        grid=(batch // seqs,),
        in_specs=[
            pl.BlockSpec((seqs * seq_len, POOL_WIDTH), lambda b: (b, 0)),
            _layer_block((POOL_WIDTH, POOL_WIDTH), layer),
            _layer_block((1, POOL_WIDTH), layer),
        ],
        out_specs=pl.BlockSpec((seqs * seq_len, POOL_WIDTH), lambda b: (b, 0)),
        out_shape=jax.ShapeDtypeStruct((batch * seq_len, POOL_WIDTH), BF16),
        compiler_params=_cparams(("arbitrary",)),
        name="pool_mix",
    )(u_pool, w_bd, scale)


def _dup_heads(x):
    lane = lax.broadcasted_iota(jnp.int32, x.shape, 1)
    sw = pltpu.roll(x, HEAD_DIM, axis=1)
    lo = lane < HEAD_DIM
    return jnp.concatenate([jnp.where(lo, x, sw), jnp.where(lo, sw, x)], axis=1).astype(BF16)


def _attn_kernel(*refs, has_local, nblocks, sub, seqs, layer):
    if has_local:
        sink_ref, q_ref, kp_ref, kc_ref, kn_ref, vp_ref, vc_ref, vn_ref, ck_ref, cv_ref, o_ref = refs
        kwin = jnp.concatenate([kp_ref[...], kc_ref[...], kn_ref[...]], axis=0)
        vwin = jnp.concatenate([vp_ref[...], vc_ref[...], vn_ref[...]], axis=0)
        kctx, vctx = _dup_heads(ck_ref[...]), _dup_heads(cv_ref[...])
        jj = lax.broadcasted_iota(jnp.int32, (BLOCK, GQA_GROUP * BLOCK), 0)
        rr = lax.broadcasted_iota(jnp.int32, (BLOCK, GQA_GROUP * BLOCK), 1) % BLOCK
    else:
        sink_ref, q_ref, kd_ref, vd_ref, o_ref = refs
    i = pl.program_id(1)
    lane_q = lax.broadcasted_iota(jnp.int32, (BLOCK, LANES), 1)
    head_of_col = lax.broadcasted_iota(jnp.int32, (1, GQA_GROUP * BLOCK), 1) // BLOCK
    log2e = math.log2(math.e)

    def values_t(v2):
        vt = jnp.transpose(v2.astype(F32))
        row = lax.broadcasted_iota(jnp.int32, vt.shape, 0)
        return jnp.where(row < HEAD_DIM, vt, 1.0).astype(BF16)

    def sink_row(kvh):
        row = jnp.full((1, GQA_GROUP * BLOCK), sink_ref[layer, kvh * GQA_GROUP + GQA_GROUP - 1] * log2e, F32)
        for g in range(GQA_GROUP - 1):
            row = jnp.where(head_of_col == g, sink_ref[layer, kvh * GQA_GROUP + g] * log2e, row)
        return row

    if has_local:
        vt_win = [values_t(vwin[:, kvh * LANES:(kvh + 1) * LANES]) for kvh in range(N_KV_HEADS)]
        vt_ctx = [values_t(vctx[:, kvh * LANES:(kvh + 1) * LANES]) for kvh in range(N_KV_HEADS)]
    else:
        vt_own = [values_t(vd_ref[:, kvh * LANES:(kvh + 1) * LANES]) for kvh in range(N_KV_HEADS)]
    units = [(sb, kvh) for sb in range(seqs * sub) for kvh in range(N_KV_HEADS)]
    scores, values = [], []
    for sb, kvh in units:
        q = q_ref[sb * BLOCK:(sb + 1) * BLOCK, :]
        qparts = []
        for hd in range(kvh * GQA_GROUP, (kvh + 1) * GQA_GROUP):
            qp = q[:, (hd // 2) * LANES:(hd // 2 + 1) * LANES]
            keep = (lane_q < HEAD_DIM) == (hd % 2 == 0)
            qparts.append(jnp.where(keep, qp, jnp.zeros_like(qp)))
        qs = jnp.concatenate(qparts, axis=0)
        cols = slice(kvh * LANES, (kvh + 1) * LANES)
        if has_local:
            loc = slice(sb * BLOCK, (sb + 3) * BLOCK)
            kk = jnp.concatenate([kwin[loc, cols], kctx[:, cols]], axis=0)
            vt = jnp.concatenate([vt_win[kvh][:, loc], vt_ctx[kvh]], axis=1)
        else:
            own = slice((sb // sub) * sub * BLOCK, (sb // sub + 1) * sub * BLOCK)
            kk = kd_ref[own, cols]
            vt = vt_own[kvh][:, own]
        s = lax.dot_general(kk, qs, (((1,), (1,)), ((), ())), preferred_element_type=F32)
        if has_local:
            gb = i * sub + sb
            below = jj >= rr + jnp.where(gb >= 1, 0, BLOCK)
            above = jj <= rr - jnp.where(gb <= nblocks - 2, 0, BLOCK)
            s = jnp.concatenate([jnp.where(below, s[:BLOCK, :], NEG_INF), s[BLOCK:2 * BLOCK, :],
                                 jnp.where(above, s[2 * BLOCK:3 * BLOCK, :], NEG_INF), s[3 * BLOCK:, :]], axis=0)
        scores.append(s)
        values.append(vt)

    sinks = [sink_row(kvh) for kvh in range(N_KV_HEADS)]
    maxes = [jnp.maximum(jnp.max(s, axis=0, keepdims=True), sinks[kvh]) for (_, kvh), s in zip(units, scores)]
    probs = [jnp.exp2(s - m).astype(BF16) for s, m in zip(scores, maxes)]
    sums = [_dot(vt, e) for vt, e in zip(values, probs)]
    outs = []
    for (_, kvh), ow, m in zip(units, sums, maxes):
        den = ow[HEAD_DIM:, :] + jnp.exp2(sinks[kvh] - m)
        outs.append(ow[:HEAD_DIM, :] / den)
    for sb in range(seqs * sub):
        blks = []
        for kvh in range(N_KV_HEADS):
            o = outs[sb * N_KV_HEADS + kvh]
            for p in range(GQA_GROUP // 2):
                pair = jnp.concatenate([o[:, 2 * p * BLOCK:(2 * p + 1) * BLOCK], o[:, (2 * p + 1) * BLOCK:(2 * p + 2) * BLOCK]],
                                       axis=0)
                blks.append(jnp.transpose(pair))
        o_ref[sb * BLOCK:(sb + 1) * BLOCK, :] = jnp.concatenate(blks, axis=1).astype(o_ref.dtype)


def _attention(q, kd, vd, sink, batch, seq_len, ctx_k=None, ctx_v=None, layer=0):
    assert WINDOW == BLOCK
    nb = seq_len // BLOCK
    has_local = ctx_k is not None
    sub = min(ATTN_SUB, nb)
    steps = nb // sub
    seqs = 1
    if not has_local:
        assert steps == 1
        seqs = math.gcd(CTX_SEQS, batch)
        batch //= seqs
    qb = seqs * sub * BLOCK
    qspec = pl.BlockSpec((qb, ATTN_WIDTH), lambda b, i: (b * steps + i, 0))
    sspec = pl.BlockSpec(memory_space=pltpu.SMEM)
    if has_local:
        lc = ctx_k.shape[2]
        edge = lambda f: pl.BlockSpec((BLOCK, 2 * KV_WIDTH), f)
        prev = lambda b, i: (b * nb + jnp.maximum(i * sub - 1, 0), 0)
        nxt = lambda b, i: (b * nb + jnp.minimum((i + 1) * sub, nb - 1), 0)
        cur = pl.BlockSpec((qb, 2 * KV_WIDTH), lambda b, i: (b * steps + i, 0))
        cspec = pl.BlockSpec((None, None, lc, KV_WIDTH), lambda b, i: (b, layer, 0, 0))
        in_specs = [sspec, qspec, edge(prev), cur, edge(nxt), edge(prev), cur, edge(nxt), cspec, cspec]
        args = [sink, q, kd, kd, kd, vd, vd, vd, ctx_k, ctx_v]
    else:
        kv = pl.BlockSpec((qb, 2 * KV_WIDTH), lambda b, i: (b, 0))
        in_specs = [sspec, qspec, kv, kv]
        args = [sink, q, kd, vd]
    return pl.pallas_call(
        functools.partial(_attn_kernel, has_local=has_local, nblocks=nb, sub=sub, seqs=seqs, layer=layer),
        grid=(batch, steps),
        in_specs=in_specs,
        out_specs=pl.BlockSpec((qb, ATTN_WIDTH), lambda b, i: (b * steps + i, 0)),
        out_shape=jax.ShapeDtypeStruct(q.shape, BF16),
        compiler_params=_cparams(("arbitrary", "arbitrary")),
        name="attn_latent" if has_local else "attn_context",
    )(*args)


def _filter_kernel(zt_ref, zb_ref, w1_ref, b1_ref, w2_ref, b2_ref, w3_ref, b3_ref, fr_ref, dl_ref, k_ref, sum_ref,
                   *, seq_len, rows):
    i = pl.program_id(0)
    half = rows // 2
    oc = HY_ORDER * HY_WIDTH
    z = jnp.concatenate([zt_ref[...], zb_ref[...]], axis=1)
    h = jnp.sin(fr_ref[0:1, :] * (_dot(z.astype(BF16), w1_ref[...].astype(BF16)) + b1_ref[...]))
    h = jnp.sin(fr_ref[1:2, :] * (_dot(h.astype(BF16), w2_ref[...].astype(BF16)) + b2_ref[...]))
    h3 = _dot(h.astype(BF16), w3_ref[...].astype(BF16)) + b3_ref[...]
    total = jnp.zeros((1, oc), F32)
    for part, z_ref in enumerate((zt_ref, zb_ref)):
        row = i * rows + part * half + lax.broadcasted_iota(jnp.int32, (half, oc), 0)
        t = z_ref[:, 0:1]
        decay = jnp.exp(-t * jnp.abs(dl_ref[...]))
        fwd_dir = h3[:, part * 2 * oc:part * 2 * oc + oc]
        bwd_dir = h3[:, part * 2 * oc + oc:(part + 1) * 2 * oc]
        sel = jnp.where(row < seq_len, fwd_dir, jnp.where(row > seq_len, bwd_dir, 0.0))
        k = sel * (decay + HY_MOD_SHIFT)
        k_ref[part * half:(part + 1) * half, :] = k
        total = total + jnp.sum(jnp.abs(k), axis=0, keepdims=True)

    @pl.when(i == 0)
    def _():
        sum_ref[...] = jnp.zeros_like(sum_ref)

    sum_ref[...] += total


def _block_diag2(w):
    z = jnp.zeros_like(w)
    return jnp.concatenate([jnp.concatenate([w, z], axis=-1), jnp.concatenate([z, w], axis=-1)], axis=-2)


def _filter_weights(w1, b1, w2, b2, w3, b3, freq, deltas):
    two = lambda a: jnp.concatenate([a, a], axis=-1)
    w1p = jnp.pad(w1, ((0, 0), (0, LANES - HY_EMB_DIM), (0, 0)))
    return (_block_diag2(w1p), two(b1)[:, None, :], _block_diag2(w2), two(b2)[:, None, :],
            _block_diag2(w3), two(b3)[:, None, :], two(freq), deltas.reshape(DEPTH, 1, HY_ORDER * HY_WIDTH))


def _hyena_filter(z_ext, fw, seq_len, layer):
    n = 2 * seq_len
    rows = min(1024, n)
    half = rows // 2
    oc = HY_ORDER * HY_WIDTH
    hid = 2 * HY_FILTER_HIDDEN
    const = lambda i: (0, 0)
    return pl.pallas_call(
        functools.partial(_filter_kernel, seq_len=seq_len, rows=rows),
        grid=(n // rows,),
        in_specs=[
            pl.BlockSpec((half, LANES), lambda i: (2 * i, 0)),
            pl.BlockSpec((half, LANES), lambda i: (2 * i + 1, 0)),
            _layer_block((2 * LANES, hid), layer),
            _layer_block((1, hid), layer),
            _layer_block((hid, hid), layer),
            _layer_block((1, hid), layer),
            _layer_block((hid, 4 * oc), layer),
            _layer_block((1, 4 * oc), layer),
            _layer_block((2, hid), layer),
            _layer_block((1, oc), layer),
        ],
        out_specs=[pl.BlockSpec((rows, oc), lambda i: (i, 0)), pl.BlockSpec((1, oc), const)],
        out_shape=[jax.ShapeDtypeStruct((n, oc), F32), jax.ShapeDtypeStruct((1, oc), F32)],
        compiler_params=_cparams(("arbitrary",)),
        name="hyena_filter",
    )(z_ext, z_ext, *fw)


def _spectrum_kernel(k_ref, sum_ref, fwd_ref, kf_ref, hprev_ref, g0_ref, *, blk):
    t = pl.program_id(0)
    a = k_ref[...] / (sum_ref[...] + 1e-6)
    ha = _dot(fwd_ref[...], a.astype(BF16))
    tf = FREQ_TILE
    par = lax.broadcasted_iota(jnp.int32, (tf, 1), 0) % 2
    sgn = (1 - 2 * par).astype(F32)

    @pl.when(t > 0)
    def _():
        g0 = g0_ref[...]
        for c in range(blk // tf):
            re = slice(2 * c * tf, (2 * c + 1) * tf)
            im = slice((2 * c + 1) * tf, (2 * c + 2) * tf)
            kre = ha[re, :] - sgn * hprev_ref[im, :]
            kim = ha[im, :] + sgn * (hprev_ref[re, :] - g0)
            for o in range(HY_ORDER):
                lanes = slice(o * HY_WIDTH, (o + 1) * HY_WIDTH)
                kf_ref[o, c, 0:tf, :] = kre[:, lanes]
                kf_ref[o, c, tf:2 * tf, :] = kim[:, lanes]

    hprev_ref[...] = ha
    g0_ref[...] = a[0:1, :]


def _filter_spectrum(kraw, ksum, fwd, seq_len, blk):
    nb = seq_len // blk
    nlags = 2 * nb - 1
    oc = HY_ORDER * HY_WIDTH
    m = 2 * blk
    tf = FREQ_TILE
    nfc = blk // tf
    return pl.pallas_call(
        functools.partial(_spectrum_kernel, blk=blk),
        grid=(2 * nb,),
        in_specs=[
            pl.BlockSpec((blk, oc), lambda t: ((nb + t) % (2 * nb), 0)),
            pl.BlockSpec((1, oc), lambda t: (0, 0)),
            pl.BlockSpec((m, blk), lambda t: (0, 0)),
        ],
        out_specs=pl.BlockSpec((HY_ORDER, nfc, None, 2 * tf, HY_WIDTH), lambda t: (0, 0, jnp.maximum(t - 1, 0), 0, 0)),
        out_shape=jax.ShapeDtypeStruct((HY_ORDER, nfc, nlags, 2 * tf, HY_WIDTH), F32),
        scratch_shapes=[pltpu.VMEM((m, oc), F32), pltpu.VMEM((1, oc), F32)],
        compiler_params=_cparams(("arbitrary",)),
        name="filter_spectrum",
    )(kraw, ksum, fwd)


def _conv_kernel(z_ref, g_ref, swz_ref, sbz_ref, swg_ref, sbg_ref, bd_ref, kf_ref, fwd_ref, inv_ref, o_ref,
                 zb_ref, zc_ref, yf_ref, *, seq_len, blk, conv_z, seqs, sub):
    fc = pl.program_id(1)
    nfc = pl.num_programs(1)
    nb = seq_len // blk
    tf = FREQ_TILE
    blocks = [((g // nb) * seq_len, (g % nb) * blk) for g in range(seqs * nb)]

    @pl.when(fc == 0)
    def _():
        for base, r0 in blocks:
            rows = slice(base + r0, base + r0 + blk)
            if conv_z:
                zc_ref[rows, :] = _short_conv_chunk(z_ref, base, r0, blk, seq_len, swz_ref[...], sbz_ref[...])
                zb_ref[rows, :] = zc_ref[rows, :].astype(BF16)
            else:
                zb_ref[rows, :] = z_ref[rows, :].astype(BF16)

    zf = [[_dot(fwd_ref[s * 2 * tf:(s + 1) * 2 * tf, :], zb_ref[g * blk:(g + 1) * blk, :]) for g in range(seqs * nb)]
          for s in range(sub)]

    for s in range(sub):
        col = pl.multiple_of((fc * sub + s) * 2 * tf, 2 * tf)
        for g0 in range(0, seqs * nb, nb):
            for bi in range(nb):
                for r in range(0, tf, MAC_ROWS):
                    re = slice(r, r + MAC_ROWS)
                    im = slice(tf + r, tf + r + MAC_ROWS)
                    yr = jnp.zeros((MAC_ROWS, HY_WIDTH), F32)
                    yi = jnp.zeros((MAC_ROWS, HY_WIDTH), F32)
                    for bj in range(nb):
                        lag = bi - bj + nb - 1
                        kr, ki = kf_ref[s, lag, re, :], kf_ref[s, lag, im, :]
                        zr, zi = zf[s][g0 + bj][re, :], zf[s][g0 + bj][im, :]
                        yr = yr + (kr * zr - ki * zi)
                        yi = yi + (kr * zi + ki * zr)
                    yf_ref[g0 + bi, pl.ds(col + r, MAC_ROWS), :] = yr.astype(BF16)
                    yf_ref[g0 + bi, pl.ds(col + tf + r, MAC_ROWS), :] = yi.astype(BF16)

    @pl.when(fc == nfc - 1)
    def _():
        ys, gates = [], []
        for g, (base, r0) in enumerate(blocks):
            ys.append(_dot(inv_ref[...], yf_ref[g]))
            gates.append(_short_conv_chunk(g_ref, base, r0, blk, seq_len, swg_ref[...], sbg_ref[...]))
        for (base, r0), y, gate in zip(blocks, ys, gates):
            rows = slice(base + r0, base + r0 + blk)
            z = zc_ref[rows, :] if conv_z else z_ref[rows, :]
            o_ref[rows, :] = (gate * (y + bd_ref[...] * z)).astype(o_ref.dtype)


def _hyena_conv(z_src, z_col, conv_z, g_src, g_col, short_w, short_b, bd, kf, order, fwd, inv,
                batch, seq_len, blk, out_dtype, layer):
    nb = seq_len // blk
    nlags = 2 * nb - 1
    tf = FREQ_TILE
    sub = min(FREQ_SUB, blk // tf)
    nfc = blk // (sub * tf)
    w = HY_WIDTH
    seqs = _seqs_per_step(batch, seq_len)
    step_rows = seqs * seq_len
    return pl.pallas_call(
        functools.partial(_conv_kernel, seq_len=seq_len, blk=blk, conv_z=conv_z, seqs=seqs, sub=sub),
        grid=(batch // seqs, nfc),
        in_specs=[
            pl.BlockSpec((step_rows, w), lambda b, f: (b, 0)),
            pl.BlockSpec((step_rows, w), lambda b, f: (b, 0)),
            pl.BlockSpec((None, 3, w), lambda b, f: (layer, 0, z_col if conv_z else 0)),
            pl.BlockSpec((None, 1, w), lambda b, f: (layer, 0, z_col if conv_z else 0)),
            pl.BlockSpec((None, 3, w), lambda b, f: (layer, 0, g_col)),
            pl.BlockSpec((None, 1, w), lambda b, f: (layer, 0, g_col)),
            pl.BlockSpec((None, 1, w), lambda b, f: (layer * HY_ORDER + order, 0, 0)),
            pl.BlockSpec((None, sub, nlags, 2 * tf, w), lambda b, f: (order, f, 0, 0, 0)),
            pl.BlockSpec((sub * 2 * tf, blk), lambda b, f: (f, 0)),
            pl.BlockSpec((blk, 2 * blk), lambda b, f: (0, 0)),
        ],
        out_specs=pl.BlockSpec((step_rows, w), lambda b, f: (b, 0)),
        out_shape=jax.ShapeDtypeStruct((batch * seq_len, w), out_dtype),
        scratch_shapes=[
            pltpu.VMEM((step_rows, w), BF16),
            pltpu.VMEM((step_rows if conv_z else SUBLANES, w), F32),
            pltpu.VMEM((seqs * nb, 2 * blk, w), BF16),
        ],
        compiler_params=_cparams(("arbitrary", "arbitrary")),
        name="hyena_conv",
    )(z_src, g_src, short_w, short_b, short_w, short_b, bd, kf, fwd, inv)


def _dft_tables(blk):
    m = 2 * blk
    tf = FREQ_TILE
    nt = blk // tf
    q = 32
    f2 = 2 * jnp.arange(blk, dtype=jnp.int32) + 1

    def cos_sin(times):
        r = (f2[:, None] * times[None, :]) % (2 * m)
        ang = r.astype(F32) * (math.pi / m)
        return jnp.cos(ang), jnp.sin(ang)

    ch, sh = cos_sin(jnp.arange(0, blk, q, dtype=jnp.int32))
    cl, sl = cos_sin(jnp.arange(q, dtype=jnp.int32))
    rep = lambda a: jnp.repeat(a, q, axis=1)
    til = lambda a: jnp.tile(a, (1, blk // q))
    c = rep(ch) * til(cl) - rep(sh) * til(sl)
    s = rep(sh) * til(cl) + rep(ch) * til(sl)
    tiles = [slice(t * tf, (t + 1) * tf) for t in range(nt)]
    fwd = jnp.concatenate([part[t, :] for t in tiles for part in (c, -s)], axis=0).astype(BF16)
    inv = fwd.T * (2.0 / m)
    return fwd, inv.astype(BF16)


def _filter_embedding(seq_len):
    t = jnp.linspace(0.0, 1.0, seq_len, dtype=F32)[:, None]
    bands = (HY_EMB_DIM - 1) // 2
    f = jnp.linspace(1e-4, bands - 1, bands, dtype=F32)[None, :]
    w = 2.0 * math.pi * jnp.arange(seq_len, dtype=F32)[:, None] / seq_len
    z = jnp.concatenate([t, jnp.cos(f * w), -jnp.sin(f * w)], axis=-1)
    z_ext = jnp.concatenate([z, jnp.zeros((1, HY_EMB_DIM), F32), jnp.flip(z[1:], axis=0)], axis=0)
    return jnp.pad(z_ext, ((0, 0), (0, LANES - HY_EMB_DIM)))


def _rope_tables(seq_len):
    quarter = HEAD_DIM // 4
    inv = jnp.tile(ROPE_THETA ** (-jnp.arange(quarter, dtype=F32) / quarter), LANES // quarter)
    lane = jnp.arange(LANES)
    by_row = (lane % HEAD_DIM < HEAD_DIM // 2)[None, None, :]
    first = (lane % (2 * quarter) < quarter)[None, :]

    def cos_sin(npos):
        ang = jnp.arange(npos).astype(F32)[:, None] * inv[None, :]
        return jnp.cos(ang), jnp.sin(ang)

    (cr, sr), (cc, sc) = cos_sin(seq_len // GRID_W), cos_sin(GRID_W)
    lay = lambda r, c: jnp.where(by_row, r[:, None, :], c[None, :, :]).reshape(seq_len, LANES)
    cos, sin = lay(cr, cc), lay(sr, sc)
    return cos, jnp.where(first, -sin, 0.0), jnp.where(first, 0.0, sin)


def _block_ones(width):
    h = jnp.arange(width) // HEAD_DIM
    return (h[:, None] == h[None, :]).astype(BF16)


def _stream_layer(x, p, wts, mod, batch, seq_len, rope_tabs, ctx, layer, hy_blk, consts, cast_next=None):
    gq, gk, fwd, inv, z_ext = consts
    latent = ctx is not None
    outs, next_wts = _token_a(x, mod, p["norm"], wts, gq, gk, p["qg"], p["kg"], rope_tabs, seq_len, layer, cast_next)
    x1, u_pool, q, kd, vd, hy_v, hy_x1, hy_x2 = outs[:8]
    k, v = (None, None) if latent else outs[8:]
    y_pool = _pool_mix(u_pool, p["pool_w"], p["pool_scale"], batch, seq_len, layer)
    if latent:
        a = _attention(q, kd, vd, p["sink"], batch, seq_len, ctx[0], ctx[1], layer)
    else:
        a = _attention(q, kd, vd, p["sink"], batch, seq_len, layer=layer)
    kraw, ksum = _hyena_filter(z_ext, p["filter"], seq_len, layer)
    kf = _filter_spectrum(kraw, ksum, fwd, seq_len, hy_blk)
    z1 = _hyena_conv(hy_v, 0, True, hy_x1, 1, p["short_w"], p["short_b"], p["hy_bias"], kf, 0, fwd, inv,
                     batch, seq_len, hy_blk, F32, layer)
    y_hy = _hyena_conv(z1, 0, False, hy_x2, 2, p["short_w"], p["short_b"], p["hy_bias"], kf, 1, fwd, inv,
                       batch, seq_len, hy_blk, BF16, layer)
    x3 = _token_b(x1, y_pool, a, y_hy, mod, p["norm"], wts, seq_len, layer, latent)
    return x3, k, v, next_wts


def _layer_params(norm_w, pool_w, pool_scale, q_norm, k_norm, attn_sink, hy_short_w, hy_short_b, hy_f_w1, hy_f_b1,
                  hy_f_w2, hy_f_b2, hy_f_w3, hy_f_b3, hy_sin_freq, hy_decay, hy_bias):
    eye = jnp.eye(len(POOL_WINDOWS), dtype=F32)
    pool_bd = (eye[None, :, None, :, None] * pool_w[:, :, :, None, :]).reshape(DEPTH, POOL_WIDTH, POOL_WIDTH)
    return {
        "norm": norm_w,
        "pool_w": pool_bd.astype(BF16), "pool_scale": pool_scale[:, None, :],
        "qg": jnp.tile(q_norm, (1, N_HEADS))[:, None, :], "kg": jnp.tile(k_norm, (1, N_KV_HEADS))[:, None, :],
        "sink": attn_sink,
        "short_w": hy_short_w, "short_b": hy_short_b[:, None, :],
        "filter": _filter_weights(hy_f_w1, hy_f_b1, hy_f_w2, hy_f_b2, hy_f_w3, hy_f_b3, hy_sin_freq, hy_decay),
        "hy_bias": hy_bias.reshape(DEPTH * HY_ORDER, 1, HY_WIDTH),
    }


def kernel(x_prompt, x_sample, cache_k, cache_v, c, c_ctx, ada_w, ada_b, norm_w, ffn_wg, ffn_wu, ffn_wd, w_in, w_out, pool_w, pool_scale, q_norm, k_norm, attn_sink, hy_short_w, hy_short_b, hy_f_w1, hy_f_b1, hy_f_w2, hy_f_b2, hy_f_w3, hy_f_b3, hy_sin_freq, hy_decay, hy_bias):
    bp, lp, _ = x_prompt.shape
    bs, ls, _ = x_sample.shape
    lc = cache_k.shape[2]

    cond = jnp.concatenate([c_ctx[None, :], c, jnp.zeros((COND_ROWS - 1 - bs, D_MODEL), F32)], axis=0)
    mod = _ada_mod(cond, ada_w, ada_b).reshape(DEPTH, COND_ROWS, 1, N_MOD * D_MODEL)

    gq, gk = _block_ones(ATTN_WIDTH), _block_ones(KV_WIDTH)
    rope_tabs = _rope_tables(ls)
    blk_p, blk_s = min(lp, HY_BLOCK), min(ls, HY_BLOCK)
    consts_p = (gq, gk) + _dft_tables(blk_p) + (_filter_embedding(lp),)
    consts_s = (gq, gk) + _dft_tables(blk_s) + (_filter_embedding(ls),)
    ctx_k = cache_k.reshape(bs, DEPTH, lc, KV_WIDTH)
    ctx_v = cache_v.reshape(bs, DEPTH, lc, KV_WIDTH)

    flat = _flat_weights(ffn_wg, ffn_wu, ffn_wd, w_in, w_out)
    wts = _cast_layer(flat, 0)
    yp = x_prompt.reshape(bp * lp, D_MODEL)
    ys = x_sample.reshape(bs * ls, D_MODEL)
    p = _layer_params(norm_w, pool_w, pool_scale, q_norm, k_norm, attn_sink, hy_short_w, hy_short_b, hy_f_w1, hy_f_b1,
                      hy_f_w2, hy_f_b2, hy_f_w3, hy_f_b3, hy_sin_freq, hy_decay, hy_bias)
    ks, vs = [], []
    for l in range(DEPTH):
        yp, k_l, v_l, _ = _stream_layer(yp, p, wts, mod, bp, lp, None, None, l, blk_p, consts_p)
        ks.append(k_l.reshape(bp, lp, N_KV_HEADS, HEAD_DIM))
        vs.append(v_l.reshape(bp, lp, N_KV_HEADS, HEAD_DIM))
        cast_next = (flat, l + 1) if l + 1 < DEPTH else None
        ys, _, _, wts = _stream_layer(ys, p, wts, mod, bs, ls, rope_tabs, (ctx_k, ctx_v), l, blk_s, consts_s, cast_next)
    return (yp.reshape(bp, lp, D_MODEL), ys.reshape(bs, ls, D_MODEL),
            jnp.stack(ks, axis=1), jnp.stack(vs, axis=1))
```

```python
import functools
import math

import jax
import jax.numpy as jnp
from jax import lax
from jax.experimental import pallas as pl
from jax.experimental.pallas import tpu as pltpu

F32 = jnp.float32
BF16 = jnp.bfloat16

D_MODEL = 1024
DEPTH = 2
GRID_W = 64
POOL_WINDOWS = (2, 4, 8, 16)
POOL_WIDTH = 256
POOL_GROUP = 64
HEAD_DIM = 64
N_HEADS = 8
N_KV_HEADS = 2
GQA_GROUP = 4
ATTN_WIDTH = 512
KV_WIDTH = 128
WINDOW = 128
BLOCK = 128
ROPE_THETA = 10000.0
HY_WIDTH = 256
HY_ORDER = 2
HY_EMB_DIM = 33
HY_FILTER_HIDDEN = 64
HY_MOD_SHIFT = 0.05
D_FF = 2816
IN_WIDTH = 1792
N_MOD = 9
NORM_EPS = 1e-6
NEG_INF = -1e30

LANES = 128
SUBLANES = 8
VMEM_LIMIT = 56 * 1024 * 1024

TOKEN_TILE = 512
TOKEN_TILE_B = 1024
MXU_DIM = 256
FF_CHUNKS = (1280, 1536)
assert sum(FF_CHUNKS) == D_FF and all(w % MXU_DIM == 0 for w in FF_CHUNKS)
SEQ_CHUNK = 512
SEQ_STEP_ROWS = 2048
HY_BLOCK = 1024
FREQ_TILE = 128
FREQ_SUB = 2
MAC_ROWS = 32
ATTN_SUB = 8
CTX_SEQS = 4
ONES_ROWS = 16
COND_ROWS = 8


def _cparams(sem):
    return pltpu.CompilerParams(dimension_semantics=sem, vmem_limit_bytes=VMEM_LIMIT)


def _dot(a, b):
    return jnp.dot(a, b, preferred_element_type=F32)


def _layer_block(tail, layer):
    return pl.BlockSpec((None,) + tuple(tail), lambda *_: (layer,) + (0,) * len(tail))


def _cast_kernel(*refs):
    n = len(refs) // 2
    for i_ref, o_ref in zip(refs[:n], refs[n:]):
        o_ref[...] = i_ref[...].astype(o_ref.dtype)


def _flat_weights(ffn_wg, ffn_wu, ffn_wd, w_in, w_out):
    arrs = {"wg": ffn_wg, "wu": ffn_wu, "wd": ffn_wd, "w_in": w_in, "w_out": w_out}
    return {k: a.reshape(-1, a.shape[-1]) for k, a in arrs.items()}


def _layer_cast_specs(flat, layer, steps):
    in_specs, out_specs, out_shapes = [], [], []
    for a in flat.values():
        rows = a.shape[0] // DEPTH
        blk = rows // steps
        in_specs.append(pl.BlockSpec((blk, a.shape[1]), lambda i, first=layer * steps: (first + i, 0)))
        out_specs.append(pl.BlockSpec((blk, a.shape[1]), lambda i: (i, 0)))
        out_shapes.append(jax.ShapeDtypeStruct((rows, a.shape[1]), BF16))
    return in_specs, out_specs, out_shapes


def _layer_weights(outs):
    wg, wu, wd, w_in, w_out = outs
    return {"wg": wg.reshape(2, D_MODEL, D_FF), "wu": wu.reshape(2, D_MODEL, D_FF),
            "wd": wd.reshape(2, D_FF, D_MODEL), "w_in": w_in, "w_out": w_out}


def _cast_layer(flat, layer):
    in_specs, out_specs, out_shapes = _layer_cast_specs(flat, layer, steps=16)
    outs = pl.pallas_call(
        _cast_kernel,
        grid=(16,),
        in_specs=in_specs,
        out_specs=out_specs,
        out_shape=out_shapes,
        compiler_params=_cparams(("arbitrary",)),
        name="cast_weights",
    )(*flat.values())
    return _layer_weights(outs)


def _ada_kernel(c_ref, w_ref, b_ref, o_ref):
    c = c_ref[...]
    s = (c * jax.nn.sigmoid(c)).astype(BF16)
    o_ref[...] = _dot(s, w_ref[...].astype(BF16)) + b_ref[...]


def _ada_mod(cond, ada_w, ada_b):
    tn = 3072
    nw = N_MOD * D_MODEL
    return pl.pallas_call(
        _ada_kernel,
        grid=(DEPTH, nw // tn),
        in_specs=[
            pl.BlockSpec((COND_ROWS, D_MODEL), lambda l, j: (0, 0)),
            pl.BlockSpec((None, D_MODEL, tn), lambda l, j: (l, 0, j)),
            pl.BlockSpec((None, 1, tn), lambda l, j: (l, 0, j)),
        ],
        out_specs=pl.BlockSpec((None, COND_ROWS, tn), lambda l, j: (l, 0, j)),
        out_shape=jax.ShapeDtypeStruct((DEPTH, COND_ROWS, nw), F32),
        compiler_params=_cparams(("arbitrary", "arbitrary")),
        name="ada_mod",
    )(cond, ada_w, ada_b.reshape(DEPTH, 1, nw))


def _mod_slice(mod_ref, k):
    return mod_ref[:, k * D_MODEL:(k + 1) * D_MODEL]


def _rms_mod(x, gain, scale, shift):
    y = x * lax.rsqrt(jnp.mean(x * x, axis=-1, keepdims=True) + NORM_EPS)
    return (y * gain) * (1.0 + scale) + shift


def _swiglu(hb, wg_ref, wu_ref, wd_ref):
    acc = None
    lo = 0
    for width in FF_CHUNKS:
        sl = slice(lo, lo + width)
        lo += width
        g = _dot(hb, wg_ref[:, sl])
        u = _dot(hb, wu_ref[:, sl])
        a = ((g * jax.nn.sigmoid(g)) * u).astype(BF16)
        y = _dot(a, wd_ref[sl, :])
        acc = y if acc is None else acc + y
    return acc


def _head_norm(x, gmat, gain):
    ss = _dot((x * x).astype(BF16), gmat)
    return (x * lax.rsqrt(ss * (1.0 / HEAD_DIM) + NORM_EPS)) * gain


def _rope(x, cos, sa, sb):
    w = x.shape[1]
    xn = pltpu.roll(x, w - 16, axis=1)
    xp = pltpu.roll(x, 16, axis=1)
    return x * cos + xn * sa + xp * sb


def _token_a_kernel(*refs, rope, n_cast):
    n_in = 11 + (3 if rope else 0) + n_cast
    ins, outs = refs[:n_in], refs[n_in:]
    x_ref, mod_ref, nrm_ref, wg_ref, wu_ref, wd_ref, win_ref, gq_ref, gk_ref, qg_ref, kg_ref = ins[:11]
    if rope:
        cos_ref, sa_ref, sb_ref = ins[11:14]
    x1_ref, up_ref, q_ref, kd_ref, vd_ref, hv_ref, hx1_ref, hx2_ref = outs[:8]
    if not rope:
        k_ref, v_ref = outs[8:10]
    for i_ref, o_ref in zip(ins[n_in - n_cast:], outs[len(outs) - n_cast:]):
        o_ref[...] = i_ref[...].astype(o_ref.dtype)
    x = x_ref[...]
    sh1, sc1, g1 = _mod_slice(mod_ref, 0), _mod_slice(mod_ref, 1), _mod_slice(mod_ref, 2)
    sh2, sc2 = _mod_slice(mod_ref, 3), _mod_slice(mod_ref, 4)
    h = _rms_mod(x, nrm_ref[0:1, :], sc1, sh1).astype(BF16)
    x1 = x + (0.5 * g1) * _swiglu(h, wg_ref, wu_ref, wd_ref)
    x1_ref[...] = x1
    h2 = _rms_mod(x1, nrm_ref[1:2, :], sc2, sh2).astype(BF16)
    s1 = POOL_WIDTH
    s2 = s1 + ATTN_WIDTH
    s3 = s2 + KV_WIDTH
    s4 = s3 + KV_WIDTH
    qkv = _dot(h2, win_ref[:, s1:s4])
    q = _head_norm(qkv[:, :ATTN_WIDTH], gq_ref[...], qg_ref[...])
    k = _head_norm(qkv[:, ATTN_WIDTH:ATTN_WIDTH + KV_WIDTH], gk_ref[...], kg_ref[...])
    v = qkv[:, ATTN_WIDTH + KV_WIDTH:]
    up_ref[...] = _dot(h2, win_ref[:, :s1])
    hy = _dot(h2, win_ref[:, s4:])
    for j, ref in enumerate((hv_ref, hx1_ref, hx2_ref)):
        ref[...] = hy[:, j * HY_WIDTH:(j + 1) * HY_WIDTH]
    if rope:
        cos, sa, sb = cos_ref[...], sa_ref[...], sb_ref[...]
        reps = ATTN_WIDTH // LANES
        q = _rope(q, jnp.concatenate([cos] * reps, axis=1), jnp.concatenate([sa] * reps, axis=1),
                  jnp.concatenate([sb] * reps, axis=1))
        k = _rope(k, cos, sa, sb)
    q_ref[...] = (q * (HEAD_DIM ** -0.5 * math.log2(math.e))).astype(BF16)
    kd_ref[...] = _dup_heads(k)
    vd_ref[...] = _dup_heads(v)
    if not rope:
        k_ref[...] = k
        v_ref[...] = v


def _token_a(x, mod, nrm, wts, gq, gk, qg, kg, rope_tabs, seq_len, layer, cast_next=None):
    n = x.shape[0]
    tm = TOKEN_TILE
    tiles_per_seq = max(seq_len // tm, 1)
    rope = rope_tabs is not None
    if rope:
        mod_row = lambda i: (layer, 1 + i // tiles_per_seq, 0, 0)
    else:
        mod_row = lambda i: (layer, 0, 0, 0)
    const = lambda i: (0, 0)
    row = lambda i: (i, 0)
    in_specs = [
        pl.BlockSpec((tm, D_MODEL), row),
        pl.BlockSpec((None, None, 1, N_MOD * D_MODEL), mod_row),
        _layer_block((3, D_MODEL), layer),
        pl.BlockSpec((None, D_MODEL, D_FF), lambda i: (0, 0, 0)),
        pl.BlockSpec((None, D_MODEL, D_FF), lambda i: (0, 0, 0)),
        pl.BlockSpec((None, D_FF, D_MODEL), lambda i: (0, 0, 0)),
        pl.BlockSpec((D_MODEL, IN_WIDTH), const),
        pl.BlockSpec((ATTN_WIDTH, ATTN_WIDTH), const),
        pl.BlockSpec((KV_WIDTH, KV_WIDTH), const),
        _layer_block((1, ATTN_WIDTH), layer),
        _layer_block((1, KV_WIDTH), layer),
    ]
    args = [x, mod, nrm, wts["wg"], wts["wu"], wts["wd"], wts["w_in"], gq, gk, qg, kg]
    if rope:
        tab = pl.BlockSpec((tm, LANES), lambda i: (i % tiles_per_seq, 0))
        in_specs += [tab, tab, tab]
        args += list(rope_tabs)
    widths = (D_MODEL, POOL_WIDTH, ATTN_WIDTH, 2 * KV_WIDTH, 2 * KV_WIDTH) + (HY_WIDTH,) * (HY_ORDER + 1)
    dtypes = (F32, F32, BF16, BF16, BF16) + (F32,) * (HY_ORDER + 1)
    if not rope:
        widths += (KV_WIDTH, KV_WIDTH)
        dtypes += (F32, F32)
    out_specs = [pl.BlockSpec((tm, w), row) for w in widths]
    out_shapes = [jax.ShapeDtypeStruct((n, w), d) for w, d in zip(widths, dtypes)]
    n_cast = 0
    if cast_next is not None:
        flat, next_layer = cast_next
        c_in, c_out, c_shapes = _layer_cast_specs(flat, next_layer, steps=n // tm)
        in_specs += c_in
        args += list(flat.values())
        out_specs += c_out
        out_shapes += c_shapes
        n_cast = len(c_in)
    outs = pl.pallas_call(
        functools.partial(_token_a_kernel, rope=rope, n_cast=n_cast),
        grid=(n // tm,),
        in_specs=in_specs,
        out_specs=out_specs,
        out_shape=out_shapes,
        compiler_params=_cparams(("arbitrary",)),
        name="token_a_rope" if rope else "token_a",
    )(*args)
    if n_cast:
        return outs[:len(outs) - n_cast], _layer_weights(outs[len(outs) - n_cast:])
    return outs, None


def _token_b_kernel(x_ref, yp_ref, a_ref, yh_ref, mod_ref, nrm_ref, wo_ref, wg_ref, wu_ref, wd_ref, o_ref):
    x1 = x_ref[...]
    g2 = _mod_slice(mod_ref, 5)
    sh3, sc3, g3 = _mod_slice(mod_ref, 6), _mod_slice(mod_ref, 7), _mod_slice(mod_ref, 8)
    cat = jnp.concatenate([yp_ref[...], a_ref[...], yh_ref[...]], axis=1)
    x2 = x1 + g2 * _dot(cat, wo_ref[...])
    h3 = _rms_mod(x2, nrm_ref[2:3, :], sc3, sh3).astype(BF16)
    o_ref[...] = x2 + (0.5 * g3) * _swiglu(h3, wg_ref, wu_ref, wd_ref)


def _token_b(x1, y_pool, a, y_hy, mod, nrm, wts, seq_len, layer, per_seq_cond):
    n = x1.shape[0]
    tm = min(TOKEN_TILE_B, n)
    tiles_per_seq = max(seq_len // tm, 1)
    if per_seq_cond:
        mod_row = lambda i: (layer, 1 + i // tiles_per_seq, 0, 0)
    else:
        mod_row = lambda i: (layer, 0, 0, 0)
    const = lambda i: (0, 0)
    row = lambda i: (i, 0)
    return pl.pallas_call(
        _token_b_kernel,
        grid=(n // tm,),
        in_specs=[
            pl.BlockSpec((tm, D_MODEL), row),
            pl.BlockSpec((tm, POOL_WIDTH), row),
            pl.BlockSpec((tm, ATTN_WIDTH), row),
            pl.BlockSpec((tm, HY_WIDTH), row),
            pl.BlockSpec((None, None, 1, N_MOD * D_MODEL), mod_row),
            _layer_block((3, D_MODEL), layer),
            pl.BlockSpec((D_MODEL, D_MODEL), const),
            pl.BlockSpec((None, D_MODEL, D_FF), lambda i: (1, 0, 0)),
            pl.BlockSpec((None, D_MODEL, D_FF), lambda i: (1, 0, 0)),
            pl.BlockSpec((None, D_FF, D_MODEL), lambda i: (1, 0, 0)),
        ],
        out_specs=pl.BlockSpec((tm, D_MODEL), row),
        out_shape=jax.ShapeDtypeStruct((n, D_MODEL), F32),
        compiler_params=_cparams(("arbitrary",)),
        name="token_b",
    )(x1, y_pool, a, y_hy, mod, nrm, wts["w_out"], wts["wg"], wts["wu"], wts["wd"])


def _seqs_per_step(batch, seq_len):
    return min(batch, max(1, SEQ_STEP_ROWS // seq_len))


def _halo_rows(src_ref, base, r0, rows, seq_len):
    c = src_ref.shape[1]
    zero = jnp.zeros((SUBLANES, c), F32)
    lo = base + r0
    prev = src_ref[lo - SUBLANES:lo, :] if r0 > 0 else zero
    nxt = src_ref[lo + rows:lo + rows + SUBLANES, :] if r0 + rows < seq_len else zero
    return prev, nxt


def _short_conv_chunk(src_ref, base, r0, rows, seq_len, w, b):
    x = src_ref[base + r0:base + r0 + rows, :]
    prev, nxt = _halo_rows(src_ref, base, r0, rows, seq_len)
    ridx = lax.broadcasted_iota(jnp.int32, x.shape, 0)
    xp = jnp.where(ridx == 0, prev[SUBLANES - 1:SUBLANES, :], pltpu.roll(x, 1, axis=0))
    xn = jnp.where(ridx == rows - 1, nxt[0:1, :], pltpu.roll(x, rows - 1, axis=0))
    return xp * w[0:1, :] + x * w[1:2, :] + xn * w[2:3, :] + b


def _pool_kernel(u_ref, w_ref, scale_ref, o_ref, *, seq_len, rows, seqs):
    lane = lax.broadcasted_iota(jnp.int32, (rows, POOL_WIDTH), 1)
    grp = lane // POOL_GROUP
    half = jnp.where(grp == 0, 1, jnp.where(grp == 1, 2, jnp.where(grp == 2, 4, 8)))
    odd_grp = lax.broadcasted_iota(jnp.int32, (rows, LANES), 1) >= POOL_GROUP
    ext = rows + 2 * SUBLANES
    back = lambda v, s: pltpu.roll(v, s, axis=0)
    fwd = lambda v, s: pltpu.roll(v, ext - s, axis=0)
    core = lambda v: v[SUBLANES:SUBLANES + rows, :]
    for base, r0 in [(sq * seq_len, r0) for sq in range(seqs) for r0 in range(0, seq_len, rows)]:
        x = u_ref[base + r0:base + r0 + rows, :]
        prev, nxt = _halo_rows(u_ref, base, r0, rows, seq_len)
        a = jnp.concatenate([prev, x, nxt], axis=0)
        lo, hi = a[:, :LANES], a[:, LANES:]
        b1 = back(lo, 1)
        b2 = b1 + back(b1, 1)
        f2 = lo + fwd(lo, 1)
        sum_lo = jnp.where(odd_grp, core(b2) + core(f2), core(b1) + core(lo))
        c1 = back(hi, 1)
        c2 = c1 + back(c1, 1)
        c4 = c2 + back(c2, 2)
        c8 = c4 + back(c4, 4)
        g2 = hi + fwd(hi, 1)
        g4 = g2 + fwd(g2, 2)
        g8 = g4 + fwd(g4, 4)
        sum_hi = jnp.where(odd_grp, core(c8) + core(g8), core(c4) + core(g4))
        wsum = jnp.concatenate([sum_lo, sum_hi], axis=1)
        t = r0 + lax.broadcasted_iota(jnp.int32, (rows, POOL_WIDTH), 0)
        cnt = jnp.minimum(t + half, seq_len) - jnp.maximum(t - half, 0)
        d = wsum / cnt.astype(F32) - x
        y = _dot(d.astype(BF16), w_ref[...]) * scale_ref[...]
        o_ref[base + r0:base + r0 + rows, :] = y.astype(o_ref.dtype)


def _pool_mix(u_pool, w_bd, scale, batch, seq_len, layer):
    rows = min(SEQ_CHUNK // 2, seq_len)
    seqs = _seqs_per_step(batch, seq_len)
    return pl.pallas_call(
        functools.partial(_pool_kernel, seq_len=seq_len, rows=rows, seqs=seqs),
        grid=(batch // seqs,),
        in_specs=[
            pl.BlockSpec((seqs * seq_len, POOL_WIDTH), lambda b: (b, 0)),
            _layer_block((POOL_WIDTH, POOL_WIDTH), layer),
            _layer_block((1, POOL_WIDTH), layer),
        ],
        out_specs=pl.BlockSpec((seqs * seq_len, POOL_WIDTH), lambda b: (b, 0)),
        out_shape=jax.ShapeDtypeStruct((batch * seq_len, POOL_WIDTH), BF16),
        compiler_params=_cparams(("arbitrary",)),
        name="pool_mix",
    )(u_pool, w_bd, scale)


def _dup_heads(x):
    lane = lax.broadcasted_iota(jnp.int32, x.shape, 1)
    sw = pltpu.roll(x, HEAD_DIM, axis=1)
    lo = lane < HEAD_DIM
    return jnp.concatenate([jnp.where(lo, x, sw), jnp.where(lo, sw, x)], axis=1).astype(BF16)


def _attn_kernel(*refs, has_local, nblocks, sub, seqs, layer):
    if has_local:
        sink_ref, q_ref, kp_ref, kc_ref, kn_ref, vp_ref, vc_ref, vn_ref, ck_ref, cv_ref, o_ref = refs
        kwin = jnp.concatenate([kp_ref[...], kc_ref[...], kn_ref[...]], axis=0)
        vwin = jnp.concatenate([vp_ref[...], vc_ref[...], vn_ref[...]], axis=0)
        kctx, vctx = _dup_heads(ck_ref[...]), _dup_heads(cv_ref[...])
        jj = lax.broadcasted_iota(jnp.int32, (BLOCK, GQA_GROUP * BLOCK), 0)
        rr = lax.broadcasted_iota(jnp.int32, (BLOCK, GQA_GROUP * BLOCK), 1) % BLOCK
    else:
        sink_ref, q_ref, kd_ref, vd_ref, o_ref = refs
    i = pl.program_id(1)
    lane_q = lax.broadcasted_iota(jnp.int32, (BLOCK, LANES), 1)
    head_of_col = lax.broadcasted_iota(jnp.int32, (1, GQA_GROUP * BLOCK), 1) // BLOCK
    log2e = math.log2(math.e)

    def values_t(v2):
        vt = jnp.transpose(v2.astype(F32))[:HEAD_DIM + ONES_ROWS, :]
        row = lax.broadcasted_iota(jnp.int32, vt.shape, 0)
        return jnp.where(row < HEAD_DIM, vt, 1.0).astype(BF16)

    def sink_row(kvh):
        row = jnp.full((1, GQA_GROUP * BLOCK), sink_ref[layer, kvh * GQA_GROUP + GQA_GROUP - 1] * log2e, F32)
        for g in range(GQA_GROUP - 1):
            row = jnp.where(head_of_col == g, sink_ref[layer, kvh * GQA_GROUP + g] * log2e, row)
        return row

    if has_local:
        vt_win = [values_t(vwin[:, kvh * LANES:(kvh + 1) * LANES]) for kvh in range(N_KV_HEADS)]
        vt_ctx = [values_t(vctx[:, kvh * LANES:(kvh + 1) * LANES]) for kvh in range(N_KV_HEADS)]
    else:
        vt_own = [values_t(vd_ref[:, kvh * LANES:(kvh + 1) * LANES]) for kvh in range(N_KV_HEADS)]
    units = [(sb, kvh) for sb in range(seqs * sub) for kvh in range(N_KV_HEADS)]
    scores, values = [], []
    for sb, kvh in units:
        q = q_ref[sb * BLOCK:(sb + 1) * BLOCK, :]
        qparts = []
        for hd in range(kvh * GQA_GROUP, (kvh + 1) * GQA_GROUP):
            qp = q[:, (hd // 2) * LANES:(hd // 2 + 1) * LANES]
            keep = (lane_q < HEAD_DIM) == (hd % 2 == 0)
            qparts.append(jnp.where(keep, qp, jnp.zeros_like(qp)))
        qs = jnp.concatenate(qparts, axis=0)
        cols = slice(kvh * LANES, (kvh + 1) * LANES)
        if has_local:
            loc = slice(sb * BLOCK, (sb + 3) * BLOCK)
            kk = jnp.concatenate([kwin[loc, cols], kctx[:, cols]], axis=0)
            vt = jnp.concatenate([vt_win[kvh][:, loc], vt_ctx[kvh]], axis=1)
        else:
            own = slice((sb // sub) * sub * BLOCK, (sb // sub + 1) * sub * BLOCK)
            kk = kd_ref[own, cols]
            vt = vt_own[kvh][:, own]
        s = lax.dot_general(kk, qs, (((1,), (1,)), ((), ())), preferred_element_type=F32)
        if has_local:
            gb = i * sub + sb
            below = jj >= rr + jnp.where(gb >= 1, 0, BLOCK)
            above = jj <= rr - jnp.where(gb <= nblocks - 2, 0, BLOCK)
            s = jnp.concatenate([jnp.where(below, s[:BLOCK, :], NEG_INF), s[BLOCK:2 * BLOCK, :],
                                 jnp.where(above, s[2 * BLOCK:3 * BLOCK, :], NEG_INF), s[3 * BLOCK:, :]], axis=0)
        scores.append(s)
        values.append(vt)

    sinks = [sink_row(kvh) for kvh in range(N_KV_HEADS)]
    maxes = [jnp.maximum(jnp.max(s, axis=0, keepdims=True), sinks[kvh]) for (_, kvh), s in zip(units, scores)]
    probs = [jnp.exp2(s - m).astype(BF16) for s, m in zip(scores, maxes)]
    sums = [_dot(vt, e) for vt, e in zip(values, probs)]
    outs = []
    for (_, kvh), ow, m in zip(units, sums, maxes):
        den = ow[HEAD_DIM:HEAD_DIM + 1, :] + jnp.exp2(sinks[kvh] - m)
        outs.append(ow[:HEAD_DIM, :] / den)
    for sb in range(seqs * sub):
        blks = []
        for kvh in range(N_KV_HEADS):
            o = outs[sb * N_KV_HEADS + kvh]
            for p in range(GQA_GROUP // 2):
                pair = jnp.concatenate([o[:, 2 * p * BLOCK:(2 * p + 1) * BLOCK], o[:, (2 * p + 1) * BLOCK:(2 * p + 2) * BLOCK]],
                                       axis=0)
                blks.append(jnp.transpose(pair))
        o_ref[sb * BLOCK:(sb + 1) * BLOCK, :] = jnp.concatenate(blks, axis=1).astype(o_ref.dtype)


def _attention(q, kd, vd, sink, batch, seq_len, ctx_k=None, ctx_v=None, layer=0):
    assert WINDOW == BLOCK
    nb = seq_len // BLOCK
    has_local = ctx_k is not None
    sub = min(ATTN_SUB, nb)
    steps = nb // sub
    seqs = 1
    if not has_local:
        assert steps == 1
        seqs = math.gcd(CTX_SEQS, batch)
        batch //= seqs
    qb = seqs * sub * BLOCK
    qspec = pl.BlockSpec((qb, ATTN_WIDTH), lambda b, i: (b * steps + i, 0))
    sspec = pl.BlockSpec(memory_space=pltpu.SMEM)
    if has_local:
        lc = ctx_k.shape[2]
        edge = lambda f: pl.BlockSpec((BLOCK, 2 * KV_WIDTH), f)
        prev = lambda b, i: (b * nb + jnp.maximum(i * sub - 1, 0), 0)
        nxt = lambda b, i: (b * nb + jnp.minimum((i + 1) * sub, nb - 1), 0)
        cur = pl.BlockSpec((qb, 2 * KV_WIDTH), lambda b, i: (b * steps + i, 0))
        cspec = pl.BlockSpec((None, None, lc, KV_WIDTH), lambda b, i: (b, layer, 0, 0))
        in_specs = [sspec, qspec, edge(prev), cur, edge(nxt), edge(prev), cur, edge(nxt), cspec, cspec]
        args = [sink, q, kd, kd, kd, vd, vd, vd, ctx_k, ctx_v]
    else:
        kv = pl.BlockSpec((qb, 2 * KV_WIDTH), lambda b, i: (b, 0))
        in_specs = [sspec, qspec, kv, kv]
        args = [sink, q, kd, vd]
    return pl.pallas_call(
        functools.partial(_attn_kernel, has_local=has_local, nblocks=nb, sub=sub, seqs=seqs, layer=layer),
        grid=(batch, steps),
        in_specs=in_specs,
        out_specs=pl.BlockSpec((qb, ATTN_WIDTH), lambda b, i: (b * steps + i, 0)),
        out_shape=jax.ShapeDtypeStruct(q.shape, BF16),
        compiler_params=_cparams(("arbitrary", "arbitrary")),
        name="attn_latent" if has_local else "attn_context",
    )(*args)


def _filter_kernel(zt_ref, zb_ref, w1_ref, b1_ref, w2_ref, b2_ref, w3_ref, b3_ref, fr_ref, dl_ref, k_ref, sum_ref,
                   *, seq_len, rows):
    i = pl.program_id(0)
    half = rows // 2
    oc = HY_ORDER * HY_WIDTH
    z = jnp.concatenate([zt_ref[...], zb_ref[...]], axis=1)
    h = jnp.sin(fr_ref[0:1, :] * (_dot(z.astype(BF16), w1_ref[...].astype(BF16)) + b1_ref[...]))
    h = jnp.sin(fr_ref[1:2, :] * (_dot(h.astype(BF16), w2_ref[...].astype(BF16)) + b2_ref[...]))
    h3 = _dot(h.astype(BF16), w3_ref[...].astype(BF16)) + b3_ref[...]
    total = jnp.zeros((1, oc), F32)
    for part, z_ref in enumerate((zt_ref, zb_ref)):
        row = i * rows + part * half + lax.broadcasted_iota(jnp.int32, (half, oc), 0)
        t = z_ref[:, 0:1]
        decay = jnp.exp(-t * jnp.abs(dl_ref[...]))
        fwd_dir = h3[:, part * 2 * oc:part * 2 * oc + oc]
        bwd_dir = h3[:, part * 2 * oc + oc:(part + 1) * 2 * oc]
        sel = jnp.where(row < seq_len, fwd_dir, jnp.where(row > seq_len, bwd_dir, 0.0))
        k = sel * (decay + HY_MOD_SHIFT)
        k_ref[part * half:(part + 1) * half, :] = k
        total = total + jnp.sum(jnp.abs(k), axis=0, keepdims=True)

    @pl.when(i == 0)
    def _():
        sum_ref[...] = jnp.zeros_like(sum_ref)

    sum_ref[...] += total


def _block_diag2(w):
    z = jnp.zeros_like(w)
    return jnp.concatenate([jnp.concatenate([w, z], axis=-1), jnp.concatenate([z, w], axis=-1)], axis=-2)


def _filter_weights(w1, b1, w2, b2, w3, b3, freq, deltas):
    two = lambda a: jnp.concatenate([a, a], axis=-1)
    w1p = jnp.pad(w1, ((0, 0), (0, LANES - HY_EMB_DIM), (0, 0)))
    return (_block_diag2(w1p), two(b1)[:, None, :], _block_diag2(w2), two(b2)[:, None, :],
            _block_diag2(w3), two(b3)[:, None, :], two(freq), deltas.reshape(DEPTH, 1, HY_ORDER * HY_WIDTH))


def _hyena_filter(z_ext, fw, seq_len, layer):
    n = 2 * seq_len
    rows = min(1024, n)
    half = rows // 2
    oc = HY_ORDER * HY_WIDTH
    hid = 2 * HY_FILTER_HIDDEN
    const = lambda i: (0, 0)
    return pl.pallas_call(
        functools.partial(_filter_kernel, seq_len=seq_len, rows=rows),
        grid=(n // rows,),
        in_specs=[
            pl.BlockSpec((half, LANES), lambda i: (2 * i, 0)),
            pl.BlockSpec((half, LANES), lambda i: (2 * i + 1, 0)),
            _layer_block((2 * LANES, hid), layer),
            _layer_block((1, hid), layer),
            _layer_block((hid, hid), layer),
            _layer_block((1, hid), layer),
            _layer_block((hid, 4 * oc), layer),
            _layer_block((1, 4 * oc), layer),
            _layer_block((2, hid), layer),
            _layer_block((1, oc), layer),
        ],
        out_specs=[pl.BlockSpec((rows, oc), lambda i: (i, 0)), pl.BlockSpec((1, oc), const)],
        out_shape=[jax.ShapeDtypeStruct((n, oc), F32), jax.ShapeDtypeStruct((1, oc), F32)],
        compiler_params=_cparams(("arbitrary",)),
        name="hyena_filter",
    )(z_ext, z_ext, *fw)


def _spectrum_kernel(k_ref, sum_ref, fwd_ref, kf_ref, hprev_ref, g0_ref, *, blk):
    t = pl.program_id(0)
    a = k_ref[...] / (sum_ref[...] + 1e-6)
    ha = _dot(fwd_ref[...], a.astype(BF16))
    tf = FREQ_TILE
    par = lax.broadcasted_iota(jnp.int32, (tf, 1), 0) % 2
    sgn = (1 - 2 * par).astype(F32)

    @pl.when(t > 0)
    def _():
        g0 = g0_ref[...]
        for c in range(blk // tf):
            re = slice(2 * c * tf, (2 * c + 1) * tf)
            im = slice((2 * c + 1) * tf, (2 * c + 2) * tf)
            kre = ha[re, :] - sgn * hprev_ref[im, :]
            kim = ha[im, :] + sgn * (hprev_ref[re, :] - g0)
            for o in range(HY_ORDER):
                lanes = slice(o * HY_WIDTH, (o + 1) * HY_WIDTH)
                kf_ref[o, c, 0:tf, :] = kre[:, lanes]
                kf_ref[o, c, tf:2 * tf, :] = kim[:, lanes]

    hprev_ref[...] = ha
    g0_ref[...] = a[0:1, :]


def _filter_spectrum(kraw, ksum, fwd, seq_len, blk):
    nb = seq_len // blk
    nlags = 2 * nb - 1
    oc = HY_ORDER * HY_WIDTH
    m = 2 * blk
    tf = FREQ_TILE
    nfc = blk // tf
    return pl.pallas_call(
        functools.partial(_spectrum_kernel, blk=blk),
        grid=(2 * nb,),
        in_specs=[
            pl.BlockSpec((blk, oc), lambda t: ((nb + t) % (2 * nb), 0)),
            pl.BlockSpec((1, oc), lambda t: (0, 0)),
            pl.BlockSpec((m, blk), lambda t: (0, 0)),
        ],
        out_specs=pl.BlockSpec((HY_ORDER, nfc, None, 2 * tf, HY_WIDTH), lambda t: (0, 0, jnp.maximum(t - 1, 0), 0, 0)),
        out_shape=jax.ShapeDtypeStruct((HY_ORDER, nfc, nlags, 2 * tf, HY_WIDTH), F32),
        scratch_shapes=[pltpu.VMEM((m, oc), F32), pltpu.VMEM((1, oc), F32)],
        compiler_params=_cparams(("arbitrary",)),
        name="filter_spectrum",
    )(kraw, ksum, fwd)


def _conv_kernel(z_ref, g_ref, swz_ref, sbz_ref, swg_ref, sbg_ref, bd_ref, kf_ref, fwd_ref, inv_ref, o_ref,
                 zb_ref, zc_ref, yf_ref, *, seq_len, blk, conv_z, seqs, sub):
    fc = pl.program_id(1)
    nfc = pl.num_programs(1)
    nb = seq_len // blk
    tf = FREQ_TILE
    blocks = [((g // nb) * seq_len, (g % nb) * blk) for g in range(seqs * nb)]

    @pl.when(fc == 0)
    def _():
        for base, r0 in blocks:
            rows = slice(base + r0, base + r0 + blk)
            if conv_z:
                zc_ref[rows, :] = _short_conv_chunk(z_ref, base, r0, blk, seq_len, swz_ref[...], sbz_ref[...])
                zb_ref[rows, :] = zc_ref[rows, :].astype(BF16)
            else:
                zb_ref[rows, :] = z_ref[rows, :].astype(BF16)

    zf = [[_dot(fwd_ref[s * 2 * tf:(s + 1) * 2 * tf, :], zb_ref[g * blk:(g + 1) * blk, :]) for g in range(seqs * nb)]
          for s in range(sub)]

    for s in range(sub):
        col = pl.multiple_of((fc * sub + s) * 2 * tf, 2 * tf)
        for g0 in range(0, seqs * nb, nb):
            for bi in range(nb):
                for r in range(0, tf, MAC_ROWS):
                    re = slice(r, r + MAC_ROWS)
                    im = slice(tf + r, tf + r + MAC_ROWS)
                    yr = jnp.zeros((MAC_ROWS, HY_WIDTH), F32)
                    yi = jnp.zeros((MAC_ROWS, HY_WIDTH), F32)
                    for bj in range(nb):
                        lag = bi - bj + nb - 1
                        kr, ki = kf_ref[s, lag, re, :], kf_ref[s, lag, im, :]
                        zr, zi = zf[s][g0 + bj][re, :], zf[s][g0 + bj][im, :]
                        yr = yr + (kr * zr - ki * zi)
                        yi = yi + (kr * zi + ki * zr)
                    yf_ref[g0 + bi, pl.ds(col + r, MAC_ROWS), :] = yr.astype(BF16)
                    yf_ref[g0 + bi, pl.ds(col + tf + r, MAC_ROWS), :] = yi.astype(BF16)

    @pl.when(fc == nfc - 1)
    def _():
        ys, gates = [], []
        for g, (base, r0) in enumerate(blocks):
            ys.append(_dot(inv_ref[...], yf_ref[g]))
            gates.append(_short_conv_chunk(g_ref, base, r0, blk, seq_len, swg_ref[...], sbg_ref[...]))
        for (base, r0), y, gate in zip(blocks, ys, gates):
            rows = slice(base + r0, base + r0 + blk)
            z = zc_ref[rows, :] if conv_z else z_ref[rows, :]
            o_ref[rows, :] = (gate * (y + bd_ref[...] * z)).astype(o_ref.dtype)


def _hyena_conv(z_src, z_col, conv_z, g_src, g_col, short_w, short_b, bd, kf, order, fwd, inv,
                batch, seq_len, blk, out_dtype, layer):
    nb = seq_len // blk
    nlags = 2 * nb - 1
    tf = FREQ_TILE
    sub = min(FREQ_SUB, blk // tf)
    nfc = blk // (sub * tf)
    w = HY_WIDTH
    seqs = _seqs_per_step(batch, seq_len)
    step_rows = seqs * seq_len
    return pl.pallas_call(
        functools.partial(_conv_kernel, seq_len=seq_len, blk=blk, conv_z=conv_z, seqs=seqs, sub=sub),
        grid=(batch // seqs, nfc),
        in_specs=[
            pl.BlockSpec((step_rows, w), lambda b, f: (b, 0)),
            pl.BlockSpec((step_rows, w), lambda b, f: (b, 0)),
            pl.BlockSpec((None, 3, w), lambda b, f: (layer, 0, z_col if conv_z else 0)),
            pl.BlockSpec((None, 1, w), lambda b, f: (layer, 0, z_col if conv_z else 0)),
            pl.BlockSpec((None, 3, w), lambda b, f: (layer, 0, g_col)),
            pl.BlockSpec((None, 1, w), lambda b, f: (layer, 0, g_col)),
            pl.BlockSpec((None, 1, w), lambda b, f: (layer * HY_ORDER + order, 0, 0)),
            pl.BlockSpec((None, sub, nlags, 2 * tf, w), lambda b, f: (order, f, 0, 0, 0)),
            pl.BlockSpec((sub * 2 * tf, blk), lambda b, f: (f, 0)),
            pl.BlockSpec((blk, 2 * blk), lambda b, f: (0, 0)),
        ],
        out_specs=pl.BlockSpec((step_rows, w), lambda b, f: (b, 0)),
        out_shape=jax.ShapeDtypeStruct((batch * seq_len, w), out_dtype),
        scratch_shapes=[
            pltpu.VMEM((step_rows, w), BF16),
            pltpu.VMEM((step_rows if conv_z else SUBLANES, w), F32),
            pltpu.VMEM((seqs * nb, 2 * blk, w), BF16),
        ],
        compiler_params=_cparams(("arbitrary", "arbitrary")),
        name="hyena_conv",
    )(z_src, g_src, short_w, short_b, short_w, short_b, bd, kf, fwd, inv)


def _dft_tables(blk):
    m = 2 * blk
    tf = FREQ_TILE
    nt = blk // tf
    q = 32
    f2 = 2 * jnp.arange(blk, dtype=jnp.int32) + 1

    def cos_sin(times):
        r = (f2[:, None] * times[None, :]) % (2 * m)
        ang = r.astype(F32) * (math.pi / m)
        return jnp.cos(ang), jnp.sin(ang)

    ch, sh = cos_sin(jnp.arange(0, blk, q, dtype=jnp.int32))
    cl, sl = cos_sin(jnp.arange(q, dtype=jnp.int32))
    rep = lambda a: jnp.repeat(a, q, axis=1)
    til = lambda a: jnp.tile(a, (1, blk // q))
    c = rep(ch) * til(cl) - rep(sh) * til(sl)
    s = rep(sh) * til(cl) + rep(ch) * til(sl)
    tiles = [slice(t * tf, (t + 1) * tf) for t in range(nt)]
    fwd = jnp.concatenate([part[t, :] for t in tiles for part in (c, -s)], axis=0).astype(BF16)
    inv = fwd.T * (2.0 / m)
    return fwd, inv.astype(BF16)


def _filter_embedding(seq_len):
    t = jnp.linspace(0.0, 1.0, seq_len, dtype=F32)[:, None]
    bands = (HY_EMB_DIM - 1) // 2
    f = jnp.linspace(1e-4, bands - 1, bands, dtype=F32)[None, :]
    w = 2.0 * math.pi * jnp.arange(seq_len, dtype=F32)[:, None] / seq_len
    z = jnp.concatenate([t, jnp.cos(f * w), -jnp.sin(f * w)], axis=-1)
    z_ext = jnp.concatenate([z, jnp.zeros((1, HY_EMB_DIM), F32), jnp.flip(z[1:], axis=0)], axis=0)
    return jnp.pad(z_ext, ((0, 0), (0, LANES - HY_EMB_DIM)))


def _rope_tables(seq_len):
    quarter = HEAD_DIM // 4
    inv = jnp.tile(ROPE_THETA ** (-jnp.arange(quarter, dtype=F32) / quarter), LANES // quarter)
    lane = jnp.arange(LANES)
    by_row = (lane % HEAD_DIM < HEAD_DIM // 2)[None, None, :]
    first = (lane % (2 * quarter) < quarter)[None, :]

    def cos_sin(npos):
        ang = jnp.arange(npos).astype(F32)[:, None] * inv[None, :]
        return jnp.cos(ang), jnp.sin(ang)

    (cr, sr), (cc, sc) = cos_sin(seq_len // GRID_W), cos_sin(GRID_W)
    lay = lambda r, c: jnp.where(by_row, r[:, None, :], c[None, :, :]).reshape(seq_len, LANES)
    cos, sin = lay(cr, cc), lay(sr, sc)
    return cos, jnp.where(first, -sin, 0.0), jnp.where(first, 0.0, sin)


def _block_ones(width):
    h = jnp.arange(width) // HEAD_DIM
    return (h[:, None] == h[None, :]).astype(BF16)


def _stream_layer(x, p, wts, mod, batch, seq_len, rope_tabs, ctx, layer, hy_blk, consts, cast_next=None):
    gq, gk, fwd, inv, z_ext = consts
    latent = ctx is not None
    outs, next_wts = _token_a(x, mod, p["norm"], wts, gq, gk, p["qg"], p["kg"], rope_tabs, seq_len, layer, cast_next)
    x1, u_pool, q, kd, vd, hy_v, hy_x1, hy_x2 = outs[:8]
    k, v = (None, None) if latent else outs[8:]
    y_pool = _pool_mix(u_pool, p["pool_w"], p["pool_scale"], batch, seq_len, layer)
    if latent:
        a = _attention(q, kd, vd, p["sink"], batch, seq_len, ctx[0], ctx[1], layer)
    else:
        a = _attention(q, kd, vd, p["sink"], batch, seq_len, layer=layer)
    kraw, ksum = _hyena_filter(z_ext, p["filter"], seq_len, layer)
    kf = _filter_spectrum(kraw, ksum, fwd, seq_len, hy_blk)
    z1 = _hyena_conv(hy_v, 0, True, hy_x1, 1, p["short_w"], p["short_b"], p["hy_bias"], kf, 0, fwd, inv,
                     batch, seq_len, hy_blk, F32, layer)
    y_hy = _hyena_conv(z1, 0, False, hy_x2, 2, p["short_w"], p["short_b"], p["hy_bias"], kf, 1, fwd, inv,
                       batch, seq_len, hy_blk, BF16, layer)
    x3 = _token_b(x1, y_pool, a, y_hy, mod, p["norm"], wts, seq_len, layer, latent)
    return x3, k, v, next_wts


def _layer_params(norm_w, pool_w, pool_scale, q_norm, k_norm, attn_sink, hy_short_w, hy_short_b, hy_f_w1, hy_f_b1,
                  hy_f_w2, hy_f_b2, hy_f_w3, hy_f_b3, hy_sin_freq, hy_decay, hy_bias):
    eye = jnp.eye(len(POOL_WINDOWS), dtype=F32)
    pool_bd = (eye[None, :, None, :, None] * pool_w[:, :, :, None, :]).reshape(DEPTH, POOL_WIDTH, POOL_WIDTH)
    return {
        "norm": norm_w,
        "pool_w": pool_bd.astype(BF16), "pool_scale": pool_scale[:, None, :],
        "qg": jnp.tile(q_norm, (1, N_HEADS))[:, None, :], "kg": jnp.tile(k_norm, (1, N_KV_HEADS))[:, None, :],
        "sink": attn_sink,
        "short_w": hy_short_w, "short_b": hy_short_b[:, None, :],
        "filter": _filter_weights(hy_f_w1, hy_f_b1, hy_f_w2, hy_f_b2, hy_f_w3, hy_f_b3, hy_sin_freq, hy_decay),
        "hy_bias": hy_bias.reshape(DEPTH * HY_ORDER, 1, HY_WIDTH),
    }


def kernel(x_prompt, x_sample, cache_k, cache_v, c, c_ctx, ada_w, ada_b, norm_w, ffn_wg, ffn_wu, ffn_wd, w_in, w_out, pool_w, pool_scale, q_norm, k_norm, attn_sink, hy_short_w, hy_short_b, hy_f_w1, hy_f_b1, hy_f_w2, hy_f_b2, hy_f_w3, hy_f_b3, hy_sin_freq, hy_decay, hy_bias):
    bp, lp, _ = x_prompt.shape
    bs, ls, _ = x_sample.shape
    lc = cache_k.shape[2]

    cond = jnp.concatenate([c_ctx[None, :], c, jnp.zeros((COND_ROWS - 1 - bs, D_MODEL), F32)], axis=0)
    mod = _ada_mod(cond, ada_w, ada_b).reshape(DEPTH, COND_ROWS, 1, N_MOD * D_MODEL)

    gq, gk = _block_ones(ATTN_WIDTH), _block_ones(KV_WIDTH)
    rope_tabs = _rope_tables(ls)
    blk_p, blk_s = min(lp, HY_BLOCK), min(ls, HY_BLOCK)
    consts_p = (gq, gk) + _dft_tables(blk_p) + (_filter_embedding(lp),)
    consts_s = (gq, gk) + _dft_tables(blk_s) + (_filter_embedding(ls),)
    ctx_k = cache_k.reshape(bs, DEPTH, lc, KV_WIDTH)
    ctx_v = cache_v.reshape(bs, DEPTH, lc, KV_WIDTH)

    flat = _flat_weights(ffn_wg, ffn_wu, ffn_wd, w_in, w_out)
    wts = _cast_layer(flat, 0)
    yp = x_prompt.reshape(bp * lp, D_MODEL)
    ys = x_sample.reshape(bs * ls, D_MODEL)
    p = _layer_params(norm_w, pool_w, pool_scale, q_norm, k_norm, attn_sink, hy_short_w, hy_short_b, hy_f_w1, hy_f_b1,
                      hy_f_w2, hy_f_b2, hy_f_w3, hy_f_b3, hy_sin_freq, hy_decay, hy_bias)
    ks, vs = [], []
    for l in range(DEPTH):
        yp, k_l, v_l, _ = _stream_layer(yp, p, wts, mod, bp, lp, None, None, l, blk_p, consts_p)
        ks.append(k_l.reshape(bp, lp, N_KV_HEADS, HEAD_DIM))
        vs.append(v_l.reshape(bp, lp, N_KV_HEADS, HEAD_DIM))
        cast_next = (flat, l + 1) if l + 1 < DEPTH else None
        ys, _, _, wts = _stream_layer(ys, p, wts, mod, bs, ls, rope_tabs, (ctx_k, ctx_v), l, blk_s, consts_s, cast_next)
    return (yp.reshape(bp, lp, D_MODEL), ys.reshape(bs, ls, D_MODEL),
            jnp.stack(ks, axis=1), jnp.stack(vs, axis=1))
```

```python
import functools
import math

import jax
import jax.numpy as jnp
from jax import lax
from jax.experimental import pallas as pl
from jax.experimental.pallas import tpu as pltpu

F32 = jnp.float32
BF16 = jnp.bfloat16

D_MODEL = 1024
DEPTH = 2
GRID_W = 64
POOL_WINDOWS = (2, 4, 8, 16)
POOL_WIDTH = 256
POOL_GROUP = 64
HEAD_DIM = 64
N_HEADS = 8
N_KV_HEADS = 2
GQA_GROUP = 4
ATTN_WIDTH = 512
KV_WIDTH = 128
WINDOW = 128
BLOCK = 128
ROPE_THETA = 10000.0
HY_WIDTH = 256
HY_ORDER = 2
HY_EMB_DIM = 33
HY_FILTER_HIDDEN = 64
HY_MOD_SHIFT = 0.05
D_FF = 2816
IN_WIDTH = 1792
N_MOD = 9
NORM_EPS = 1e-6
NEG_INF = -1e30

LANES = 128
SUBLANES = 8
VMEM_LIMIT = 56 * 1024 * 1024

TOKEN_TILE = 512
TOKEN_TILE_B = 1024
MXU_DIM = 256
FF_CHUNKS = (1280, 1536)
assert sum(FF_CHUNKS) == D_FF and all(w % MXU_DIM == 0 for w in FF_CHUNKS)
SEQ_CHUNK = 512
SEQ_STEP_ROWS = 4096
HY_BLOCK = 1024
FREQ_TILE = 128
FREQ_SUB = 2
MAC_ROWS = 32
ATTN_SUB = 8
CTX_SEQS = 8
COND_ROWS = 8


def _cparams(sem):
    return pltpu.CompilerParams(dimension_semantics=sem, vmem_limit_bytes=VMEM_LIMIT)


def _dot(a, b):
    return jnp.dot(a, b, preferred_element_type=F32)


def _layer_block(tail, layer):
    return pl.BlockSpec((None,) + tuple(tail), lambda *_: (layer,) + (0,) * len(tail))


def _cast_kernel(*refs):
    n = len(refs) // 2
    for i_ref, o_ref in zip(refs[:n], refs[n:]):
        o_ref[...] = i_ref[...].astype(o_ref.dtype)


def _flat_weights(ffn_wg, ffn_wu, ffn_wd, w_in, w_out):
    arrs = {"wg": ffn_wg, "wu": ffn_wu, "wd": ffn_wd, "w_in": w_in, "w_out": w_out}
    return {k: a.reshape(-1, a.shape[-1]) for k, a in arrs.items()}


def _layer_cast_specs(flat, layer, steps):
    in_specs, out_specs, out_shapes = [], [], []
    for a in flat.values():
        rows = a.shape[0] // DEPTH
        blk = rows // steps
        in_specs.append(pl.BlockSpec((blk, a.shape[1]), lambda i, first=layer * steps: (first + i, 0)))
        out_specs.append(pl.BlockSpec((blk, a.shape[1]), lambda i: (i, 0)))
        out_shapes.append(jax.ShapeDtypeStruct((rows, a.shape[1]), BF16))
    return in_specs, out_specs, out_shapes


def _layer_weights(outs):
    wg, wu, wd, w_in, w_out = outs
    return {"wg": wg.reshape(2, D_MODEL, D_FF), "wu": wu.reshape(2, D_MODEL, D_FF),
            "wd": wd.reshape(2, D_FF, D_MODEL), "w_in": w_in, "w_out": w_out}


def _cast_layer(flat, layer):
    in_specs, out_specs, out_shapes = _layer_cast_specs(flat, layer, steps=16)
    outs = pl.pallas_call(
        _cast_kernel,
        grid=(16,),
        in_specs=in_specs,
        out_specs=out_specs,
        out_shape=out_shapes,
        compiler_params=_cparams(("arbitrary",)),
        name="cast_weights",
    )(*flat.values())
    return _layer_weights(outs)


def _ada_kernel(c_ref, w_ref, b_ref, o_ref):
    c = c_ref[...]
    s = (c * jax.nn.sigmoid(c)).astype(BF16)
    o_ref[...] = _dot(s, w_ref[...].astype(BF16)) + b_ref[...]


def _ada_mod(cond, ada_w, ada_b):
    tn = 3072
    nw = N_MOD * D_MODEL
    return pl.pallas_call(
        _ada_kernel,
        grid=(DEPTH, nw // tn),
        in_specs=[
            pl.BlockSpec((COND_ROWS, D_MODEL), lambda l, j: (0, 0)),
            pl.BlockSpec((None, D_MODEL, tn), lambda l, j: (l, 0, j)),
            pl.BlockSpec((None, 1, tn), lambda l, j: (l, 0, j)),
        ],
        out_specs=pl.BlockSpec((None, COND_ROWS, tn), lambda l, j: (l, 0, j)),
        out_shape=jax.ShapeDtypeStruct((DEPTH, COND_ROWS, nw), F32),
        compiler_params=_cparams(("arbitrary", "arbitrary")),
        name="ada_mod",
    )(cond, ada_w, ada_b.reshape(DEPTH, 1, nw))


def _mod_slice(mod_ref, k):
    return mod_ref[:, k * D_MODEL:(k + 1) * D_MODEL]


def _rms_mod(x, gain, scale, shift):
    y = x * lax.rsqrt(jnp.mean(x * x, axis=-1, keepdims=True) + NORM_EPS)
    return (y * gain) * (1.0 + scale) + shift


def _swiglu(hb, wg_ref, wu_ref, wd_ref):
    acc = None
    lo = 0
    for width in FF_CHUNKS:
        sl = slice(lo, lo + width)
        lo += width
        g = _dot(hb, wg_ref[:, sl])
        u = _dot(hb, wu_ref[:, sl])
        a = ((g * jax.nn.sigmoid(g)) * u).astype(BF16)
        y = _dot(a, wd_ref[sl, :])
        acc = y if acc is None else acc + y
    return acc


def _head_norm(x, gmat, gain):
    ss = _dot((x * x).astype(BF16), gmat)
    return (x * lax.rsqrt(ss * (1.0 / HEAD_DIM) + NORM_EPS)) * gain


def _rope(x, cos, sa, sb):
    w = x.shape[1]
    xn = pltpu.roll(x, w - 16, axis=1)
    xp = pltpu.roll(x, 16, axis=1)
    return x * cos + xn * sa + xp * sb


def _token_a_kernel(*refs, rope, n_cast):
    n_in = 11 + (3 if rope else 0) + n_cast
    ins, outs = refs[:n_in], refs[n_in:]
    x_ref, mod_ref, nrm_ref, wg_ref, wu_ref, wd_ref, win_ref, gq_ref, gk_ref, qg_ref, kg_ref = ins[:11]
    if rope:
        cos_ref, sa_ref, sb_ref = ins[11:14]
    x1_ref, up_ref, q_ref, kd_ref, vd_ref, hv_ref, hx1_ref, hx2_ref = outs[:8]
    if not rope:
        k_ref, v_ref = outs[8:10]
    for i_ref, o_ref in zip(ins[n_in - n_cast:], outs[len(outs) - n_cast:]):
        o_ref[...] = i_ref[...].astype(o_ref.dtype)
    x = x_ref[...]
    sh1, sc1, g1 = _mod_slice(mod_ref, 0), _mod_slice(mod_ref, 1), _mod_slice(mod_ref, 2)
    sh2, sc2 = _mod_slice(mod_ref, 3), _mod_slice(mod_ref, 4)
    h = _rms_mod(x, nrm_ref[0:1, :], sc1, sh1).astype(BF16)
    x1 = x + (0.5 * g1) * _swiglu(h, wg_ref, wu_ref, wd_ref)
    x1_ref[...] = x1
    h2 = _rms_mod(x1, nrm_ref[1:2, :], sc2, sh2).astype(BF16)
    s1 = POOL_WIDTH
    s2 = s1 + ATTN_WIDTH
    s3 = s2 + KV_WIDTH
    s4 = s3 + KV_WIDTH
    qkv = _dot(h2, win_ref[:, s1:s4])
    q = _head_norm(qkv[:, :ATTN_WIDTH], gq_ref[...], qg_ref[...])
    k = _head_norm(qkv[:, ATTN_WIDTH:ATTN_WIDTH + KV_WIDTH], gk_ref[...], kg_ref[...])
    v = qkv[:, ATTN_WIDTH + KV_WIDTH:]
    up_ref[...] = _dot(h2, win_ref[:, :s1])
    hy = _dot(h2, win_ref[:, s4:])
    for j, ref in enumerate((hv_ref, hx1_ref, hx2_ref)):
        ref[...] = hy[:, j * HY_WIDTH:(j + 1) * HY_WIDTH]
    if rope:
        cos, sa, sb = cos_ref[...], sa_ref[...], sb_ref[...]
        reps = ATTN_WIDTH // LANES
        q = _rope(q, jnp.concatenate([cos] * reps, axis=1), jnp.concatenate([sa] * reps, axis=1),
                  jnp.concatenate([sb] * reps, axis=1))
        k = _rope(k, cos, sa, sb)
    q_ref[...] = (q * (HEAD_DIM ** -0.5 * math.log2(math.e))).astype(BF16)
    kd_ref[...] = _dup_heads(k)
    vd_ref[...] = _dup_heads(v)
    if not rope:
        k_ref[...] = k
        v_ref[...] = v


def _token_a(x, mod, nrm, wts, gq, gk, qg, kg, rope_tabs, seq_len, layer, cast_next=None):
    n = x.shape[0]
    tm = TOKEN_TILE
    tiles_per_seq = max(seq_len // tm, 1)
    rope = rope_tabs is not None
    if rope:
        mod_row = lambda i: (layer, 1 + i // tiles_per_seq, 0, 0)
    else:
        mod_row = lambda i: (layer, 0, 0, 0)
    const = lambda i: (0, 0)
    row = lambda i: (i, 0)
    in_specs = [
        pl.BlockSpec((tm, D_MODEL), row),
        pl.BlockSpec((None, None, 1, N_MOD * D_MODEL), mod_row),
        _layer_block((3, D_MODEL), layer),
        pl.BlockSpec((None, D_MODEL, D_FF), lambda i: (0, 0, 0)),
        pl.BlockSpec((None, D_MODEL, D_FF), lambda i: (0, 0, 0)),
        pl.BlockSpec((None, D_FF, D_MODEL), lambda i: (0, 0, 0)),
        pl.BlockSpec((D_MODEL, IN_WIDTH), const),
        pl.BlockSpec((ATTN_WIDTH, ATTN_WIDTH), const),
        pl.BlockSpec((KV_WIDTH, KV_WIDTH), const),
        _layer_block((1, ATTN_WIDTH), layer),
        _layer_block((1, KV_WIDTH), layer),
    ]
    args = [x, mod, nrm, wts["wg"], wts["wu"], wts["wd"], wts["w_in"], gq, gk, qg, kg]
    if rope:
        tab = pl.BlockSpec((tm, LANES), lambda i: (i % tiles_per_seq, 0))
        in_specs += [tab, tab, tab]
        args += list(rope_tabs)
    widths = (D_MODEL, POOL_WIDTH, ATTN_WIDTH, 2 * KV_WIDTH, 2 * KV_WIDTH) + (HY_WIDTH,) * (HY_ORDER + 1)
    dtypes = (F32, F32, BF16, BF16, BF16) + (F32,) * (HY_ORDER + 1)
    if not rope:
        widths += (KV_WIDTH, KV_WIDTH)
        dtypes += (F32, F32)
    out_specs = [pl.BlockSpec((tm, w), row) for w in widths]
    out_shapes = [jax.ShapeDtypeStruct((n, w), d) for w, d in zip(widths, dtypes)]
    n_cast = 0
    if cast_next is not None:
        flat, next_layer = cast_next
        c_in, c_out, c_shapes = _layer_cast_specs(flat, next_layer, steps=n // tm)
        in_specs += c_in
        args += list(flat.values())
        out_specs += c_out
        out_shapes += c_shapes
        n_cast = len(c_in)
    outs = pl.pallas_call(
        functools.partial(_token_a_kernel, rope=rope, n_cast=n_cast),
        grid=(n // tm,),
        in_specs=in_specs,
        out_specs=out_specs,
        out_shape=out_shapes,
        compiler_params=_cparams(("arbitrary",)),
        name="token_a_rope" if rope else "token_a",
    )(*args)
    if n_cast:
        return outs[:len(outs) - n_cast], _layer_weights(outs[len(outs) - n_cast:])
    return outs, None


def _token_b_kernel(x_ref, yp_ref, a_ref, yh_ref, mod_ref, nrm_ref, wo_ref, wg_ref, wu_ref, wd_ref, o_ref):
    x1 = x_ref[...]
    g2 = _mod_slice(mod_ref, 5)
    sh3, sc3, g3 = _mod_slice(mod_ref, 6), _mod_slice(mod_ref, 7), _mod_slice(mod_ref, 8)
    cat = jnp.concatenate([yp_ref[...], a_ref[...], yh_ref[...]], axis=1)
    x2 = x1 + g2 * _dot(cat, wo_ref[...])
    h3 = _rms_mod(x2, nrm_ref[2:3, :], sc3, sh3).astype(BF16)
    o_ref[...] = x2 + (0.5 * g3) * _swiglu(h3, wg_ref, wu_ref, wd_ref)


def _token_b(x1, y_pool, a, y_hy, mod, nrm, wts, seq_len, layer, per_seq_cond):
    n = x1.shape[0]
    tm = min(TOKEN_TILE_B, n)
    tiles_per_seq = max(seq_len // tm, 1)
    if per_seq_cond:
        mod_row = lambda i: (layer, 1 + i // tiles_per_seq, 0, 0)
    else:
        mod_row = lambda i: (layer, 0, 0, 0)
    const = lambda i: (0, 0)
    row = lambda i: (i, 0)
    return pl.pallas_call(
        _token_b_kernel,
        grid=(n // tm,),
        in_specs=[
            pl.BlockSpec((tm, D_MODEL), row),
            pl.BlockSpec((tm, POOL_WIDTH), row),
            pl.BlockSpec((tm, ATTN_WIDTH), row),
            pl.BlockSpec((tm, HY_WIDTH), row),
            pl.BlockSpec((None, None, 1, N_MOD * D_MODEL), mod_row),
            _layer_block((3, D_MODEL), layer),
            pl.BlockSpec((D_MODEL, D_MODEL), const),
            pl.BlockSpec((None, D_MODEL, D_FF), lambda i: (1, 0, 0)),
            pl.BlockSpec((None, D_MODEL, D_FF), lambda i: (1, 0, 0)),
            pl.BlockSpec((None, D_FF, D_MODEL), lambda i: (1, 0, 0)),
        ],
        out_specs=pl.BlockSpec((tm, D_MODEL), row),
        out_shape=jax.ShapeDtypeStruct((n, D_MODEL), F32),
        compiler_params=_cparams(("arbitrary",)),
        name="token_b",
    )(x1, y_pool, a, y_hy, mod, nrm, wts["w_out"], wts["wg"], wts["wu"], wts["wd"])


def _seqs_per_step(batch, seq_len):
    return min(batch, max(1, SEQ_STEP_ROWS // seq_len))


def _halo_rows(src_ref, base, r0, rows, seq_len):
    c = src_ref.shape[1]
    zero = jnp.zeros((SUBLANES, c), F32)
    lo = base + r0
    prev = src_ref[lo - SUBLANES:lo, :] if r0 > 0 else zero
    nxt = src_ref[lo + rows:lo + rows + SUBLANES, :] if r0 + rows < seq_len else zero
    return prev, nxt


def _short_conv_chunk(src_ref, base, r0, rows, seq_len, w, b):
    x = src_ref[base + r0:base + r0 + rows, :]
    prev, nxt = _halo_rows(src_ref, base, r0, rows, seq_len)
    ridx = lax.broadcasted_iota(jnp.int32, x.shape, 0)
    xp = jnp.where(ridx == 0, prev[SUBLANES - 1:SUBLANES, :], pltpu.roll(x, 1, axis=0))
    xn = jnp.where(ridx == rows - 1, nxt[0:1, :], pltpu.roll(x, rows - 1, axis=0))
    return xp * w[0:1, :] + x * w[1:2, :] + xn * w[2:3, :] + b


def _pool_kernel(u_ref, w_ref, scale_ref, o_ref, *, seq_len, rows, seqs):
    lane = lax.broadcasted_iota(jnp.int32, (rows, POOL_WIDTH), 1)
    grp = lane // POOL_GROUP
    half = jnp.where(grp == 0, 1, jnp.where(grp == 1, 2, jnp.where(grp == 2, 4, 8)))
    odd_grp = lax.broadcasted_iota(jnp.int32, (rows, LANES), 1) >= POOL_GROUP
    ext = rows + 2 * SUBLANES
    back = lambda v, s: pltpu.roll(v, s, axis=0)
    fwd = lambda v, s: pltpu.roll(v, ext - s, axis=0)
    core = lambda v: v[SUBLANES:SUBLANES + rows, :]
    for base, r0 in [(sq * seq_len, r0) for sq in range(seqs) for r0 in range(0, seq_len, rows)]:
        x = u_ref[base + r0:base + r0 + rows, :]
        prev, nxt = _halo_rows(u_ref, base, r0, rows, seq_len)
        a = jnp.concatenate([prev, x, nxt], axis=0)
        lo, hi = a[:, :LANES], a[:, LANES:]
        b1 = back(lo, 1)
        b2 = b1 + back(b1, 1)
        f2 = lo + fwd(lo, 1)
        sum_lo = jnp.where(odd_grp, core(b2) + core(f2), core(b1) + core(lo))
        c1 = back(hi, 1)
        c2 = c1 + back(c1, 1)
        c4 = c2 + back(c2, 2)
        c8 = c4 + back(c4, 4)
        g2 = hi + fwd(hi, 1)
        g4 = g2 + fwd(g2, 2)
        g8 = g4 + fwd(g4, 4)
        sum_hi = jnp.where(odd_grp, core(c8) + core(g8), core(c4) + core(g4))
        wsum = jnp.concatenate([sum_lo, sum_hi], axis=1)
        t = r0 + lax.broadcasted_iota(jnp.int32, (rows, POOL_WIDTH), 0)
        cnt = jnp.minimum(t + half, seq_len) - jnp.maximum(t - half, 0)
        d = wsum / cnt.astype(F32) - x
        y = _dot(d.astype(BF16), w_ref[...]) * scale_ref[...]
        o_ref[base + r0:base + r0 + rows, :] = y.astype(o_ref.dtype)


def _pool_mix(u_pool, w_bd, scale, batch, seq_len, layer):
    rows = min(SEQ_CHUNK // 2, seq_len)
    seqs = _seqs_per_step(batch, seq_len)
    return pl.pallas_call(
        functools.partial(_pool_kernel, seq_len=seq_len, rows=rows, seqs=seqs),
        grid=(batch // seqs,),
        in_specs=[
            pl.BlockSpec((seqs * seq_len, POOL_WIDTH), lambda b: (b, 0)),
            _layer_block((POOL_WIDTH, POOL_WIDTH), layer),
            _layer_block((1, POOL_WIDTH), layer),
        ],
        out_specs=pl.BlockSpec((seqs * seq_len, POOL_WIDTH), lambda b: (b, 0)),
        out_shape=jax.ShapeDtypeStruct((batch * seq_len, POOL_WIDTH), BF16),
        compiler_params=_cparams(("arbitrary",)),
        name="pool_mix",
    )(u_pool, w_bd, scale)


def _dup_heads(x):
    lane = lax.broadcasted_iota(jnp.int32, x.shape, 1)
    sw = pltpu.roll(x, HEAD_DIM, axis=1)
    lo = lane < HEAD_DIM
    return jnp.concatenate([jnp.where(lo, x, sw), jnp.where(lo, sw, x)], axis=1).astype(BF16)


def _attn_kernel(*refs, has_local, nblocks, sub, seqs, layer):
    if has_local:
        sink_ref, q_ref, kp_ref, kc_ref, kn_ref, vp_ref, vc_ref, vn_ref, ck_ref, cv_ref, o_ref = refs
        kwin = jnp.concatenate([kp_ref[...], kc_ref[...], kn_ref[...]], axis=0)
        vwin = jnp.concatenate([vp_ref[...], vc_ref[...], vn_ref[...]], axis=0)
        kctx, vctx = _dup_heads(ck_ref[...]), _dup_heads(cv_ref[...])
        jj = lax.broadcasted_iota(jnp.int32, (BLOCK, GQA_GROUP * BLOCK), 0)
        rr = lax.broadcasted_iota(jnp.int32, (BLOCK, GQA_GROUP * BLOCK), 1) % BLOCK
    else:
        sink_ref, q_ref, kd_ref, vd_ref, o_ref = refs
    i = pl.program_id(1)
    lane_q = lax.broadcasted_iota(jnp.int32, (BLOCK, LANES), 1)
    head_of_col = lax.broadcasted_iota(jnp.int32, (1, GQA_GROUP * BLOCK), 1) // BLOCK
    log2e = math.log2(math.e)

    def values_t(v2):
        vt = jnp.transpose(v2.astype(F32))
        row = lax.broadcasted_iota(jnp.int32, vt.shape, 0)
        return jnp.where(row < HEAD_DIM, vt, 1.0).astype(BF16)

    def sink_row(kvh):
        row = jnp.full((1, GQA_GROUP * BLOCK), sink_ref[layer, kvh * GQA_GROUP + GQA_GROUP - 1] * log2e, F32)
        for g in range(GQA_GROUP - 1):
            row = jnp.where(head_of_col == g, sink_ref[layer, kvh * GQA_GROUP + g] * log2e, row)
        return row

    if has_local:
        vt_win = [values_t(vwin[:, kvh * LANES:(kvh + 1) * LANES]) for kvh in range(N_KV_HEADS)]
        vt_ctx = [values_t(vctx[:, kvh * LANES:(kvh + 1) * LANES]) for kvh in range(N_KV_HEADS)]
    else:
        vt_own = [values_t(vd_ref[:, kvh * LANES:(kvh + 1) * LANES]) for kvh in range(N_KV_HEADS)]
    units = [(sb, kvh) for sb in range(seqs * sub) for kvh in range(N_KV_HEADS)]
    scores, values = [], []
    for sb, kvh in units:
        q = q_ref[sb * BLOCK:(sb + 1) * BLOCK, :]
        qparts = []
        for hd in range(kvh * GQA_GROUP, (kvh + 1) * GQA_GROUP):
            qp = q[:, (hd // 2) * LANES:(hd // 2 + 1) * LANES]
            keep = (lane_q < HEAD_DIM) == (hd % 2 == 0)
            qparts.append(jnp.where(keep, qp, jnp.zeros_like(qp)))
        qs = jnp.concatenate(qparts, axis=0)
        cols = slice(kvh * LANES, (kvh + 1) * LANES)
        if has_local:
            loc = slice(sb * BLOCK, (sb + 3) * BLOCK)
            kk = jnp.concatenate([kwin[loc, cols], kctx[:, cols]], axis=0)
            vt = jnp.concatenate([vt_win[kvh][:, loc], vt_ctx[kvh]], axis=1)
        else:
            own = slice((sb // sub) * sub * BLOCK, (sb // sub + 1) * sub * BLOCK)
            kk = kd_ref[own, cols]
            vt = vt_own[kvh][:, own]
        s = lax.dot_general(kk, qs, (((1,), (1,)), ((), ())), preferred_element_type=F32)
        if has_local:
            gb = i * sub + sb
            below = jj >= rr + jnp.where(gb >= 1, 0, BLOCK)
            above = jj <= rr - jnp.where(gb <= nblocks - 2, 0, BLOCK)
            s = jnp.concatenate([jnp.where(below, s[:BLOCK, :], NEG_INF), s[BLOCK:2 * BLOCK, :],
                                 jnp.where(above, s[2 * BLOCK:3 * BLOCK, :], NEG_INF), s[3 * BLOCK:, :]], axis=0)
        scores.append(s)
        values.append(vt)

    sinks = [sink_row(kvh) for kvh in range(N_KV_HEADS)]
    maxes = [jnp.maximum(jnp.max(s, axis=0, keepdims=True), sinks[kvh]) for (_, kvh), s in zip(units, scores)]
    probs = [jnp.exp2(s - m).astype(BF16) for s, m in zip(scores, maxes)]
    sums = [_dot(vt, e) for vt, e in zip(values, probs)]
    outs = []
    for (_, kvh), ow, m in zip(units, sums, maxes):
        den = ow[HEAD_DIM:, :] + jnp.exp2(sinks[kvh] - m)
        outs.append(ow[:HEAD_DIM, :] / den)
    for sb in range(seqs * sub):
        blks = []
        for kvh in range(N_KV_HEADS):
            o = outs[sb * N_KV_HEADS + kvh]
            for p in range(GQA_GROUP // 2):
                pair = jnp.concatenate([o[:, 2 * p * BLOCK:(2 * p + 1) * BLOCK], o[:, (2 * p + 1) * BLOCK:(2 * p + 2) * BLOCK]],
                                       axis=0)
                blks.append(jnp.transpose(pair))
        o_ref[sb * BLOCK:(sb + 1) * BLOCK, :] = jnp.concatenate(blks, axis=1).astype(o_ref.dtype)


def _attention(q, kd, vd, sink, batch, seq_len, ctx_k=None, ctx_v=None, layer=0):
    assert WINDOW == BLOCK
    nb = seq_len // BLOCK
    has_local = ctx_k is not None
    sub = min(ATTN_SUB, nb)
    steps = nb // sub
    seqs = 1
    if not has_local:
        assert steps == 1
        seqs = math.gcd(CTX_SEQS, batch)
        batch //= seqs
    qb = seqs * sub * BLOCK
    qspec = pl.BlockSpec((qb, ATTN_WIDTH), lambda b, i: (b * steps + i, 0))
    sspec = pl.BlockSpec(memory_space=pltpu.SMEM)
    if has_local:
        lc = ctx_k.shape[2]
        edge = lambda f: pl.BlockSpec((BLOCK, 2 * KV_WIDTH), f)
        prev = lambda b, i: (b * nb + jnp.maximum(i * sub - 1, 0), 0)
        nxt = lambda b, i: (b * nb + jnp.minimum((i + 1) * sub, nb - 1), 0)
        cur = pl.BlockSpec((qb, 2 * KV_WIDTH), lambda b, i: (b * steps + i, 0))
        cspec = pl.BlockSpec((None, None, lc, KV_WIDTH), lambda b, i: (b, layer, 0, 0))
        in_specs = [sspec, qspec, edge(prev), cur, edge(nxt), edge(prev), cur, edge(nxt), cspec, cspec]
        args = [sink, q, kd, kd, kd, vd, vd, vd, ctx_k, ctx_v]
    else:
        kv = pl.BlockSpec((qb, 2 * KV_WIDTH), lambda b, i: (b, 0))
        in_specs = [sspec, qspec, kv, kv]
        args = [sink, q, kd, vd]
    return pl.pallas_call(
        functools.partial(_attn_kernel, has_local=has_local, nblocks=nb, sub=sub, seqs=seqs, layer=layer),
        grid=(batch, steps),
        in_specs=in_specs,
        out_specs=pl.BlockSpec((qb, ATTN_WIDTH), lambda b, i: (b * steps + i, 0)),
        out_shape=jax.ShapeDtypeStruct(q.shape, BF16),
        compiler_params=_cparams(("arbitrary", "arbitrary")),
        name="attn_latent" if has_local else "attn_context",
    )(*args)


def _filter_kernel(zt_ref, zb_ref, w1_ref, b1_ref, w2_ref, b2_ref, w3_ref, b3_ref, fr_ref, dl_ref, k_ref, sum_ref,
                   *, seq_len, rows):
    i = pl.program_id(0)
    half = rows // 2
    oc = HY_ORDER * HY_WIDTH
    z = jnp.concatenate([zt_ref[...], zb_ref[...]], axis=1)
    h = jnp.sin(fr_ref[0:1, :] * (_dot(z.astype(BF16), w1_ref[...].astype(BF16)) + b1_ref[...]))
    h = jnp.sin(fr_ref[1:2, :] * (_dot(h.astype(BF16), w2_ref[...].astype(BF16)) + b2_ref[...]))
    h3 = _dot(h.astype(BF16), w3_ref[...].astype(BF16)) + b3_ref[...]
    total = jnp.zeros((1, oc), F32)
    for part, z_ref in enumerate((zt_ref, zb_ref)):
        row = i * rows + part * half + lax.broadcasted_iota(jnp.int32, (half, oc), 0)
        t = z_ref[:, 0:1]
        decay = jnp.exp(-t * jnp.abs(dl_ref[...]))
        fwd_dir = h3[:, part * 2 * oc:part * 2 * oc + oc]
        bwd_dir = h3[:, part * 2 * oc + oc:(part + 1) * 2 * oc]
        sel = jnp.where(row < seq_len, fwd_dir, jnp.where(row > seq_len, bwd_dir, 0.0))
        k = sel * (decay + HY_MOD_SHIFT)
        k_ref[part * half:(part + 1) * half, :] = k
        total = total + jnp.sum(jnp.abs(k), axis=0, keepdims=True)

    @pl.when(i == 0)
    def _():
        sum_ref[...] = jnp.zeros_like(sum_ref)

    sum_ref[...] += total


def _block_diag2(w):
    z = jnp.zeros_like(w)
    return jnp.concatenate([jnp.concatenate([w, z], axis=-1), jnp.concatenate([z, w], axis=-1)], axis=-2)


def _filter_weights(w1, b1, w2, b2, w3, b3, freq, deltas):
    two = lambda a: jnp.concatenate([a, a], axis=-1)
    w1p = jnp.pad(w1, ((0, 0), (0, LANES - HY_EMB_DIM), (0, 0)))
    return (_block_diag2(w1p), two(b1)[:, None, :], _block_diag2(w2), two(b2)[:, None, :],
            _block_diag2(w3), two(b3)[:, None, :], two(freq), deltas.reshape(DEPTH, 1, HY_ORDER * HY_WIDTH))


def _hyena_filter(z_ext, fw, seq_len, layer):
    n = 2 * seq_len
    rows = min(1024, n)
    half = rows // 2
    oc = HY_ORDER * HY_WIDTH
    hid = 2 * HY_FILTER_HIDDEN
    const = lambda i: (0, 0)
    return pl.pallas_call(
        functools.partial(_filter_kernel, seq_len=seq_len, rows=rows),
        grid=(n // rows,),
        in_specs=[
            pl.BlockSpec((half, LANES), lambda i: (2 * i, 0)),
            pl.BlockSpec((half, LANES), lambda i: (2 * i + 1, 0)),
            _layer_block((2 * LANES, hid), layer),
            _layer_block((1, hid), layer),
            _layer_block((hid, hid), layer),
            _layer_block((1, hid), layer),
            _layer_block((hid, 4 * oc), layer),
            _layer_block((1, 4 * oc), layer),
            _layer_block((2, hid), layer),
            _layer_block((1, oc), layer),
        ],
        out_specs=[pl.BlockSpec((rows, oc), lambda i: (i, 0)), pl.BlockSpec((1, oc), const)],
        out_shape=[jax.ShapeDtypeStruct((n, oc), F32), jax.ShapeDtypeStruct((1, oc), F32)],
        compiler_params=_cparams(("arbitrary",)),
        name="hyena_filter",
    )(z_ext, z_ext, *fw)


def _spectrum_kernel(k_ref, sum_ref, fwd_ref, kf_ref, hprev_ref, g0_ref, *, blk):
    t = pl.program_id(0)
    a = k_ref[...] / (sum_ref[...] + 1e-6)
    ha = _dot(fwd_ref[...], a.astype(BF16))
    tf = FREQ_TILE
    par = lax.broadcasted_iota(jnp.int32, (tf, 1), 0) % 2
    sgn = (1 - 2 * par).astype(F32)

    @pl.when(t > 0)
    def _():
        g0 = g0_ref[...]
        for c in range(blk // tf):
            re = slice(2 * c * tf, (2 * c + 1) * tf)
            im = slice((2 * c + 1) * tf, (2 * c + 2) * tf)
            kre = ha[re, :] - sgn * hprev_ref[im, :]
            kim = ha[im, :] + sgn * (hprev_ref[re, :] - g0)
            for o in range(HY_ORDER):
                lanes = slice(o * HY_WIDTH, (o + 1) * HY_WIDTH)
                kf_ref[o, c, 0:tf, :] = kre[:, lanes]
                kf_ref[o, c, tf:2 * tf, :] = kim[:, lanes]

    hprev_ref[...] = ha
    g0_ref[...] = a[0:1, :]


def _filter_spectrum(kraw, ksum, fwd, seq_len, blk):
    nb = seq_len // blk
    nlags = 2 * nb - 1
    oc = HY_ORDER * HY_WIDTH
    m = 2 * blk
    tf = FREQ_TILE
    nfc = blk // tf
    return pl.pallas_call(
        functools.partial(_spectrum_kernel, blk=blk),
        grid=(2 * nb,),
        in_specs=[
            pl.BlockSpec((blk, oc), lambda t: ((nb + t) % (2 * nb), 0)),
            pl.BlockSpec((1, oc), lambda t: (0, 0)),
            pl.BlockSpec((m, blk), lambda t: (0, 0)),
        ],
        out_specs=pl.BlockSpec((HY_ORDER, nfc, None, 2 * tf, HY_WIDTH), lambda t: (0, 0, jnp.maximum(t - 1, 0), 0, 0)),
        out_shape=jax.ShapeDtypeStruct((HY_ORDER, nfc, nlags, 2 * tf, HY_WIDTH), F32),
        scratch_shapes=[pltpu.VMEM((m, oc), F32), pltpu.VMEM((1, oc), F32)],
        compiler_params=_cparams(("arbitrary",)),
        name="filter_spectrum",
    )(kraw, ksum, fwd)


def _conv_kernel(z_ref, g_ref, swz_ref, sbz_ref, swg_ref, sbg_ref, bd_ref, kf_ref, fwd_ref, inv_ref, o_ref,
                 zb_ref, zc_ref, yf_ref, *, seq_len, blk, conv_z, seqs, sub):
    fc = pl.program_id(1)
    nfc = pl.num_programs(1)
    nb = seq_len // blk
    tf = FREQ_TILE
    blocks = [((g // nb) * seq_len, (g % nb) * blk) for g in range(seqs * nb)]

    @pl.when(fc == 0)
    def _():
        for base, r0 in blocks:
            rows = slice(base + r0, base + r0 + blk)
            if conv_z:
                zc_ref[rows, :] = _short_conv_chunk(z_ref, base, r0, blk, seq_len, swz_ref[...], sbz_ref[...])
                zb_ref[rows, :] = zc_ref[rows, :].astype(BF16)
            else:
                zb_ref[rows, :] = z_ref[rows, :].astype(BF16)

    zf = [[_dot(fwd_ref[s * 2 * tf:(s + 1) * 2 * tf, :], zb_ref[g * blk:(g + 1) * blk, :]) for g in range(seqs * nb)]
          for s in range(sub)]

    for s in range(sub):
        col = pl.multiple_of((fc * sub + s) * 2 * tf, 2 * tf)
        for g0 in range(0, seqs * nb, nb):
            for bi in range(nb):
                for r in range(0, tf, MAC_ROWS):
                    re = slice(r, r + MAC_ROWS)
                    im = slice(tf + r, tf + r + MAC_ROWS)
                    yr = jnp.zeros((MAC_ROWS, HY_WIDTH), F32)
                    yi = jnp.zeros((MAC_ROWS, HY_WIDTH), F32)
                    for bj in range(nb):
                        lag = bi - bj + nb - 1
                        kr, ki = kf_ref[s, lag, re, :], kf_ref[s, lag, im, :]
                        zr, zi = zf[s][g0 + bj][re, :], zf[s][g0 + bj][im, :]
                        yr = yr + (kr * zr - ki * zi)
                        yi = yi + (kr * zi + ki * zr)
                    yf_ref[g0 + bi, pl.ds(col + r, MAC_ROWS), :] = yr.astype(BF16)
                    yf_ref[g0 + bi, pl.ds(col + tf + r, MAC_ROWS), :] = yi.astype(BF16)

    @pl.when(fc == nfc - 1)
    def _():
        ys, gates = [], []
        for g, (base, r0) in enumerate(blocks):
            ys.append(_dot(inv_ref[...], yf_ref[g]))
            gates.append(_short_conv_chunk(g_ref, base, r0, blk, seq_len, swg_ref[...], sbg_ref[...]))
        for (base, r0), y, gate in zip(blocks, ys, gates):
            rows = slice(base + r0, base + r0 + blk)
            z = zc_ref[rows, :] if conv_z else z_ref[rows, :]
            o_ref[rows, :] = (gate * (y + bd_ref[...] * z)).astype(o_ref.dtype)


def _hyena_conv(z_src, z_col, conv_z, g_src, g_col, short_w, short_b, bd, kf, order, fwd, inv,
                batch, seq_len, blk, out_dtype, layer):
    nb = seq_len // blk
    nlags = 2 * nb - 1
    tf = FREQ_TILE
    sub = min(FREQ_SUB, blk // tf)
    nfc = blk // (sub * tf)
    w = HY_WIDTH
    seqs = _seqs_per_step(batch, seq_len)
    step_rows = seqs * seq_len
    return pl.pallas_call(
        functools.partial(_conv_kernel, seq_len=seq_len, blk=blk, conv_z=conv_z, seqs=seqs, sub=sub),
        grid=(batch // seqs, nfc),
        in_specs=[
            pl.BlockSpec((step_rows, w), lambda b, f: (b, 0)),
            pl.BlockSpec((step_rows, w), lambda b, f: (b, 0)),
            pl.BlockSpec((None, 3, w), lambda b, f: (layer, 0, z_col if conv_z else 0)),
            pl.BlockSpec((None, 1, w), lambda b, f: (layer, 0, z_col if conv_z else 0)),
            pl.BlockSpec((None, 3, w), lambda b, f: (layer, 0, g_col)),
            pl.BlockSpec((None, 1, w), lambda b, f: (layer, 0, g_col)),
            pl.BlockSpec((None, 1, w), lambda b, f: (layer * HY_ORDER + order, 0, 0)),
            pl.BlockSpec((None, sub, nlags, 2 * tf, w), lambda b, f: (order, f, 0, 0, 0)),
            pl.BlockSpec((sub * 2 * tf, blk), lambda b, f: (f, 0)),
            pl.BlockSpec((blk, 2 * blk), lambda b, f: (0, 0)),
        ],
        out_specs=pl.BlockSpec((step_rows, w), lambda b, f: (b, 0)),
        out_shape=jax.ShapeDtypeStruct((batch * seq_len, w), out_dtype),
        scratch_shapes=[
            pltpu.VMEM((step_rows, w), BF16),
            pltpu.VMEM((step_rows if conv_z else SUBLANES, w), F32),
            pltpu.VMEM((seqs * nb, 2 * blk, w), BF16),
        ],
        compiler_params=_cparams(("arbitrary", "arbitrary")),
        name="hyena_conv",
    )(z_src, g_src, short_w, short_b, short_w, short_b, bd, kf, fwd, inv)


def _dft_tables(blk):
    m = 2 * blk
    tf = FREQ_TILE
    nt = blk // tf
    q = 32
    f2 = 2 * jnp.arange(blk, dtype=jnp.int32) + 1

    def cos_sin(times):
        r = (f2[:, None] * times[None, :]) % (2 * m)
        ang = r.astype(F32) * (math.pi / m)
        return jnp.cos(ang), jnp.sin(ang)

    ch, sh = cos_sin(jnp.arange(0, blk, q, dtype=jnp.int32))
    cl, sl = cos_sin(jnp.arange(q, dtype=jnp.int32))
    rep = lambda a: jnp.repeat(a, q, axis=1)
    til = lambda a: jnp.tile(a, (1, blk // q))
    c = rep(ch) * til(cl) - rep(sh) * til(sl)
    s = rep(sh) * til(cl) + rep(ch) * til(sl)
    tiles = [slice(t * tf, (t + 1) * tf) for t in range(nt)]
    fwd = jnp.concatenate([part[t, :] for t in tiles for part in (c, -s)], axis=0).astype(BF16)
    inv = fwd.T * (2.0 / m)
    return fwd, inv.astype(BF16)


def _filter_embedding(seq_len):
    t = jnp.linspace(0.0, 1.0, seq_len, dtype=F32)[:, None]
    bands = (HY_EMB_DIM - 1) // 2
    f = jnp.linspace(1e-4, bands - 1, bands, dtype=F32)[None, :]
    w = 2.0 * math.pi * jnp.arange(seq_len, dtype=F32)[:, None] / seq_len
    z = jnp.concatenate([t, jnp.cos(f * w), -jnp.sin(f * w)], axis=-1)
    z_ext = jnp.concatenate([z, jnp.zeros((1, HY_EMB_DIM), F32), jnp.flip(z[1:], axis=0)], axis=0)
    return jnp.pad(z_ext, ((0, 0), (0, LANES - HY_EMB_DIM)))


def _rope_tables(seq_len):
    quarter = HEAD_DIM // 4
    inv = jnp.tile(ROPE_THETA ** (-jnp.arange(quarter, dtype=F32) / quarter), LANES // quarter)
    lane = jnp.arange(LANES)
    by_row = (lane % HEAD_DIM < HEAD_DIM // 2)[None, None, :]
    first = (lane % (2 * quarter) < quarter)[None, :]

    def cos_sin(npos):
        ang = jnp.arange(npos).astype(F32)[:, None] * inv[None, :]
        return jnp.cos(ang), jnp.sin(ang)

    (cr, sr), (cc, sc) = cos_sin(seq_len // GRID_W), cos_sin(GRID_W)
    lay = lambda r, c: jnp.where(by_row, r[:, None, :], c[None, :, :]).reshape(seq_len, LANES)
    cos, sin = lay(cr, cc), lay(sr, sc)
    return cos, jnp.where(first, -sin, 0.0), jnp.where(first, 0.0, sin)


def _block_ones(width):
    h = jnp.arange(width) // HEAD_DIM
    return (h[:, None] == h[None, :]).astype(BF16)


def _stream_layer(x, p, wts, mod, batch, seq_len, rope_tabs, ctx, layer, hy_blk, consts, cast_next=None):
    gq, gk, fwd, inv, z_ext = consts
    latent = ctx is not None
    outs, next_wts = _token_a(x, mod, p["norm"], wts, gq, gk, p["qg"], p["kg"], rope_tabs, seq_len, layer, cast_next)
    x1, u_pool, q, kd, vd, hy_v, hy_x1, hy_x2 = outs[:8]
    k, v = (None, None) if latent else outs[8:]
    y_pool = _pool_mix(u_pool, p["pool_w"], p["pool_scale"], batch, seq_len, layer)
    if latent:
        a = _attention(q, kd, vd, p["sink"], batch, seq_len, ctx[0], ctx[1], layer)
    else:
        a = _attention(q, kd, vd, p["sink"], batch, seq_len, layer=layer)
    kraw, ksum = _hyena_filter(z_ext, p["filter"], seq_len, layer)
    kf = _filter_spectrum(kraw, ksum, fwd, seq_len, hy_blk)
    z1 = _hyena_conv(hy_v, 0, True, hy_x1, 1, p["short_w"], p["short_b"], p["hy_bias"], kf, 0, fwd, inv,
                     batch, seq_len, hy_blk, F32, layer)
    y_hy = _hyena_conv(z1, 0, False, hy_x2, 2, p["short_w"], p["short_b"], p["hy_bias"], kf, 1, fwd, inv,
                       batch, seq_len, hy_blk, BF16, layer)
    x3 = _token_b(x1, y_pool, a, y_hy, mod, p["norm"], wts, seq_len, layer, latent)
    return x3, k, v, next_wts


def _layer_params(norm_w, pool_w, pool_scale, q_norm, k_norm, attn_sink, hy_short_w, hy_short_b, hy_f_w1, hy_f_b1,
                  hy_f_w2, hy_f_b2, hy_f_w3, hy_f_b3, hy_sin_freq, hy_decay, hy_bias):
    eye = jnp.eye(len(POOL_WINDOWS), dtype=F32)
    pool_bd = (eye[None, :, None, :, None] * pool_w[:, :, :, None, :]).reshape(DEPTH, POOL_WIDTH, POOL_WIDTH)
    return {
        "norm": norm_w,
        "pool_w": pool_bd.astype(BF16), "pool_scale": pool_scale[:, None, :],
        "qg": jnp.tile(q_norm, (1, N_HEADS))[:, None, :], "kg": jnp.tile(k_norm, (1, N_KV_HEADS))[:, None, :],
        "sink": attn_sink,
        "short_w": hy_short_w, "short_b": hy_short_b[:, None, :],
        "filter": _filter_weights(hy_f_w1, hy_f_b1, hy_f_w2, hy_f_b2, hy_f_w3, hy_f_b3, hy_sin_freq, hy_decay),
        "hy_bias": hy_bias.reshape(DEPTH * HY_ORDER, 1, HY_WIDTH),
    }


def kernel(x_prompt, x_sample, cache_k, cache_v, c, c_ctx, ada_w, ada_b, norm_w, ffn_wg, ffn_wu, ffn_wd, w_in, w_out, pool_w, pool_scale, q_norm, k_norm, attn_sink, hy_short_w, hy_short_b, hy_f_w1, hy_f_b1, hy_f_w2, hy_f_b2, hy_f_w3, hy_f_b3, hy_sin_freq, hy_decay, hy_bias):
    bp, lp, _ = x_prompt.shape
    bs, ls, _ = x_sample.shape
    lc = cache_k.shape[2]

    cond = jnp.concatenate([c_ctx[None, :], c, jnp.zeros((COND_ROWS - 1 - bs, D_MODEL), F32)], axis=0)
    mod = _ada_mod(cond, ada_w, ada_b).reshape(DEPTH, COND_ROWS, 1, N_MOD * D_MODEL)

    gq, gk = _block_ones(ATTN_WIDTH), _block_ones(KV_WIDTH)
    rope_tabs = _rope_tables(ls)
    blk_p, blk_s = min(lp, HY_BLOCK), min(ls, HY_BLOCK)
    consts_p = (gq, gk) + _dft_tables(blk_p) + (_filter_embedding(lp),)
    consts_s = (gq, gk) + _dft_tables(blk_s) + (_filter_embedding(ls),)
    ctx_k = cache_k.reshape(bs, DEPTH, lc, KV_WIDTH)
    ctx_v = cache_v.reshape(bs, DEPTH, lc, KV_WIDTH)

    flat = _flat_weights(ffn_wg, ffn_wu, ffn_wd, w_in, w_out)
    wts = _cast_layer(flat, 0)
    yp = x_prompt.reshape(bp * lp, D_MODEL)
    ys = x_sample.reshape(bs * ls, D_MODEL)
    p = _layer_params(norm_w, pool_w, pool_scale, q_norm, k_norm, attn_sink, hy_short_w, hy_short_b, hy_f_w1, hy_f_b1,
                      hy_f_w2, hy_f_b2, hy_f_w3, hy_f_b3, hy_sin_freq, hy_decay, hy_bias)
    ks, vs = [], []
    for l in range(DEPTH):
        yp, k_l, v_l, _ = _stream_layer(yp, p, wts, mod, bp, lp, None, None, l, blk_p, consts_p)
        ks.append(k_l.reshape(bp, lp, N_KV_HEADS, HEAD_DIM))
        vs.append(v_l.reshape(bp, lp, N_KV_HEADS, HEAD_DIM))
        cast_next = (flat, l + 1) if l + 1 < DEPTH else None
        ys, _, _, wts = _stream_layer(ys, p, wts, mod, bs, ls, rope_tabs, (ctx_k, ctx_v), l, blk_s, consts_s, cast_next)
    return (yp.reshape(bp, lp, D_MODEL), ys.reshape(bs, ls, D_MODEL),
            jnp.stack(ks, axis=1), jnp.stack(vs, axis=1))
```
